```python
import jax, jax.numpy as jnp
from jax import lax
import numpy as np

D_MODEL = 1024
BATCH = 8
SEQ = 4096
DEPTH = 1

D_FF = 2816
D_PLE = 256
D_GMLP = D_MODEL
N_SGU_GROUPS = 4
CHUNK = 128
D_POOL = D_MODEL
POOL_WINDOWS = (2, 4, 8, 16)
N_POOL_GROUPS = len(POOL_WINDOWS)
D_IN = 2 * D_GMLP + D_POOL + 2 * D_MODEL
EPS = 1e-6

kernel_name = "hybrid_sgu_pool_macaron_layer"


def _rmsnorm(x, g):
    xf = x.astype(jnp.float32)
    y = xf * lax.rsqrt(jnp.mean(xf * xf, axis=-1, keepdims=True) + EPS)
    return (y * g.astype(jnp.float32)).astype(x.dtype)


def _layernorm(x, g):
    xf = x.astype(jnp.float32)
    mu = jnp.mean(xf, axis=-1, keepdims=True)
    xc = xf - mu
    y = xc * lax.rsqrt(jnp.mean(xc * xc, axis=-1, keepdims=True) + EPS)
    return (y * g.astype(jnp.float32)).astype(x.dtype)


def _swiglu(xn, w_gate, w_up, w_down):
    return (jax.nn.silu(xn @ w_gate) * (xn @ w_up)) @ w_down


def _spatial_gating(u, v, norm_g, w_s, b_s):
    B, S, _ = v.shape
    dg = D_GMLP // N_SGU_GROUPS
    v = _layernorm(v, norm_g)
    vc = v.reshape(B, S // CHUNK, CHUNK, N_SGU_GROUPS, dg)
    causal = jnp.tril(jnp.ones((CHUNK, CHUNK), dtype=bool))
    ws = jnp.where(causal[None], w_s, 0.0).astype(v.dtype)
    sv = jnp.einsum('gts,bcsgd->bctgd', ws, vc) + b_s.T[:, :, None].astype(v.dtype)
    return u * sv.reshape(B, S, D_GMLP)


def _pool_mixer(c, pool_w, pool_scale):
    B, S, _ = c.shape
    dg = D_POOL // N_POOL_GROUPS
    cf = c.astype(jnp.float32)
    cs = jnp.concatenate([jnp.zeros((B, 1, D_POOL), jnp.float32), jnp.cumsum(cf, axis=1)], axis=1)
    t = jnp.arange(S)
    outs = []
    for gi, w in enumerate(POOL_WINDOWS):
        lo = jnp.maximum(t + 1 - w, 0)
        sl = slice(gi * dg, (gi + 1) * dg)
        csg = cs[:, :, sl]
        count = (t + 1 - lo).astype(jnp.float32)[None, :, None]
        mean = (csg[:, 1:] - csg[:, lo]) / count
        diff = (mean - cf[:, :, sl]).astype(c.dtype)
        outs.append(jnp.einsum('bsc,cd->bsd', diff, pool_w[gi]))
    return jnp.concatenate(outs, axis=-1) * pool_scale


def _token_mixer(xn, w_in, sgu_norm_g, sgu_w, sgu_b, pool_w, pool_scale, w_out_a, w_out_b, w_o):
    z = xn @ w_in
    i1 = D_GMLP
    i2 = 2 * D_GMLP
    i3 = i2 + D_POOL
    i4 = i3 + D_MODEL
    u = jax.nn.gelu(z[..., :i1])
    v = jax.nn.gelu(z[..., i1:i2])
    c = z[..., i2:i3]
    ga = z[..., i3:i4]
    gb = z[..., i4:]
    a = _spatial_gating(u, v, sgu_norm_g, sgu_w, sgu_b)
    b = _pool_mixer(c, pool_w, pool_scale)
    y = jax.nn.sigmoid(ga) * (a @ w_out_a) + jax.nn.sigmoid(gb) * (b @ w_out_b)
    return y @ w_o


def _gain(k, shape):
    return 1.0 + 0.02 * jax.random.normal(k, shape, jnp.float32)


def _w(k, shape, fan_in):
    return jax.random.normal(k, shape, jnp.float32) * (fan_in ** -0.5)


def _fwd_setup_inputs(seed: int = 0) -> dict:
    key = jax.random.key(seed)
    ks = jax.random.split(key, 32)
    L = DEPTH
    dgp = D_POOL // N_POOL_GROUPS
    return {
        "x": jax.random.normal(ks[0], (BATCH, SEQ, D_MODEL), jnp.float32),
        "p": jax.random.normal(ks[1], (DEPTH, BATCH, SEQ, D_PLE), jnp.float32),
        "ffn1_pre_g": _gain(ks[2], (L, D_MODEL)),
        "ffn1_w_gate": _w(ks[3], (L, D_MODEL, D_FF), D_MODEL),
        "ffn1_w_up": _w(ks[4], (L, D_MODEL, D_FF), D_MODEL),
        "ffn1_w_down": _w(ks[5], (L, D_FF, D_MODEL), D_FF),
        "ffn1_post_g": _gain(ks[6], (L, D_MODEL)),
        "mix_pre_g": _gain(ks[7], (L, D_MODEL)),
        "w_in": _w(ks[8], (L, D_MODEL, D_IN), D_MODEL),
        "sgu_norm_g": _gain(ks[9], (L, D_GMLP)),
        "sgu_w": _w(ks[10], (L, N_SGU_GROUPS, CHUNK, CHUNK), CHUNK),
        "sgu_b": _gain(ks[11], (L, N_SGU_GROUPS, CHUNK)),
        "pool_w": _w(ks[12], (L, N_POOL_GROUPS, dgp, dgp), dgp),
        "pool_scale": _gain(ks[13], (L, D_POOL)),
        "w_out_a": _w(ks[14], (L, D_GMLP, D_MODEL), D_GMLP),
        "w_out_b": _w(ks[15], (L, D_POOL, D_MODEL), D_POOL),
        "w_o": _w(ks[16], (L, D_MODEL, D_MODEL), D_MODEL),
        "mix_post_g": _gain(ks[17], (L, D_MODEL)),
        "ffn2_pre_g": _gain(ks[18], (L, D_MODEL)),
        "ffn2_w_gate": _w(ks[19], (L, D_MODEL, D_FF), D_MODEL),
        "ffn2_w_up": _w(ks[20], (L, D_MODEL, D_FF), D_MODEL),
        "ffn2_w_down": _w(ks[21], (L, D_FF, D_MODEL), D_FF),
        "ffn2_post_g": _gain(ks[22], (L, D_MODEL)),
        "ple_pre_g": _gain(ks[23], (L, D_MODEL)),
        "ple_w_gate": _w(ks[24], (L, D_MODEL, D_MODEL), D_MODEL),
        "ple_w_proj": _w(ks[25], (L, D_PLE, D_MODEL), D_PLE),
        "ple_post_g": _gain(ks[26], (L, D_MODEL)),
    }


def _fwd_reference(x, p, ffn1_pre_g, ffn1_w_gate, ffn1_w_up, ffn1_w_down, ffn1_post_g,
              mix_pre_g, w_in, sgu_norm_g, sgu_w, sgu_b, pool_w, pool_scale,
              w_out_a, w_out_b, w_o, mix_post_g,
              ffn2_pre_g, ffn2_w_gate, ffn2_w_up, ffn2_w_down, ffn2_post_g,
              ple_pre_g, ple_w_gate, ple_w_proj, ple_post_g):
    h = x
    for i in range(DEPTH):
        f = _swiglu(_rmsnorm(h, ffn1_pre_g[i]), ffn1_w_gate[i], ffn1_w_up[i], ffn1_w_down[i])
        h = h + 0.5 * _rmsnorm(f, ffn1_post_g[i])
        m = _token_mixer(_rmsnorm(h, mix_pre_g[i]), w_in[i], sgu_norm_g[i], sgu_w[i], sgu_b[i],
                         pool_w[i], pool_scale[i], w_out_a[i], w_out_b[i], w_o[i])
        h = h + _rmsnorm(m, mix_post_g[i])
        f = _swiglu(_rmsnorm(h, ffn2_pre_g[i]), ffn2_w_gate[i], ffn2_w_up[i], ffn2_w_down[i])
        h = h + 0.5 * _rmsnorm(f, ffn2_post_g[i])
        gate = jax.nn.sigmoid(_rmsnorm(h, ple_pre_g[i]) @ ple_w_gate[i])
        e = p[i] @ ple_w_proj[i]
        h = h + _rmsnorm(gate * e, ple_post_g[i])
    return h


import jax as _jax
import jax.numpy as _jnp

TWIN_FORMAT = 'train_step'
FWD_PARAMS = ['x', 'p', 'ffn1_pre_g', 'ffn1_w_gate', 'ffn1_w_up', 'ffn1_w_down', 'ffn1_post_g', 'mix_pre_g', 'w_in', 'sgu_norm_g', 'sgu_w', 'sgu_b', 'pool_w', 'pool_scale', 'w_out_a', 'w_out_b', 'w_o', 'mix_post_g', 'ffn2_pre_g', 'ffn2_w_gate', 'ffn2_w_up', 'ffn2_w_down', 'ffn2_post_g', 'ple_pre_g', 'ple_w_gate', 'ple_w_proj', 'ple_post_g']
TWIN_WEIGHTS = ['ffn1_pre_g', 'ffn1_w_gate', 'ffn1_w_up', 'ffn1_w_down', 'ffn1_post_g', 'mix_pre_g', 'w_in', 'sgu_norm_g', 'sgu_w', 'sgu_b', 'pool_w', 'pool_scale', 'w_out_a', 'w_out_b', 'w_o', 'mix_post_g', 'ffn2_pre_g', 'ffn2_w_gate', 'ffn2_w_up', 'ffn2_w_down', 'ffn2_post_g', 'ple_pre_g', 'ple_w_gate', 'ple_w_proj', 'ple_post_g']
TWIN_DIFF_INPUT = 'x'
TWIN_INPUTS = ['x', 'p', 'ffn1_pre_g', 'ffn1_w_gate', 'ffn1_w_up', 'ffn1_w_down', 'ffn1_post_g', 'mix_pre_g', 'w_in', 'sgu_norm_g', 'sgu_w', 'sgu_b', 'pool_w', 'pool_scale', 'w_out_a', 'w_out_b', 'w_o', 'mix_post_g', 'ffn2_pre_g', 'ffn2_w_gate', 'ffn2_w_up', 'ffn2_w_down', 'ffn2_post_g', 'ple_pre_g', 'ple_w_gate', 'ple_w_proj', 'ple_post_g', 'loss_target', 'm_ffn1_pre_g', 'm_ffn1_w_gate', 'm_ffn1_w_up', 'm_ffn1_w_down', 'm_ffn1_post_g', 'm_mix_pre_g', 'm_w_in', 'm_sgu_norm_g', 'm_sgu_w', 'm_sgu_b', 'm_pool_w', 'm_pool_scale', 'm_w_out_a', 'm_w_out_b', 'm_w_o', 'm_mix_post_g', 'm_ffn2_pre_g', 'm_ffn2_w_gate', 'm_ffn2_w_up', 'm_ffn2_w_down', 'm_ffn2_post_g', 'm_ple_pre_g', 'm_ple_w_gate', 'm_ple_w_proj', 'm_ple_post_g', 'v_ffn1_pre_g', 'v_ffn1_w_gate', 'v_ffn1_w_up', 'v_ffn1_w_down', 'v_ffn1_post_g', 'v_mix_pre_g', 'v_w_in', 'v_sgu_norm_g', 'v_sgu_w', 'v_sgu_b', 'v_pool_w', 'v_pool_scale', 'v_w_out_a', 'v_w_out_b', 'v_w_o', 'v_mix_post_g', 'v_ffn2_pre_g', 'v_ffn2_w_gate', 'v_ffn2_w_up', 'v_ffn2_w_down', 'v_ffn2_post_g', 'v_ple_pre_g', 'v_ple_w_gate', 'v_ple_w_proj', 'v_ple_post_g']
TWIN_OUTPUTS = ['loss', 'grad_x', 'grad_ffn1_pre_g', 'grad_ffn1_w_gate', 'grad_ffn1_w_up', 'grad_ffn1_w_down', 'grad_ffn1_post_g', 'grad_mix_pre_g', 'grad_w_in', 'grad_sgu_norm_g', 'grad_sgu_w', 'grad_sgu_b', 'grad_pool_w', 'grad_pool_scale', 'grad_w_out_a', 'grad_w_out_b', 'grad_w_o', 'grad_mix_post_g', 'grad_ffn2_pre_g', 'grad_ffn2_w_gate', 'grad_ffn2_w_up', 'grad_ffn2_w_down', 'grad_ffn2_post_g', 'grad_ple_pre_g', 'grad_ple_w_gate', 'grad_ple_w_proj', 'grad_ple_post_g', 'delta_ffn1_pre_g', 'delta_ffn1_w_gate', 'delta_ffn1_w_up', 'delta_ffn1_w_down', 'delta_ffn1_post_g', 'delta_mix_pre_g', 'delta_w_in', 'delta_sgu_norm_g', 'delta_sgu_w', 'delta_sgu_b', 'delta_pool_w', 'delta_pool_scale', 'delta_w_out_a', 'delta_w_out_b', 'delta_w_o', 'delta_mix_post_g', 'delta_ffn2_pre_g', 'delta_ffn2_w_gate', 'delta_ffn2_w_up', 'delta_ffn2_w_down', 'delta_ffn2_post_g', 'delta_ple_pre_g', 'delta_ple_w_gate', 'delta_ple_w_proj', 'delta_ple_post_g', 'new_m_ffn1_pre_g', 'new_m_ffn1_w_gate', 'new_m_ffn1_w_up', 'new_m_ffn1_w_down', 'new_m_ffn1_post_g', 'new_m_mix_pre_g', 'new_m_w_in', 'new_m_sgu_norm_g', 'new_m_sgu_w', 'new_m_sgu_b', 'new_m_pool_w', 'new_m_pool_scale', 'new_m_w_out_a', 'new_m_w_out_b', 'new_m_w_o', 'new_m_mix_post_g', 'new_m_ffn2_pre_g', 'new_m_ffn2_w_gate', 'new_m_ffn2_w_up', 'new_m_ffn2_w_down', 'new_m_ffn2_post_g', 'new_m_ple_pre_g', 'new_m_ple_w_gate', 'new_m_ple_w_proj', 'new_m_ple_post_g', 'new_v_ffn1_pre_g', 'new_v_ffn1_w_gate', 'new_v_ffn1_w_up', 'new_v_ffn1_w_down', 'new_v_ffn1_post_g', 'new_v_mix_pre_g', 'new_v_w_in', 'new_v_sgu_norm_g', 'new_v_sgu_w', 'new_v_sgu_b', 'new_v_pool_w', 'new_v_pool_scale', 'new_v_w_out_a', 'new_v_w_out_b', 'new_v_w_o', 'new_v_mix_post_g', 'new_v_ffn2_pre_g', 'new_v_ffn2_w_gate', 'new_v_ffn2_w_up', 'new_v_ffn2_w_down', 'new_v_ffn2_post_g', 'new_v_ple_pre_g', 'new_v_ple_w_gate', 'new_v_ple_w_proj', 'new_v_ple_post_g']
TWIN_LEAF_KINDS = {'loss': 'loss', 'grad_x': 'grad_x', 'grad_ffn1_pre_g': 'grad_w', 'grad_ffn1_w_gate': 'grad_w', 'grad_ffn1_w_up': 'grad_w', 'grad_ffn1_w_down': 'grad_w', 'grad_ffn1_post_g': 'grad_w', 'grad_mix_pre_g': 'grad_w', 'grad_w_in': 'grad_w', 'grad_sgu_norm_g': 'grad_w', 'grad_sgu_w': 'grad_w', 'grad_sgu_b': 'grad_w', 'grad_pool_w': 'grad_w', 'grad_pool_scale': 'grad_w', 'grad_w_out_a': 'grad_w', 'grad_w_out_b': 'grad_w', 'grad_w_o': 'grad_w', 'grad_mix_post_g': 'grad_w', 'grad_ffn2_pre_g': 'grad_w', 'grad_ffn2_w_gate': 'grad_w', 'grad_ffn2_w_up': 'grad_w', 'grad_ffn2_w_down': 'grad_w', 'grad_ffn2_post_g': 'grad_w', 'grad_ple_pre_g': 'grad_w', 'grad_ple_w_gate': 'grad_w', 'grad_ple_w_proj': 'grad_w', 'grad_ple_post_g': 'grad_w', 'delta_ffn1_pre_g': 'delta_w', 'delta_ffn1_w_gate': 'delta_w', 'delta_ffn1_w_up': 'delta_w', 'delta_ffn1_w_down': 'delta_w', 'delta_ffn1_post_g': 'delta_w', 'delta_mix_pre_g': 'delta_w', 'delta_w_in': 'delta_w', 'delta_sgu_norm_g': 'delta_w', 'delta_sgu_w': 'delta_w', 'delta_sgu_b': 'delta_w', 'delta_pool_w': 'delta_w', 'delta_pool_scale': 'delta_w', 'delta_w_out_a': 'delta_w', 'delta_w_out_b': 'delta_w', 'delta_w_o': 'delta_w', 'delta_mix_post_g': 'delta_w', 'delta_ffn2_pre_g': 'delta_w', 'delta_ffn2_w_gate': 'delta_w', 'delta_ffn2_w_up': 'delta_w', 'delta_ffn2_w_down': 'delta_w', 'delta_ffn2_post_g': 'delta_w', 'delta_ple_pre_g': 'delta_w', 'delta_ple_w_gate': 'delta_w', 'delta_ple_w_proj': 'delta_w', 'delta_ple_post_g': 'delta_w', 'new_m_ffn1_pre_g': 'new_m', 'new_m_ffn1_w_gate': 'new_m', 'new_m_ffn1_w_up': 'new_m', 'new_m_ffn1_w_down': 'new_m', 'new_m_ffn1_post_g': 'new_m', 'new_m_mix_pre_g': 'new_m', 'new_m_w_in': 'new_m', 'new_m_sgu_norm_g': 'new_m', 'new_m_sgu_w': 'new_m', 'new_m_sgu_b': 'new_m', 'new_m_pool_w': 'new_m', 'new_m_pool_scale': 'new_m', 'new_m_w_out_a': 'new_m', 'new_m_w_out_b': 'new_m', 'new_m_w_o': 'new_m', 'new_m_mix_post_g': 'new_m', 'new_m_ffn2_pre_g': 'new_m', 'new_m_ffn2_w_gate': 'new_m', 'new_m_ffn2_w_up': 'new_m', 'new_m_ffn2_w_down': 'new_m', 'new_m_ffn2_post_g': 'new_m', 'new_m_ple_pre_g': 'new_m', 'new_m_ple_w_gate': 'new_m', 'new_m_ple_w_proj': 'new_m', 'new_m_ple_post_g': 'new_m', 'new_v_ffn1_pre_g': 'new_v', 'new_v_ffn1_w_gate': 'new_v', 'new_v_ffn1_w_up': 'new_v', 'new_v_ffn1_w_down': 'new_v', 'new_v_ffn1_post_g': 'new_v', 'new_v_mix_pre_g': 'new_v', 'new_v_w_in': 'new_v', 'new_v_sgu_norm_g': 'new_v', 'new_v_sgu_w': 'new_v', 'new_v_sgu_b': 'new_v', 'new_v_pool_w': 'new_v', 'new_v_pool_scale': 'new_v', 'new_v_w_out_a': 'new_v', 'new_v_w_out_b': 'new_v', 'new_v_w_o': 'new_v', 'new_v_mix_post_g': 'new_v', 'new_v_ffn2_pre_g': 'new_v', 'new_v_ffn2_w_gate': 'new_v', 'new_v_ffn2_w_up': 'new_v', 'new_v_ffn2_w_down': 'new_v', 'new_v_ffn2_post_g': 'new_v', 'new_v_ple_pre_g': 'new_v', 'new_v_ple_w_gate': 'new_v', 'new_v_ple_w_proj': 'new_v', 'new_v_ple_post_g': 'new_v'}


def _forward(args):
    return _fwd_reference(*[args[k] for k in FWD_PARAMS])


def _output_shape():
    out = _jax.eval_shape(lambda: _forward(_fwd_setup_inputs(0)))
    return out.shape, out.dtype

N_MICROBATCH = 1
ADAM_LR = 0.001
ADAM_B1 = 0.9
ADAM_B2 = 0.999
ADAM_EPS = 1e-08
ADAM_WD = 0.01
ADAM_STEP = 10
PER_EXAMPLE_BATCH_AXIS = {'x': 0, 'p': 1, 'loss_target': 0}
SHARED_INPUTS = []
_WEIGHT_DTYPES = {'ffn1_pre_g': _jnp.float32, 'ffn1_w_gate': _jnp.float32, 'ffn1_w_up': _jnp.float32, 'ffn1_w_down': _jnp.float32, 'ffn1_post_g': _jnp.float32, 'mix_pre_g': _jnp.float32, 'w_in': _jnp.float32, 'sgu_norm_g': _jnp.float32, 'sgu_w': _jnp.float32, 'sgu_b': _jnp.float32, 'pool_w': _jnp.float32, 'pool_scale': _jnp.float32, 'w_out_a': _jnp.float32, 'w_out_b': _jnp.float32, 'w_o': _jnp.float32, 'mix_post_g': _jnp.float32, 'ffn2_pre_g': _jnp.float32, 'ffn2_w_gate': _jnp.float32, 'ffn2_w_up': _jnp.float32, 'ffn2_w_down': _jnp.float32, 'ffn2_post_g': _jnp.float32, 'ple_pre_g': _jnp.float32, 'ple_w_gate': _jnp.float32, 'ple_w_proj': _jnp.float32, 'ple_post_g': _jnp.float32}
MOMENT_SCALE = {'ffn1_pre_g': 4.505213e-01, 'ffn1_w_gate': 1.826857e-01, 'ffn1_w_up': 1.928542e-01, 'ffn1_w_down': 3.298453e-01, 'ffn1_post_g': 7.929582e+00, 'mix_pre_g': 6.488388e-01, 'w_in': 2.965486e-01, 'sgu_norm_g': 1.499389e-01, 'sgu_w': 2.089140e-01, 'sgu_b': 3.017904e-01, 'pool_w': 5.193381e-01, 'pool_scale': 5.529920e-01, 'w_out_a': 4.475656e-01, 'w_out_b': 5.264295e-01, 'w_o': 7.078034e-01, 'mix_post_g': 3.193764e+01, 'ffn2_pre_g': 4.205194e-01, 'ffn2_w_gate': 1.408324e-01, 'ffn2_w_up': 1.980593e-01, 'ffn2_w_down': 3.306219e-01, 'ffn2_post_g': 7.823357e+00, 'ple_pre_g': 1.324700e-01, 'ple_w_gate': 1.333556e-01, 'ple_w_proj': 3.332215e-01, 'ple_post_g': 3.219932e+01}


def _to_microbatches(a, axis):
    t = _jnp.moveaxis(a, axis, 0)
    t = t.reshape((N_MICROBATCH, t.shape[0] // N_MICROBATCH) + t.shape[1:])
    return _jnp.moveaxis(t, 1, axis + 1)


def setup_inputs(seed: int = 0) -> dict:
    inp = _fwd_setup_inputs(seed)
    key = _jax.random.fold_in(_jax.random.key(seed), 7919)
    shape, _ = _output_shape()
    out = dict(inp)
    out["loss_target"] = _jax.random.normal(_jax.random.fold_in(key, 0), shape, _jnp.float32)
    for i, name in enumerate(TWIN_WEIGHTS):
        w = inp[name].astype(_jnp.float32)
        if MOMENT_SCALE is None:
            s = _jnp.sqrt(_jnp.mean(_jnp.square(w)) + 1e-30)
        else:
            s = MOMENT_SCALE[name]
        km, kv = _jax.random.split(_jax.random.fold_in(key, i + 1))
        out[name] = w
        out["m_" + name] = s * _jax.random.normal(km, w.shape, _jnp.float32)
        out["v_" + name] = (s * s) * _jax.random.uniform(kv, w.shape, _jnp.float32, 0.5, 1.5)
    if N_MICROBATCH > 1:
        for name, axis in PER_EXAMPLE_BATCH_AXIS.items():
            out[name] = _to_microbatches(out[name], axis)
    return {'x': out['x'], 'p': out['p'], 'ffn1_pre_g': out['ffn1_pre_g'], 'ffn1_w_gate': out['ffn1_w_gate'], 'ffn1_w_up': out['ffn1_w_up'], 'ffn1_w_down': out['ffn1_w_down'], 'ffn1_post_g': out['ffn1_post_g'], 'mix_pre_g': out['mix_pre_g'], 'w_in': out['w_in'], 'sgu_norm_g': out['sgu_norm_g'], 'sgu_w': out['sgu_w'], 'sgu_b': out['sgu_b'], 'pool_w': out['pool_w'], 'pool_scale': out['pool_scale'], 'w_out_a': out['w_out_a'], 'w_out_b': out['w_out_b'], 'w_o': out['w_o'], 'mix_post_g': out['mix_post_g'], 'ffn2_pre_g': out['ffn2_pre_g'], 'ffn2_w_gate': out['ffn2_w_gate'], 'ffn2_w_up': out['ffn2_w_up'], 'ffn2_w_down': out['ffn2_w_down'], 'ffn2_post_g': out['ffn2_post_g'], 'ple_pre_g': out['ple_pre_g'], 'ple_w_gate': out['ple_w_gate'], 'ple_w_proj': out['ple_w_proj'], 'ple_post_g': out['ple_post_g'], 'loss_target': out['loss_target'], 'm_ffn1_pre_g': out['m_ffn1_pre_g'], 'm_ffn1_w_gate': out['m_ffn1_w_gate'], 'm_ffn1_w_up': out['m_ffn1_w_up'], 'm_ffn1_w_down': out['m_ffn1_w_down'], 'm_ffn1_post_g': out['m_ffn1_post_g'], 'm_mix_pre_g': out['m_mix_pre_g'], 'm_w_in': out['m_w_in'], 'm_sgu_norm_g': out['m_sgu_norm_g'], 'm_sgu_w': out['m_sgu_w'], 'm_sgu_b': out['m_sgu_b'], 'm_pool_w': out['m_pool_w'], 'm_pool_scale': out['m_pool_scale'], 'm_w_out_a': out['m_w_out_a'], 'm_w_out_b': out['m_w_out_b'], 'm_w_o': out['m_w_o'], 'm_mix_post_g': out['m_mix_post_g'], 'm_ffn2_pre_g': out['m_ffn2_pre_g'], 'm_ffn2_w_gate': out['m_ffn2_w_gate'], 'm_ffn2_w_up': out['m_ffn2_w_up'], 'm_ffn2_w_down': out['m_ffn2_w_down'], 'm_ffn2_post_g': out['m_ffn2_post_g'], 'm_ple_pre_g': out['m_ple_pre_g'], 'm_ple_w_gate': out['m_ple_w_gate'], 'm_ple_w_proj': out['m_ple_w_proj'], 'm_ple_post_g': out['m_ple_post_g'], 'v_ffn1_pre_g': out['v_ffn1_pre_g'], 'v_ffn1_w_gate': out['v_ffn1_w_gate'], 'v_ffn1_w_up': out['v_ffn1_w_up'], 'v_ffn1_w_down': out['v_ffn1_w_down'], 'v_ffn1_post_g': out['v_ffn1_post_g'], 'v_mix_pre_g': out['v_mix_pre_g'], 'v_w_in': out['v_w_in'], 'v_sgu_norm_g': out['v_sgu_norm_g'], 'v_sgu_w': out['v_sgu_w'], 'v_sgu_b': out['v_sgu_b'], 'v_pool_w': out['v_pool_w'], 'v_pool_scale': out['v_pool_scale'], 'v_w_out_a': out['v_w_out_a'], 'v_w_out_b': out['v_w_out_b'], 'v_w_o': out['v_w_o'], 'v_mix_post_g': out['v_mix_post_g'], 'v_ffn2_pre_g': out['v_ffn2_pre_g'], 'v_ffn2_w_gate': out['v_ffn2_w_gate'], 'v_ffn2_w_up': out['v_ffn2_w_up'], 'v_ffn2_w_down': out['v_ffn2_w_down'], 'v_ffn2_post_g': out['v_ffn2_post_g'], 'v_ple_pre_g': out['v_ple_pre_g'], 'v_ple_w_gate': out['v_ple_w_gate'], 'v_ple_w_proj': out['v_ple_w_proj'], 'v_ple_post_g': out['v_ple_post_g']}


def _loss(weights, diff, rest, loss_target):
    with _jax.named_scope("forward"):
        args = {**rest, TWIN_DIFF_INPUT: diff, **{k: w.astype(_WEIGHT_DTYPES[k]) for k, w in weights.items()}}
        y = _forward(args)
    with _jax.named_scope("loss_head"):
        err = _jnp.square(y.astype(_jnp.float32) - loss_target)
        return 0.5 * _jnp.sum(_jnp.mean(err, axis=-1)) if err.ndim else 0.5 * err


def _adamw(w, g, m, v):
    m = ADAM_B1 * m + (1.0 - ADAM_B1) * g
    v = ADAM_B2 * v + (1.0 - ADAM_B2) * _jnp.square(g)
    m_hat = m / (1.0 - ADAM_B1 ** ADAM_STEP)
    v_hat = v / (1.0 - ADAM_B2 ** ADAM_STEP)
    delta = -ADAM_LR * (m_hat / (_jnp.sqrt(v_hat) + ADAM_EPS) + ADAM_WD * w)
    return delta, m, v


def reference(x, p, ffn1_pre_g, ffn1_w_gate, ffn1_w_up, ffn1_w_down, ffn1_post_g, mix_pre_g, w_in, sgu_norm_g, sgu_w, sgu_b, pool_w, pool_scale, w_out_a, w_out_b, w_o, mix_post_g, ffn2_pre_g, ffn2_w_gate, ffn2_w_up, ffn2_w_down, ffn2_post_g, ple_pre_g, ple_w_gate, ple_w_proj, ple_post_g, loss_target, m_ffn1_pre_g, m_ffn1_w_gate, m_ffn1_w_up, m_ffn1_w_down, m_ffn1_post_g, m_mix_pre_g, m_w_in, m_sgu_norm_g, m_sgu_w, m_sgu_b, m_pool_w, m_pool_scale, m_w_out_a, m_w_out_b, m_w_o, m_mix_post_g, m_ffn2_pre_g, m_ffn2_w_gate, m_ffn2_w_up, m_ffn2_w_down, m_ffn2_post_g, m_ple_pre_g, m_ple_w_gate, m_ple_w_proj, m_ple_post_g, v_ffn1_pre_g, v_ffn1_w_gate, v_ffn1_w_up, v_ffn1_w_down, v_ffn1_post_g, v_mix_pre_g, v_w_in, v_sgu_norm_g, v_sgu_w, v_sgu_b, v_pool_w, v_pool_scale, v_w_out_a, v_w_out_b, v_w_o, v_mix_post_g, v_ffn2_pre_g, v_ffn2_w_gate, v_ffn2_w_up, v_ffn2_w_down, v_ffn2_post_g, v_ple_pre_g, v_ple_w_gate, v_ple_w_proj, v_ple_post_g):
    given = dict(x=x, p=p, ffn1_pre_g=ffn1_pre_g, ffn1_w_gate=ffn1_w_gate, ffn1_w_up=ffn1_w_up, ffn1_w_down=ffn1_w_down, ffn1_post_g=ffn1_post_g, mix_pre_g=mix_pre_g, w_in=w_in, sgu_norm_g=sgu_norm_g, sgu_w=sgu_w, sgu_b=sgu_b, pool_w=pool_w, pool_scale=pool_scale, w_out_a=w_out_a, w_out_b=w_out_b, w_o=w_o, mix_post_g=mix_post_g, ffn2_pre_g=ffn2_pre_g, ffn2_w_gate=ffn2_w_gate, ffn2_w_up=ffn2_w_up, ffn2_w_down=ffn2_w_down, ffn2_post_g=ffn2_post_g, ple_pre_g=ple_pre_g, ple_w_gate=ple_w_gate, ple_w_proj=ple_w_proj, ple_post_g=ple_post_g, loss_target=loss_target, m_ffn1_pre_g=m_ffn1_pre_g, m_ffn1_w_gate=m_ffn1_w_gate, m_ffn1_w_up=m_ffn1_w_up, m_ffn1_w_down=m_ffn1_w_down, m_ffn1_post_g=m_ffn1_post_g, m_mix_pre_g=m_mix_pre_g, m_w_in=m_w_in, m_sgu_norm_g=m_sgu_norm_g, m_sgu_w=m_sgu_w, m_sgu_b=m_sgu_b, m_pool_w=m_pool_w, m_pool_scale=m_pool_scale, m_w_out_a=m_w_out_a, m_w_out_b=m_w_out_b, m_w_o=m_w_o, m_mix_post_g=m_mix_post_g, m_ffn2_pre_g=m_ffn2_pre_g, m_ffn2_w_gate=m_ffn2_w_gate, m_ffn2_w_up=m_ffn2_w_up, m_ffn2_w_down=m_ffn2_w_down, m_ffn2_post_g=m_ffn2_post_g, m_ple_pre_g=m_ple_pre_g, m_ple_w_gate=m_ple_w_gate, m_ple_w_proj=m_ple_w_proj, m_ple_post_g=m_ple_post_g, v_ffn1_pre_g=v_ffn1_pre_g, v_ffn1_w_gate=v_ffn1_w_gate, v_ffn1_w_up=v_ffn1_w_up, v_ffn1_w_down=v_ffn1_w_down, v_ffn1_post_g=v_ffn1_post_g, v_mix_pre_g=v_mix_pre_g, v_w_in=v_w_in, v_sgu_norm_g=v_sgu_norm_g, v_sgu_w=v_sgu_w, v_sgu_b=v_sgu_b, v_pool_w=v_pool_w, v_pool_scale=v_pool_scale, v_w_out_a=v_w_out_a, v_w_out_b=v_w_out_b, v_w_o=v_w_o, v_mix_post_g=v_mix_post_g, v_ffn2_pre_g=v_ffn2_pre_g, v_ffn2_w_gate=v_ffn2_w_gate, v_ffn2_w_up=v_ffn2_w_up, v_ffn2_w_down=v_ffn2_w_down, v_ffn2_post_g=v_ffn2_post_g, v_ple_pre_g=v_ple_pre_g, v_ple_w_gate=v_ple_w_gate, v_ple_w_proj=v_ple_w_proj, v_ple_post_g=v_ple_post_g)
    weights = {n: given[n] for n in TWIN_WEIGHTS}
    shared = {n: given[n] for n in SHARED_INPUTS}
    per_example = {n: given[n] for n in ['x', 'p']}
    grad_fn = _jax.value_and_grad(_loss, argnums=(0, 1))

    def one_microbatch(ex, loss_target):
        ex = dict(ex)
        diff = ex.pop(TWIN_DIFF_INPUT)
        return grad_fn(weights, diff, {**shared, **ex}, loss_target)

    if N_MICROBATCH == 1:
        loss, (grad_w, grad_x) = one_microbatch(per_example, given["loss_target"])
    else:
        def body(carry, xs):
            loss_sum, grad_sum = carry
            l_k, (gw_k, gx_k) = one_microbatch(xs[0], xs[1])
            with _jax.named_scope("update"):
                return (loss_sum + l_k, _jax.tree.map(_jnp.add, grad_sum, gw_k)), gx_k

        init = (_jnp.zeros((), _jnp.float32), _jax.tree.map(_jnp.zeros_like, weights))
        (loss, grad_w), grad_x = _jax.lax.scan(body, init, (per_example, given["loss_target"]))
    with _jax.named_scope("update"):
        delta_w, new_m, new_v = {}, {}, {}
        for n in TWIN_WEIGHTS:
            delta_w[n], new_m[n], new_v[n] = _adamw(weights[n], grad_w[n], given["m_" + n], given["v_" + n])
    return (loss, grad_x, *[grad_w[n] for n in TWIN_WEIGHTS], *[delta_w[n] for n in TWIN_WEIGHTS],
            *[new_m[n] for n in TWIN_WEIGHTS], *[new_v[n] for n in TWIN_WEIGHTS])
```

```python
import functools
import math

import jax
import jax.numpy as jnp
from jax import lax
from jax.experimental import pallas as pl
from jax.experimental.pallas import tpu as pltpu

EPS = 1e-6
CHUNK = 128
N_SGU_GROUPS = 4
POOL_WINDOWS = (2, 4, 8, 16)
ADAM_LR = 0.001
ADAM_B1 = 0.9
ADAM_B2 = 0.999
ADAM_EPS = 1e-08
ADAM_WD = 0.01
ADAM_STEP = 10

N_DEV = 8
AXES = ("x", "y", "c")
MESH = pl.DeviceIdType.MESH
V7X_VMEM_BYTES = 64 * 1024 * 1024
VMEM_LIMIT = V7X_VMEM_BYTES - 8 * 1024 * 1024
BF = jnp.bfloat16
F32 = jnp.float32

_DN = {
    "nn": (((1,), (0,)), ((), ())),
    "nt": (((1,), (1,)), ((), ())),
    "tn": (((0,), (0,)), ((), ())),
}


def _params(sem=None):
    return pltpu.CompilerParams(dimension_semantics=sem, vmem_limit_bytes=VMEM_LIMIT)


def _tile(n, target, align=128):
    t = min(n, target)
    t -= t % align
    while t >= align:
        if n % t == 0:
            return t
        t -= align
    return n


def _sig(x):
    return 1.0 / (1.0 + jnp.exp(-x))


_GELU_K = math.sqrt(2.0 / math.pi)
_GELU_C = 0.044715


def _gelu(x):
    return 0.5 * x * (1.0 + jnp.tanh(_GELU_K * (x + _GELU_C * x * x * x)))


def _gelu_grad(x):
    t = jnp.tanh(_GELU_K * (x + _GELU_C * x * x * x))
    return 0.5 * (1.0 + t) + 0.5 * x * (1.0 - t * t) * _GELU_K * (1.0 + 3.0 * _GELU_C * x * x)


def _rms_r(x):
    return lax.rsqrt(jnp.mean(x * x, axis=-1, keepdims=True) + EPS)


def _rms_bwd(x, g, dy):
    r = _rms_r(x)
    xh = x * r
    gy = dy * g
    dx = r * (gy - xh * jnp.mean(xh * gy, axis=-1, keepdims=True))
    return dx, jnp.sum(dy * xh, axis=0, keepdims=True)


def _matmul(name, mats, *, m, n, tm, tn, nk=1, extras=(), outs, epilogue, n_acc, j_outer=False):
    ni, nj = m // tm, n // tn
    assert ni * tm == m and nj * tn == n, (name, m, n, tm, tn)
    if j_outer:
        grid = (nj, ni, nk)

        def ij(g0, g1):
            return g1, g0
    else:
        grid = (ni, nj, nk)

        def ij(g0, g1):
            return g0, g1

    in_specs, args = [], []
    for mt in mats:
        mode, tk = mt["mode"], mt["tk"]
        ao, bo = mt.get("a_off", (0, 0)), mt.get("b_off", (0, 0))
        if mode == "tn":
            sa = pl.BlockSpec((tk, tm), lambda g0, g1, kk, ao=ao: (ao[0] + kk, ao[1] + ij(g0, g1)[0]))
        else:
            sa = pl.BlockSpec((tm, tk), lambda g0, g1, kk, ao=ao: (ao[0] + ij(g0, g1)[0], ao[1] + kk))
        if mode == "nt":
            sb = pl.BlockSpec((tn, tk), lambda g0, g1, kk, bo=bo: (bo[0] + ij(g0, g1)[1], bo[1] + kk))
        else:
            sb = pl.BlockSpec((tk, tn), lambda g0, g1, kk, bo=bo: (bo[0] + kk, bo[1] + ij(g0, g1)[1]))
        in_specs += [sa, sb]
        args += [mt["a"], mt["b"]]
    n_mat_refs = len(args)
    for ex in extras:
        if ex[0] == "mn":
            in_specs.append(pl.BlockSpec((tm, tn), lambda g0, g1, kk, c=ex[2]: (ij(g0, g1)[0], c + ij(g0, g1)[1])))
        else:
            in_specs.append(pl.BlockSpec((1, tn), lambda g0, g1, kk: (0, ij(g0, g1)[1])))
        args.append(ex[1])
    n_in = len(args)
    out_specs, out_shape = [], []
    for o in outs:
        if o[0] == "mn":
            out_specs.append(pl.BlockSpec((tm, tn), lambda g0, g1, kk, c=o[3]: (ij(g0, g1)[0], c + ij(g0, g1)[1])))
        else:
            assert nj == 1, name
            out_specs.append(pl.BlockSpec((1, tn), lambda g0, g1, kk: (0, 0)))
        out_shape.append(jax.ShapeDtypeStruct(o[1], o[2]))
    n_out = len(outs)

    def body(*refs):
        mat_refs = refs[:n_mat_refs]
        ex_refs = refs[n_mat_refs:n_in]
        out_refs = refs[n_in:n_in + n_out]
        acc_refs = refs[n_in + n_out:]
        i = ij(pl.program_id(0), pl.program_id(1))[0]
        kk = pl.program_id(2)

        def products():
            res = [None] * n_acc
            for idx, mt in enumerate(mats):
                a = mat_refs[2 * idx][...].astype(BF)
                b = mat_refs[2 * idx + 1][...].astype(BF)
                p = lax.dot_general(a, b, _DN[mt["mode"]], preferred_element_type=F32)
                q = mt["acc"]
                res[q] = p if res[q] is None else res[q] + p
            return res

        def finish(accs):
            vals = epilogue(accs, [r[...] for r in ex_refs])
            for o, r, v in zip(outs, out_refs, vals):
                if o[0] == "mn":
                    r[...] = v.astype(o[2])
                else:
                    @pl.when(i == 0)
                    def _():
                        r[...] = v

                    @pl.when(i != 0)
                    def _():
                        r[...] += v

        if nk == 1:
            finish(products())
        else:
            res = products()

            @pl.when(kk == 0)
            def _():
                for q in range(n_acc):
                    acc_refs[q][...] = res[q]

            @pl.when(kk != 0)
            def _():
                for q in range(n_acc):
                    acc_refs[q][...] += res[q]

            @pl.when(kk == nk - 1)
            def _():
                finish([r[...] for r in acc_refs])

    scratch = [pltpu.VMEM((tm, tn), F32) for _ in range(n_acc)] if nk > 1 else []
    return pl.pallas_call(
        body,
        name=name,
        grid=grid,
        in_specs=in_specs,
        out_specs=out_specs,
        out_shape=out_shape,
        scratch_shapes=scratch,
        compiler_params=_params(("arbitrary", "arbitrary", "arbitrary")),
    )(*args)


def _rowwise(name, ins, outs, fn, *, t, tm):
    ni = t // tm
    assert ni * tm == t, (name, t, tm)
    in_specs, args = [], []
    for s in ins:
        if s[0] == "row":
            in_specs.append(pl.BlockSpec((tm, s[2]), lambda i, c=s[3]: (i, c)))
        else:
            nd = s[1].ndim
            in_specs.append(pl.BlockSpec(s[1].shape, lambda i, nd=nd: (0,) * nd))
        args.append(s[1])
    out_specs, out_shape = [], []
    for o in outs:
        if o[0] == "row":
            out_specs.append(pl.BlockSpec((tm, o[3]), lambda i, c=o[4]: (i, c)))
        else:
            nd = len(o[1])
            out_specs.append(pl.BlockSpec(o[1], lambda i, nd=nd: (0,) * nd))
        out_shape.append(jax.ShapeDtypeStruct(o[1], o[2]))
    n_in = len(args)

    def body(*refs):
        i = pl.program_id(0)
        vals = fn([r[...] for r in refs[:n_in]])
        for o, r, v in zip(outs, refs[n_in:], vals):
            if o[0] == "row":
                r[...] = v.astype(o[2])
            else:
                @pl.when(i == 0)
                def _():
                    r[...] = v

                @pl.when(i != 0)
                def _():
                    r[...] += v

    return pl.pallas_call(
        body,
        name=name,
        grid=(ni,),
        in_specs=in_specs,
        out_specs=out_specs,
        out_shape=out_shape,
        compiler_params=_params(("arbitrary",)),
    )(*args)


def _place():
    return lax.axis_index("x"), lax.axis_index("y"), lax.axis_index("c")


def _all_gather(name, shards):
    n = len(shards)

    def body(*refs):
        ins, outs = refs[:n], refs[n:2 * n]
        send_sems, recv_sems, local_sems = refs[2 * n:]
        x, y, c = _place()
        me, sibling = (x, y, c), (x, y, 1 - c)
        chips = [(1 - x, y), (x, 1 - y), (1 - x, 1 - y)]

        def slot(i, px, py, pc):
            return outs[i].at[4 * px + 2 * py + pc]

        def copy(i, k, block, to, src=None):
            return pltpu.make_async_remote_copy(
                src_ref=slot(i, *block) if src is None else src,
                dst_ref=slot(i, *block),
                send_sem=send_sems.at[i, k],
                recv_sem=recv_sems.at[i, k],
                device_id=to,
                device_id_type=MESH,
            )

        mine = [pltpu.make_async_copy(ins[i], slot(i, *me), local_sems.at[i]) for i in range(n)]
        for cp in mine:
            cp.start()
        first = []
        for i in range(n):
            first += [copy(i, 1 + j, me, (*chip, c), src=ins[i]) for j, chip in enumerate(chips)]
            first.append(copy(i, 0, me, sibling, src=ins[i]))
        for cp in first:
            cp.start()
        passed = []
        for i in range(n):
            for j, chip in enumerate(chips):
                copy(i, 1 + j, (*chip, c), me).wait_recv()
                fwd = copy(i, 4 + j, (*chip, c), sibling)
                fwd.start()
                passed.append(fwd)
        for i in range(n):
            copy(i, 0, sibling, me).wait_recv()
            for j, chip in enumerate(chips):
                copy(i, 4 + j, (*chip, 1 - c), me).wait_recv()
        for cp in first + passed:
            cp.wait_send()
        for cp in mine:
            cp.wait()

    hbm = pl.BlockSpec(memory_space=pltpu.HBM)
    outs = pl.pallas_call(
        body,
        name=name,
        in_specs=[hbm] * n,
        out_specs=[hbm] * n,
        out_shape=[jax.ShapeDtypeStruct((N_DEV,) + s.shape, s.dtype) for s in shards],
        scratch_shapes=[
            pltpu.SemaphoreType.DMA((n, 7)),
            pltpu.SemaphoreType.DMA((n, 7)),
            pltpu.SemaphoreType.DMA((n,)),
        ],
    )(*shards)
    return [o.reshape((N_DEV * s.shape[0],) + s.shape[1:]) for o, s in zip(outs, shards)]


def _exchange(name, parts, whole):
    n_p, n_w = len(parts), len(whole)
    n = n_p + n_w

    def body(*refs):
        ins, outs = refs[:n], refs[n:2 * n]
        send_sems, recv_sems, local_sems = refs[2 * n:]
        x, y, c = _place()
        me = 4 * x + 2 * y + c

        def src(i, place):
            return ins[i].at[place] if i < n_p else ins[i]

        mine = [pltpu.make_async_copy(src(i, me), outs[i].at[me], local_sems.at[i]) for i in range(n)]
        for cp in mine:
            cp.start()
        copies = []
        for i in range(n):
            for d in range(1, N_DEV):
                px = 1 - x if d & 4 else x
                py = 1 - y if d & 2 else y
                pc = 1 - c if d & 1 else c
                copies.append(
                    pltpu.make_async_remote_copy(
                        src_ref=src(i, 4 * px + 2 * py + pc),
                        dst_ref=outs[i].at[me],
                        send_sem=send_sems.at[i, d - 1],
                        recv_sem=recv_sems.at[i, d - 1],
                        device_id=(px, py, pc),
                        device_id_type=MESH,
                    )
                )
        for cp in copies:
            cp.start()
        for cp in copies:
            cp.wait()
        for cp in mine:
            cp.wait()

    hbm = pl.BlockSpec(memory_space=pltpu.HBM)
    shapes = [jax.ShapeDtypeStruct(a.shape, a.dtype) for a in parts]
    shapes += [jax.ShapeDtypeStruct((N_DEV,) + a.shape, a.dtype) for a in whole]
    return pl.pallas_call(
        body,
        name=name,
        in_specs=[hbm] * n,
        out_specs=[hbm] * n,
        out_shape=shapes,
        scratch_shapes=[
            pltpu.SemaphoreType.DMA((n, 7)),
            pltpu.SemaphoreType.DMA((n, 7)),
            pltpu.SemaphoreType.DMA((n,)),
        ],
    )(*parts, *whole)


def _adam_vals(w, g, m, v):
    m = ADAM_B1 * m + (1.0 - ADAM_B1) * g
    v = ADAM_B2 * v + (1.0 - ADAM_B2) * (g * g)
    m_hat = m / (1.0 - ADAM_B1 ** ADAM_STEP)
    v_hat = v / (1.0 - ADAM_B2 ** ADAM_STEP)
    delta = -ADAM_LR * (m_hat / (jnp.sqrt(v_hat) + ADAM_EPS) + ADAM_WD * w)
    return delta, m, v


def _sum_parts(st_ref):
    g = st_ref[0].astype(F32)
    for k in range(1, N_DEV):
        g = g + st_ref[k].astype(F32)
    return g


def _reduce_adam(name, st, w, m, v):
    rows, cols = w.shape
    tr = _tile(rows, 512, 16)

    def body(st_ref, w_ref, m_ref, v_ref, g_out, d_out, m_out, v_out):
        g = _sum_parts(st_ref)
        d, mm, vv = _adam_vals(w_ref[...], g, m_ref[...], v_ref[...])
        g_out[...] = g
        d_out[...] = d
        m_out[...] = mm
        v_out[...] = vv

    blk = pl.BlockSpec((tr, cols), lambda i: (i, 0))
    return pl.pallas_call(
        body,
        name=name,
        grid=(rows // tr,),
        in_specs=[pl.BlockSpec((N_DEV, tr, cols), lambda i: (0, i, 0)), blk, blk, blk],
        out_specs=[blk] * 4,
        out_shape=[jax.ShapeDtypeStruct(w.shape, F32)] * 4,
        compiler_params=_params(("arbitrary",)),
    )(st, w, m, v)


def _reduce_only(name, st):
    _, rows, cols = st.shape
    tr = _tile(rows, 512, 16)

    def body(st_ref, g_out):
        g_out[...] = _sum_parts(st_ref)

    return pl.pallas_call(
        body,
        name=name,
        grid=(rows // tr,),
        in_specs=[pl.BlockSpec((N_DEV, tr, cols), lambda i: (0, i, 0))],
        out_specs=pl.BlockSpec((tr, cols), lambda i: (i, 0)),
        out_shape=jax.ShapeDtypeStruct((rows, cols), F32),
        compiler_params=_params(("arbitrary",)),
    )(st)


def _adam_only(name, w, g, m, v):
    rows, cols = w.shape
    tr = _tile(rows, 512, 16)

    def body(w_ref, g_ref, m_ref, v_ref, d_out, m_out, v_out):
        d, mm, vv = _adam_vals(w_ref[...], g_ref[...], m_ref[...], v_ref[...])
        d_out[...] = d
        m_out[...] = mm
        v_out[...] = vv

    blk = pl.BlockSpec((tr, cols), lambda i: (i, 0))
    return pl.pallas_call(
        body,
        name=name,
        grid=(rows // tr,),
        in_specs=[blk] * 4,
        out_specs=[blk] * 3,
        out_shape=[jax.ShapeDtypeStruct(w.shape, F32)] * 3,
        compiler_params=_params(("arbitrary",)),
    )(w, g, m, v)


def _prenorm(name, h, g, t, d, tm):
    def fn(v):
        x, gg = v
        return [x * _rms_r(x) * gg]

    return _rowwise(name, [("row", h, d, 0), ("full", g)], [("row", (t, d), BF, d, 0)], fn, t=t, tm=tm)[0]


def _ffn_fwd(tag, h, xn, wgt, wut, wd, g_post, t, d, f, tm):
    tf = _tile(f, 1408)

    def up_epi(accs, ex):
        gg, uu = accs
        return [gg, uu, gg * _sig(gg) * uu]

    mats = [dict(a=xn, b=wgt, mode="nt", acc=0, tk=d), dict(a=xn, b=wut, mode="nt", acc=1, tk=d)]
    gate, up, hid = _matmul(
        tag + "_up", mats, m=t, n=f, tm=tm, tn=tf,
        outs=[("mn", (t, f), BF, 0)] * 3, epilogue=up_epi, n_acc=2, j_outer=True)

    def down_epi(accs, ex):
        ff = accs[0]
        hh, gg = ex
        return [ff, hh + 0.5 * ff * _rms_r(ff) * gg]

    fo, h_new = _matmul(
        tag + "_down", [dict(a=hid, b=wd, mode="nn", acc=0, tk=f)], m=t, n=d, tm=tm, tn=d,
        extras=[("mn", h, 0), ("n", g_post)],
        outs=[("mn", (t, d), F32, 0)] * 2, epilogue=down_epi, n_acc=1)
    return gate, up, hid, fo, h_new


def _postnorm_bwd(name, dh, fo, g, scale, t, d, tm):
    def fn(v):
        dy, ff, gg = v
        dx, dg = _rms_bwd(ff, gg, dy * scale)
        return [dx, dg]

    return _rowwise(
        name, [("row", dh, d, 0), ("row", fo, d, 0), ("full", g)],
        [("row", (t, d), BF, d, 0), ("acc", (1, d), F32)], fn, t=t, tm=tm)


def _prenorm_bwd_epi(accs, ex):
    hh, dh, gg = ex
    dx, dg = _rms_bwd(hh, gg, accs[0])
    return [dh + dx, dg]


def _ffn_bwd(tag, dh, h_in, xn, gate, up, hid, fo, wgt, wut, wd, g_pre, g_post, t, d, f, tm):
    tf = _tile(f, 1408)
    tk_t = _tile(t, 512)
    df, dg_post = _postnorm_bwd(tag + "_post_bwd", dh, fo, g_post, 0.5, t, d, tm)

    def hid_epi(accs, ex):
        dhid = accs[0]
        gg, uu = ex[0].astype(F32), ex[1].astype(F32)
        s = _sig(gg)
        return [dhid * uu * s * (1.0 + gg * (1.0 - s)), dhid * gg * s]

    dgate, dup = _matmul(
        tag + "_dhid", [dict(a=df, b=wd, mode="nt", acc=0, tk=d)], m=t, n=f, tm=tm, tn=tf,
        extras=[("mn", gate, 0), ("mn", up, 0)],
        outs=[("mn", (t, f), BF, 0)] * 2, epilogue=hid_epi, n_acc=1, j_outer=True)

    dwd = _matmul(
        tag + "_dwd", [dict(a=hid, b=df, mode="tn", acc=0, tk=tk_t)], m=f, n=d, tm=tf, tn=d, nk=t // tk_t,
        outs=[("mn", (f, d), BF, 0)], epilogue=lambda accs, ex: accs, n_acc=1)[0]
    dwgt, dwut = _matmul(
        tag + "_dwgu",
        [dict(a=dgate, b=xn, mode="tn", acc=0, tk=tk_t), dict(a=dup, b=xn, mode="tn", acc=1, tk=tk_t)],
        m=f, n=d, tm=tf, tn=d, nk=t // tk_t,
        outs=[("mn", (f, d), BF, 0)] * 2, epilogue=lambda accs, ex: accs, n_acc=2)

    nk = f // tf
    dh_in, dg_pre = _matmul(
        tag + "_dx",
        [dict(a=dgate, b=wgt, mode="nn", acc=0, tk=tf), dict(a=dup, b=wut, mode="nn", acc=0, tk=tf)],
        m=t, n=d, tm=tm, tn=d, nk=nk,
        extras=[("mn", h_in, 0), ("mn", dh, 0), ("n", g_pre)],
        outs=[("mn", (t, d), F32, 0), ("acc", (1, d), F32)], epilogue=_prenorm_bwd_epi, n_acc=1)
    return dh_in, dg_pre, dg_post, dwgt, dwut, dwd


def _causal_mask():
    r = lax.broadcasted_iota(jnp.int32, (CHUNK, CHUNK), 0)
    c = lax.broadcasted_iota(jnp.int32, (CHUNK, CHUNK), 1)
    return r >= c


def _layernorm_parts(v):
    mu = jnp.mean(v, axis=-1, keepdims=True)
    vc = v - mu
    rstd = lax.rsqrt(jnp.mean(vc * vc, axis=-1, keepdims=True) + EPS)
    return vc * rstd, rstd


def _sgu_fwd(z, g_sgu, w_s, b_col, t, d, tm):
    dg = d // N_SGU_GROUPS
    n_chunk = tm // CHUNK

    def body(zu_ref, zv_ref, g_ref, w_ref, b_ref, a_ref):
        u = _gelu(zu_ref[...])
        vhat, _ = _layernorm_parts(_gelu(zv_ref[...]))
        vn = (vhat * g_ref[...]).astype(BF)
        mask = _causal_mask()
        for gi in range(N_SGU_GROUPS):
            ws = jnp.where(mask, w_ref[gi], 0.0).astype(BF)
            bias = b_ref[gi]
            for ci in range(n_chunk):
                rows, cols = slice(ci * CHUNK, (ci + 1) * CHUNK), slice(gi * dg, (gi + 1) * dg)
                sv = jnp.dot(ws, vn[rows, cols], preferred_element_type=F32) + bias
                a_ref[rows, cols] = (u[rows, cols] * sv).astype(BF)

    return pl.pallas_call(
        body,
        name="sgu_fwd",
        grid=(t // tm,),
        in_specs=[
            pl.BlockSpec((tm, d), lambda i: (i, 0)),
            pl.BlockSpec((tm, d), lambda i: (i, 1)),
            pl.BlockSpec((1, d), lambda i: (0, 0)),
            pl.BlockSpec(w_s.shape, lambda i: (0, 0, 0)),
            pl.BlockSpec(b_col.shape, lambda i: (0, 0, 0)),
        ],
        out_specs=pl.BlockSpec((tm, d), lambda i: (i, 0)),
        out_shape=jax.ShapeDtypeStruct((t, d), BF),
        compiler_params=_params(("arbitrary",)),
    )(z, z, g_sgu, w_s, b_col)


def _sgu_bwd(z, da, dz, g_sgu, w_s, b_col, t, d, tm):
    dg = d // N_SGU_GROUPS
    n_chunk = tm // CHUNK

    def body(zu_ref, zv_ref, da_ref, dz_in, g_ref, w_ref, b_ref, dz_ref, dw_ref, db_ref, dgn_ref, dvn_ref):
        del dz_in
        dzu_ref, dzv_ref = dz_ref.at[:, pl.ds(0, d)], dz_ref.at[:, pl.ds(d, d)]
        i = pl.program_id(0)
        zu, zv = zu_ref[...], zv_ref[...]
        u = _gelu(zu)
        vhat, rstd = _layernorm_parts(_gelu(zv))
        gn = g_ref[...]
        vn = (vhat * gn).astype(BF)
        da_v = da_ref[...].astype(F32)
        dsv_all = (da_v * u).astype(BF)
        mask = _causal_mask()
        for gi in range(N_SGU_GROUPS):
            ws = jnp.where(mask, w_ref[gi], 0.0).astype(BF)
            bias = b_ref[gi]
            dw = jnp.zeros((CHUNK, CHUNK), F32)
            dbias = jnp.zeros((CHUNK, 1), F32)
            for ci in range(n_chunk):
                rows, cols = slice(ci * CHUNK, (ci + 1) * CHUNK), slice(gi * dg, (gi + 1) * dg)
                vn_c, dsv = vn[rows, cols], dsv_all[rows, cols]
                sv = jnp.dot(ws, vn_c, preferred_element_type=F32) + bias
                dzu_ref[rows, cols] = (da_v[rows, cols] * sv * _gelu_grad(zu[rows, cols])).astype(BF)
                dw = dw + lax.dot_general(dsv, vn_c, _DN["nt"], preferred_element_type=F32)
                dbias = dbias + jnp.sum(dsv.astype(F32), axis=1, keepdims=True)
                dvn_ref[rows, cols] = lax.dot_general(ws, dsv, _DN["tn"], preferred_element_type=F32)
            dw = jnp.where(mask, dw, 0.0)

            @pl.when(i == 0)
            def _():
                dw_ref[gi] = dw
                db_ref[gi] = dbias

            @pl.when(i != 0)
            def _():
                dw_ref[gi] += dw
                db_ref[gi] += dbias

        dvn = dvn_ref[...]
        dgn = jnp.sum(dvn * vhat, axis=0, keepdims=True)

        @pl.when(i == 0)
        def _():
            dgn_ref[...] = dgn

        @pl.when(i != 0)
        def _():
            dgn_ref[...] += dgn

        dvh = dvn * gn
        dv = rstd * (dvh - jnp.mean(dvh, axis=-1, keepdims=True) - vhat * jnp.mean(dvh * vhat, axis=-1, keepdims=True))
        dzv_ref[...] = (dv * _gelu_grad(zv)).astype(BF)

    return pl.pallas_call(
        body,
        name="sgu_bwd",
        grid=(t // tm,),
        in_specs=[
            pl.BlockSpec((tm, d), lambda i: (i, 0)),
            pl.BlockSpec((tm, d), lambda i: (i, 1)),
            pl.BlockSpec((tm, d), lambda i: (i, 0)),
            pl.BlockSpec(memory_space=pl.ANY),
            pl.BlockSpec((1, d), lambda i: (0, 0)),
            pl.BlockSpec(w_s.shape, lambda i: (0, 0, 0)),
            pl.BlockSpec(b_col.shape, lambda i: (0, 0, 0)),
        ],
        out_specs=[
            pl.BlockSpec((tm, 2 * d), lambda i: (i, 0)),
            pl.BlockSpec(w_s.shape, lambda i: (0, 0, 0)),
            pl.BlockSpec(b_col.shape, lambda i: (0, 0, 0)),
            pl.BlockSpec((1, d), lambda i: (0, 0)),
        ],
        out_shape=[
            jax.ShapeDtypeStruct(dz.shape, BF),
            jax.ShapeDtypeStruct(w_s.shape, F32),
            jax.ShapeDtypeStruct(b_col.shape, F32),
            jax.ShapeDtypeStruct((1, d), F32),
        ],
        scratch_shapes=[pltpu.VMEM((tm, d), F32)],
        input_output_aliases={3: 0},
        compiler_params=_params(("arbitrary",)),
    )(z, z, da, dz, g_sgu, w_s, b_col)


def _shift_down(x, k, row):
    return jnp.where(row >= k, pltpu.roll(x, k, 0), 0.0)


def _shift_up(x, k, row, t):
    return jnp.where(row < t - k, pltpu.roll(x, t - k, 0), 0.0)


def _doublings(window):
    steps = int(math.log2(window))
    assert 2 ** steps == window
    return [2 ** s for s in range(steps)]


def _pool_diff(c, window, row):
    s = c
    for k in _doublings(window):
        s = s + _shift_down(s, k, row)
    count = jnp.minimum(row + 1, window).astype(F32)
    return s / count - c, count


def _pool_fwd(z, pool_w, pool_scale, t, d):
    dgp = d // len(POOL_WINDOWS)
    cblk = (2 * d) // dgp

    def body(zc_ref, w_ref, s_ref, b_ref):
        row = lax.broadcasted_iota(jnp.int32, (t, 1), 0)
        for gi, window in enumerate(POOL_WINDOWS):
            @pl.when(pl.program_id(0) == gi)
            def _(window=window):
                diff, _ = _pool_diff(zc_ref[...], window, row)
                out = jnp.dot(diff.astype(BF), w_ref[...], preferred_element_type=F32)
                b_ref[...] = (out * s_ref[...]).astype(BF)

    return pl.pallas_call(
        body,
        name="pool_fwd",
        grid=(len(POOL_WINDOWS),),
        in_specs=[
            pl.BlockSpec((t, dgp), lambda g: (0, cblk + g)),
            pl.BlockSpec((None, dgp, dgp), lambda g: (g, 0, 0)),
            pl.BlockSpec((1, dgp), lambda g: (0, g)),
        ],
        out_specs=pl.BlockSpec((t, dgp), lambda g: (0, g)),
        out_shape=jax.ShapeDtypeStruct((t, d), BF),
        compiler_params=_params(("arbitrary",)),
    )(z, pool_w, pool_scale)


def _pool_bwd(z, db, dz, pool_w, pool_scale, t, d):
    dgp = d // len(POOL_WINDOWS)
    cblk = (2 * d) // dgp

    def body(zc_ref, db_ref, dz_in, w_ref, s_ref, dzc_ref, dw_ref, ds_ref):
        del dz_in
        row = lax.broadcasted_iota(jnp.int32, (t, 1), 0)
        for gi, window in enumerate(POOL_WINDOWS):
            @pl.when(pl.program_id(0) == gi)
            def _(window=window):
                diff, count = _pool_diff(zc_ref[...], window, row)
                diff = diff.astype(BF)
                w = w_ref[...]
                dbv = db_ref[...].astype(F32)
                out = jnp.dot(diff, w, preferred_element_type=F32)
                ds_ref[...] = jnp.sum(dbv * out, axis=0, keepdims=True)
                dout = (dbv * s_ref[...]).astype(BF)
                dw_ref[...] = lax.dot_general(diff, dout, _DN["tn"], preferred_element_type=F32).astype(BF)
                ddiff = lax.dot_general(dout, w, _DN["nt"], preferred_element_type=F32)
                s = ddiff / count
                for k in _doublings(window):
                    s = s + _shift_up(s, k, row, t)
                dzc_ref[...] = (s - ddiff).astype(BF)

    return pl.pallas_call(
        body,
        name="pool_bwd",
        grid=(len(POOL_WINDOWS),),
        in_specs=[
            pl.BlockSpec((t, dgp), lambda g: (0, cblk + g)),
            pl.BlockSpec((t, dgp), lambda g: (0, g)),
            pl.BlockSpec(memory_space=pl.ANY),
            pl.BlockSpec((None, dgp, dgp), lambda g: (g, 0, 0)),
            pl.BlockSpec((1, dgp), lambda g: (0, g)),
        ],
        out_specs=[
            pl.BlockSpec((t, dgp), lambda g: (0, cblk + g)),
            pl.BlockSpec((None, dgp, dgp), lambda g: (g, 0, 0)),
            pl.BlockSpec((1, dgp), lambda g: (0, g)),
        ],
        out_shape=[
            jax.ShapeDtypeStruct(dz.shape, BF),
            jax.ShapeDtypeStruct(pool_w.shape, BF),
            jax.ShapeDtypeStruct((1, d), F32),
        ],
        input_output_aliases={2: 0},
        compiler_params=_params(("arbitrary",)),
    )(z, db, dz, pool_w, pool_scale)


def _mix_dy(dm, w_o, z, ya, yb, t, d, tm):
    def body(dm_ref, w_ref, gate_ref, ya_ref, yb_ref, dyab_ref, dz_ref, dy_ref):
        j = pl.program_id(1)

        @pl.when(j == 0)
        def _():
            dy_ref[...] = lax.dot_general(dm_ref[...], w_ref[...], _DN["nt"], preferred_element_type=F32)

        dy = dy_ref[...]
        s = _sig(gate_ref[...])
        yv = jnp.where(j == 0, ya_ref[...], yb_ref[...]).astype(F32)
        dyab_ref[...] = (dy * s).astype(BF)
        dz_ref[...] = (dy * yv * s * (1.0 - s)).astype(BF)

    row = pl.BlockSpec((tm, d), lambda i, j: (i, 0))
    return pl.pallas_call(
        body,
        name="mix_dy",
        grid=(t // tm, 2),
        in_specs=[row, pl.BlockSpec((d, d), lambda i, j: (0, 0)), pl.BlockSpec((tm, d), lambda i, j: (i, 3 + j)), row, row],
        out_specs=[pl.BlockSpec((tm, d), lambda i, j: (i, j)), pl.BlockSpec((tm, d), lambda i, j: (i, 3 + j))],
        out_shape=[jax.ShapeDtypeStruct((t, 2 * d), BF), jax.ShapeDtypeStruct((t, 5 * d), BF)],
        scratch_shapes=[pltpu.VMEM((tm, d), F32)],
        compiler_params=_params(("arbitrary", "arbitrary")),
    )(dm, w_o, z, ya, yb)


def _pack_rows(flat_list, width):
    flat = jnp.concatenate([a.reshape(-1) for a in flat_list])
    rows = -(-flat.shape[0] // (8 * width)) * 8
    return jnp.pad(flat, (0, rows * width - flat.shape[0])).reshape(rows, width)


def _unpack_rows(packed, like):
    flat, out, pos = packed.reshape(-1), [], 0
    for a in like:
        out.append(flat[pos:pos + a.size].reshape(a.shape))
        pos += a.size
    return out


def kernel(x, p, ffn1_pre_g, ffn1_w_gate, ffn1_w_up, ffn1_w_down, ffn1_post_g, mix_pre_g, w_in, sgu_norm_g, sgu_w, sgu_b, pool_w, pool_scale, w_out_a, w_out_b, w_o, mix_post_g, ffn2_pre_g, ffn2_w_gate, ffn2_w_up, ffn2_w_down, ffn2_post_g, ple_pre_g, ple_w_gate, ple_w_proj, ple_post_g, loss_target, m_ffn1_pre_g, m_ffn1_w_gate, m_ffn1_w_up, m_ffn1_w_down, m_ffn1_post_g, m_mix_pre_g, m_w_in, m_sgu_norm_g, m_sgu_w, m_sgu_b, m_pool_w, m_pool_scale, m_w_out_a, m_w_out_b, m_w_o, m_mix_post_g, m_ffn2_pre_g, m_ffn2_w_gate, m_ffn2_w_up, m_ffn2_w_down, m_ffn2_post_g, m_ple_pre_g, m_ple_w_gate, m_ple_w_proj, m_ple_post_g, v_ffn1_pre_g, v_ffn1_w_gate, v_ffn1_w_up, v_ffn1_w_down, v_ffn1_post_g, v_mix_pre_g, v_w_in, v_sgu_norm_g, v_sgu_w, v_sgu_b, v_pool_w, v_pool_scale, v_w_out_a, v_w_out_b, v_w_o, v_mix_post_g, v_ffn2_pre_g, v_ffn2_w_gate, v_ffn2_w_up, v_ffn2_w_down, v_ffn2_post_g, v_ple_pre_g, v_ple_w_gate, v_ple_w_proj, v_ple_post_g):
    args = dict(locals())
    names = ["ffn1_pre_g", "ffn1_w_gate", "ffn1_w_up", "ffn1_w_down", "ffn1_post_g", "mix_pre_g", "w_in",
             "sgu_norm_g", "sgu_w", "sgu_b", "pool_w", "pool_scale", "w_out_a", "w_out_b", "w_o", "mix_post_g",
             "ffn2_pre_g", "ffn2_w_gate", "ffn2_w_up", "ffn2_w_down", "ffn2_post_g", "ple_pre_g", "ple_w_gate",
             "ple_w_proj", "ple_post_g"]
    w = {k: args[k][0] for k in names}
    mom = {k: args["m_" + k][0] for k in names}
    var = {k: args["v_" + k][0] for k in names}

    assert x.shape[0] == 1 and p.shape[:2] == (1, 1)
    t, d = x.shape[1], x.shape[2]
    f = ffn1_w_gate.shape[2] * N_DEV
    d_in = w_in.shape[2] * N_DEV
    d_ple = p.shape[3]
    n_pool = len(POOL_WINDOWS)
    dgp = d // n_pool
    assert d_in == 5 * d and t % CHUNK == 0
    tm = _tile(t, 512, CHUNK)
    xs, ps, target = x[0], p[0, 0], loss_target[0]

    col_sharded = ("ffn1_w_gate", "ffn1_w_up", "ffn2_w_gate", "ffn2_w_up", "w_in", "ple_w_proj")
    row_sharded = ("ffn1_w_down", "ffn2_w_down", "w_out_a", "w_out_b", "w_o", "ple_w_gate")
    order = ["ffn1_w_gate", "ffn1_w_up", "ffn1_w_down", "w_in", "pool_w", "w_out_a", "w_out_b", "w_o",
             "ffn2_w_gate", "ffn2_w_up", "ffn2_w_down", "ple_w_gate", "ple_w_proj"]
    shards = []
    for k in order:
        if k in col_sharded:
            shards.append(w[k].T.astype(BF))
        elif k == "pool_w":
            shards.append(w[k].reshape(-1, dgp).astype(BF))
        else:
            shards.append(w[k].astype(BF))
    full = dict(zip(order, _all_gather("gather_weights", shards)))
    rows_pw = dgp // N_DEV
    pool_full = full["pool_w"].reshape(N_DEV, n_pool, rows_pw, dgp).transpose(1, 0, 2, 3).reshape(n_pool, dgp, dgp)
    b_col = w["sgu_b"][:, :, None]
    gains = {k: w[k][None, :] for k in names if w[k].ndim == 1}

    xn1 = _prenorm("ffn1_prenorm", xs, gains["ffn1_pre_g"], t, d, tm)
    g1, u1, hid1, f1, h1 = _ffn_fwd("ffn1", xs, xn1, full["ffn1_w_gate"], full["ffn1_w_up"], full["ffn1_w_down"],
                                    gains["ffn1_post_g"], t, d, f, tm)

    xn2 = _prenorm("mix_prenorm", h1, gains["mix_pre_g"], t, d, tm)
    z = _matmul("mix_in", [dict(a=xn2, b=full["w_in"], mode="nt", acc=0, tk=d)], m=t, n=d_in, tm=tm, tn=d,
                outs=[("mn", (t, d_in), F32, 0)], epilogue=lambda accs, ex: accs, n_acc=1, j_outer=True)[0]
    a_br = _sgu_fwd(z, gains["sgu_norm_g"], w["sgu_w"], b_col, t, d, tm)
    b_br = _pool_fwd(z, pool_full, gains["pool_scale"], t, d)

    def merge_epi(accs, ex):
        ya, yb = accs
        ga, gb = ex
        return [ya, yb, _sig(ga) * ya + _sig(gb) * yb]

    ya, yb, y = _matmul(
        "mix_merge",
        [dict(a=a_br, b=full["w_out_a"], mode="nn", acc=0, tk=d), dict(a=b_br, b=full["w_out_b"], mode="nn", acc=1, tk=d)],
        m=t, n=d, tm=tm, tn=d, extras=[("mn", z, 3), ("mn", z, 4)],
        outs=[("mn", (t, d), BF, 0)] * 3, epilogue=merge_epi, n_acc=2)

    def proj_epi(accs, ex):
        mm = accs[0]
        hh, gg = ex
        return [mm, hh + mm * _rms_r(mm) * gg]

    m_out, h2 = _matmul(
        "mix_proj", [dict(a=y, b=full["w_o"], mode="nn", acc=0, tk=d)], m=t, n=d, tm=tm, tn=d,
        extras=[("mn", h1, 0), ("n", gains["mix_post_g"])],
        outs=[("mn", (t, d), F32, 0)] * 2, epilogue=proj_epi, n_acc=1)

    xn3 = _prenorm("ffn2_prenorm", h2, gains["ffn2_pre_g"], t, d, tm)
    g2, u2, hid2, f2, h3 = _ffn_fwd("ffn2", h2, xn3, full["ffn2_w_gate"], full["ffn2_w_up"], full["ffn2_w_down"],
                                    gains["ffn2_post_g"], t, d, f, tm)

    xn4 = _prenorm("ple_prenorm", h3, gains["ple_pre_g"], t, d, tm)

    def ple_epi(accs, ex):
        gl, e = accs
        hh, tgt, gg = ex
        q = _sig(gl) * e
        err = hh + q * _rms_r(q) * gg - tgt
        return [gl, e, err * (1.0 / d), jnp.sum(err * err, axis=0, keepdims=True)]

    gl, e_ple, dh4, loss_vec = _matmul(
        "ple_fwd",
        [dict(a=xn4, b=full["ple_w_gate"], mode="nn", acc=0, tk=d), dict(a=ps, b=full["ple_w_proj"], mode="nt", acc=1, tk=d_ple)],
        m=t, n=d, tm=tm, tn=d, extras=[("mn", h3, 0), ("mn", target, 0), ("n", gains["ple_post_g"])],
        outs=[("mn", (t, d), F32, 0)] * 3 + [("acc", (1, d), F32)], epilogue=ple_epi, n_acc=2)
    loss = lax.psum(jnp.sum(loss_vec) * (0.5 / d), AXES)

    tk_t = _tile(t, 512)

    def ple_post_fn(v):
        dy, gl_v, e_v, gg = v
        s = _sig(gl_v)
        dq, dg = _rms_bwd(s * e_v, gg, dy)
        return [dq * e_v * s * (1.0 - s), dq * s, dg]

    dgl, de, dg_ple_post = _rowwise(
        "ple_post_bwd", [("row", dh4, d, 0), ("row", gl, d, 0), ("row", e_ple, d, 0), ("full", gains["ple_post_g"])],
        [("row", (t, d), BF, d, 0), ("row", (t, d), BF, d, 0), ("acc", (1, d), F32)], ple_post_fn, t=t, tm=tm)
    ident = lambda accs, ex: accs
    dw_ple_gate = _matmul("ple_dwg", [dict(a=xn4, b=dgl, mode="tn", acc=0, tk=tk_t)], m=d, n=d, tm=d, tn=d,
                          nk=t // tk_t, outs=[("mn", (d, d), BF, 0)], epilogue=ident, n_acc=1)[0]
    dw_ple_proj_t = _matmul("ple_dwp", [dict(a=de, b=ps, mode="tn", acc=0, tk=tk_t)], m=d, n=d_ple, tm=d, tn=d_ple,
                            nk=t // tk_t, outs=[("mn", (d, d_ple), BF, 0)], epilogue=ident, n_acc=1)[0]
    dh3, dg_ple_pre = _matmul(
        "ple_dx", [dict(a=dgl, b=full["ple_w_gate"], mode="nt", acc=0, tk=d)], m=t, n=d, tm=tm, tn=d,
        extras=[("mn", h3, 0), ("mn", dh4, 0), ("n", gains["ple_pre_g"])],
        outs=[("mn", (t, d), F32, 0), ("acc", (1, d), F32)], epilogue=_prenorm_bwd_epi, n_acc=1)

    dh2, dg_f2_pre, dg_f2_post, dwg2, dwu2, dwd2 = _ffn_bwd(
        "ffn2", dh3, h2, xn3, g2, u2, hid2, f2, full["ffn2_w_gate"], full["ffn2_w_up"], full["ffn2_w_down"],
        gains["ffn2_pre_g"], gains["ffn2_post_g"], t, d, f, tm)

    dm, dg_mix_post = _postnorm_bwd("mix_post_bwd", dh2, m_out, gains["mix_post_g"], 1.0, t, d, tm)

    dyab, dz = _mix_dy(dm, full["w_o"], z, ya, yb, t, d, tm)
    dw_o = _matmul("mix_dwo", [dict(a=y, b=dm, mode="tn", acc=0, tk=tk_t)], m=d, n=d, tm=d, tn=d, nk=t // tk_t,
                   outs=[("mn", (d, d), BF, 0)], epilogue=ident, n_acc=1)[0]
    da, db = _matmul(
        "mix_dab",
        [dict(a=dyab, b=full["w_out_a"], mode="nt", acc=0, tk=d),
         dict(a=dyab, b=full["w_out_b"], mode="nt", acc=1, tk=d, a_off=(0, 1))],
        m=t, n=d, tm=tm, tn=d, outs=[("mn", (t, d), BF, 0)] * 2, epilogue=ident, n_acc=2)
    dw_out_a, dw_out_b = _matmul(
        "mix_dwab",
        [dict(a=a_br, b=dyab, mode="tn", acc=0, tk=tk_t),
         dict(a=b_br, b=dyab, mode="tn", acc=1, tk=tk_t, b_off=(0, 1))],
        m=d, n=d, tm=d, tn=d, nk=t // tk_t, outs=[("mn", (d, d), BF, 0)] * 2, epilogue=ident, n_acc=2)
    dz, dsgu_w, dsgu_b, dg_sgu = _sgu_bwd(z, da, dz, gains["sgu_norm_g"], w["sgu_w"], b_col, t, d, _tile(t, 256, CHUNK))
    dz, dpool_w, dg_pool = _pool_bwd(z, db, dz, pool_full, gains["pool_scale"], t, d)
    dw_in_t = _matmul("mix_dwin", [dict(a=dz, b=xn2, mode="tn", acc=0, tk=tk_t)], m=d_in, n=d, tm=d, tn=d,
                      nk=t // tk_t, outs=[("mn", (d_in, d), BF, 0)], epilogue=ident, n_acc=1)[0]
    dh1, dg_mix_pre = _matmul(
        "mix_dx", [dict(a=dz, b=full["w_in"], mode="nn", acc=0, tk=d)], m=t, n=d, tm=tm, tn=d, nk=d_in // d,
        extras=[("mn", h1, 0), ("mn", dh2, 0), ("n", gains["mix_pre_g"])],
        outs=[("mn", (t, d), F32, 0), ("acc", (1, d), F32)], epilogue=_prenorm_bwd_epi, n_acc=1)

    grad_x, dg_f1_pre, dg_f1_post, dwg1, dwu1, dwd1 = _ffn_bwd(
        "ffn1", dh1, xs, xn1, g1, u1, hid1, f1, full["ffn1_w_gate"], full["ffn1_w_up"], full["ffn1_w_down"],
        gains["ffn1_pre_g"], gains["ffn1_post_g"], t, d, f, tm)

    partial = {
        "ffn1_w_gate": dwg1, "ffn1_w_up": dwu1, "ffn1_w_down": dwd1, "ffn2_w_gate": dwg2, "ffn2_w_up": dwu2,
        "ffn2_w_down": dwd2, "w_in": dw_in_t, "w_out_a": dw_out_a, "w_out_b": dw_out_b, "w_o": dw_o,
        "ple_w_gate": dw_ple_gate, "ple_w_proj": dw_ple_proj_t,
        "pool_w": dpool_w.reshape(n_pool, N_DEV, rows_pw, dgp).transpose(1, 0, 2, 3).reshape(N_DEV * n_pool * rows_pw, dgp),
    }
    small_names = [k for k in names if k not in partial]
    small_grads = {
        "ffn1_pre_g": dg_f1_pre, "ffn1_post_g": dg_f1_post, "mix_pre_g": dg_mix_pre, "sgu_norm_g": dg_sgu,
        "sgu_w": dsgu_w, "sgu_b": dsgu_b, "pool_scale": dg_pool, "mix_post_g": dg_mix_post, "ffn2_pre_g": dg_f2_pre,
        "ffn2_post_g": dg_f2_post, "ple_pre_g": dg_ple_pre, "ple_post_g": dg_ple_post,
    }
    big_names = list(partial)
    parts = [partial[k].reshape(N_DEV, partial[k].shape[0] // N_DEV, partial[k].shape[1]) for k in big_names]
    small_packed = _pack_rows([small_grads[k] for k in small_names], d)
    staged = _exchange("exchange_grads", parts, [small_packed])
    staged_big = dict(zip(big_names, staged[:-1]))

    grad, delta, new_m, new_v = {}, {}, {}, {}
    for k in big_names:
        if k in col_sharded:
            grad[k] = _reduce_only("sum_" + k, staged_big[k]).T
            delta[k], new_m[k], new_v[k] = _adam_only("adam_" + k, w[k], grad[k], mom[k], var[k])
        else:
            shape = w[k].shape
            flat = (-1, shape[-1])
            res = _reduce_adam("adam_" + k, staged_big[k], w[k].reshape(flat), mom[k].reshape(flat), var[k].reshape(flat))
            grad[k], delta[k], new_m[k], new_v[k] = [r.reshape(shape) for r in res]
    small_like = [w[k] for k in small_names]
    res = _reduce_adam(
        "adam_small", staged[-1], _pack_rows(small_like, d), _pack_rows([mom[k] for k in small_names], d),
        _pack_rows([var[k] for k in small_names], d))
    for dst, packed in zip((grad, delta, new_m, new_v), res):
        dst.update(zip(small_names, _unpack_rows(packed, small_like)))

    out = [loss, grad_x[None]]
    for group in (grad, delta, new_m, new_v):
        out += [group[k][None] for k in names]
    return tuple(out)
```

```python
import functools
import math

import jax
import jax.numpy as jnp
from jax import lax
from jax.experimental import pallas as pl
from jax.experimental.pallas import tpu as pltpu

EPS = 1e-6
CHUNK = 128
N_SGU_GROUPS = 4
POOL_WINDOWS = (2, 4, 8, 16)
ADAM_LR = 0.001
ADAM_B1 = 0.9
ADAM_B2 = 0.999
ADAM_EPS = 1e-08
ADAM_WD = 0.01
ADAM_STEP = 10

N_DEV = 8
AXES = ("x", "y", "c")
MESH = pl.DeviceIdType.MESH
V7X_VMEM_BYTES = 64 * 1024 * 1024
VMEM_LIMIT = V7X_VMEM_BYTES - 8 * 1024 * 1024
BF = jnp.bfloat16
F32 = jnp.float32

_DN = {
    "nn": (((1,), (0,)), ((), ())),
    "nt": (((1,), (1,)), ((), ())),
    "tn": (((0,), (0,)), ((), ())),
}


def _params(sem=None):
    return pltpu.CompilerParams(dimension_semantics=sem, vmem_limit_bytes=VMEM_LIMIT)


def _tile(n, target, align=128):
    t = min(n, target)
    t -= t % align
    while t >= align:
        if n % t == 0:
            return t
        t -= align
    return n


def _sig(x):
    return 1.0 / (1.0 + jnp.exp(-x))


_GELU_K = math.sqrt(2.0 / math.pi)
_GELU_C = 0.044715


def _gelu(x):
    return 0.5 * x * (1.0 + jnp.tanh(_GELU_K * (x + _GELU_C * x * x * x)))


def _gelu_grad(x):
    t = jnp.tanh(_GELU_K * (x + _GELU_C * x * x * x))
    return 0.5 * (1.0 + t) + 0.5 * x * (1.0 - t * t) * _GELU_K * (1.0 + 3.0 * _GELU_C * x * x)


def _rms_r(x):
    return lax.rsqrt(jnp.mean(x * x, axis=-1, keepdims=True) + EPS)


def _rms_bwd(x, g, dy):
    r = _rms_r(x)
    xh = x * r
    gy = dy * g
    dx = r * (gy - xh * jnp.mean(xh * gy, axis=-1, keepdims=True))
    return dx, jnp.sum(dy * xh, axis=0, keepdims=True)


def _matmul(name, mats, *, m, n, tm, tn, nk=1, extras=(), outs, epilogue, n_acc, j_outer=False, deps=()):
    ni, nj = m // tm, n // tn
    assert ni * tm == m and nj * tn == n, (name, m, n, tm, tn)
    if j_outer:
        grid = (nj, ni, nk)

        def ij(g0, g1):
            return g1, g0
    else:
        grid = (ni, nj, nk)

        def ij(g0, g1):
            return g0, g1

    in_specs, args = [], []
    for mt in mats:
        mode, tk = mt["mode"], mt["tk"]
        ao, bo = mt.get("a_off", (0, 0)), mt.get("b_off", (0, 0))
        if mode == "tn":
            sa = pl.BlockSpec((tk, tm), lambda g0, g1, kk, ao=ao: (ao[0] + kk, ao[1] + ij(g0, g1)[0]))
        else:
            sa = pl.BlockSpec((tm, tk), lambda g0, g1, kk, ao=ao: (ao[0] + ij(g0, g1)[0], ao[1] + kk))
        if mode == "nt":
            sb = pl.BlockSpec((tn, tk), lambda g0, g1, kk, bo=bo: (bo[0] + ij(g0, g1)[1], bo[1] + kk))
        else:
            sb = pl.BlockSpec((tk, tn), lambda g0, g1, kk, bo=bo: (bo[0] + kk, bo[1] + ij(g0, g1)[1]))
        in_specs += [sa, sb]
        args += [mt["a"], mt["b"]]
    n_mat_refs = len(args)
    for ex in extras:
        if ex[0] == "mn":
            in_specs.append(pl.BlockSpec((tm, tn), lambda g0, g1, kk, c=ex[2]: (ij(g0, g1)[0], c + ij(g0, g1)[1])))
        else:
            in_specs.append(pl.BlockSpec((1, tn), lambda g0, g1, kk: (0, ij(g0, g1)[1])))
        args.append(ex[1])
    n_ex_end = len(args)
    in_specs += [pl.BlockSpec(memory_space=pl.ANY)] * len(deps)
    args += list(deps)
    n_in = len(args)
    out_specs, out_shape = [], []
    for o in outs:
        if o[0] == "mn":
            out_specs.append(pl.BlockSpec((tm, tn), lambda g0, g1, kk, c=o[3]: (ij(g0, g1)[0], c + ij(g0, g1)[1])))
        else:
            assert nj == 1, name
            out_specs.append(pl.BlockSpec((1, tn), lambda g0, g1, kk: (0, 0)))
        out_shape.append(jax.ShapeDtypeStruct(o[1], o[2]))
    n_out = len(outs)

    def body(*refs):
        mat_refs = refs[:n_mat_refs]
        ex_refs = refs[n_mat_refs:n_ex_end]
        out_refs = refs[n_in:n_in + n_out]
        acc_refs = refs[n_in + n_out:]
        i = ij(pl.program_id(0), pl.program_id(1))[0]
        kk = pl.program_id(2)

        def products():
            res = [None] * n_acc
            for idx, mt in enumerate(mats):
                a = mat_refs[2 * idx][...].astype(BF)
                b = mat_refs[2 * idx + 1][...].astype(BF)
                p = lax.dot_general(a, b, _DN[mt["mode"]], preferred_element_type=F32)
                q = mt["acc"]
                res[q] = p if res[q] is None else res[q] + p
            return res

        def finish(accs):
            vals = epilogue(accs, [r[...] for r in ex_refs])
            for o, r, v in zip(outs, out_refs, vals):
                if o[0] == "mn":
                    r[...] = v.astype(o[2])
                else:
                    @pl.when(i == 0)
                    def _():
                        r[...] = v

                    @pl.when(i != 0)
                    def _():
                        r[...] += v

        if nk == 1:
            finish(products())
        else:
            res = products()

            @pl.when(kk == 0)
            def _():
                for q in range(n_acc):
                    acc_refs[q][...] = res[q]

            @pl.when(kk != 0)
            def _():
                for q in range(n_acc):
                    acc_refs[q][...] += res[q]

            @pl.when(kk == nk - 1)
            def _():
                finish([r[...] for r in acc_refs])

    scratch = [pltpu.VMEM((tm, tn), F32) for _ in range(n_acc)] if nk > 1 else []
    return pl.pallas_call(
        body,
        name=name,
        grid=grid,
        in_specs=in_specs,
        out_specs=out_specs,
        out_shape=out_shape,
        scratch_shapes=scratch,
        compiler_params=_params(("arbitrary", "arbitrary", "arbitrary")),
    )(*args)


def _rowwise(name, ins, outs, fn, *, t, tm, deps=()):
    ni = t // tm
    assert ni * tm == t, (name, t, tm)
    in_specs, args = [], []
    for s in ins:
        if s[0] == "row":
            in_specs.append(pl.BlockSpec((tm, s[2]), lambda i, c=s[3]: (i, c)))
        else:
            nd = s[1].ndim
            in_specs.append(pl.BlockSpec(s[1].shape, lambda i, nd=nd: (0,) * nd))
        args.append(s[1])
    out_specs, out_shape = [], []
    for o in outs:
        if o[0] == "row":
            out_specs.append(pl.BlockSpec((tm, o[3]), lambda i, c=o[4]: (i, c)))
        else:
            nd = len(o[1])
            out_specs.append(pl.BlockSpec(o[1], lambda i, nd=nd: (0,) * nd))
        out_shape.append(jax.ShapeDtypeStruct(o[1], o[2]))
    n_read = len(args)
    in_specs += [pl.BlockSpec(memory_space=pl.ANY)] * len(deps)
    args += list(deps)
    n_in = len(args)

    def body(*refs):
        i = pl.program_id(0)
        vals = fn([r[...] for r in refs[:n_read]])
        for o, r, v in zip(outs, refs[n_in:], vals):
            if o[0] == "row":
                r[...] = v.astype(o[2])
            else:
                @pl.when(i == 0)
                def _():
                    r[...] = v

                @pl.when(i != 0)
                def _():
                    r[...] += v

    return pl.pallas_call(
        body,
        name=name,
        grid=(ni,),
        in_specs=in_specs,
        out_specs=out_specs,
        out_shape=out_shape,
        compiler_params=_params(("arbitrary",)),
    )(*args)


def _place():
    return lax.axis_index("x"), lax.axis_index("y"), lax.axis_index("c")


_HBM = pl.BlockSpec(memory_space=pltpu.HBM)
_SEM = pl.BlockSpec(memory_space=pltpu.SEMAPHORE)
_ANY = pl.BlockSpec(memory_space=pl.ANY)
_EFFECT = pltpu.SideEffectType.DATAFLOW_SIDE_EFFECTING


def _remote(src, dst, send_sems, recv_sems, k, to):
    return pltpu.make_async_remote_copy(
        src_ref=src, dst_ref=dst, send_sem=send_sems.at[k], recv_sem=recv_sems.at[k],
        device_id=to, device_id_type=MESH)


def _copy_start(name, bufs, plan, n_copies, local=None, deps=()):
    nb, nd = len(bufs), len(deps)
    n_local = 0 if local is None else nb // 2

    def body(*refs):
        buf_refs = refs[:nb]
        send_sems, recv_sems = refs[nb + nd], refs[nb + nd + 1]
        token = refs[nb + nd + 2 + nb]
        for cp in plan(buf_refs, send_sems, recv_sems):
            cp.start()
        if local is not None:
            local_sems = refs[nb + nd + 3 + nb]
            mine = [pltpu.make_async_copy(a, b, local_sems.at[k]) for k, (a, b) in enumerate(local(buf_refs))]
            for cp in mine:
                cp.start()
            for cp in mine:
                cp.wait()
        token[...] = jnp.zeros_like(token)

    res = pl.pallas_call(
        body,
        name=name,
        in_specs=[_HBM] * nb + [_ANY] * nd,
        out_specs=[_SEM, _SEM] + [_HBM] * nb + [pl.BlockSpec(memory_space=pltpu.VMEM)],
        out_shape=[pltpu.SemaphoreType.DMA((n_copies,)), pltpu.SemaphoreType.DMA((n_copies,))]
        + [pltpu.HBM(b.shape, b.dtype) for b in bufs] + [jax.ShapeDtypeStruct((8, 128), F32)],
        scratch_shapes=[pltpu.SemaphoreType.DMA((n_local,))] if n_local else [],
        input_output_aliases={i: 2 + i for i in range(nb)},
        compiler_params=pltpu.CompilerParams(has_side_effects=_EFFECT),
    )(*[pltpu.with_memory_space_constraint(b, pltpu.HBM) for b in bufs], *deps)
    return res[0], res[1], list(res[2:2 + nb]), res[2 + nb]


def _copy_wait(name, send_sems, recv_sems, bufs, plan, after):
    nb = len(bufs)

    def body(*refs):
        for cp in plan(refs[:nb], refs[nb], refs[nb + 1]):
            cp.wait_send()
            cp.wait_recv()

    return list(pl.pallas_call(
        body,
        name=name,
        in_specs=[_HBM] * nb + [_SEM, _SEM, _ANY],
        out_specs=[_HBM] * nb,
        out_shape=[pltpu.HBM(b.shape, b.dtype) for b in bufs],
        input_output_aliases={i: i for i in range(nb)},
        compiler_params=pltpu.CompilerParams(has_side_effects=_EFFECT),
    )(*bufs, send_sems, recv_sems, after))


def _chips(x, y):
    return [(1 - x, y), (x, 1 - y), (1 - x, 1 - y)]


def _gather_plan_ici(n):
    def plan(refs, send_sems, recv_sems):
        x, y, c = _place()
        me = 4 * x + 2 * y + c
        targets = [(x, y, 1 - c)] + [(*chip, c) for chip in _chips(x, y)]
        return [_remote(refs[i], refs[n + i].at[me], send_sems, recv_sems, 4 * i + k, to)
                for i in range(n) for k, to in enumerate(targets)]

    def local(refs):
        x, y, c = _place()
        return [(refs[i], refs[n + i].at[4 * x + 2 * y + c]) for i in range(n)]

    return plan, local


def _gather_plan_d2d(n):
    def plan(refs, send_sems, recv_sems):
        x, y, c = _place()
        cps = []
        for i in range(n):
            for j, (cx, cy) in enumerate(_chips(x, y)):
                block = refs[i].at[4 * cx + 2 * cy + c]
                cps.append(_remote(block, block, send_sems, recv_sems, 3 * i + j, (x, y, 1 - c)))
        return cps

    return plan


def _scatter_plan_d2d(n):
    def plan(refs, send_sems, recv_sems):
        x, y, c = _place()
        return [_remote(refs[i].at[q, 1 - c], refs[n + i].at[q], send_sems, recv_sems, 4 * i + q, (x, y, 1 - c))
                for i in range(n) for q in range(4)]

    return plan


def _scatter_plan_ici(n):
    def plan(refs, send_sems, recv_sems):
        x, y, c = _place()
        return [_remote(refs[i].at[2 * cx + cy], refs[n + i].at[2 * x + y], send_sems, recv_sems, 3 * i + j, (cx, cy, c))
                for i in range(n) for j, (cx, cy) in enumerate(_chips(x, y))]

    def local(refs):
        x, y, _ = _place()
        return [(refs[i].at[2 * x + y], refs[n + i].at[2 * x + y]) for i in range(n)]

    return plan, local


def _pair_add(name, core, parts, landed):
    n = len(parts)

    def body(core_ref, *refs):
        del core_ref
        for i in range(n):
            refs[2 * n + i][...] = (refs[i][...].astype(F32) + refs[n + i][...].astype(F32)).astype(BF)

    in_specs = [pl.BlockSpec((None, None) + p.shape[2:], lambda q, core_ref: (q, core_ref[0], 0, 0)) for p in parts]
    in_specs += [pl.BlockSpec((None,) + a.shape[1:], lambda q, core_ref: (q, 0, 0)) for a in landed]
    return list(pl.pallas_call(
        body,
        name=name,
        grid_spec=pltpu.PrefetchScalarGridSpec(
            num_scalar_prefetch=1, grid=(4,), in_specs=in_specs,
            out_specs=[pl.BlockSpec((None,) + a.shape[1:], lambda q, core_ref: (q, 0, 0)) for a in landed]),
        out_shape=[jax.ShapeDtypeStruct(a.shape, BF) for a in landed],
        compiler_params=_params(("arbitrary",)),
    )(core, *parts, *landed))


def _exchange(name, parts, whole):
    n_p, n_w = len(parts), len(whole)
    n = n_p + n_w

    def body(*refs):
        ins, outs = refs[:n], refs[n:2 * n]
        send_sems, recv_sems, local_sems = refs[2 * n:]
        x, y, c = _place()
        me = 4 * x + 2 * y + c

        def src(i, place):
            return ins[i].at[place] if i < n_p else ins[i]

        mine = [pltpu.make_async_copy(src(i, me), outs[i].at[me], local_sems.at[i]) for i in range(n)]
        for cp in mine:
            cp.start()
        copies = []
        for i in range(n):
            for d in range(1, N_DEV):
                px = 1 - x if d & 4 else x
                py = 1 - y if d & 2 else y
                pc = 1 - c if d & 1 else c
                copies.append(
                    pltpu.make_async_remote_copy(
                        src_ref=src(i, 4 * px + 2 * py + pc),
                        dst_ref=outs[i].at[me],
                        send_sem=send_sems.at[i, d - 1],
                        recv_sem=recv_sems.at[i, d - 1],
                        device_id=(px, py, pc),
                        device_id_type=MESH,
                    )
                )
        for cp in copies:
            cp.start()
        for cp in copies:
            cp.wait()
        for cp in mine:
            cp.wait()

    hbm = pl.BlockSpec(memory_space=pltpu.HBM)
    shapes = [jax.ShapeDtypeStruct(a.shape, a.dtype) for a in parts]
    shapes += [jax.ShapeDtypeStruct((N_DEV,) + a.shape, a.dtype) for a in whole]
    return pl.pallas_call(
        body,
        name=name,
        in_specs=[hbm] * n,
        out_specs=[hbm] * n,
        out_shape=shapes,
        scratch_shapes=[
            pltpu.SemaphoreType.DMA((n, 7)),
            pltpu.SemaphoreType.DMA((n, 7)),
            pltpu.SemaphoreType.DMA((n,)),
        ],
    )(*parts, *whole)


def _adam_vals(w, g, m, v):
    m = ADAM_B1 * m + (1.0 - ADAM_B1) * g
    v = ADAM_B2 * v + (1.0 - ADAM_B2) * (g * g)
    m_hat = m / (1.0 - ADAM_B1 ** ADAM_STEP)
    v_hat = v / (1.0 - ADAM_B2 ** ADAM_STEP)
    delta = -ADAM_LR * (m_hat / (jnp.sqrt(v_hat) + ADAM_EPS) + ADAM_WD * w)
    return delta, m, v


def _sum_parts(st_ref):
    g = st_ref[0].astype(F32)
    for k in range(1, st_ref.shape[0]):
        g = g + st_ref[k].astype(F32)
    return g


def _reduce_adam(name, st, w, m, v):
    rows, cols = w.shape
    tr = _tile(rows, 512, 16)

    def body(st_ref, w_ref, m_ref, v_ref, g_out, d_out, m_out, v_out):
        g = _sum_parts(st_ref)
        d, mm, vv = _adam_vals(w_ref[...], g, m_ref[...], v_ref[...])
        g_out[...] = g
        d_out[...] = d
        m_out[...] = mm
        v_out[...] = vv

    blk = pl.BlockSpec((tr, cols), lambda i: (i, 0))
    return pl.pallas_call(
        body,
        name=name,
        grid=(rows // tr,),
        in_specs=[pl.BlockSpec((st.shape[0], tr, cols), lambda i: (0, i, 0)), blk, blk, blk],
        out_specs=[blk] * 4,
        out_shape=[jax.ShapeDtypeStruct(w.shape, F32)] * 4,
        compiler_params=_params(("arbitrary",)),
    )(st, w, m, v)


def _reduce_only(name, st):
    _, rows, cols = st.shape
    tr = _tile(rows, 512, 16)

    def body(st_ref, g_out):
        g_out[...] = _sum_parts(st_ref)

    return pl.pallas_call(
        body,
        name=name,
        grid=(rows // tr,),
        in_specs=[pl.BlockSpec((st.shape[0], tr, cols), lambda i: (0, i, 0))],
        out_specs=pl.BlockSpec((tr, cols), lambda i: (i, 0)),
        out_shape=jax.ShapeDtypeStruct((rows, cols), F32),
        compiler_params=_params(("arbitrary",)),
    )(st)


def _adam_only(name, w, g, m, v):
    rows, cols = w.shape
    tr = _tile(rows, 512, 16)

    def body(w_ref, g_ref, m_ref, v_ref, d_out, m_out, v_out):
        d, mm, vv = _adam_vals(w_ref[...], g_ref[...], m_ref[...], v_ref[...])
        d_out[...] = d
        m_out[...] = mm
        v_out[...] = vv

    blk = pl.BlockSpec((tr, cols), lambda i: (i, 0))
    return pl.pallas_call(
        body,
        name=name,
        grid=(rows // tr,),
        in_specs=[blk] * 4,
        out_specs=[blk] * 3,
        out_shape=[jax.ShapeDtypeStruct(w.shape, F32)] * 3,
        compiler_params=_params(("arbitrary",)),
    )(w, g, m, v)


def _prenorm(name, h, g, t, d, tm, deps=()):
    def fn(v):
        x, gg = v
        return [x * _rms_r(x) * gg]

    return _rowwise(name, [("row", h, d, 0), ("full", g)], [("row", (t, d), BF, d, 0)], fn, t=t, tm=tm, deps=deps)[0]


def _ffn_fwd(tag, h, xn, wgt, wut, wd, g_post, t, d, f, tm):
    tf = _tile(f, 1408)

    def up_epi(accs, ex):
        gg, uu = accs
        return [gg, uu, gg * _sig(gg) * uu]

    mats = [dict(a=xn, b=wgt, mode="nt", acc=0, tk=d), dict(a=xn, b=wut, mode="nt", acc=1, tk=d)]
    gate, up, hid = _matmul(
        tag + "_up", mats, m=t, n=f, tm=tm, tn=tf,
        outs=[("mn", (t, f), BF, 0)] * 3, epilogue=up_epi, n_acc=2, j_outer=True)

    def down_epi(accs, ex):
        ff = accs[0]
        hh, gg = ex
        return [ff, hh + 0.5 * ff * _rms_r(ff) * gg]

    fo, h_new = _matmul(
        tag + "_down", [dict(a=hid, b=wd, mode="nn", acc=0, tk=f)], m=t, n=d, tm=tm, tn=d,
        extras=[("mn", h, 0), ("n", g_post)],
        outs=[("mn", (t, d), F32, 0)] * 2, epilogue=down_epi, n_acc=1)
    return gate, up, hid, fo, h_new


def _postnorm_bwd(name, dh, fo, g, scale, t, d, tm, deps=()):
    def fn(v):
        dy, ff, gg = v
        dx, dg = _rms_bwd(ff, gg, dy * scale)
        return [dx, dg]

    return _rowwise(
        name, [("row", dh, d, 0), ("row", fo, d, 0), ("full", g)],
        [("row", (t, d), BF, d, 0), ("acc", (1, d), F32)], fn, t=t, tm=tm, deps=deps)


def _prenorm_bwd_epi(accs, ex):
    hh, dh, gg = ex
    dx, dg = _rms_bwd(hh, gg, accs[0])
    return [dh + dx, dg]


def _ffn_bwd_weights(tag, dh, xn, gate, up, hid, fo, wd, g_post, t, d, f, tm, deps=()):
    tf = _tile(f, 1408)
    tk_t = _tile(t, 512)
    df, dg_post = _postnorm_bwd(tag + "_post_bwd", dh, fo, g_post, 0.5, t, d, tm, deps=deps)

    def hid_epi(accs, ex):
        dhid = accs[0]
        gg, uu = ex[0].astype(F32), ex[1].astype(F32)
        s = _sig(gg)
        return [dhid * uu * s * (1.0 + gg * (1.0 - s)), dhid * gg * s]

    dgate, dup = _matmul(
        tag + "_dhid", [dict(a=df, b=wd, mode="nt", acc=0, tk=d)], m=t, n=f, tm=tm, tn=tf,
        extras=[("mn", gate, 0), ("mn", up, 0)],
        outs=[("mn", (t, f), BF, 0)] * 2, epilogue=hid_epi, n_acc=1, j_outer=True)

    dwd = _matmul(
        tag + "_dwd", [dict(a=hid, b=df, mode="tn", acc=0, tk=tk_t)], m=f, n=d, tm=tf, tn=d, nk=t // tk_t,
        outs=[("mn", (f, d), BF, 0)], epilogue=lambda accs, ex: accs, n_acc=1)[0]
    dwgt, dwut = _matmul(
        tag + "_dwgu",
        [dict(a=dgate, b=xn, mode="tn", acc=0, tk=tk_t), dict(a=dup, b=xn, mode="tn", acc=1, tk=tk_t)],
        m=f, n=d, tm=tf, tn=d, nk=t // tk_t,
        outs=[("mn", (f, d), BF, 0)] * 2, epilogue=lambda accs, ex: accs, n_acc=2)
    return dgate, dup, dg_post, dwgt, dwut, dwd


def _ffn_bwd_dx(tag, dh, h_in, dgate, dup, wgt, wut, g_pre, t, d, f, tm, deps=()):
    tf = _tile(f, 1408)
    return _matmul(
        tag + "_dx",
        [dict(a=dgate, b=wgt, mode="nn", acc=0, tk=tf), dict(a=dup, b=wut, mode="nn", acc=0, tk=tf)],
        m=t, n=d, tm=tm, tn=d, nk=f // tf,
        extras=[("mn", h_in, 0), ("mn", dh, 0), ("n", g_pre)],
        outs=[("mn", (t, d), F32, 0), ("acc", (1, d), F32)], epilogue=_prenorm_bwd_epi, n_acc=1, deps=deps)


def _causal_mask():
    r = lax.broadcasted_iota(jnp.int32, (CHUNK, CHUNK), 0)
    c = lax.broadcasted_iota(jnp.int32, (CHUNK, CHUNK), 1)
    return r >= c


def _layernorm_parts(v):
    mu = jnp.mean(v, axis=-1, keepdims=True)
    vc = v - mu
    rstd = lax.rsqrt(jnp.mean(vc * vc, axis=-1, keepdims=True) + EPS)
    return vc * rstd, rstd


def _sgu_fwd(z, g_sgu, w_s, b_col, t, d, tm):
    dg = d // N_SGU_GROUPS
    n_chunk = tm // CHUNK

    def body(zu_ref, zv_ref, g_ref, w_ref, b_ref, a_ref):
        u = _gelu(zu_ref[...])
        vhat, _ = _layernorm_parts(_gelu(zv_ref[...]))
        vn = (vhat * g_ref[...]).astype(BF)
        mask = _causal_mask()
        for gi in range(N_SGU_GROUPS):
            ws = jnp.where(mask, w_ref[gi], 0.0).astype(BF)
            bias = b_ref[gi]
            for ci in range(n_chunk):
                rows, cols = slice(ci * CHUNK, (ci + 1) * CHUNK), slice(gi * dg, (gi + 1) * dg)
                sv = jnp.dot(ws, vn[rows, cols], preferred_element_type=F32) + bias
                a_ref[rows, cols] = (u[rows, cols] * sv).astype(BF)

    return pl.pallas_call(
        body,
        name="sgu_fwd",
        grid=(t // tm,),
        in_specs=[
            pl.BlockSpec((tm, d), lambda i: (i, 0)),
            pl.BlockSpec((tm, d), lambda i: (i, 1)),
            pl.BlockSpec((1, d), lambda i: (0, 0)),
            pl.BlockSpec(w_s.shape, lambda i: (0, 0, 0)),
            pl.BlockSpec(b_col.shape, lambda i: (0, 0, 0)),
        ],
        out_specs=pl.BlockSpec((tm, d), lambda i: (i, 0)),
        out_shape=jax.ShapeDtypeStruct((t, d), BF),
        compiler_params=_params(("arbitrary",)),
    )(z, z, g_sgu, w_s, b_col)


def _sgu_bwd(z, da, dz, g_sgu, w_s, b_col, t, d, tm):
    dg = d // N_SGU_GROUPS
    n_chunk = tm // CHUNK

    def body(zu_ref, zv_ref, da_ref, dz_in, g_ref, w_ref, b_ref, dz_ref, dw_ref, db_ref, dgn_ref, dvn_ref):
        del dz_in
        dzu_ref, dzv_ref = dz_ref.at[:, pl.ds(0, d)], dz_ref.at[:, pl.ds(d, d)]
        i = pl.program_id(0)
        zu, zv = zu_ref[...], zv_ref[...]
        u = _gelu(zu)
        vhat, rstd = _layernorm_parts(_gelu(zv))
        gn = g_ref[...]
        vn = (vhat * gn).astype(BF)
        da_v = da_ref[...].astype(F32)
        dsv_all = (da_v * u).astype(BF)
        mask = _causal_mask()
        for gi in range(N_SGU_GROUPS):
            ws = jnp.where(mask, w_ref[gi], 0.0).astype(BF)
            bias = b_ref[gi]
            dw = jnp.zeros((CHUNK, CHUNK), F32)
            dbias = jnp.zeros((CHUNK, 1), F32)
            for ci in range(n_chunk):
                rows, cols = slice(ci * CHUNK, (ci + 1) * CHUNK), slice(gi * dg, (gi + 1) * dg)
                vn_c, dsv = vn[rows, cols], dsv_all[rows, cols]
                sv = jnp.dot(ws, vn_c, preferred_element_type=F32) + bias
                dzu_ref[rows, cols] = (da_v[rows, cols] * sv * _gelu_grad(zu[rows, cols])).astype(BF)
                dw = dw + lax.dot_general(dsv, vn_c, _DN["nt"], preferred_element_type=F32)
                dbias = dbias + jnp.sum(dsv.astype(F32), axis=1, keepdims=True)
                dvn_ref[rows, cols] = lax.dot_general(ws, dsv, _DN["tn"], preferred_element_type=F32)
            dw = jnp.where(mask, dw, 0.0)

            @pl.when(i == 0)
            def _():
                dw_ref[gi] = dw
                db_ref[gi] = dbias

            @pl.when(i != 0)
            def _():
                dw_ref[gi] += dw
                db_ref[gi] += dbias

        dvn = dvn_ref[...]
        dgn = jnp.sum(dvn * vhat, axis=0, keepdims=True)

        @pl.when(i == 0)
        def _():
            dgn_ref[...] = dgn

        @pl.when(i != 0)
        def _():
            dgn_ref[...] += dgn

        dvh = dvn * gn
        dv = rstd * (dvh - jnp.mean(dvh, axis=-1, keepdims=True) - vhat * jnp.mean(dvh * vhat, axis=-1, keepdims=True))
        dzv_ref[...] = (dv * _gelu_grad(zv)).astype(BF)

    return pl.pallas_call(
        body,
        name="sgu_bwd",
        grid=(t // tm,),
        in_specs=[
            pl.BlockSpec((tm, d), lambda i: (i, 0)),
            pl.BlockSpec((tm, d), lambda i: (i, 1)),
            pl.BlockSpec((tm, d), lambda i: (i, 0)),
            pl.BlockSpec(memory_space=pl.ANY),
            pl.BlockSpec((1, d), lambda i: (0, 0)),
            pl.BlockSpec(w_s.shape, lambda i: (0, 0, 0)),
            pl.BlockSpec(b_col.shape, lambda i: (0, 0, 0)),
        ],
        out_specs=[
            pl.BlockSpec((tm, 2 * d), lambda i: (i, 0)),
            pl.BlockSpec(w_s.shape, lambda i: (0, 0, 0)),
            pl.BlockSpec(b_col.shape, lambda i: (0, 0, 0)),
            pl.BlockSpec((1, d), lambda i: (0, 0)),
        ],
        out_shape=[
            jax.ShapeDtypeStruct(dz.shape, BF),
            jax.ShapeDtypeStruct(w_s.shape, F32),
            jax.ShapeDtypeStruct(b_col.shape, F32),
            jax.ShapeDtypeStruct((1, d), F32),
        ],
        scratch_shapes=[pltpu.VMEM((tm, d), F32)],
        input_output_aliases={3: 0},
        compiler_params=_params(("arbitrary",)),
    )(z, z, da, dz, g_sgu, w_s, b_col)


def _shift_down(x, k, row):
    return jnp.where(row >= k, pltpu.roll(x, k, 0), 0.0)


def _shift_up(x, k, row, t):
    return jnp.where(row < t - k, pltpu.roll(x, t - k, 0), 0.0)


def _doublings(window):
    steps = int(math.log2(window))
    assert 2 ** steps == window
    return [2 ** s for s in range(steps)]


def _pool_diff(c, window, row):
    s = c
    for k in _doublings(window):
        s = s + _shift_down(s, k, row)
    count = jnp.minimum(row + 1, window).astype(F32)
    return s / count - c, count


def _pool_fwd(z, pool_w, pool_scale, t, d):
    dgp = d // len(POOL_WINDOWS)
    cblk = (2 * d) // dgp

    def body(zc_ref, w_ref, s_ref, b_ref):
        row = lax.broadcasted_iota(jnp.int32, (t, 1), 0)
        for gi, window in enumerate(POOL_WINDOWS):
            @pl.when(pl.program_id(0) == gi)
            def _(window=window):
                diff, _ = _pool_diff(zc_ref[...], window, row)
                out = jnp.dot(diff.astype(BF), w_ref[...], preferred_element_type=F32)
                b_ref[...] = (out * s_ref[...]).astype(BF)

    return pl.pallas_call(
        body,
        name="pool_fwd",
        grid=(len(POOL_WINDOWS),),
        in_specs=[
            pl.BlockSpec((t, dgp), lambda g: (0, cblk + g)),
            pl.BlockSpec((None, dgp, dgp), lambda g: (g, 0, 0)),
            pl.BlockSpec((1, dgp), lambda g: (0, g)),
        ],
        out_specs=pl.BlockSpec((t, dgp), lambda g: (0, g)),
        out_shape=jax.ShapeDtypeStruct((t, d), BF),
        compiler_params=_params(("arbitrary",)),
    )(z, pool_w, pool_scale)


def _pool_bwd(z, db, dz, pool_w, pool_scale, t, d):
    dgp = d // len(POOL_WINDOWS)
    cblk = (2 * d) // dgp

    def body(zc_ref, db_ref, dz_in, w_ref, s_ref, dzc_ref, dw_ref, ds_ref):
        del dz_in
        row = lax.broadcasted_iota(jnp.int32, (t, 1), 0)
        for gi, window in enumerate(POOL_WINDOWS):
            @pl.when(pl.program_id(0) == gi)
            def _(window=window):
                diff, count = _pool_diff(zc_ref[...], window, row)
                diff = diff.astype(BF)
                w = w_ref[...]
                dbv = db_ref[...].astype(F32)
                out = jnp.dot(diff, w, preferred_element_type=F32)
                ds_ref[...] = jnp.sum(dbv * out, axis=0, keepdims=True)
                dout = (dbv * s_ref[...]).astype(BF)
                dw_ref[...] = lax.dot_general(diff, dout, _DN["tn"], preferred_element_type=F32).astype(BF)
                ddiff = lax.dot_general(dout, w, _DN["nt"], preferred_element_type=F32)
                s = ddiff / count
                for k in _doublings(window):
                    s = s + _shift_up(s, k, row, t)
                dzc_ref[...] = (s - ddiff).astype(BF)

    return pl.pallas_call(
        body,
        name="pool_bwd",
        grid=(len(POOL_WINDOWS),),
        in_specs=[
            pl.BlockSpec((t, dgp), lambda g: (0, cblk + g)),
            pl.BlockSpec((t, dgp), lambda g: (0, g)),
            pl.BlockSpec(memory_space=pl.ANY),
            pl.BlockSpec((None, dgp, dgp), lambda g: (g, 0, 0)),
            pl.BlockSpec((1, dgp), lambda g: (0, g)),
        ],
        out_specs=[
            pl.BlockSpec((t, dgp), lambda g: (0, cblk + g)),
            pl.BlockSpec((None, dgp, dgp), lambda g: (g, 0, 0)),
            pl.BlockSpec((1, dgp), lambda g: (0, g)),
        ],
        out_shape=[
            jax.ShapeDtypeStruct(dz.shape, BF),
            jax.ShapeDtypeStruct(pool_w.shape, BF),
            jax.ShapeDtypeStruct((1, d), F32),
        ],
        input_output_aliases={2: 0},
        compiler_params=_params(("arbitrary",)),
    )(z, db, dz, pool_w, pool_scale)


def _mix_dy(dm, w_o, z, ya, yb, t, d, tm):
    def body(dm_ref, w_ref, gate_ref, ya_ref, yb_ref, dyab_ref, dz_ref, dy_ref):
        j = pl.program_id(1)

        @pl.when(j == 0)
        def _():
            dy_ref[...] = lax.dot_general(dm_ref[...], w_ref[...], _DN["nt"], preferred_element_type=F32)

        dy = dy_ref[...]
        s = _sig(gate_ref[...])
        yv = jnp.where(j == 0, ya_ref[...], yb_ref[...]).astype(F32)
        dyab_ref[...] = (dy * s).astype(BF)
        dz_ref[...] = (dy * yv * s * (1.0 - s)).astype(BF)

    row = pl.BlockSpec((tm, d), lambda i, j: (i, 0))
    return pl.pallas_call(
        body,
        name="mix_dy",
        grid=(t // tm, 2),
        in_specs=[row, pl.BlockSpec((d, d), lambda i, j: (0, 0)), pl.BlockSpec((tm, d), lambda i, j: (i, 3 + j)), row, row],
        out_specs=[pl.BlockSpec((tm, d), lambda i, j: (i, j)), pl.BlockSpec((tm, d), lambda i, j: (i, 3 + j))],
        out_shape=[jax.ShapeDtypeStruct((t, 2 * d), BF), jax.ShapeDtypeStruct((t, 5 * d), BF)],
        scratch_shapes=[pltpu.VMEM((tm, d), F32)],
        compiler_params=_params(("arbitrary", "arbitrary")),
    )(dm, w_o, z, ya, yb)


def _gather_start(tag, shards, deps=()):
    n = len(shards)
    lands = [lax.empty((N_DEV,) + s.shape, s.dtype) for s in shards]
    plan, local = _gather_plan_ici(n)
    send, recv, bufs, token = _copy_start(tag + "_ici_start", list(shards) + lands, plan, 4 * n, local=local, deps=deps)
    return dict(tag=tag, n=n, send=send, recv=recv, bufs=bufs, plan=plan), token


def _gather_forward(st, after):
    n, tag = st["n"], st["tag"]
    bufs = _copy_wait(tag + "_ici_wait", st["send"], st["recv"], st["bufs"], st["plan"], after)
    plan = _gather_plan_d2d(n)
    send, recv, lands, token = _copy_start(tag + "_d2d_start", bufs[n:], plan, 3 * n)
    return dict(tag=tag, n=n, send=send, recv=recv, bufs=lands, plan=plan), token


def _gather_finish(st, after):
    lands = _copy_wait(st["tag"] + "_d2d_wait", st["send"], st["recv"], st["bufs"], st["plan"], after)
    return [a.reshape((N_DEV * a.shape[1],) + a.shape[2:]) for a in lands]


def _scatter_start(tag, parts, deps=()):
    n = len(parts)
    p4 = [a.reshape(4, 2, a.shape[0] // N_DEV, a.shape[1]) for a in parts]
    lands = [lax.empty((4,) + a.shape[2:], BF) for a in p4]
    plan = _scatter_plan_d2d(n)
    send, recv, bufs, token = _copy_start(tag + "_d2d_start", p4 + lands, plan, 4 * n, deps=deps)
    return dict(tag=tag, n=n, send=send, recv=recv, bufs=bufs, plan=plan), token


def _scatter_middle(st, core, after):
    n, tag = st["n"], st["tag"]
    bufs = _copy_wait(tag + "_d2d_wait", st["send"], st["recv"], st["bufs"], st["plan"], after)
    sums = _pair_add(tag + "_pair_add", core, bufs[:n], bufs[n:])
    lands = [lax.empty(a.shape, BF) for a in sums]
    plan, local = _scatter_plan_ici(n)
    send, recv, bufs, token = _copy_start(tag + "_ici_start", sums + lands, plan, 3 * n, local=local)
    return dict(tag=tag, n=n, send=send, recv=recv, bufs=bufs, plan=plan), token


def _scatter_finish(st, after):
    bufs = _copy_wait(st["tag"] + "_ici_wait", st["send"], st["recv"], st["bufs"], st["plan"], after)
    return bufs[st["n"]:]


def _pack_rows(flat_list, width):
    flat = jnp.concatenate([a.reshape(-1) for a in flat_list])
    rows = -(-flat.shape[0] // (8 * width)) * 8
    return jnp.pad(flat, (0, rows * width - flat.shape[0])).reshape(rows, width)


def _unpack_rows(packed, like):
    flat, out, pos = packed.reshape(-1), [], 0
    for a in like:
        out.append(flat[pos:pos + a.size].reshape(a.shape))
        pos += a.size
    return out


def kernel(x, p, ffn1_pre_g, ffn1_w_gate, ffn1_w_up, ffn1_w_down, ffn1_post_g, mix_pre_g, w_in, sgu_norm_g, sgu_w, sgu_b, pool_w, pool_scale, w_out_a, w_out_b, w_o, mix_post_g, ffn2_pre_g, ffn2_w_gate, ffn2_w_up, ffn2_w_down, ffn2_post_g, ple_pre_g, ple_w_gate, ple_w_proj, ple_post_g, loss_target, m_ffn1_pre_g, m_ffn1_w_gate, m_ffn1_w_up, m_ffn1_w_down, m_ffn1_post_g, m_mix_pre_g, m_w_in, m_sgu_norm_g, m_sgu_w, m_sgu_b, m_pool_w, m_pool_scale, m_w_out_a, m_w_out_b, m_w_o, m_mix_post_g, m_ffn2_pre_g, m_ffn2_w_gate, m_ffn2_w_up, m_ffn2_w_down, m_ffn2_post_g, m_ple_pre_g, m_ple_w_gate, m_ple_w_proj, m_ple_post_g, v_ffn1_pre_g, v_ffn1_w_gate, v_ffn1_w_up, v_ffn1_w_down, v_ffn1_post_g, v_mix_pre_g, v_w_in, v_sgu_norm_g, v_sgu_w, v_sgu_b, v_pool_w, v_pool_scale, v_w_out_a, v_w_out_b, v_w_o, v_mix_post_g, v_ffn2_pre_g, v_ffn2_w_gate, v_ffn2_w_up, v_ffn2_w_down, v_ffn2_post_g, v_ple_pre_g, v_ple_w_gate, v_ple_w_proj, v_ple_post_g):
    args = dict(locals())
    names = ["ffn1_pre_g", "ffn1_w_gate", "ffn1_w_up", "ffn1_w_down", "ffn1_post_g", "mix_pre_g", "w_in",
             "sgu_norm_g", "sgu_w", "sgu_b", "pool_w", "pool_scale", "w_out_a", "w_out_b", "w_o", "mix_post_g",
             "ffn2_pre_g", "ffn2_w_gate", "ffn2_w_up", "ffn2_w_down", "ffn2_post_g", "ple_pre_g", "ple_w_gate",
             "ple_w_proj", "ple_post_g"]
    w = {k: args[k][0] for k in names}
    mom = {k: args["m_" + k][0] for k in names}
    var = {k: args["v_" + k][0] for k in names}

    assert x.shape[0] == 1 and p.shape[:2] == (1, 1)
    t, d = x.shape[1], x.shape[2]
    f = ffn1_w_gate.shape[2] * N_DEV
    d_in = w_in.shape[2] * N_DEV
    d_ple = p.shape[3]
    n_pool = len(POOL_WINDOWS)
    dgp = d // n_pool
    assert d_in == 5 * d and t % CHUNK == 0
    tm = _tile(t, 512, CHUNK)
    xs, ps, target = x[0], p[0, 0], loss_target[0]

    col_sharded = ("ffn1_w_gate", "ffn1_w_up", "ffn2_w_gate", "ffn2_w_up", "w_in", "ple_w_proj")
    row_sharded = ("ffn1_w_down", "ffn2_w_down", "w_out_a", "w_out_b", "w_o", "ple_w_gate")
    groups = [["ffn1_w_gate", "ffn1_w_up", "ffn1_w_down"],
              ["w_in", "pool_w", "w_out_a", "w_out_b", "w_o"],
              ["ffn2_w_gate", "ffn2_w_up", "ffn2_w_down", "ple_w_gate", "ple_w_proj"]]

    def shard_of(k):
        if k in col_sharded:
            return w[k].T.astype(BF)
        if k == "pool_w":
            return w[k].reshape(-1, dgp).astype(BF)
        return w[k].astype(BF)

    rows_pw = dgp // N_DEV
    b_col = w["sgu_b"][:, :, None]
    gains = {k: w[k][None, :] for k in names if w[k].ndim == 1}
    core = lax.axis_index("c").astype(jnp.int32).reshape(1)
    full = {}

    ag1, tok = _gather_start("gather1", [shard_of(k) for k in groups[0]])
    xn1 = _prenorm("ffn1_prenorm", xs, gains["ffn1_pre_g"], t, d, tm, deps=[tok])
    ag1, tok = _gather_forward(ag1, xn1)
    ag2, tok = _gather_start("gather2", [shard_of(k) for k in groups[1]], deps=[tok])
    full.update(zip(groups[0], _gather_finish(ag1, tok)))
    g1, u1, hid1, f1, h1 = _ffn_fwd("ffn1", xs, xn1, full["ffn1_w_gate"], full["ffn1_w_up"], full["ffn1_w_down"],
                                    gains["ffn1_post_g"], t, d, f, tm)

    ag2, tok = _gather_forward(ag2, h1)
    ag3, tok = _gather_start("gather3", [shard_of(k) for k in groups[2]], deps=[tok])
    xn2 = _prenorm("mix_prenorm", h1, gains["mix_pre_g"], t, d, tm, deps=[tok])
    full.update(zip(groups[1], _gather_finish(ag2, xn2)))
    pool_full = full["pool_w"].reshape(N_DEV, n_pool, rows_pw, dgp).transpose(1, 0, 2, 3).reshape(n_pool, dgp, dgp)
    z = _matmul("mix_in", [dict(a=xn2, b=full["w_in"], mode="nt", acc=0, tk=d)], m=t, n=d_in, tm=tm, tn=d,
                outs=[("mn", (t, d_in), F32, 0)], epilogue=lambda accs, ex: accs, n_acc=1, j_outer=True)[0]
    a_br = _sgu_fwd(z, gains["sgu_norm_g"], w["sgu_w"], b_col, t, d, tm)
    b_br = _pool_fwd(z, pool_full, gains["pool_scale"], t, d)

    def merge_epi(accs, ex):
        ya, yb = accs
        ga, gb = ex
        return [ya, yb, _sig(ga) * ya + _sig(gb) * yb]

    ya, yb, y = _matmul(
        "mix_merge",
        [dict(a=a_br, b=full["w_out_a"], mode="nn", acc=0, tk=d), dict(a=b_br, b=full["w_out_b"], mode="nn", acc=1, tk=d)],
        m=t, n=d, tm=tm, tn=d, extras=[("mn", z, 3), ("mn", z, 4)],
        outs=[("mn", (t, d), BF, 0)] * 3, epilogue=merge_epi, n_acc=2)

    def proj_epi(accs, ex):
        mm = accs[0]
        hh, gg = ex
        return [mm, hh + mm * _rms_r(mm) * gg]

    ag3, tok = _gather_forward(ag3, y)
    m_out, h2 = _matmul(
        "mix_proj", [dict(a=y, b=full["w_o"], mode="nn", acc=0, tk=d)], m=t, n=d, tm=tm, tn=d,
        extras=[("mn", h1, 0), ("n", gains["mix_post_g"])],
        outs=[("mn", (t, d), F32, 0)] * 2, epilogue=proj_epi, n_acc=1, deps=[tok])
    full.update(zip(groups[2], _gather_finish(ag3, h2)))

    xn3 = _prenorm("ffn2_prenorm", h2, gains["ffn2_pre_g"], t, d, tm)
    g2, u2, hid2, f2, h3 = _ffn_fwd("ffn2", h2, xn3, full["ffn2_w_gate"], full["ffn2_w_up"], full["ffn2_w_down"],
                                    gains["ffn2_post_g"], t, d, f, tm)

    xn4 = _prenorm("ple_prenorm", h3, gains["ple_pre_g"], t, d, tm)

    def ple_epi(accs, ex):
        gl, e = accs
        hh, tgt, gg = ex
        q = _sig(gl) * e
        err = hh + q * _rms_r(q) * gg - tgt
        return [gl, e, err * (1.0 / d), jnp.sum(err * err, axis=0, keepdims=True)]

    gl, e_ple, dh4, loss_vec = _matmul(
        "ple_fwd",
        [dict(a=xn4, b=full["ple_w_gate"], mode="nn", acc=0, tk=d), dict(a=ps, b=full["ple_w_proj"], mode="nt", acc=1, tk=d_ple)],
        m=t, n=d, tm=tm, tn=d, extras=[("mn", h3, 0), ("mn", target, 0), ("n", gains["ple_post_g"])],
        outs=[("mn", (t, d), F32, 0)] * 3 + [("acc", (1, d), F32)], epilogue=ple_epi, n_acc=2)
    loss = lax.psum(jnp.sum(loss_vec) * (0.5 / d), AXES)

    tk_t = _tile(t, 512)

    def ple_post_fn(v):
        dy, gl_v, e_v, gg = v
        s = _sig(gl_v)
        dq, dg = _rms_bwd(s * e_v, gg, dy)
        return [dq * e_v * s * (1.0 - s), dq * s, dg]

    dgl, de, dg_ple_post = _rowwise(
        "ple_post_bwd", [("row", dh4, d, 0), ("row", gl, d, 0), ("row", e_ple, d, 0), ("full", gains["ple_post_g"])],
        [("row", (t, d), BF, d, 0), ("row", (t, d), BF, d, 0), ("acc", (1, d), F32)], ple_post_fn, t=t, tm=tm)
    ident = lambda accs, ex: accs
    dw_ple_gate = _matmul("ple_dwg", [dict(a=xn4, b=dgl, mode="tn", acc=0, tk=tk_t)], m=d, n=d, tm=d, tn=d,
                          nk=t // tk_t, outs=[("mn", (d, d), BF, 0)], epilogue=ident, n_acc=1)[0]
    dw_ple_proj_t = _matmul("ple_dwp", [dict(a=de, b=ps, mode="tn", acc=0, tk=tk_t)], m=d, n=d_ple, tm=d, tn=d_ple,
                            nk=t // tk_t, outs=[("mn", (d, d_ple), BF, 0)], epilogue=ident, n_acc=1)[0]
    dh3, dg_ple_pre = _matmul(
        "ple_dx", [dict(a=dgl, b=full["ple_w_gate"], mode="nt", acc=0, tk=d)], m=t, n=d, tm=tm, tn=d,
        extras=[("mn", h3, 0), ("mn", dh4, 0), ("n", gains["ple_pre_g"])],
        outs=[("mn", (t, d), F32, 0), ("acc", (1, d), F32)], epilogue=_prenorm_bwd_epi, n_acc=1)

    dgate2, dup2, dg_f2_post, dwg2, dwu2, dwd2 = _ffn_bwd_weights(
        "ffn2", dh3, xn3, g2, u2, hid2, f2, full["ffn2_w_down"], gains["ffn2_post_g"], t, d, f, tm)
    rs_names = [["ple_w_gate", "ple_w_proj", "ffn2_w_down", "ffn2_w_gate", "ffn2_w_up"],
                ["w_o", "w_out_a", "w_out_b", "pool_w", "w_in"],
                ["ffn1_w_down", "ffn1_w_gate", "ffn1_w_up"]]
    rs1, tok = _scatter_start("scatter1", [dw_ple_gate, dw_ple_proj_t, dwd2, dwg2, dwu2])
    dh2, dg_f2_pre = _ffn_bwd_dx("ffn2", dh3, h2, dgate2, dup2, full["ffn2_w_gate"], full["ffn2_w_up"],
                                 gains["ffn2_pre_g"], t, d, f, tm, deps=[tok])
    rs1, tok = _scatter_middle(rs1, core, dh2)

    dm, dg_mix_post = _postnorm_bwd("mix_post_bwd", dh2, m_out, gains["mix_post_g"], 1.0, t, d, tm, deps=[tok])

    dyab, dz = _mix_dy(dm, full["w_o"], z, ya, yb, t, d, tm)
    dw_o = _matmul("mix_dwo", [dict(a=y, b=dm, mode="tn", acc=0, tk=tk_t)], m=d, n=d, tm=d, tn=d, nk=t // tk_t,
                   outs=[("mn", (d, d), BF, 0)], epilogue=ident, n_acc=1)[0]
    da, db = _matmul(
        "mix_dab",
        [dict(a=dyab, b=full["w_out_a"], mode="nt", acc=0, tk=d),
         dict(a=dyab, b=full["w_out_b"], mode="nt", acc=1, tk=d, a_off=(0, 1))],
        m=t, n=d, tm=tm, tn=d, outs=[("mn", (t, d), BF, 0)] * 2, epilogue=ident, n_acc=2)
    dw_out_a, dw_out_b = _matmul(
        "mix_dwab",
        [dict(a=a_br, b=dyab, mode="tn", acc=0, tk=tk_t),
         dict(a=b_br, b=dyab, mode="tn", acc=1, tk=tk_t, b_off=(0, 1))],
        m=d, n=d, tm=d, tn=d, nk=t // tk_t, outs=[("mn", (d, d), BF, 0)] * 2, epilogue=ident, n_acc=2)
    dz, dsgu_w, dsgu_b, dg_sgu = _sgu_bwd(z, da, dz, gains["sgu_norm_g"], w["sgu_w"], b_col, t, d, _tile(t, 256, CHUNK))
    dz, dpool_w, dg_pool = _pool_bwd(z, db, dz, pool_full, gains["pool_scale"], t, d)
    dw_in_t = _matmul("mix_dwin", [dict(a=dz, b=xn2, mode="tn", acc=0, tk=tk_t)], m=d_in, n=d, tm=d, tn=d,
                      nk=t // tk_t, outs=[("mn", (d_in, d), BF, 0)], epilogue=ident, n_acc=1)[0]
    dpool_rows = dpool_w.reshape(n_pool, N_DEV, rows_pw, dgp).transpose(1, 0, 2, 3).reshape(N_DEV * n_pool * rows_pw, dgp)
    rs2, tok = _scatter_start("scatter2", [dw_o, dw_out_a, dw_out_b, dpool_rows, dw_in_t])
    dh1, dg_mix_pre = _matmul(
        "mix_dx", [dict(a=dz, b=full["w_in"], mode="nn", acc=0, tk=d)], m=t, n=d, tm=tm, tn=d, nk=d_in // d,
        extras=[("mn", h1, 0), ("mn", dh2, 0), ("n", gains["mix_pre_g"])],
        outs=[("mn", (t, d), F32, 0), ("acc", (1, d), F32)], epilogue=_prenorm_bwd_epi, n_acc=1, deps=[tok])
    rs2, tok = _scatter_middle(rs2, core, dh1)

    dgate1, dup1, dg_f1_post, dwg1, dwu1, dwd1 = _ffn_bwd_weights(
        "ffn1", dh1, xn1, g1, u1, hid1, f1, full["ffn1_w_down"], gains["ffn1_post_g"], t, d, f, tm, deps=[tok])
    rs3, tok = _scatter_start("scatter3", [dwd1, dwg1, dwu1])
    grad_x, dg_f1_pre = _ffn_bwd_dx("ffn1", dh1, xs, dgate1, dup1, full["ffn1_w_gate"], full["ffn1_w_up"],
                                    gains["ffn1_pre_g"], t, d, f, tm, deps=[tok])
    rs3, tok = _scatter_middle(rs3, core, grad_x)

    grad, delta, new_m, new_v = {}, {}, {}, {}

    def update(group, staged):
        last = None
        for k, st in zip(group, staged):
            if k in col_sharded:
                grad[k] = _reduce_only("sum_" + k, st).T
                delta[k], new_m[k], new_v[k] = _adam_only("adam_" + k, w[k], grad[k], mom[k], var[k])
            else:
                shape = w[k].shape
                flat = (-1, shape[-1])
                res = _reduce_adam("adam_" + k, st, w[k].reshape(flat), mom[k].reshape(flat), var[k].reshape(flat))
                grad[k], delta[k], new_m[k], new_v[k] = [r.reshape(shape) for r in res]
            last = delta[k]
        return last

    last = update(rs_names[0], _scatter_finish(rs1, tok))
    last = update(rs_names[1], _scatter_finish(rs2, last))

    small_grads = {
        "ffn1_pre_g": dg_f1_pre, "ffn1_post_g": dg_f1_post, "mix_pre_g": dg_mix_pre, "sgu_norm_g": dg_sgu,
        "sgu_w": dsgu_w, "sgu_b": dsgu_b, "pool_scale": dg_pool, "mix_post_g": dg_mix_post, "ffn2_pre_g": dg_f2_pre,
        "ffn2_post_g": dg_f2_post, "ple_pre_g": dg_ple_pre, "ple_post_g": dg_ple_post,
    }
    small_names = [k for k in names if k in small_grads]
    small_packed = _pack_rows([small_grads[k] for k in small_names], d)
    staged_small = _exchange("exchange_small", [], [small_packed])[0]
    small_like = [w[k] for k in small_names]
    res = _reduce_adam(
        "adam_small", staged_small, _pack_rows(small_like, d), _pack_rows([mom[k] for k in small_names], d),
        _pack_rows([var[k] for k in small_names], d))
    for dst, packed in zip((grad, delta, new_m, new_v), res):
        dst.update(zip(small_names, _unpack_rows(packed, small_like)))
    update(rs_names[2], _scatter_finish(rs3, res[1]))

    out = [loss, grad_x[None]]
    for group in (grad, delta, new_m, new_v):
        out += [group[k][None] for k in names]
    return tuple(out)
```

```python
import math

import jax
import jax.numpy as jnp
from jax import lax
from jax.experimental import pallas as pl
from jax.experimental.pallas import tpu as pltpu

EPS = 1e-6
CHUNK = 128
N_SGU_GROUPS = 4
POOL_WINDOWS = (2, 4, 8, 16)
ADAM_LR = 0.001
ADAM_B1 = 0.9
ADAM_B2 = 0.999
ADAM_EPS = 1e-08
ADAM_WD = 0.01
ADAM_STEP = 10

N_DEV = 8
AXES = ("x", "y", "c")
MESH = pl.DeviceIdType.MESH
V7X_VMEM_BYTES = 64 * 1024 * 1024
VMEM_LIMIT = V7X_VMEM_BYTES - 8 * 1024 * 1024
BF = jnp.bfloat16
F32 = jnp.float32

_DN = {
    "nn": (((1,), (0,)), ((), ())),
    "nt": (((1,), (1,)), ((), ())),
    "tn": (((0,), (0,)), ((), ())),
}


def _params(sem=None):
    return pltpu.CompilerParams(dimension_semantics=sem, vmem_limit_bytes=VMEM_LIMIT)


def _tile(n, target, align=128):
    t = min(n, target)
    t -= t % align
    while t >= align:
        if n % t == 0:
            return t
        t -= align
    return n


def _sig(x):
    return 1.0 / (1.0 + jnp.exp(-x))


_GELU_K = math.sqrt(2.0 / math.pi)
_GELU_C = 0.044715


def _gelu(x):
    return 0.5 * x * (1.0 + jnp.tanh(_GELU_K * (x + _GELU_C * x * x * x)))


def _gelu_grad(x):
    t = jnp.tanh(_GELU_K * (x + _GELU_C * x * x * x))
    return 0.5 * (1.0 + t) + 0.5 * x * (1.0 - t * t) * _GELU_K * (1.0 + 3.0 * _GELU_C * x * x)


def _rms_r(x):
    return lax.rsqrt(jnp.mean(x * x, axis=-1, keepdims=True) + EPS)


def _rms_bwd(x, g, dy):
    r = _rms_r(x)
    xh = x * r
    gy = dy * g
    dx = r * (gy - xh * jnp.mean(xh * gy, axis=-1, keepdims=True))
    return dx, jnp.sum(dy * xh, axis=0, keepdims=True)


def _matmul(name, mats, *, m, n, tm, tn, nk=1, extras=(), outs, epilogue, n_acc, j_outer=False, deps=()):
    ni, nj = m // tm, n // tn
    assert ni * tm == m and nj * tn == n, (name, m, n, tm, tn)
    if j_outer:
        grid = (nj, ni, nk)

        def ij(g0, g1):
            return g1, g0
    else:
        grid = (ni, nj, nk)

        def ij(g0, g1):
            return g0, g1

    in_specs, args = [], []
    for mt in mats:
        mode, tk = mt["mode"], mt["tk"]
        ao, bo = mt.get("a_off", (0, 0)), mt.get("b_off", (0, 0))
        if mode == "tn":
            sa = pl.BlockSpec((tk, tm), lambda g0, g1, kk, ao=ao: (ao[0] + kk, ao[1] + ij(g0, g1)[0]))
        else:
            sa = pl.BlockSpec((tm, tk), lambda g0, g1, kk, ao=ao: (ao[0] + ij(g0, g1)[0], ao[1] + kk))
        if mode == "nt":
            sb = pl.BlockSpec((tn, tk), lambda g0, g1, kk, bo=bo: (bo[0] + ij(g0, g1)[1], bo[1] + kk))
        else:
            sb = pl.BlockSpec((tk, tn), lambda g0, g1, kk, bo=bo: (bo[0] + kk, bo[1] + ij(g0, g1)[1]))
        in_specs += [sa, sb]
        args += [mt["a"], mt["b"]]
    n_mat_refs = len(args)
    for ex in extras:
        if ex[0] == "mn":
            in_specs.append(pl.BlockSpec((tm, tn), lambda g0, g1, kk, c=ex[2]: (ij(g0, g1)[0], c + ij(g0, g1)[1])))
        else:
            in_specs.append(pl.BlockSpec((1, tn), lambda g0, g1, kk: (0, ij(g0, g1)[1])))
        args.append(ex[1])
    n_ex_end = len(args)
    in_specs += [pl.BlockSpec(memory_space=pl.ANY)] * len(deps)
    args += list(deps)
    n_in = len(args)
    out_specs, out_shape = [], []
    for o in outs:
        if o[0] == "mn":
            out_specs.append(pl.BlockSpec((tm, tn), lambda g0, g1, kk, c=o[3]: (ij(g0, g1)[0], c + ij(g0, g1)[1])))
        else:
            assert nj == 1, name
            out_specs.append(pl.BlockSpec((1, tn), lambda g0, g1, kk: (0, 0)))
        out_shape.append(jax.ShapeDtypeStruct(o[1], o[2]))
    n_out = len(outs)

    def body(*refs):
        mat_refs = refs[:n_mat_refs]
        ex_refs = refs[n_mat_refs:n_ex_end]
        out_refs = refs[n_in:n_in + n_out]
        acc_refs = refs[n_in + n_out:]
        i = ij(pl.program_id(0), pl.program_id(1))[0]
        kk = pl.program_id(2)

        def products():
            res = [None] * n_acc
            for idx, mt in enumerate(mats):
                a = mat_refs[2 * idx][...].astype(BF)
                b = mat_refs[2 * idx + 1][...].astype(BF)
                p = lax.dot_general(a, b, _DN[mt["mode"]], preferred_element_type=F32)
                q = mt["acc"]
                res[q] = p if res[q] is None else res[q] + p
            return res

        def finish(accs):
            vals = epilogue(accs, [r[...] for r in ex_refs])
            for o, r, v in zip(outs, out_refs, vals):
                if o[0] == "mn":
                    r[...] = v.astype(o[2])
                else:
                    @pl.when(i == 0)
                    def _():
                        r[...] = v

                    @pl.when(i != 0)
                    def _():
                        r[...] += v

        if nk == 1:
            finish(products())
        else:
            res = products()

            @pl.when(kk == 0)
            def _():
                for q in range(n_acc):
                    acc_refs[q][...] = res[q]

            @pl.when(kk != 0)
            def _():
                for q in range(n_acc):
                    acc_refs[q][...] += res[q]

            @pl.when(kk == nk - 1)
            def _():
                finish([r[...] for r in acc_refs])

    scratch = [pltpu.VMEM((tm, tn), F32) for _ in range(n_acc)] if nk > 1 else []
    return pl.pallas_call(
        body,
        name=name,
        grid=grid,
        in_specs=in_specs,
        out_specs=out_specs,
        out_shape=out_shape,
        scratch_shapes=scratch,
        compiler_params=_params(("arbitrary", "arbitrary", "arbitrary")),
    )(*args)


def _rowwise(name, ins, outs, fn, *, t, tm, deps=()):
    ni = t // tm
    assert ni * tm == t, (name, t, tm)
    in_specs, args = [], []
    for s in ins:
        if s[0] == "row":
            in_specs.append(pl.BlockSpec((tm, s[2]), lambda i, c=s[3]: (i, c)))
        else:
            nd = s[1].ndim
            in_specs.append(pl.BlockSpec(s[1].shape, lambda i, nd=nd: (0,) * nd))
        args.append(s[1])
    out_specs, out_shape = [], []
    for o in outs:
        if o[0] == "row":
            out_specs.append(pl.BlockSpec((tm, o[3]), lambda i, c=o[4]: (i, c)))
        else:
            nd = len(o[1])
            out_specs.append(pl.BlockSpec(o[1], lambda i, nd=nd: (0,) * nd))
        out_shape.append(jax.ShapeDtypeStruct(o[1], o[2]))
    n_read = len(args)
    in_specs += [pl.BlockSpec(memory_space=pl.ANY)] * len(deps)
    args += list(deps)
    n_in = len(args)

    def body(*refs):
        i = pl.program_id(0)
        vals = fn([r[...] for r in refs[:n_read]])
        for o, r, v in zip(outs, refs[n_in:], vals):
            if o[0] == "row":
                r[...] = v.astype(o[2])
            else:
                @pl.when(i == 0)
                def _():
                    r[...] = v

                @pl.when(i != 0)
                def _():
                    r[...] += v

    return pl.pallas_call(
        body,
        name=name,
        grid=(ni,),
        in_specs=in_specs,
        out_specs=out_specs,
        out_shape=out_shape,
        compiler_params=_params(("arbitrary",)),
    )(*args)


def _place():
    return lax.axis_index("x"), lax.axis_index("y"), lax.axis_index("c")


_HBM = pl.BlockSpec(memory_space=pltpu.HBM)
_SEM = pl.BlockSpec(memory_space=pltpu.SEMAPHORE)
_ANY = pl.BlockSpec(memory_space=pl.ANY)
_EFFECT = pltpu.SideEffectType.DATAFLOW_SIDE_EFFECTING


def _remote(src, dst, send_sems, recv_sems, k, to):
    return pltpu.make_async_remote_copy(
        src_ref=src, dst_ref=dst, send_sem=send_sems.at[k], recv_sem=recv_sems.at[k],
        device_id=to, device_id_type=MESH)


def _copy_start(name, bufs, plan, n_copies, local=None, deps=()):
    nb, nd = len(bufs), len(deps)
    n_local = 0 if local is None else nb // 2

    def body(*refs):
        buf_refs = refs[:nb]
        send_sems, recv_sems = refs[nb + nd], refs[nb + nd + 1]
        token = refs[nb + nd + 2 + nb]
        if local is not None:
            local_sems = refs[nb + nd + 3 + nb]
            mine = [pltpu.make_async_copy(a, b, local_sems.at[k]) for k, (a, b) in enumerate(local(buf_refs))]
            for cp in mine:
                cp.start()
            for cp in mine:
                cp.wait()
        for cp in plan(buf_refs, send_sems, recv_sems):
            cp.start()
        token[...] = jnp.zeros_like(token)

    res = pl.pallas_call(
        body,
        name=name,
        in_specs=[_HBM] * nb + [_ANY] * nd,
        out_specs=[_SEM, _SEM] + [_HBM] * nb + [pl.BlockSpec(memory_space=pltpu.VMEM)],
        out_shape=[pltpu.SemaphoreType.DMA((n_copies,)), pltpu.SemaphoreType.DMA((n_copies,))]
        + [pltpu.HBM(b.shape, b.dtype) for b in bufs] + [jax.ShapeDtypeStruct((8, 128), F32)],
        scratch_shapes=[pltpu.SemaphoreType.DMA((n_local,))] if n_local else [],
        input_output_aliases={i: 2 + i for i in range(nb)},
        compiler_params=pltpu.CompilerParams(has_side_effects=_EFFECT),
    )(*[pltpu.with_memory_space_constraint(b, pltpu.HBM) for b in bufs], *deps)
    return res[0], res[1], list(res[2:2 + nb]), res[2 + nb]


def _copy_wait(name, send_sems, recv_sems, bufs, plan, after):
    nb = len(bufs)
    after = list(after) if isinstance(after, (list, tuple)) else [after]

    def body(*refs):
        for cp in plan(refs[:nb], refs[nb], refs[nb + 1]):
            cp.wait_send()
            cp.wait_recv()

    return list(pl.pallas_call(
        body,
        name=name,
        in_specs=[_HBM] * nb + [_SEM, _SEM] + [_ANY] * len(after),
        out_specs=[_HBM] * nb,
        out_shape=[pltpu.HBM(b.shape, b.dtype) for b in bufs],
        input_output_aliases={i: i for i in range(nb)},
        compiler_params=pltpu.CompilerParams(has_side_effects=_EFFECT),
    )(*bufs, send_sems, recv_sems, *after))


def _chips(x, y):
    return [(1 - x, y), (x, 1 - y), (1 - x, 1 - y)]


def _gather_plan_ici(n):
    def plan(refs, send_sems, recv_sems):
        x, y, c = _place()
        me = 4 * x + 2 * y + c
        targets = [(x, y, 1 - c)] + [(*chip, c) for chip in _chips(x, y)]
        return [_remote(refs[i], refs[n + i].at[me], send_sems, recv_sems, 4 * i + k, to)
                for i in range(n) for k, to in enumerate(targets)]

    def local(refs):
        x, y, c = _place()
        return [(refs[i], refs[n + i].at[4 * x + 2 * y + c]) for i in range(n)]

    return plan, local


def _gather_plan_d2d(n):
    def plan(refs, send_sems, recv_sems):
        x, y, c = _place()
        cps = []
        for i in range(n):
            for j, (cx, cy) in enumerate(_chips(x, y)):
                block = refs[i].at[4 * cx + 2 * cy + c]
                cps.append(_remote(block, block, send_sems, recv_sems, 3 * i + j, (x, y, 1 - c)))
        return cps

    return plan


def _scatter_plan_d2d(n):
    def plan(refs, send_sems, recv_sems):
        x, y, c = _place()
        return [_remote(refs[i].at[q, 1 - c], refs[n + i].at[q], send_sems, recv_sems, 4 * i + q, (x, y, 1 - c))
                for i in range(n) for q in range(4)]

    return plan


def _scatter_plan_ici(n):
    def plan(refs, send_sems, recv_sems):
        x, y, c = _place()
        return [_remote(refs[i].at[2 * cx + cy], refs[n + i].at[2 * x + y], send_sems, recv_sems, 3 * i + j, (cx, cy, c))
                for i in range(n) for j, (cx, cy) in enumerate(_chips(x, y))]

    def local(refs):
        x, y, _ = _place()
        return [(refs[i].at[2 * x + y], refs[n + i].at[2 * x + y]) for i in range(n)]

    return plan, local


def _broadcast_plan():
    def plan(refs, send_sems, recv_sems):
        x, y, c = _place()
        cps = []
        for dd in range(1, N_DEV):
            peer = (1 - x if dd & 4 else x, 1 - y if dd & 2 else y, 1 - c if dd & 1 else c)
            cps.append(_remote(refs[0], refs[1].at[4 * x + 2 * y + c], send_sems, recv_sems, dd - 1, peer))
        return cps

    def local(refs):
        x, y, c = _place()
        return [(refs[0], refs[1].at[4 * x + 2 * y + c])]

    return plan, local


def _pair_add(name, core, parts, landed):
    n = len(parts)

    def body(core_ref, *refs):
        del core_ref
        for i in range(n):
            refs[2 * n + i][...] = (refs[i][...].astype(F32) + refs[n + i][...].astype(F32)).astype(BF)

    in_specs = [pl.BlockSpec((None, None) + p.shape[2:], lambda q, core_ref: (q, core_ref[0], 0, 0)) for p in parts]
    in_specs += [pl.BlockSpec((None,) + a.shape[1:], lambda q, core_ref: (q, 0, 0)) for a in landed]
    return list(pl.pallas_call(
        body,
        name=name,
        grid_spec=pltpu.PrefetchScalarGridSpec(
            num_scalar_prefetch=1, grid=(4,), in_specs=in_specs,
            out_specs=[pl.BlockSpec((None,) + a.shape[1:], lambda q, core_ref: (q, 0, 0)) for a in landed]),
        out_shape=[jax.ShapeDtypeStruct(a.shape, BF) for a in landed],
        compiler_params=_params(("arbitrary",)),
    )(core, *parts, *landed))


def _adam_vals(w, g, m, v):
    m = ADAM_B1 * m + (1.0 - ADAM_B1) * g
    v = ADAM_B2 * v + (1.0 - ADAM_B2) * (g * g)
    m_hat = m / (1.0 - ADAM_B1 ** ADAM_STEP)
    v_hat = v / (1.0 - ADAM_B2 ** ADAM_STEP)
    delta = -ADAM_LR * (m_hat / (jnp.sqrt(v_hat) + ADAM_EPS) + ADAM_WD * w)
    return delta, m, v


def _sum_parts(st_ref):
    g = st_ref[0].astype(F32)
    for k in range(1, st_ref.shape[0]):
        g = g + st_ref[k].astype(F32)
    return g


def _reduce_adam(name, st, w, m, v):
    rows, cols = w.shape
    tr = _tile(rows, 512, 16)

    def body(st_ref, w_ref, m_ref, v_ref, g_out, d_out, m_out, v_out):
        g = _sum_parts(st_ref)
        d, mm, vv = _adam_vals(w_ref[...], g, m_ref[...], v_ref[...])
        g_out[...] = g
        d_out[...] = d
        m_out[...] = mm
        v_out[...] = vv

    blk = pl.BlockSpec((tr, cols), lambda i: (i, 0))
    return pl.pallas_call(
        body,
        name=name,
        grid=(rows // tr,),
        in_specs=[pl.BlockSpec((st.shape[0], tr, cols), lambda i: (0, i, 0)), blk, blk, blk],
        out_specs=[blk] * 4,
        out_shape=[jax.ShapeDtypeStruct(w.shape, F32)] * 4,
        compiler_params=_params(("arbitrary",)),
    )(st, w, m, v)


def _reduce_only(name, st):
    _, rows, cols = st.shape
    tr = _tile(rows, 512, 16)

    def body(st_ref, g_out):
        g_out[...] = _sum_parts(st_ref)

    return pl.pallas_call(
        body,
        name=name,
        grid=(rows // tr,),
        in_specs=[pl.BlockSpec((st.shape[0], tr, cols), lambda i: (0, i, 0))],
        out_specs=pl.BlockSpec((tr, cols), lambda i: (i, 0)),
        out_shape=jax.ShapeDtypeStruct((rows, cols), F32),
        compiler_params=_params(("arbitrary",)),
    )(st)


def _adam_only(name, w, g, m, v):
    rows, cols = w.shape
    tr = _tile(rows, 512, 16)

    def body(w_ref, g_ref, m_ref, v_ref, d_out, m_out, v_out):
        d, mm, vv = _adam_vals(w_ref[...], g_ref[...], m_ref[...], v_ref[...])
        d_out[...] = d
        m_out[...] = mm
        v_out[...] = vv

    blk = pl.BlockSpec((tr, cols), lambda i: (i, 0))
    return pl.pallas_call(
        body,
        name=name,
        grid=(rows // tr,),
        in_specs=[blk] * 4,
        out_specs=[blk] * 3,
        out_shape=[jax.ShapeDtypeStruct(w.shape, F32)] * 3,
        compiler_params=_params(("arbitrary",)),
    )(w, g, m, v)


def _prenorm(name, h, g, t, d, tm, deps=()):
    def fn(v):
        x, gg = v
        return [x * _rms_r(x) * gg]

    return _rowwise(name, [("row", h, d, 0), ("full", g)], [("row", (t, d), BF, d, 0)], fn, t=t, tm=tm, deps=deps)[0]


def _ffn_fwd(tag, h, xn, wgt, wut, wd, g_post, t, d, f, tm):
    tf = _tile(f, 1408)

    def up_epi(accs, ex):
        gg, uu = accs
        return [gg, uu, gg * _sig(gg) * uu]

    mats = [dict(a=xn, b=wgt, mode="nt", acc=0, tk=d), dict(a=xn, b=wut, mode="nt", acc=1, tk=d)]
    gate, up, hid = _matmul(
        tag + "_up", mats, m=t, n=f, tm=tm, tn=tf,
        outs=[("mn", (t, f), BF, 0)] * 3, epilogue=up_epi, n_acc=2, j_outer=True)

    def down_epi(accs, ex):
        ff = accs[0]
        hh, gg = ex
        return [ff, hh + 0.5 * ff * _rms_r(ff) * gg]

    fo, h_new = _matmul(
        tag + "_down", [dict(a=hid, b=wd, mode="nn", acc=0, tk=f)], m=t, n=d, tm=tm, tn=d,
        extras=[("mn", h, 0), ("n", g_post)],
        outs=[("mn", (t, d), F32, 0)] * 2, epilogue=down_epi, n_acc=1)
    return gate, up, hid, fo, h_new


def _postnorm_bwd(name, dh, fo, g, scale, t, d, tm, deps=()):
    def fn(v):
        dy, ff, gg = v
        dx, dg = _rms_bwd(ff, gg, dy * scale)
        return [dx, dg]

    return _rowwise(
        name, [("row", dh, d, 0), ("row", fo, d, 0), ("full", g)],
        [("row", (t, d), BF, d, 0), ("acc", (1, d), F32)], fn, t=t, tm=tm, deps=deps)


def _prenorm_bwd_epi(accs, ex):
    hh, dh, gg = ex
    dx, dg = _rms_bwd(hh, gg, accs[0])
    return [dh + dx, dg]


def _ffn_bwd_weights(tag, dh, xn, gate, up, hid, fo, wd, g_post, t, d, f, tm, deps=()):
    tf = _tile(f, 1408)
    tk_t = _tile(t, 512)
    df, dg_post = _postnorm_bwd(tag + "_post_bwd", dh, fo, g_post, 0.5, t, d, tm, deps=deps)

    def hid_epi(accs, ex):
        dhid = accs[0]
        gg, uu = ex[0].astype(F32), ex[1].astype(F32)
        s = _sig(gg)
        return [dhid * uu * s * (1.0 + gg * (1.0 - s)), dhid * gg * s]

    dgate, dup = _matmul(
        tag + "_dhid", [dict(a=df, b=wd, mode="nt", acc=0, tk=d)], m=t, n=f, tm=tm, tn=tf,
        extras=[("mn", gate, 0), ("mn", up, 0)],
        outs=[("mn", (t, f), BF, 0)] * 2, epilogue=hid_epi, n_acc=1, j_outer=True)

    dwd = _matmul(
        tag + "_dwd", [dict(a=hid, b=df, mode="tn", acc=0, tk=tk_t)], m=f, n=d, tm=tf, tn=d, nk=t // tk_t,
        outs=[("mn", (f, d), BF, 0)], epilogue=lambda accs, ex: accs, n_acc=1)[0]
    dwgt, dwut = _matmul(
        tag + "_dwgu",
        [dict(a=dgate, b=xn, mode="tn", acc=0, tk=tk_t), dict(a=dup, b=xn, mode="tn", acc=1, tk=tk_t)],
        m=f, n=d, tm=tf, tn=d, nk=t // tk_t,
        outs=[("mn", (f, d), BF, 0)] * 2, epilogue=lambda accs, ex: accs, n_acc=2)
    return dgate, dup, dg_post, dwgt, dwut, dwd


def _ffn_bwd_dx(tag, dh, h_in, dgate, dup, wgt, wut, g_pre, t, d, f, tm, deps=()):
    tf = _tile(f, 1408)
    return _matmul(
        tag + "_dx",
        [dict(a=dgate, b=wgt, mode="nn", acc=0, tk=tf), dict(a=dup, b=wut, mode="nn", acc=0, tk=tf)],
        m=t, n=d, tm=tm, tn=d, nk=f // tf,
        extras=[("mn", h_in, 0), ("mn", dh, 0), ("n", g_pre)],
        outs=[("mn", (t, d), F32, 0), ("acc", (1, d), F32)], epilogue=_prenorm_bwd_epi, n_acc=1, deps=deps)


def _causal_mask():
    r = lax.broadcasted_iota(jnp.int32, (CHUNK, CHUNK), 0)
    c = lax.broadcasted_iota(jnp.int32, (CHUNK, CHUNK), 1)
    return r >= c


def _layernorm_parts(v):
    mu = jnp.mean(v, axis=-1, keepdims=True)
    vc = v - mu
    rstd = lax.rsqrt(jnp.mean(vc * vc, axis=-1, keepdims=True) + EPS)
    return vc * rstd, rstd


def _sgu_fwd(z, g_sgu, w_s, b_col, t, d, tm):
    dg = d // N_SGU_GROUPS
    n_chunk = tm // CHUNK

    def body(zu_ref, zv_ref, g_ref, w_ref, b_ref, a_ref):
        u = _gelu(zu_ref[...])
        vhat, _ = _layernorm_parts(_gelu(zv_ref[...]))
        vn = (vhat * g_ref[...]).astype(BF)
        mask = _causal_mask()
        for gi in range(N_SGU_GROUPS):
            ws = jnp.where(mask, w_ref[gi], 0.0).astype(BF)
            bias = b_ref[gi]
            for ci in range(n_chunk):
                rows, cols = slice(ci * CHUNK, (ci + 1) * CHUNK), slice(gi * dg, (gi + 1) * dg)
                sv = jnp.dot(ws, vn[rows, cols], preferred_element_type=F32) + bias
                a_ref[rows, cols] = (u[rows, cols] * sv).astype(BF)

    return pl.pallas_call(
        body,
        name="sgu_fwd",
        grid=(t // tm,),
        in_specs=[
            pl.BlockSpec((tm, d), lambda i: (i, 0)),
            pl.BlockSpec((tm, d), lambda i: (i, 1)),
            pl.BlockSpec((1, d), lambda i: (0, 0)),
            pl.BlockSpec(w_s.shape, lambda i: (0, 0, 0)),
            pl.BlockSpec(b_col.shape, lambda i: (0, 0, 0)),
        ],
        out_specs=pl.BlockSpec((tm, d), lambda i: (i, 0)),
        out_shape=jax.ShapeDtypeStruct((t, d), BF),
        compiler_params=_params(("arbitrary",)),
    )(z, z, g_sgu, w_s, b_col)


def _sgu_bwd(z, da, dz, g_sgu, w_s, b_col, t, d, tm):
    dg = d // N_SGU_GROUPS
    n_chunk = tm // CHUNK

    def body(zu_ref, zv_ref, da_ref, dz_in, g_ref, w_ref, b_ref, dz_ref, dw_ref, db_ref, dgn_ref, dvn_ref):
        del dz_in
        dzu_ref, dzv_ref = dz_ref.at[:, pl.ds(0, d)], dz_ref.at[:, pl.ds(d, d)]
        i = pl.program_id(0)
        zu, zv = zu_ref[...], zv_ref[...]
        u = _gelu(zu)
        vhat, rstd = _layernorm_parts(_gelu(zv))
        gn = g_ref[...]
        vn = (vhat * gn).astype(BF)
        da_v = da_ref[...].astype(F32)
        dsv_all = (da_v * u).astype(BF)
        mask = _causal_mask()
        for gi in range(N_SGU_GROUPS):
            ws = jnp.where(mask, w_ref[gi], 0.0).astype(BF)
            bias = b_ref[gi]
            dw = jnp.zeros((CHUNK, CHUNK), F32)
            dbias = jnp.zeros((CHUNK, 1), F32)
            for ci in range(n_chunk):
                rows, cols = slice(ci * CHUNK, (ci + 1) * CHUNK), slice(gi * dg, (gi + 1) * dg)
                vn_c, dsv = vn[rows, cols], dsv_all[rows, cols]
                sv = jnp.dot(ws, vn_c, preferred_element_type=F32) + bias
                dzu_ref[rows, cols] = (da_v[rows, cols] * sv * _gelu_grad(zu[rows, cols])).astype(BF)
                dw = dw + lax.dot_general(dsv, vn_c, _DN["nt"], preferred_element_type=F32)
                dbias = dbias + jnp.sum(dsv.astype(F32), axis=1, keepdims=True)
                dvn_ref[rows, cols] = lax.dot_general(ws, dsv, _DN["tn"], preferred_element_type=F32)
            dw = jnp.where(mask, dw, 0.0)

            @pl.when(i == 0)
            def _():
                dw_ref[gi] = dw
                db_ref[gi] = dbias

            @pl.when(i != 0)
            def _():
                dw_ref[gi] += dw
                db_ref[gi] += dbias

        dvn = dvn_ref[...]
        dgn = jnp.sum(dvn * vhat, axis=0, keepdims=True)

        @pl.when(i == 0)
        def _():
            dgn_ref[...] = dgn

        @pl.when(i != 0)
        def _():
            dgn_ref[...] += dgn

        dvh = dvn * gn
        dv = rstd * (dvh - jnp.mean(dvh, axis=-1, keepdims=True) - vhat * jnp.mean(dvh * vhat, axis=-1, keepdims=True))
        dzv_ref[...] = (dv * _gelu_grad(zv)).astype(BF)

    return pl.pallas_call(
        body,
        name="sgu_bwd",
        grid=(t // tm,),
        in_specs=[
            pl.BlockSpec((tm, d), lambda i: (i, 0)),
            pl.BlockSpec((tm, d), lambda i: (i, 1)),
            pl.BlockSpec((tm, d), lambda i: (i, 0)),
            pl.BlockSpec(memory_space=pl.ANY),
            pl.BlockSpec((1, d), lambda i: (0, 0)),
            pl.BlockSpec(w_s.shape, lambda i: (0, 0, 0)),
            pl.BlockSpec(b_col.shape, lambda i: (0, 0, 0)),
        ],
        out_specs=[
            pl.BlockSpec((tm, 2 * d), lambda i: (i, 0)),
            pl.BlockSpec(w_s.shape, lambda i: (0, 0, 0)),
            pl.BlockSpec(b_col.shape, lambda i: (0, 0, 0)),
            pl.BlockSpec((1, d), lambda i: (0, 0)),
        ],
        out_shape=[
            jax.ShapeDtypeStruct(dz.shape, BF),
            jax.ShapeDtypeStruct(w_s.shape, F32),
            jax.ShapeDtypeStruct(b_col.shape, F32),
            jax.ShapeDtypeStruct((1, d), F32),
        ],
        scratch_shapes=[pltpu.VMEM((tm, d), F32)],
        input_output_aliases={3: 0},
        compiler_params=_params(("arbitrary",)),
    )(z, z, da, dz, g_sgu, w_s, b_col)


def _shift_down(x, k, row):
    return jnp.where(row >= k, pltpu.roll(x, k, 0), 0.0)


def _shift_up(x, k, row, t):
    return jnp.where(row < t - k, pltpu.roll(x, t - k, 0), 0.0)


def _doublings(window):
    steps = int(math.log2(window))
    assert 2 ** steps == window
    return [2 ** s for s in range(steps)]


def _pool_diff(c, window, row):
    s = c
    for k in _doublings(window):
        s = s + _shift_down(s, k, row)
    count = jnp.minimum(row + 1, window).astype(F32)
    return s / count - c, count


def _pool_fwd(z, pool_w, pool_scale, t, d):
    dgp = d // len(POOL_WINDOWS)
    cblk = (2 * d) // dgp

    def body(zc_ref, w_ref, s_ref, b_ref):
        row = lax.broadcasted_iota(jnp.int32, (t, 1), 0)
        for gi, window in enumerate(POOL_WINDOWS):
            @pl.when(pl.program_id(0) == gi)
            def _(window=window):
                diff, _ = _pool_diff(zc_ref[...], window, row)
                out = jnp.dot(diff.astype(BF), w_ref[...], preferred_element_type=F32)
                b_ref[...] = (out * s_ref[...]).astype(BF)

    return pl.pallas_call(
        body,
        name="pool_fwd",
        grid=(len(POOL_WINDOWS),),
        in_specs=[
            pl.BlockSpec((t, dgp), lambda g: (0, cblk + g)),
            pl.BlockSpec((None, dgp, dgp), lambda g: (g, 0, 0)),
            pl.BlockSpec((1, dgp), lambda g: (0, g)),
        ],
        out_specs=pl.BlockSpec((t, dgp), lambda g: (0, g)),
        out_shape=jax.ShapeDtypeStruct((t, d), BF),
        compiler_params=_params(("arbitrary",)),
    )(z, pool_w, pool_scale)


def _pool_bwd(z, db, dz, pool_w, pool_scale, t, d):
    dgp = d // len(POOL_WINDOWS)
    cblk = (2 * d) // dgp

    def body(zc_ref, db_ref, dz_in, w_ref, s_ref, dzc_ref, dw_ref, ds_ref):
        del dz_in
        row = lax.broadcasted_iota(jnp.int32, (t, 1), 0)
        for gi, window in enumerate(POOL_WINDOWS):
            @pl.when(pl.program_id(0) == gi)
            def _(window=window):
                diff, count = _pool_diff(zc_ref[...], window, row)
                diff = diff.astype(BF)
                w = w_ref[...]
                dbv = db_ref[...].astype(F32)
                out = jnp.dot(diff, w, preferred_element_type=F32)
                ds_ref[...] = jnp.sum(dbv * out, axis=0, keepdims=True)
                dout = (dbv * s_ref[...]).astype(BF)
                dw_ref[...] = lax.dot_general(diff, dout, _DN["tn"], preferred_element_type=F32).astype(BF)
                ddiff = lax.dot_general(dout, w, _DN["nt"], preferred_element_type=F32)
                s = ddiff / count
                for k in _doublings(window):
                    s = s + _shift_up(s, k, row, t)
                dzc_ref[...] = (s - ddiff).astype(BF)

    return pl.pallas_call(
        body,
        name="pool_bwd",
        grid=(len(POOL_WINDOWS),),
        in_specs=[
            pl.BlockSpec((t, dgp), lambda g: (0, cblk + g)),
            pl.BlockSpec((t, dgp), lambda g: (0, g)),
            pl.BlockSpec(memory_space=pl.ANY),
            pl.BlockSpec((None, dgp, dgp), lambda g: (g, 0, 0)),
            pl.BlockSpec((1, dgp), lambda g: (0, g)),
        ],
        out_specs=[
            pl.BlockSpec((t, dgp), lambda g: (0, cblk + g)),
            pl.BlockSpec((None, dgp, dgp), lambda g: (g, 0, 0)),
            pl.BlockSpec((1, dgp), lambda g: (0, g)),
        ],
        out_shape=[
            jax.ShapeDtypeStruct(dz.shape, BF),
            jax.ShapeDtypeStruct(pool_w.shape, BF),
            jax.ShapeDtypeStruct((1, d), F32),
        ],
        input_output_aliases={2: 0},
        compiler_params=_params(("arbitrary",)),
    )(z, db, dz, pool_w, pool_scale)


def _mix_dy(dm, w_o, z, ya, yb, t, d, tm):
    def body(dm_ref, w_ref, gate_ref, ya_ref, yb_ref, dyab_ref, dz_ref, dy_ref):
        j = pl.program_id(1)

        @pl.when(j == 0)
        def _():
            dy_ref[...] = lax.dot_general(dm_ref[...], w_ref[...], _DN["nt"], preferred_element_type=F32)

        dy = dy_ref[...]
        s = _sig(gate_ref[...])
        yv = jnp.where(j == 0, ya_ref[...], yb_ref[...]).astype(F32)
        dyab_ref[...] = (dy * s).astype(BF)
        dz_ref[...] = (dy * yv * s * (1.0 - s)).astype(BF)

    row = pl.BlockSpec((tm, d), lambda i, j: (i, 0))
    return pl.pallas_call(
        body,
        name="mix_dy",
        grid=(t // tm, 2),
        in_specs=[row, pl.BlockSpec((d, d), lambda i, j: (0, 0)), pl.BlockSpec((tm, d), lambda i, j: (i, 3 + j)), row, row],
        out_specs=[pl.BlockSpec((tm, d), lambda i, j: (i, j)), pl.BlockSpec((tm, d), lambda i, j: (i, 3 + j))],
        out_shape=[jax.ShapeDtypeStruct((t, 2 * d), BF), jax.ShapeDtypeStruct((t, 5 * d), BF)],
        scratch_shapes=[pltpu.VMEM((tm, d), F32)],
        compiler_params=_params(("arbitrary", "arbitrary")),
    )(dm, w_o, z, ya, yb)


def _gather_start(tag, shards, deps=()):
    n = len(shards)
    lands = [lax.empty((N_DEV,) + s.shape, s.dtype) for s in shards]
    plan, local = _gather_plan_ici(n)
    send, recv, bufs, token = _copy_start(tag + "_ici_start", list(shards) + lands, plan, 4 * n, local=local, deps=deps)
    return dict(tag=tag, n=n, send=send, recv=recv, bufs=bufs, plan=plan), token


def _gather_forward(st, after):
    n, tag = st["n"], st["tag"]
    bufs = _copy_wait(tag + "_ici_wait", st["send"], st["recv"], st["bufs"], st["plan"], after)
    plan = _gather_plan_d2d(n)
    send, recv, lands, token = _copy_start(tag + "_d2d_start", bufs[n:], plan, 3 * n)
    return dict(tag=tag, n=n, send=send, recv=recv, bufs=lands, plan=plan), token


def _gather_finish(st, after):
    lands = _copy_wait(st["tag"] + "_d2d_wait", st["send"], st["recv"], st["bufs"], st["plan"], after)
    return [a.reshape((N_DEV * a.shape[1],) + a.shape[2:]) for a in lands]


def _scatter_start(tag, parts, deps=()):
    n = len(parts)
    p4 = [a.reshape(4, 2, a.shape[0] // N_DEV, a.shape[1]) for a in parts]
    lands = [lax.empty((4,) + a.shape[2:], BF) for a in p4]
    plan = _scatter_plan_d2d(n)
    send, recv, bufs, token = _copy_start(tag + "_d2d_start", p4 + lands, plan, 4 * n, deps=deps)
    return dict(tag=tag, n=n, send=send, recv=recv, bufs=bufs, plan=plan), token


def _scatter_middle(st, core, after):
    n, tag = st["n"], st["tag"]
    bufs = _copy_wait(tag + "_d2d_wait", st["send"], st["recv"], st["bufs"], st["plan"], after)
    sums = _pair_add(tag + "_pair_add", core, bufs[:n], bufs[n:])
    lands = [lax.empty(a.shape, BF) for a in sums]
    plan, local = _scatter_plan_ici(n)
    send, recv, bufs, token = _copy_start(tag + "_ici_start", sums + lands, plan, 3 * n, local=local)
    return dict(tag=tag, n=n, send=send, recv=recv, bufs=bufs, plan=plan), token


def _scatter_finish(st, after):
    bufs = _copy_wait(st["tag"] + "_ici_wait", st["send"], st["recv"], st["bufs"], st["plan"], after)
    return bufs[st["n"]:]


def _pack_rows(flat_list, width):
    flat = jnp.concatenate([a.reshape(-1) for a in flat_list])
    rows = -(-flat.shape[0] // (8 * width)) * 8
    return jnp.pad(flat, (0, rows * width - flat.shape[0])).reshape(rows, width)


def _unpack_rows(packed, like):
    flat, out, pos = packed.reshape(-1), [], 0
    for a in like:
        out.append(flat[pos:pos + a.size].reshape(a.shape))
        pos += a.size
    return out


def kernel(x, p, ffn1_pre_g, ffn1_w_gate, ffn1_w_up, ffn1_w_down, ffn1_post_g, mix_pre_g, w_in, sgu_norm_g, sgu_w, sgu_b, pool_w, pool_scale, w_out_a, w_out_b, w_o, mix_post_g, ffn2_pre_g, ffn2_w_gate, ffn2_w_up, ffn2_w_down, ffn2_post_g, ple_pre_g, ple_w_gate, ple_w_proj, ple_post_g, loss_target, m_ffn1_pre_g, m_ffn1_w_gate, m_ffn1_w_up, m_ffn1_w_down, m_ffn1_post_g, m_mix_pre_g, m_w_in, m_sgu_norm_g, m_sgu_w, m_sgu_b, m_pool_w, m_pool_scale, m_w_out_a, m_w_out_b, m_w_o, m_mix_post_g, m_ffn2_pre_g, m_ffn2_w_gate, m_ffn2_w_up, m_ffn2_w_down, m_ffn2_post_g, m_ple_pre_g, m_ple_w_gate, m_ple_w_proj, m_ple_post_g, v_ffn1_pre_g, v_ffn1_w_gate, v_ffn1_w_up, v_ffn1_w_down, v_ffn1_post_g, v_mix_pre_g, v_w_in, v_sgu_norm_g, v_sgu_w, v_sgu_b, v_pool_w, v_pool_scale, v_w_out_a, v_w_out_b, v_w_o, v_mix_post_g, v_ffn2_pre_g, v_ffn2_w_gate, v_ffn2_w_up, v_ffn2_w_down, v_ffn2_post_g, v_ple_pre_g, v_ple_w_gate, v_ple_w_proj, v_ple_post_g):
    args = dict(locals())
    names = ["ffn1_pre_g", "ffn1_w_gate", "ffn1_w_up", "ffn1_w_down", "ffn1_post_g", "mix_pre_g", "w_in",
             "sgu_norm_g", "sgu_w", "sgu_b", "pool_w", "pool_scale", "w_out_a", "w_out_b", "w_o", "mix_post_g",
             "ffn2_pre_g", "ffn2_w_gate", "ffn2_w_up", "ffn2_w_down", "ffn2_post_g", "ple_pre_g", "ple_w_gate",
             "ple_w_proj", "ple_post_g"]
    w = {k: args[k][0] for k in names}
    mom = {k: args["m_" + k][0] for k in names}
    var = {k: args["v_" + k][0] for k in names}

    assert x.shape[0] == 1 and p.shape[:2] == (1, 1)
    t, d = x.shape[1], x.shape[2]
    f = ffn1_w_gate.shape[2] * N_DEV
    d_in = w_in.shape[2] * N_DEV
    d_ple = p.shape[3]
    n_pool = len(POOL_WINDOWS)
    dgp = d // n_pool
    assert d_in == 5 * d and t % CHUNK == 0
    tm = _tile(t, 512, CHUNK)
    xs, ps, target = x[0], p[0, 0], loss_target[0]

    col_sharded = ("ffn1_w_gate", "ffn1_w_up", "ffn2_w_gate", "ffn2_w_up", "w_in", "ple_w_proj")
    groups = [["ffn1_w_gate", "ffn1_w_up", "ffn1_w_down"],
              ["w_in", "pool_w", "w_out_a", "w_out_b", "w_o"],
              ["ffn2_w_gate", "ffn2_w_up", "ffn2_w_down", "ple_w_gate", "ple_w_proj"]]

    def shard_of(k):
        if k in col_sharded:
            return w[k].T.astype(BF)
        if k == "pool_w":
            return w[k].reshape(-1, dgp).astype(BF)
        return w[k].astype(BF)

    rows_pw = dgp // N_DEV
    b_col = w["sgu_b"][:, :, None]
    gains = {k: w[k][None, :] for k in names if w[k].ndim == 1}
    core = lax.axis_index("c").astype(jnp.int32).reshape(1)
    full = {}

    ag1, tok = _gather_start("gather1", [shard_of(k) for k in groups[0]])
    xn1 = _prenorm("ffn1_prenorm", xs, gains["ffn1_pre_g"], t, d, tm, deps=[tok])
    ag1, tok = _gather_forward(ag1, xn1)
    ag2, tok = _gather_start("gather2", [shard_of(k) for k in groups[1]], deps=[tok])
    full.update(zip(groups[0], _gather_finish(ag1, tok)))
    g1, u1, hid1, f1, h1 = _ffn_fwd("ffn1", xs, xn1, full["ffn1_w_gate"], full["ffn1_w_up"], full["ffn1_w_down"],
                                    gains["ffn1_post_g"], t, d, f, tm)

    ag2, tok = _gather_forward(ag2, h1)
    ag3, tok = _gather_start("gather3", [shard_of(k) for k in groups[2]], deps=[tok])
    xn2 = _prenorm("mix_prenorm", h1, gains["mix_pre_g"], t, d, tm, deps=[tok])
    full.update(zip(groups[1], _gather_finish(ag2, xn2)))
    pool_full = full["pool_w"].reshape(N_DEV, n_pool, rows_pw, dgp).transpose(1, 0, 2, 3).reshape(n_pool, dgp, dgp)
    z = _matmul("mix_in", [dict(a=xn2, b=full["w_in"], mode="nt", acc=0, tk=d)], m=t, n=d_in, tm=tm, tn=d,
                outs=[("mn", (t, d_in), F32, 0)], epilogue=lambda accs, ex: accs, n_acc=1, j_outer=True)[0]
    a_br = _sgu_fwd(z, gains["sgu_norm_g"], w["sgu_w"], b_col, t, d, tm)
    b_br = _pool_fwd(z, pool_full, gains["pool_scale"], t, d)

    def merge_epi(accs, ex):
        ya, yb = accs
        ga, gb = ex
        return [ya, yb, _sig(ga) * ya + _sig(gb) * yb]

    ya, yb, y = _matmul(
        "mix_merge",
        [dict(a=a_br, b=full["w_out_a"], mode="nn", acc=0, tk=d), dict(a=b_br, b=full["w_out_b"], mode="nn", acc=1, tk=d)],
        m=t, n=d, tm=tm, tn=d, extras=[("mn", z, 3), ("mn", z, 4)],
        outs=[("mn", (t, d), BF, 0)] * 3, epilogue=merge_epi, n_acc=2)

    def proj_epi(accs, ex):
        mm = accs[0]
        hh, gg = ex
        return [mm, hh + mm * _rms_r(mm) * gg]

    ag3, tok = _gather_forward(ag3, y)
    m_out, h2 = _matmul(
        "mix_proj", [dict(a=y, b=full["w_o"], mode="nn", acc=0, tk=d)], m=t, n=d, tm=tm, tn=d,
        extras=[("mn", h1, 0), ("n", gains["mix_post_g"])],
        outs=[("mn", (t, d), F32, 0)] * 2, epilogue=proj_epi, n_acc=1, deps=[tok])
    full.update(zip(groups[2], _gather_finish(ag3, h2)))

    xn3 = _prenorm("ffn2_prenorm", h2, gains["ffn2_pre_g"], t, d, tm)
    g2, u2, hid2, f2, h3 = _ffn_fwd("ffn2", h2, xn3, full["ffn2_w_gate"], full["ffn2_w_up"], full["ffn2_w_down"],
                                    gains["ffn2_post_g"], t, d, f, tm)

    xn4 = _prenorm("ple_prenorm", h3, gains["ple_pre_g"], t, d, tm)

    def ple_epi(accs, ex):
        gl, e = accs
        hh, tgt, gg = ex
        q = _sig(gl) * e
        err = hh + q * _rms_r(q) * gg - tgt
        return [gl, e, err * (1.0 / d), jnp.sum(err * err, axis=0, keepdims=True)]

    gl, e_ple, dh4, loss_vec = _matmul(
        "ple_fwd",
        [dict(a=xn4, b=full["ple_w_gate"], mode="nn", acc=0, tk=d), dict(a=ps, b=full["ple_w_proj"], mode="nt", acc=1, tk=d_ple)],
        m=t, n=d, tm=tm, tn=d, extras=[("mn", h3, 0), ("mn", target, 0), ("n", gains["ple_post_g"])],
        outs=[("mn", (t, d), F32, 0)] * 3 + [("acc", (1, d), F32)], epilogue=ple_epi, n_acc=2)
    loss = lax.psum(jnp.sum(loss_vec) * (0.5 / d), AXES)

    tk_t = _tile(t, 512)

    def ple_post_fn(v):
        dy, gl_v, e_v, gg = v
        s = _sig(gl_v)
        dq, dg = _rms_bwd(s * e_v, gg, dy)
        return [dq * e_v * s * (1.0 - s), dq * s, dg]

    dgl, de, dg_ple_post = _rowwise(
        "ple_post_bwd", [("row", dh4, d, 0), ("row", gl, d, 0), ("row", e_ple, d, 0), ("full", gains["ple_post_g"])],
        [("row", (t, d), BF, d, 0), ("row", (t, d), BF, d, 0), ("acc", (1, d), F32)], ple_post_fn, t=t, tm=tm)
    ident = lambda accs, ex: accs
    dw_ple_gate = _matmul("ple_dwg", [dict(a=xn4, b=dgl, mode="tn", acc=0, tk=tk_t)], m=d, n=d, tm=d, tn=d,
                          nk=t // tk_t, outs=[("mn", (d, d), BF, 0)], epilogue=ident, n_acc=1)[0]
    dw_ple_proj_t = _matmul("ple_dwp", [dict(a=de, b=ps, mode="tn", acc=0, tk=tk_t)], m=d, n=d_ple, tm=d, tn=d_ple,
                            nk=t // tk_t, outs=[("mn", (d, d_ple), BF, 0)], epilogue=ident, n_acc=1)[0]
    dh3, dg_ple_pre = _matmul(
        "ple_dx", [dict(a=dgl, b=full["ple_w_gate"], mode="nt", acc=0, tk=d)], m=t, n=d, tm=tm, tn=d,
        extras=[("mn", h3, 0), ("mn", dh4, 0), ("n", gains["ple_pre_g"])],
        outs=[("mn", (t, d), F32, 0), ("acc", (1, d), F32)], epilogue=_prenorm_bwd_epi, n_acc=1)

    dgate2, dup2, dg_f2_post, dwg2, dwu2, dwd2 = _ffn_bwd_weights(
        "ffn2", dh3, xn3, g2, u2, hid2, f2, full["ffn2_w_down"], gains["ffn2_post_g"], t, d, f, tm)
    rs_names = [["ple_w_gate", "ple_w_proj", "ffn2_w_down", "ffn2_w_gate", "ffn2_w_up"],
                ["w_o", "w_out_a", "w_out_b", "pool_w", "w_in"],
                ["ffn1_w_down", "ffn1_w_gate", "ffn1_w_up"]]
    rs1, tok = _scatter_start("scatter1", [dw_ple_gate, dw_ple_proj_t, dwd2, dwg2, dwu2])
    dh2, dg_f2_pre = _ffn_bwd_dx("ffn2", dh3, h2, dgate2, dup2, full["ffn2_w_gate"], full["ffn2_w_up"],
                                 gains["ffn2_pre_g"], t, d, f, tm, deps=[tok])
    rs1, tok = _scatter_middle(rs1, core, dh2)

    dm, dg_mix_post = _postnorm_bwd("mix_post_bwd", dh2, m_out, gains["mix_post_g"], 1.0, t, d, tm, deps=[tok])

    dyab, dz = _mix_dy(dm, full["w_o"], z, ya, yb, t, d, tm)
    dw_o = _matmul("mix_dwo", [dict(a=y, b=dm, mode="tn", acc=0, tk=tk_t)], m=d, n=d, tm=d, tn=d, nk=t // tk_t,
                   outs=[("mn", (d, d), BF, 0)], epilogue=ident, n_acc=1)[0]
    da, db = _matmul(
        "mix_dab",
        [dict(a=dyab, b=full["w_out_a"], mode="nt", acc=0, tk=d),
         dict(a=dyab, b=full["w_out_b"], mode="nt", acc=1, tk=d, a_off=(0, 1))],
        m=t, n=d, tm=tm, tn=d, outs=[("mn", (t, d), BF, 0)] * 2, epilogue=ident, n_acc=2)
    dw_out_a, dw_out_b = _matmul(
        "mix_dwab",
        [dict(a=a_br, b=dyab, mode="tn", acc=0, tk=tk_t),
         dict(a=b_br, b=dyab, mode="tn", acc=1, tk=tk_t, b_off=(0, 1))],
        m=d, n=d, tm=d, tn=d, nk=t // tk_t, outs=[("mn", (d, d), BF, 0)] * 2, epilogue=ident, n_acc=2)
    dz, dsgu_w, dsgu_b, dg_sgu = _sgu_bwd(z, da, dz, gains["sgu_norm_g"], w["sgu_w"], b_col, t, d, _tile(t, 256, CHUNK))
    dz, dpool_w, dg_pool = _pool_bwd(z, db, dz, pool_full, gains["pool_scale"], t, d)
    dw_in_t = _matmul("mix_dwin", [dict(a=dz, b=xn2, mode="tn", acc=0, tk=tk_t)], m=d_in, n=d, tm=d, tn=d,
                      nk=t // tk_t, outs=[("mn", (d_in, d), BF, 0)], epilogue=ident, n_acc=1)[0]
    dpool_rows = dpool_w.reshape(n_pool, N_DEV, rows_pw, dgp).transpose(1, 0, 2, 3).reshape(N_DEV * n_pool * rows_pw, dgp)
    rs2, tok = _scatter_start("scatter2", [dw_o, dw_out_a, dw_out_b, dpool_rows, dw_in_t])
    dh1, dg_mix_pre = _matmul(
        "mix_dx", [dict(a=dz, b=full["w_in"], mode="nn", acc=0, tk=d)], m=t, n=d, tm=tm, tn=d, nk=d_in // d,
        extras=[("mn", h1, 0), ("mn", dh2, 0), ("n", gains["mix_pre_g"])],
        outs=[("mn", (t, d), F32, 0), ("acc", (1, d), F32)], epilogue=_prenorm_bwd_epi, n_acc=1, deps=[tok])
    rs2, tok = _scatter_middle(rs2, core, dh1)

    dgate1, dup1, dg_f1_post, dwg1, dwu1, dwd1 = _ffn_bwd_weights(
        "ffn1", dh1, xn1, g1, u1, hid1, f1, full["ffn1_w_down"], gains["ffn1_post_g"], t, d, f, tm, deps=[tok])
    rs3, tok = _scatter_start("scatter3", [dwd1, dwg1, dwu1])
    grad_x, dg_f1_pre = _ffn_bwd_dx("ffn1", dh1, xs, dgate1, dup1, full["ffn1_w_gate"], full["ffn1_w_up"],
                                    gains["ffn1_pre_g"], t, d, f, tm, deps=[tok])
    rs3, tok = _scatter_middle(rs3, core, grad_x)

    grad, delta, new_m, new_v = {}, {}, {}, {}

    def update(group, staged):
        done = []
        for k, st in zip(group, staged):
            if k in col_sharded:
                grad[k] = _reduce_only("sum_" + k, st).T
                delta[k], new_m[k], new_v[k] = _adam_only("adam_" + k, w[k], grad[k], mom[k], var[k])
            else:
                shape = w[k].shape
                flat = (-1, shape[-1])
                res = _reduce_adam("adam_" + k, st, w[k].reshape(flat), mom[k].reshape(flat), var[k].reshape(flat))
                grad[k], delta[k], new_m[k], new_v[k] = [r.reshape(shape) for r in res]
            done.append(delta[k])
        return done

    small_grads = {
        "ffn1_pre_g": dg_f1_pre, "ffn1_post_g": dg_f1_post, "mix_pre_g": dg_mix_pre, "sgu_norm_g": dg_sgu,
        "sgu_w": dsgu_w, "sgu_b": dsgu_b, "pool_scale": dg_pool, "mix_post_g": dg_mix_post, "ffn2_pre_g": dg_f2_pre,
        "ffn2_post_g": dg_f2_post, "ple_pre_g": dg_ple_pre, "ple_post_g": dg_ple_post,
    }
    small_names = [k for k in names if k in small_grads]
    small_packed = _pack_rows([small_grads[k] for k in small_names], d)
    plan_s, local_s = _broadcast_plan()
    send_s, recv_s, bufs_s, tok = _copy_start(
        "small_start", [small_packed, lax.empty((N_DEV,) + small_packed.shape, F32)], plan_s, N_DEV - 1,
        local=local_s, deps=[tok])

    done = update(rs_names[0], _scatter_finish(rs1, tok))
    done = update(rs_names[1], _scatter_finish(rs2, done))
    staged_small = _copy_wait("small_wait", send_s, recv_s, bufs_s, plan_s, done)[1]
    small_like = [w[k] for k in small_names]
    res = _reduce_adam(
        "adam_small", staged_small, _pack_rows(small_like, d), _pack_rows([mom[k] for k in small_names], d),
        _pack_rows([var[k] for k in small_names], d))
    for dst, packed in zip((grad, delta, new_m, new_v), res):
        dst.update(zip(small_names, _unpack_rows(packed, small_like)))
    update(rs_names[2], _scatter_finish(rs3, res[1]))

    out = [loss, grad_x[None]]
    for group in (grad, delta, new_m, new_v):
        out += [group[k][None] for k in names]
    return tuple(out)
```

```python
import math

import jax
import jax.numpy as jnp
from jax import lax
from jax.experimental import pallas as pl
from jax.experimental.pallas import tpu as pltpu

EPS = 1e-6
CHUNK = 128
N_SGU_GROUPS = 4
POOL_WINDOWS = (2, 4, 8, 16)
ADAM_LR = 0.001
ADAM_B1 = 0.9
ADAM_B2 = 0.999
ADAM_EPS = 1e-08
ADAM_WD = 0.01
ADAM_STEP = 10

N_DEV = 8
AXES = ("x", "y", "c")
MESH = pl.DeviceIdType.MESH
V7X_VMEM_BYTES = 64 * 1024 * 1024
VMEM_LIMIT = V7X_VMEM_BYTES - 8 * 1024 * 1024
BF = jnp.bfloat16
F32 = jnp.float32

_DN = {
    "nn": (((1,), (0,)), ((), ())),
    "nt": (((1,), (1,)), ((), ())),
    "tn": (((0,), (0,)), ((), ())),
}


def _params(sem=None):
    return pltpu.CompilerParams(dimension_semantics=sem, vmem_limit_bytes=VMEM_LIMIT)


def _tile(n, target, align=128):
    t = min(n, target)
    t -= t % align
    while t >= align:
        if n % t == 0:
            return t
        t -= align
    return n


def _sig(x):
    return 1.0 / (1.0 + jnp.exp(-x))


_GELU_K = math.sqrt(2.0 / math.pi)
_GELU_C = 0.044715


def _gelu(x):
    return 0.5 * x * (1.0 + jnp.tanh(_GELU_K * (x + _GELU_C * x * x * x)))


def _gelu_grad(x):
    t = jnp.tanh(_GELU_K * (x + _GELU_C * x * x * x))
    return 0.5 * (1.0 + t) + 0.5 * x * (1.0 - t * t) * _GELU_K * (1.0 + 3.0 * _GELU_C * x * x)


def _rms_r(x):
    return lax.rsqrt(jnp.mean(x * x, axis=-1, keepdims=True) + EPS)


def _rms_bwd(x, g, dy):
    r = _rms_r(x)
    xh = x * r
    gy = dy * g
    dx = r * (gy - xh * jnp.mean(xh * gy, axis=-1, keepdims=True))
    return dx, jnp.sum(dy * xh, axis=0, keepdims=True)


def _matmul(name, mats, *, m, n, tm, tn, nk=1, extras=(), outs, epilogue, n_acc, j_outer=False, deps=(), carry=None):
    ni, nj = m // tm, n // tn
    assert ni * tm == m and nj * tn == n, (name, m, n, tm, tn)
    if j_outer:
        grid = (nj, ni, nk)

        def ij(g0, g1):
            return g1, g0
    else:
        grid = (ni, nj, nk)

        def ij(g0, g1):
            return g0, g1

    in_specs, args = [], []
    for mt in mats:
        mode, tk = mt["mode"], mt["tk"]
        ao, bo = mt.get("a_off", (0, 0)), mt.get("b_off", (0, 0))
        if mode == "tn":
            sa = pl.BlockSpec((tk, tm), lambda g0, g1, kk, ao=ao: (ao[0] + kk, ao[1] + ij(g0, g1)[0]))
        else:
            sa = pl.BlockSpec((tm, tk), lambda g0, g1, kk, ao=ao: (ao[0] + ij(g0, g1)[0], ao[1] + kk))
        if mode == "nt":
            sb = pl.BlockSpec((tn, tk), lambda g0, g1, kk, bo=bo: (bo[0] + ij(g0, g1)[1], bo[1] + kk))
        else:
            sb = pl.BlockSpec((tk, tn), lambda g0, g1, kk, bo=bo: (bo[0] + kk, bo[1] + ij(g0, g1)[1]))
        in_specs += [sa, sb]
        args += [mt["a"], mt["b"]]
    n_mat_refs = len(args)
    for ex in extras:
        if ex[0] == "mn":
            in_specs.append(pl.BlockSpec((tm, tn), lambda g0, g1, kk, c=ex[2]: (ij(g0, g1)[0], c + ij(g0, g1)[1])))
        else:
            in_specs.append(pl.BlockSpec((1, tn), lambda g0, g1, kk: (0, ij(g0, g1)[1])))
        args.append(ex[1])
    n_ex_end = len(args)
    in_specs += [pl.BlockSpec(memory_space=pl.ANY)] * len(deps)
    args += list(deps)
    n_buf_start = len(args)
    bufs = list(carry["bufs"]) if carry else []
    nb = len(bufs)
    in_specs += [_HBM] * nb
    args += [pltpu.with_memory_space_constraint(b, pltpu.HBM) for b in bufs]
    n_in = len(args)
    out_specs, out_shape = [], []
    for o in outs:
        if o[0] == "mn":
            out_specs.append(pl.BlockSpec((tm, tn), lambda g0, g1, kk, c=o[3]: (ij(g0, g1)[0], c + ij(g0, g1)[1])))
        else:
            assert nj == 1, name
            out_specs.append(pl.BlockSpec((1, tn), lambda g0, g1, kk: (0, 0)))
        out_shape.append(jax.ShapeDtypeStruct(o[1], o[2]))
    n_out = len(outs)
    aliases = {}
    scratch = [pltpu.VMEM((tm, tn), F32) for _ in range(n_acc)] if nk > 1 else []
    n_acc_refs = len(scratch)
    if carry:
        out_specs += [_SEM, _SEM] + [_HBM] * nb
        out_shape += [pltpu.SemaphoreType.DMA((carry["n_copies"],)), pltpu.SemaphoreType.DMA((carry["n_copies"],))]
        out_shape += [pltpu.HBM(b.shape, b.dtype) for b in bufs]
        aliases = {n_buf_start + k: n_out + 2 + k for k in range(nb)}
        if carry["local"] is not None:
            scratch.append(pltpu.SemaphoreType.DMA((nb // 2,)))

    def body(*refs):
        mat_refs = refs[:n_mat_refs]
        ex_refs = refs[n_mat_refs:n_ex_end]
        out_refs = refs[n_in:n_in + n_out]
        n_all_out = n_out + (2 + nb if carry else 0)
        acc_refs = refs[n_in + n_all_out:n_in + n_all_out + n_acc_refs]
        i = ij(pl.program_id(0), pl.program_id(1))[0]
        kk = pl.program_id(2)

        if carry:
            @pl.when((pl.program_id(0) == 0) & (pl.program_id(1) == 0) & (kk == 0))
            def _():
                buf_refs = refs[n_buf_start:n_buf_start + nb]
                if carry["local"] is not None:
                    local_sems = refs[n_in + n_all_out + n_acc_refs]
                    mine = [pltpu.make_async_copy(a, b, local_sems.at[k])
                            for k, (a, b) in enumerate(carry["local"](buf_refs))]
                    for cp in mine:
                        cp.start()
                    for cp in mine:
                        cp.wait()
                for cp in carry["plan"](buf_refs, refs[n_in + n_out], refs[n_in + n_out + 1]):
                    cp.start()

        def products():
            res = [None] * n_acc
            for idx, mt in enumerate(mats):
                a = mat_refs[2 * idx][...].astype(BF)
                b = mat_refs[2 * idx + 1][...].astype(BF)
                p = lax.dot_general(a, b, _DN[mt["mode"]], preferred_element_type=F32)
                q = mt["acc"]
                res[q] = p if res[q] is None else res[q] + p
            return res

        def finish(accs):
            vals = epilogue(accs, [r[...] for r in ex_refs])
            for o, r, v in zip(outs, out_refs, vals):
                if o[0] == "mn":
                    r[...] = v.astype(o[2])
                else:
                    @pl.when(i == 0)
                    def _():
                        r[...] = v

                    @pl.when(i != 0)
                    def _():
                        r[...] += v

        if nk == 1:
            finish(products())
        else:
            res = products()

            @pl.when(kk == 0)
            def _():
                for q in range(n_acc):
                    acc_refs[q][...] = res[q]

            @pl.when(kk != 0)
            def _():
                for q in range(n_acc):
                    acc_refs[q][...] += res[q]

            @pl.when(kk == nk - 1)
            def _():
                finish([r[...] for r in acc_refs])

    res = pl.pallas_call(
        body,
        name=name,
        grid=grid,
        in_specs=in_specs,
        out_specs=out_specs,
        out_shape=out_shape,
        scratch_shapes=scratch,
        input_output_aliases=aliases,
        compiler_params=pltpu.CompilerParams(
            dimension_semantics=("arbitrary", "arbitrary", "arbitrary"), vmem_limit_bytes=VMEM_LIMIT,
            has_side_effects=_EFFECT if carry else False),
    )(*args)
    if not carry:
        return res
    state = dict(tag=name, send=res[n_out], recv=res[n_out + 1], bufs=list(res[n_out + 2:]), plan=carry["plan"])
    return list(res[:n_out]) + [state]


def _rowwise(name, ins, outs, fn, *, t, tm, deps=()):
    ni = t // tm
    assert ni * tm == t, (name, t, tm)
    in_specs, args = [], []
    for s in ins:
        if s[0] == "row":
            in_specs.append(pl.BlockSpec((tm, s[2]), lambda i, c=s[3]: (i, c)))
        else:
            nd = s[1].ndim
            in_specs.append(pl.BlockSpec(s[1].shape, lambda i, nd=nd: (0,) * nd))
        args.append(s[1])
    out_specs, out_shape = [], []
    for o in outs:
        if o[0] == "row":
            out_specs.append(pl.BlockSpec((tm, o[3]), lambda i, c=o[4]: (i, c)))
        else:
            nd = len(o[1])
            out_specs.append(pl.BlockSpec(o[1], lambda i, nd=nd: (0,) * nd))
        out_shape.append(jax.ShapeDtypeStruct(o[1], o[2]))
    n_read = len(args)
    in_specs += [pl.BlockSpec(memory_space=pl.ANY)] * len(deps)
    args += list(deps)
    n_in = len(args)

    def body(*refs):
        i = pl.program_id(0)
        vals = fn([r[...] for r in refs[:n_read]])
        for o, r, v in zip(outs, refs[n_in:], vals):
            if o[0] == "row":
                r[...] = v.astype(o[2])
            else:
                @pl.when(i == 0)
                def _():
                    r[...] = v

                @pl.when(i != 0)
                def _():
                    r[...] += v

    return pl.pallas_call(
        body,
        name=name,
        grid=(ni,),
        in_specs=in_specs,
        out_specs=out_specs,
        out_shape=out_shape,
        compiler_params=_params(("arbitrary",)),
    )(*args)


def _place():
    return lax.axis_index("x"), lax.axis_index("y"), lax.axis_index("c")


_HBM = pl.BlockSpec(memory_space=pltpu.HBM)
_SEM = pl.BlockSpec(memory_space=pltpu.SEMAPHORE)
_ANY = pl.BlockSpec(memory_space=pl.ANY)
_EFFECT = pltpu.SideEffectType.DATAFLOW_SIDE_EFFECTING


def _remote(src, dst, send_sems, recv_sems, k, to):
    return pltpu.make_async_remote_copy(
        src_ref=src, dst_ref=dst, send_sem=send_sems.at[k], recv_sem=recv_sems.at[k],
        device_id=to, device_id_type=MESH)


def _copy_start(name, bufs, plan, n_copies, local=None, deps=()):
    nb, nd = len(bufs), len(deps)
    n_local = 0 if local is None else nb // 2

    def body(*refs):
        buf_refs = refs[:nb]
        send_sems, recv_sems = refs[nb + nd], refs[nb + nd + 1]
        token = refs[nb + nd + 2 + nb]
        if local is not None:
            local_sems = refs[nb + nd + 3 + nb]
            mine = [pltpu.make_async_copy(a, b, local_sems.at[k]) for k, (a, b) in enumerate(local(buf_refs))]
            for cp in mine:
                cp.start()
            for cp in mine:
                cp.wait()
        for cp in plan(buf_refs, send_sems, recv_sems):
            cp.start()
        token[...] = jnp.zeros_like(token)

    res = pl.pallas_call(
        body,
        name=name,
        in_specs=[_HBM] * nb + [_ANY] * nd,
        out_specs=[_SEM, _SEM] + [_HBM] * nb + [pl.BlockSpec(memory_space=pltpu.VMEM)],
        out_shape=[pltpu.SemaphoreType.DMA((n_copies,)), pltpu.SemaphoreType.DMA((n_copies,))]
        + [pltpu.HBM(b.shape, b.dtype) for b in bufs] + [jax.ShapeDtypeStruct((8, 128), F32)],
        scratch_shapes=[pltpu.SemaphoreType.DMA((n_local,))] if n_local else [],
        input_output_aliases={i: 2 + i for i in range(nb)},
        compiler_params=pltpu.CompilerParams(has_side_effects=_EFFECT),
    )(*[pltpu.with_memory_space_constraint(b, pltpu.HBM) for b in bufs], *deps)
    return res[0], res[1], list(res[2:2 + nb]), res[2 + nb]


def _copy_wait(name, send_sems, recv_sems, bufs, plan, after):
    nb = len(bufs)
    after = list(after) if isinstance(after, (list, tuple)) else [after]

    def body(*refs):
        for cp in plan(refs[:nb], refs[nb], refs[nb + 1]):
            cp.wait_send()
            cp.wait_recv()

    return list(pl.pallas_call(
        body,
        name=name,
        in_specs=[_HBM] * nb + [_SEM, _SEM] + [_ANY] * len(after),
        out_specs=[_HBM] * nb,
        out_shape=[pltpu.HBM(b.shape, b.dtype) for b in bufs],
        input_output_aliases={i: i for i in range(nb)},
        compiler_params=pltpu.CompilerParams(has_side_effects=_EFFECT),
    )(*bufs, send_sems, recv_sems, *after))


def _chips(x, y):
    return [(1 - x, y), (x, 1 - y), (1 - x, 1 - y)]


def _gather_plan_ici(n):
    def plan(refs, send_sems, recv_sems):
        x, y, c = _place()
        me = 4 * x + 2 * y + c
        targets = [(x, y, 1 - c)] + [(*chip, c) for chip in _chips(x, y)]
        return [_remote(refs[i], refs[n + i].at[me], send_sems, recv_sems, 4 * i + k, to)
                for i in range(n) for k, to in enumerate(targets)]

    def local(refs):
        x, y, c = _place()
        return [(refs[i], refs[n + i].at[4 * x + 2 * y + c]) for i in range(n)]

    return plan, local


def _gather_plan_d2d(n):
    def plan(refs, send_sems, recv_sems):
        x, y, c = _place()
        cps = []
        for i in range(n):
            for j, (cx, cy) in enumerate(_chips(x, y)):
                block = refs[i].at[4 * cx + 2 * cy + c]
                cps.append(_remote(block, block, send_sems, recv_sems, 3 * i + j, (x, y, 1 - c)))
        return cps

    return plan


def _scatter_plan_d2d(n):
    def plan(refs, send_sems, recv_sems):
        x, y, c = _place()
        return [_remote(refs[i].at[q, 1 - c], refs[n + i].at[q], send_sems, recv_sems, 4 * i + q, (x, y, 1 - c))
                for i in range(n) for q in range(4)]

    return plan


def _scatter_plan_ici(n):
    def plan(refs, send_sems, recv_sems):
        x, y, c = _place()
        return [_remote(refs[i].at[2 * cx + cy], refs[n + i].at[2 * x + y], send_sems, recv_sems, 3 * i + j, (cx, cy, c))
                for i in range(n) for j, (cx, cy) in enumerate(_chips(x, y))]

    def local(refs):
        x, y, _ = _place()
        return [(refs[i].at[2 * x + y], refs[n + i].at[2 * x + y]) for i in range(n)]

    return plan, local


def _broadcast_plan():
    def plan(refs, send_sems, recv_sems):
        x, y, c = _place()
        cps = []
        for dd in range(1, N_DEV):
            peer = (1 - x if dd & 4 else x, 1 - y if dd & 2 else y, 1 - c if dd & 1 else c)
            cps.append(_remote(refs[0], refs[1].at[4 * x + 2 * y + c], send_sems, recv_sems, dd - 1, peer))
        return cps

    def local(refs):
        x, y, c = _place()
        return [(refs[0], refs[1].at[4 * x + 2 * y + c])]

    return plan, local


def _pair_add(name, core, parts, landed):
    n = len(parts)

    def body(core_ref, *refs):
        del core_ref
        for i in range(n):
            refs[2 * n + i][...] = (refs[i][...].astype(F32) + refs[n + i][...].astype(F32)).astype(BF)

    in_specs = [pl.BlockSpec((None, None) + p.shape[2:], lambda q, core_ref: (q, core_ref[0], 0, 0)) for p in parts]
    in_specs += [pl.BlockSpec((None,) + a.shape[1:], lambda q, core_ref: (q, 0, 0)) for a in landed]
    return list(pl.pallas_call(
        body,
        name=name,
        grid_spec=pltpu.PrefetchScalarGridSpec(
            num_scalar_prefetch=1, grid=(4,), in_specs=in_specs,
            out_specs=[pl.BlockSpec((None,) + a.shape[1:], lambda q, core_ref: (q, 0, 0)) for a in landed]),
        out_shape=[jax.ShapeDtypeStruct(a.shape, BF) for a in landed],
        compiler_params=_params(("arbitrary",)),
    )(core, *parts, *landed))


def _adam_vals(w, g, m, v):
    m = ADAM_B1 * m + (1.0 - ADAM_B1) * g
    v = ADAM_B2 * v + (1.0 - ADAM_B2) * (g * g)
    m_hat = m / (1.0 - ADAM_B1 ** ADAM_STEP)
    v_hat = v / (1.0 - ADAM_B2 ** ADAM_STEP)
    delta = -ADAM_LR * (m_hat / (jnp.sqrt(v_hat) + ADAM_EPS) + ADAM_WD * w)
    return delta, m, v


def _sum_parts(st_ref):
    g = st_ref[0].astype(F32)
    for k in range(1, st_ref.shape[0]):
        g = g + st_ref[k].astype(F32)
    return g


def _reduce_adam(name, st, w, m, v):
    rows, cols = w.shape
    tr = _tile(rows, 512, 16)

    def body(st_ref, w_ref, m_ref, v_ref, g_out, d_out, m_out, v_out):
        g = _sum_parts(st_ref)
        d, mm, vv = _adam_vals(w_ref[...], g, m_ref[...], v_ref[...])
        g_out[...] = g
        d_out[...] = d
        m_out[...] = mm
        v_out[...] = vv

    blk = pl.BlockSpec((tr, cols), lambda i: (i, 0))
    return pl.pallas_call(
        body,
        name=name,
        grid=(rows // tr,),
        in_specs=[pl.BlockSpec((st.shape[0], tr, cols), lambda i: (0, i, 0)), blk, blk, blk],
        out_specs=[blk] * 4,
        out_shape=[jax.ShapeDtypeStruct(w.shape, F32)] * 4,
        compiler_params=_params(("arbitrary",)),
    )(st, w, m, v)


def _reduce_only(name, st):
    _, rows, cols = st.shape
    tr = _tile(rows, 512, 16)

    def body(st_ref, g_out):
        g_out[...] = _sum_parts(st_ref)

    return pl.pallas_call(
        body,
        name=name,
        grid=(rows // tr,),
        in_specs=[pl.BlockSpec((st.shape[0], tr, cols), lambda i: (0, i, 0))],
        out_specs=pl.BlockSpec((tr, cols), lambda i: (i, 0)),
        out_shape=jax.ShapeDtypeStruct((rows, cols), F32),
        compiler_params=_params(("arbitrary",)),
    )(st)


def _adam_only(name, w, g, m, v):
    rows, cols = w.shape
    tr = _tile(rows, 512, 16)

    def body(w_ref, g_ref, m_ref, v_ref, d_out, m_out, v_out):
        d, mm, vv = _adam_vals(w_ref[...], g_ref[...], m_ref[...], v_ref[...])
        d_out[...] = d
        m_out[...] = mm
        v_out[...] = vv

    blk = pl.BlockSpec((tr, cols), lambda i: (i, 0))
    return pl.pallas_call(
        body,
        name=name,
        grid=(rows // tr,),
        in_specs=[blk] * 4,
        out_specs=[blk] * 3,
        out_shape=[jax.ShapeDtypeStruct(w.shape, F32)] * 3,
        compiler_params=_params(("arbitrary",)),
    )(w, g, m, v)


def _prenorm(name, h, g, t, d, tm, deps=()):
    def fn(v):
        x, gg = v
        return [x * _rms_r(x) * gg]

    return _rowwise(name, [("row", h, d, 0), ("full", g)], [("row", (t, d), BF, d, 0)], fn, t=t, tm=tm, deps=deps)[0]


def _ffn_up(tag, xn, wgt, wut, t, d, f, tm, carry=None):
    def up_epi(accs, ex):
        gg, uu = accs
        return [gg, uu, gg * _sig(gg) * uu]

    mats = [dict(a=xn, b=wgt, mode="nt", acc=0, tk=d), dict(a=xn, b=wut, mode="nt", acc=1, tk=d)]
    return _matmul(
        tag + "_up", mats, m=t, n=f, tm=tm, tn=_tile(f, 1408),
        outs=[("mn", (t, f), BF, 0)] * 3, epilogue=up_epi, n_acc=2, j_outer=True, carry=carry)


def _ffn_down(tag, hid, wd, h, g_post, t, d, f, tm, carry=None):
    def down_epi(accs, ex):
        ff = accs[0]
        hh, gg = ex
        return [ff, hh + 0.5 * ff * _rms_r(ff) * gg]

    return _matmul(
        tag + "_down", [dict(a=hid, b=wd, mode="nn", acc=0, tk=f)], m=t, n=d, tm=tm, tn=d,
        extras=[("mn", h, 0), ("n", g_post)],
        outs=[("mn", (t, d), F32, 0)] * 2, epilogue=down_epi, n_acc=1, carry=carry)


def _postnorm_bwd(name, dh, fo, g, scale, t, d, tm, deps=()):
    def fn(v):
        dy, ff, gg = v
        dx, dg = _rms_bwd(ff, gg, dy * scale)
        return [dx, dg]

    return _rowwise(
        name, [("row", dh, d, 0), ("row", fo, d, 0), ("full", g)],
        [("row", (t, d), BF, d, 0), ("acc", (1, d), F32)], fn, t=t, tm=tm, deps=deps)


def _prenorm_bwd_epi(accs, ex):
    hh, dh, gg = ex
    dx, dg = _rms_bwd(hh, gg, accs[0])
    return [dh + dx, dg]


def _ffn_dhid(tag, df, wd, gate, up, t, d, f, tm, carry=None):
    def hid_epi(accs, ex):
        dhid = accs[0]
        gg, uu = ex[0].astype(F32), ex[1].astype(F32)
        s = _sig(gg)
        return [dhid * uu * s * (1.0 + gg * (1.0 - s)), dhid * gg * s]

    return _matmul(
        tag + "_dhid", [dict(a=df, b=wd, mode="nt", acc=0, tk=d)], m=t, n=f, tm=tm, tn=_tile(f, 1408),
        extras=[("mn", gate, 0), ("mn", up, 0)],
        outs=[("mn", (t, f), BF, 0)] * 2, epilogue=hid_epi, n_acc=1, j_outer=True, carry=carry)


def _ffn_dwd(tag, hid, df, t, d, f, carry=None):
    tk_t = _tile(t, 512)
    return _matmul(
        tag + "_dwd", [dict(a=hid, b=df, mode="tn", acc=0, tk=tk_t)], m=f, n=d, tm=_tile(f, 1408), tn=d,
        nk=t // tk_t, outs=[("mn", (f, d), BF, 0)], epilogue=lambda accs, ex: accs, n_acc=1, carry=carry)


def _ffn_dwgu(tag, dgate, dup, xn, t, d, f, carry=None):
    tk_t = _tile(t, 512)
    return _matmul(
        tag + "_dwgu",
        [dict(a=dgate, b=xn, mode="tn", acc=0, tk=tk_t), dict(a=dup, b=xn, mode="tn", acc=1, tk=tk_t)],
        m=f, n=d, tm=_tile(f, 1408), tn=d, nk=t // tk_t,
        outs=[("mn", (f, d), BF, 0)] * 2, epilogue=lambda accs, ex: accs, n_acc=2, carry=carry)


def _ffn_dx(tag, dh, h_in, dgate, dup, wgt, wut, g_pre, t, d, f, tm, deps=(), carry=None):
    tf = _tile(f, 1408)
    return _matmul(
        tag + "_dx",
        [dict(a=dgate, b=wgt, mode="nn", acc=0, tk=tf), dict(a=dup, b=wut, mode="nn", acc=0, tk=tf)],
        m=t, n=d, tm=tm, tn=d, nk=f // tf,
        extras=[("mn", h_in, 0), ("mn", dh, 0), ("n", g_pre)],
        outs=[("mn", (t, d), F32, 0), ("acc", (1, d), F32)], epilogue=_prenorm_bwd_epi, n_acc=1, deps=deps,
        carry=carry)


def _causal_mask():
    r = lax.broadcasted_iota(jnp.int32, (CHUNK, CHUNK), 0)
    c = lax.broadcasted_iota(jnp.int32, (CHUNK, CHUNK), 1)
    return r >= c


def _layernorm_parts(v):
    mu = jnp.mean(v, axis=-1, keepdims=True)
    vc = v - mu
    rstd = lax.rsqrt(jnp.mean(vc * vc, axis=-1, keepdims=True) + EPS)
    return vc * rstd, rstd


def _sgu_fwd(z, g_sgu, w_s, b_col, t, d, tm):
    dg = d // N_SGU_GROUPS
    n_chunk = tm // CHUNK

    def body(zu_ref, zv_ref, g_ref, w_ref, b_ref, a_ref):
        u = _gelu(zu_ref[...])
        vhat, _ = _layernorm_parts(_gelu(zv_ref[...]))
        vn = (vhat * g_ref[...]).astype(BF)
        mask = _causal_mask()
        for gi in range(N_SGU_GROUPS):
            ws = jnp.where(mask, w_ref[gi], 0.0).astype(BF)
            bias = b_ref[gi]
            for ci in range(n_chunk):
                rows, cols = slice(ci * CHUNK, (ci + 1) * CHUNK), slice(gi * dg, (gi + 1) * dg)
                sv = jnp.dot(ws, vn[rows, cols], preferred_element_type=F32) + bias
                a_ref[rows, cols] = (u[rows, cols] * sv).astype(BF)

    return pl.pallas_call(
        body,
        name="sgu_fwd",
        grid=(t // tm,),
        in_specs=[
            pl.BlockSpec((tm, d), lambda i: (i, 0)),
            pl.BlockSpec((tm, d), lambda i: (i, 1)),
            pl.BlockSpec((1, d), lambda i: (0, 0)),
            pl.BlockSpec(w_s.shape, lambda i: (0, 0, 0)),
            pl.BlockSpec(b_col.shape, lambda i: (0, 0, 0)),
        ],
        out_specs=pl.BlockSpec((tm, d), lambda i: (i, 0)),
        out_shape=jax.ShapeDtypeStruct((t, d), BF),
        compiler_params=_params(("arbitrary",)),
    )(z, z, g_sgu, w_s, b_col)


def _sgu_bwd(z, da, dz, g_sgu, w_s, b_col, t, d, tm):
    dg = d // N_SGU_GROUPS
    n_chunk = tm // CHUNK

    def body(zu_ref, zv_ref, da_ref, dz_in, g_ref, w_ref, b_ref, dz_ref, dw_ref, db_ref, dgn_ref, dvn_ref):
        del dz_in
        dzu_ref, dzv_ref = dz_ref.at[:, pl.ds(0, d)], dz_ref.at[:, pl.ds(d, d)]
        i = pl.program_id(0)
        zu, zv = zu_ref[...], zv_ref[...]
        u = _gelu(zu)
        vhat, rstd = _layernorm_parts(_gelu(zv))
        gn = g_ref[...]
        vn = (vhat * gn).astype(BF)
        da_v = da_ref[...].astype(F32)
        dsv_all = (da_v * u).astype(BF)
        mask = _causal_mask()
        for gi in range(N_SGU_GROUPS):
            ws = jnp.where(mask, w_ref[gi], 0.0).astype(BF)
            bias = b_ref[gi]
            dw = jnp.zeros((CHUNK, CHUNK), F32)
            dbias = jnp.zeros((CHUNK, 1), F32)
            for ci in range(n_chunk):
                rows, cols = slice(ci * CHUNK, (ci + 1) * CHUNK), slice(gi * dg, (gi + 1) * dg)
                vn_c, dsv = vn[rows, cols], dsv_all[rows, cols]
                sv = jnp.dot(ws, vn_c, preferred_element_type=F32) + bias
                dzu_ref[rows, cols] = (da_v[rows, cols] * sv * _gelu_grad(zu[rows, cols])).astype(BF)
                dw = dw + lax.dot_general(dsv, vn_c, _DN["nt"], preferred_element_type=F32)
                dbias = dbias + jnp.sum(dsv.astype(F32), axis=1, keepdims=True)
                dvn_ref[rows, cols] = lax.dot_general(ws, dsv, _DN["tn"], preferred_element_type=F32)
            dw = jnp.where(mask, dw, 0.0)

            @pl.when(i == 0)
            def _():
                dw_ref[gi] = dw
                db_ref[gi] = dbias

            @pl.when(i != 0)
            def _():
                dw_ref[gi] += dw
                db_ref[gi] += dbias

        dvn = dvn_ref[...]
        dgn = jnp.sum(dvn * vhat, axis=0, keepdims=True)

        @pl.when(i == 0)
        def _():
            dgn_ref[...] = dgn

        @pl.when(i != 0)
        def _():
            dgn_ref[...] += dgn

        dvh = dvn * gn
        dv = rstd * (dvh - jnp.mean(dvh, axis=-1, keepdims=True) - vhat * jnp.mean(dvh * vhat, axis=-1, keepdims=True))
        dzv_ref[...] = (dv * _gelu_grad(zv)).astype(BF)

    return pl.pallas_call(
        body,
        name="sgu_bwd",
        grid=(t // tm,),
        in_specs=[
            pl.BlockSpec((tm, d), lambda i: (i, 0)),
            pl.BlockSpec((tm, d), lambda i: (i, 1)),
            pl.BlockSpec((tm, d), lambda i: (i, 0)),
            pl.BlockSpec(memory_space=pl.ANY),
            pl.BlockSpec((1, d), lambda i: (0, 0)),
            pl.BlockSpec(w_s.shape, lambda i: (0, 0, 0)),
            pl.BlockSpec(b_col.shape, lambda i: (0, 0, 0)),
        ],
        out_specs=[
            pl.BlockSpec((tm, 2 * d), lambda i: (i, 0)),
            pl.BlockSpec(w_s.shape, lambda i: (0, 0, 0)),
            pl.BlockSpec(b_col.shape, lambda i: (0, 0, 0)),
            pl.BlockSpec((1, d), lambda i: (0, 0)),
        ],
        out_shape=[
            jax.ShapeDtypeStruct(dz.shape, BF),
            jax.ShapeDtypeStruct(w_s.shape, F32),
            jax.ShapeDtypeStruct(b_col.shape, F32),
            jax.ShapeDtypeStruct((1, d), F32),
        ],
        scratch_shapes=[pltpu.VMEM((tm, d), F32)],
        input_output_aliases={3: 0},
        compiler_params=_params(("arbitrary",)),
    )(z, z, da, dz, g_sgu, w_s, b_col)


def _shift_down(x, k, row):
    return jnp.where(row >= k, pltpu.roll(x, k, 0), 0.0)


def _shift_up(x, k, row, t):
    return jnp.where(row < t - k, pltpu.roll(x, t - k, 0), 0.0)


def _doublings(window):
    steps = int(math.log2(window))
    assert 2 ** steps == window
    return [2 ** s for s in range(steps)]


def _pool_diff(c, window, row):
    s = c
    for k in _doublings(window):
        s = s + _shift_down(s, k, row)
    count = jnp.minimum(row + 1, window).astype(F32)
    return s / count - c, count


def _pool_fwd(z, pool_w, pool_scale, t, d):
    dgp = d // len(POOL_WINDOWS)
    cblk = (2 * d) // dgp

    def body(zc_ref, w_ref, s_ref, b_ref):
        row = lax.broadcasted_iota(jnp.int32, (t, 1), 0)
        for gi, window in enumerate(POOL_WINDOWS):
            @pl.when(pl.program_id(0) == gi)
            def _(window=window):
                diff, _ = _pool_diff(zc_ref[...], window, row)
                out = jnp.dot(diff.astype(BF), w_ref[...], preferred_element_type=F32)
                b_ref[...] = (out * s_ref[...]).astype(BF)

    return pl.pallas_call(
        body,
        name="pool_fwd",
        grid=(len(POOL_WINDOWS),),
        in_specs=[
            pl.BlockSpec((t, dgp), lambda g: (0, cblk + g)),
            pl.BlockSpec((None, dgp, dgp), lambda g: (g, 0, 0)),
            pl.BlockSpec((1, dgp), lambda g: (0, g)),
        ],
        out_specs=pl.BlockSpec((t, dgp), lambda g: (0, g)),
        out_shape=jax.ShapeDtypeStruct((t, d), BF),
        compiler_params=_params(("arbitrary",)),
    )(z, pool_w, pool_scale)


def _pool_bwd(z, db, dz, pool_w, pool_scale, t, d):
    dgp = d // len(POOL_WINDOWS)
    cblk = (2 * d) // dgp

    def body(zc_ref, db_ref, dz_in, w_ref, s_ref, dzc_ref, dw_ref, ds_ref):
        del dz_in
        row = lax.broadcasted_iota(jnp.int32, (t, 1), 0)
        for gi, window in enumerate(POOL_WINDOWS):
            @pl.when(pl.program_id(0) == gi)
            def _(window=window):
                diff, count = _pool_diff(zc_ref[...], window, row)
                diff = diff.astype(BF)
                w = w_ref[...]
                dbv = db_ref[...].astype(F32)
                out = jnp.dot(diff, w, preferred_element_type=F32)
                ds_ref[...] = jnp.sum(dbv * out, axis=0, keepdims=True)
                dout = (dbv * s_ref[...]).astype(BF)
                dw_ref[...] = lax.dot_general(diff, dout, _DN["tn"], preferred_element_type=F32).astype(BF)
                ddiff = lax.dot_general(dout, w, _DN["nt"], preferred_element_type=F32)
                s = ddiff / count
                for k in _doublings(window):
                    s = s + _shift_up(s, k, row, t)
                dzc_ref[...] = (s - ddiff).astype(BF)

    return pl.pallas_call(
        body,
        name="pool_bwd",
        grid=(len(POOL_WINDOWS),),
        in_specs=[
            pl.BlockSpec((t, dgp), lambda g: (0, cblk + g)),
            pl.BlockSpec((t, dgp), lambda g: (0, g)),
            pl.BlockSpec(memory_space=pl.ANY),
            pl.BlockSpec((None, dgp, dgp), lambda g: (g, 0, 0)),
            pl.BlockSpec((1, dgp), lambda g: (0, g)),
        ],
        out_specs=[
            pl.BlockSpec((t, dgp), lambda g: (0, cblk + g)),
            pl.BlockSpec((None, dgp, dgp), lambda g: (g, 0, 0)),
            pl.BlockSpec((1, dgp), lambda g: (0, g)),
        ],
        out_shape=[
            jax.ShapeDtypeStruct(dz.shape, BF),
            jax.ShapeDtypeStruct(pool_w.shape, BF),
            jax.ShapeDtypeStruct((1, d), F32),
        ],
        input_output_aliases={2: 0},
        compiler_params=_params(("arbitrary",)),
    )(z, db, dz, pool_w, pool_scale)


def _mix_dy(dm, w_o, z, ya, yb, t, d, tm):
    def body(dm_ref, w_ref, gate_ref, ya_ref, yb_ref, dyab_ref, dz_ref, dy_ref):
        j = pl.program_id(1)

        @pl.when(j == 0)
        def _():
            dy_ref[...] = lax.dot_general(dm_ref[...], w_ref[...], _DN["nt"], preferred_element_type=F32)

        dy = dy_ref[...]
        s = _sig(gate_ref[...])
        yv = jnp.where(j == 0, ya_ref[...], yb_ref[...]).astype(F32)
        dyab_ref[...] = (dy * s).astype(BF)
        dz_ref[...] = (dy * yv * s * (1.0 - s)).astype(BF)

    row = pl.BlockSpec((tm, d), lambda i, j: (i, 0))
    return pl.pallas_call(
        body,
        name="mix_dy",
        grid=(t // tm, 2),
        in_specs=[row, pl.BlockSpec((d, d), lambda i, j: (0, 0)), pl.BlockSpec((tm, d), lambda i, j: (i, 3 + j)), row, row],
        out_specs=[pl.BlockSpec((tm, d), lambda i, j: (i, j)), pl.BlockSpec((tm, d), lambda i, j: (i, 3 + j))],
        out_shape=[jax.ShapeDtypeStruct((t, 2 * d), BF), jax.ShapeDtypeStruct((t, 5 * d), BF)],
        scratch_shapes=[pltpu.VMEM((tm, d), F32)],
        compiler_params=_params(("arbitrary", "arbitrary")),
    )(dm, w_o, z, ya, yb)


def _start(name, spec, deps=()):
    send, recv, bufs, token = _copy_start(name, spec["bufs"], spec["plan"], spec["n_copies"], local=spec["local"], deps=deps)
    return dict(tag=name, send=send, recv=recv, bufs=bufs, plan=spec["plan"]), token


def _wait(st, after):
    return _copy_wait(st["tag"] + "_wait", st["send"], st["recv"], st["bufs"], st["plan"], after)


def _gather_spec(shards):
    n = len(shards)
    lands = [lax.empty((N_DEV,) + s.shape, s.dtype) for s in shards]
    plan, local = _gather_plan_ici(n)
    return dict(bufs=list(shards) + lands, plan=plan, n_copies=4 * n, local=local)


def _gather_forward(tag, states, after):
    lands = []
    for st in states:
        bufs = _wait(st, after)
        lands += bufs[len(bufs) // 2:]
    n = len(lands)
    st, token = _start(tag + "_d2d", dict(bufs=lands, plan=_gather_plan_d2d(n), n_copies=3 * n, local=None))
    return [a.reshape((N_DEV * a.shape[1],) + a.shape[2:]) for a in _wait(st, token)]


def _scatter_d2d(tag, parts, deps=()):
    n = len(parts)
    p4 = [a.reshape(4, 2, a.shape[0] // N_DEV, a.shape[1]) for a in parts]
    lands = [lax.empty((4,) + a.shape[2:], BF) for a in p4]
    return _start(tag + "_d2d", dict(bufs=p4 + lands, plan=_scatter_plan_d2d(n), n_copies=4 * n, local=None), deps=deps)


def _scatter_pair(tag, st, core, after):
    bufs = _wait(st, after)
    n = len(bufs) // 2
    return _pair_add(tag + "_pair_add", core, bufs[:n], bufs[n:])


def _scatter_spec(sums):
    n = len(sums)
    plan, local = _scatter_plan_ici(n)
    return dict(bufs=list(sums) + [lax.empty(a.shape, BF) for a in sums], plan=plan, n_copies=3 * n, local=local)


def _scatter_finish(st, after):
    bufs = _wait(st, after)
    return bufs[len(bufs) // 2:]


def _pack_rows(flat_list, width):
    flat = jnp.concatenate([a.reshape(-1) for a in flat_list])
    rows = -(-flat.shape[0] // (8 * width)) * 8
    return jnp.pad(flat, (0, rows * width - flat.shape[0])).reshape(rows, width)


def _unpack_rows(packed, like):
    flat, out, pos = packed.reshape(-1), [], 0
    for a in like:
        out.append(flat[pos:pos + a.size].reshape(a.shape))
        pos += a.size
    return out


def kernel(x, p, ffn1_pre_g, ffn1_w_gate, ffn1_w_up, ffn1_w_down, ffn1_post_g, mix_pre_g, w_in, sgu_norm_g, sgu_w, sgu_b, pool_w, pool_scale, w_out_a, w_out_b, w_o, mix_post_g, ffn2_pre_g, ffn2_w_gate, ffn2_w_up, ffn2_w_down, ffn2_post_g, ple_pre_g, ple_w_gate, ple_w_proj, ple_post_g, loss_target, m_ffn1_pre_g, m_ffn1_w_gate, m_ffn1_w_up, m_ffn1_w_down, m_ffn1_post_g, m_mix_pre_g, m_w_in, m_sgu_norm_g, m_sgu_w, m_sgu_b, m_pool_w, m_pool_scale, m_w_out_a, m_w_out_b, m_w_o, m_mix_post_g, m_ffn2_pre_g, m_ffn2_w_gate, m_ffn2_w_up, m_ffn2_w_down, m_ffn2_post_g, m_ple_pre_g, m_ple_w_gate, m_ple_w_proj, m_ple_post_g, v_ffn1_pre_g, v_ffn1_w_gate, v_ffn1_w_up, v_ffn1_w_down, v_ffn1_post_g, v_mix_pre_g, v_w_in, v_sgu_norm_g, v_sgu_w, v_sgu_b, v_pool_w, v_pool_scale, v_w_out_a, v_w_out_b, v_w_o, v_mix_post_g, v_ffn2_pre_g, v_ffn2_w_gate, v_ffn2_w_up, v_ffn2_w_down, v_ffn2_post_g, v_ple_pre_g, v_ple_w_gate, v_ple_w_proj, v_ple_post_g):
    args = dict(locals())
    names = ["ffn1_pre_g", "ffn1_w_gate", "ffn1_w_up", "ffn1_w_down", "ffn1_post_g", "mix_pre_g", "w_in",
             "sgu_norm_g", "sgu_w", "sgu_b", "pool_w", "pool_scale", "w_out_a", "w_out_b", "w_o", "mix_post_g",
             "ffn2_pre_g", "ffn2_w_gate", "ffn2_w_up", "ffn2_w_down", "ffn2_post_g", "ple_pre_g", "ple_w_gate",
             "ple_w_proj", "ple_post_g"]
    w = {k: args[k][0] for k in names}
    mom = {k: args["m_" + k][0] for k in names}
    var = {k: args["v_" + k][0] for k in names}

    assert x.shape[0] == 1 and p.shape[:2] == (1, 1)
    t, d = x.shape[1], x.shape[2]
    f = ffn1_w_gate.shape[2] * N_DEV
    d_in = w_in.shape[2] * N_DEV
    d_ple = p.shape[3]
    n_pool = len(POOL_WINDOWS)
    dgp = d // n_pool
    assert d_in == 5 * d and t % CHUNK == 0
    tm = _tile(t, 512, CHUNK)
    xs, ps, target = x[0], p[0, 0], loss_target[0]

    col_sharded = ("ffn1_w_gate", "ffn1_w_up", "ffn2_w_gate", "ffn2_w_up", "w_in", "ple_w_proj")

    def shard_of(k):
        if k in col_sharded:
            return w[k].T.astype(BF)
        if k == "pool_w":
            return w[k].reshape(-1, dgp).astype(BF)
        return w[k].astype(BF)

    rows_pw = dgp // N_DEV
    b_col = w["sgu_b"][:, :, None]
    gains = {k: w[k][None, :] for k in names if w[k].ndim == 1}
    core = lax.axis_index("c").astype(jnp.int32).reshape(1)
    full = {}

    def gather(keys):
        return _gather_spec([shard_of(k) for k in keys])

    st, tok = _start("gather_a", gather(["ffn1_w_gate", "ffn1_w_up"]))
    xn1 = _prenorm("ffn1_prenorm", xs, gains["ffn1_pre_g"], t, d, tm, deps=[tok])
    full.update(zip(["ffn1_w_gate", "ffn1_w_up"], _gather_forward("gather_a", [st], xn1)))
    g1, u1, hid1, st = _ffn_up("ffn1", xn1, full["ffn1_w_gate"], full["ffn1_w_up"], t, d, f, tm,
                               carry=gather(["ffn1_w_down", "w_in"]))
    full.update(zip(["ffn1_w_down", "w_in"], _gather_forward("gather_b", [st], hid1)))
    keys_c = ["pool_w", "w_out_a", "w_out_b", "w_o"]
    f1, h1, st = _ffn_down("ffn1", hid1, full["ffn1_w_down"], xs, gains["ffn1_post_g"], t, d, f, tm, carry=gather(keys_c))

    xn2 = _prenorm("mix_prenorm", h1, gains["mix_pre_g"], t, d, tm)
    full.update(zip(keys_c, _gather_forward("gather_c", [st], xn2)))
    pool_full = full["pool_w"].reshape(N_DEV, n_pool, rows_pw, dgp).transpose(1, 0, 2, 3).reshape(n_pool, dgp, dgp)
    keys_d = ["ffn2_w_gate", "ffn2_w_up", "ffn2_w_down"]
    z, st_d = _matmul("mix_in", [dict(a=xn2, b=full["w_in"], mode="nt", acc=0, tk=d)], m=t, n=d_in, tm=tm, tn=d,
                      outs=[("mn", (t, d_in), F32, 0)], epilogue=lambda accs, ex: accs, n_acc=1, j_outer=True,
                      carry=gather(keys_d))
    a_br = _sgu_fwd(z, gains["sgu_norm_g"], w["sgu_w"], b_col, t, d, tm)
    b_br = _pool_fwd(z, pool_full, gains["pool_scale"], t, d)

    def merge_epi(accs, ex):
        ya, yb = accs
        ga, gb = ex
        return [ya, yb, _sig(ga) * ya + _sig(gb) * yb]

    keys_e = ["ple_w_gate", "ple_w_proj"]
    ya, yb, y, st_e = _matmul(
        "mix_merge",
        [dict(a=a_br, b=full["w_out_a"], mode="nn", acc=0, tk=d), dict(a=b_br, b=full["w_out_b"], mode="nn", acc=1, tk=d)],
        m=t, n=d, tm=tm, tn=d, extras=[("mn", z, 3), ("mn", z, 4)],
        outs=[("mn", (t, d), BF, 0)] * 3, epilogue=merge_epi, n_acc=2, carry=gather(keys_e))

    def proj_epi(accs, ex):
        mm = accs[0]
        hh, gg = ex
        return [mm, hh + mm * _rms_r(mm) * gg]

    m_out, h2 = _matmul(
        "mix_proj", [dict(a=y, b=full["w_o"], mode="nn", acc=0, tk=d)], m=t, n=d, tm=tm, tn=d,
        extras=[("mn", h1, 0), ("n", gains["mix_post_g"])],
        outs=[("mn", (t, d), F32, 0)] * 2, epilogue=proj_epi, n_acc=1)
    full.update(zip(keys_d + keys_e, _gather_forward("gather_de", [st_d, st_e], h2)))

    xn3 = _prenorm("ffn2_prenorm", h2, gains["ffn2_pre_g"], t, d, tm)
    g2, u2, hid2 = _ffn_up("ffn2", xn3, full["ffn2_w_gate"], full["ffn2_w_up"], t, d, f, tm)
    f2, h3 = _ffn_down("ffn2", hid2, full["ffn2_w_down"], h2, gains["ffn2_post_g"], t, d, f, tm)

    xn4 = _prenorm("ple_prenorm", h3, gains["ple_pre_g"], t, d, tm)

    def ple_epi(accs, ex):
        gl, e = accs
        hh, tgt, gg = ex
        q = _sig(gl) * e
        err = hh + q * _rms_r(q) * gg - tgt
        return [gl, e, err * (1.0 / d), jnp.sum(err * err, axis=0, keepdims=True)]

    gl, e_ple, dh4, loss_vec = _matmul(
        "ple_fwd",
        [dict(a=xn4, b=full["ple_w_gate"], mode="nn", acc=0, tk=d), dict(a=ps, b=full["ple_w_proj"], mode="nt", acc=1, tk=d_ple)],
        m=t, n=d, tm=tm, tn=d, extras=[("mn", h3, 0), ("mn", target, 0), ("n", gains["ple_post_g"])],
        outs=[("mn", (t, d), F32, 0)] * 3 + [("acc", (1, d), F32)], epilogue=ple_epi, n_acc=2)
    loss = lax.psum(jnp.sum(loss_vec) * (0.5 / d), AXES)

    tk_t = _tile(t, 512)

    def ple_post_fn(v):
        dy, gl_v, e_v, gg = v
        s = _sig(gl_v)
        dq, dg = _rms_bwd(s * e_v, gg, dy)
        return [dq * e_v * s * (1.0 - s), dq * s, dg]

    dgl, de, dg_ple_post = _rowwise(
        "ple_post_bwd", [("row", dh4, d, 0), ("row", gl, d, 0), ("row", e_ple, d, 0), ("full", gains["ple_post_g"])],
        [("row", (t, d), BF, d, 0), ("row", (t, d), BF, d, 0), ("acc", (1, d), F32)], ple_post_fn, t=t, tm=tm)
    ident = lambda accs, ex: accs
    dw_ple_gate = _matmul("ple_dwg", [dict(a=xn4, b=dgl, mode="tn", acc=0, tk=tk_t)], m=d, n=d, tm=d, tn=d,
                          nk=t // tk_t, outs=[("mn", (d, d), BF, 0)], epilogue=ident, n_acc=1)[0]
    dw_ple_proj_t = _matmul("ple_dwp", [dict(a=de, b=ps, mode="tn", acc=0, tk=tk_t)], m=d, n=d_ple, tm=d, tn=d_ple,
                            nk=t // tk_t, outs=[("mn", (d, d_ple), BF, 0)], epilogue=ident, n_acc=1)[0]
    dh3, dg_ple_pre = _matmul(
        "ple_dx", [dict(a=dgl, b=full["ple_w_gate"], mode="nt", acc=0, tk=d)], m=t, n=d, tm=tm, tn=d,
        extras=[("mn", h3, 0), ("mn", dh4, 0), ("n", gains["ple_pre_g"])],
        outs=[("mn", (t, d), F32, 0), ("acc", (1, d), F32)], epilogue=_prenorm_bwd_epi, n_acc=1)

    flying = []

    def fly(keys, res):
        flying.append((keys, res[-1]))
        return res[:-1]

    df2, dg_f2_post = _postnorm_bwd("ffn2_post_bwd", dh3, f2, gains["ffn2_post_g"], 0.5, t, d, tm)
    dgate2, dup2 = _ffn_dhid("ffn2", df2, full["ffn2_w_down"], g2, u2, t, d, f, tm)
    dwd2 = _ffn_dwd("ffn2", hid2, df2, t, d, f)[0]
    dwg2, dwu2 = _ffn_dwgu("ffn2", dgate2, dup2, xn3, t, d, f)
    st, tok = _scatter_d2d("scatter1", [dw_ple_gate, dw_ple_proj_t, dwd2, dwg2, dwu2])
    dh2, dg_f2_pre = _ffn_dx("ffn2", dh3, h2, dgate2, dup2, full["ffn2_w_gate"], full["ffn2_w_up"],
                             gains["ffn2_pre_g"], t, d, f, tm, deps=[tok])
    q_pg, q_pp, q_d2, q_g2, q_u2 = _scatter_pair("scatter1", st, core, dh2)

    dm, dg_mix_post = _postnorm_bwd("mix_post_bwd", dh2, m_out, gains["mix_post_g"], 1.0, t, d, tm)

    dyab, dz = _mix_dy(dm, full["w_o"], z, ya, yb, t, d, tm)
    dw_o = _matmul("mix_dwo", [dict(a=y, b=dm, mode="tn", acc=0, tk=tk_t)], m=d, n=d, tm=d, tn=d, nk=t // tk_t,
                   outs=[("mn", (d, d), BF, 0)], epilogue=ident, n_acc=1)[0]
    da, db = fly(["ple_w_gate", "ple_w_proj"], _matmul(
        "mix_dab",
        [dict(a=dyab, b=full["w_out_a"], mode="nt", acc=0, tk=d),
         dict(a=dyab, b=full["w_out_b"], mode="nt", acc=1, tk=d, a_off=(0, 1))],
        m=t, n=d, tm=tm, tn=d, outs=[("mn", (t, d), BF, 0)] * 2, epilogue=ident, n_acc=2,
        carry=_scatter_spec([q_pg, q_pp])))
    dw_out_a, dw_out_b = fly(["ffn2_w_down"], _matmul(
        "mix_dwab",
        [dict(a=a_br, b=dyab, mode="tn", acc=0, tk=tk_t),
         dict(a=b_br, b=dyab, mode="tn", acc=1, tk=tk_t, b_off=(0, 1))],
        m=d, n=d, tm=d, tn=d, nk=t // tk_t, outs=[("mn", (d, d), BF, 0)] * 2, epilogue=ident, n_acc=2,
        carry=_scatter_spec([q_d2])))
    dz, dsgu_w, dsgu_b, dg_sgu = _sgu_bwd(z, da, dz, gains["sgu_norm_g"], w["sgu_w"], b_col, t, d, _tile(t, 256, CHUNK))
    dz, dpool_w, dg_pool = _pool_bwd(z, db, dz, pool_full, gains["pool_scale"], t, d)
    dw_in_t, = fly(["ffn2_w_gate", "ffn2_w_up"], _matmul(
        "mix_dwin", [dict(a=dz, b=xn2, mode="tn", acc=0, tk=tk_t)], m=d_in, n=d, tm=d, tn=d,
        nk=t // tk_t, outs=[("mn", (d_in, d), BF, 0)], epilogue=ident, n_acc=1, carry=_scatter_spec([q_g2, q_u2])))
    dpool_rows = dpool_w.reshape(n_pool, N_DEV, rows_pw, dgp).transpose(1, 0, 2, 3).reshape(N_DEV * n_pool * rows_pw, dgp)
    st, tok = _scatter_d2d("scatter2", [dw_o, dw_out_a, dw_out_b, dpool_rows, dw_in_t])
    dh1, dg_mix_pre = _matmul(
        "mix_dx", [dict(a=dz, b=full["w_in"], mode="nn", acc=0, tk=d)], m=t, n=d, tm=tm, tn=d, nk=d_in // d,
        extras=[("mn", h1, 0), ("mn", dh2, 0), ("n", gains["mix_pre_g"])],
        outs=[("mn", (t, d), F32, 0), ("acc", (1, d), F32)], epilogue=_prenorm_bwd_epi, n_acc=1, deps=[tok])
    q_o, q_oa, q_ob, q_pool, q_in = _scatter_pair("scatter2", st, core, dh1)

    df1, dg_f1_post = _postnorm_bwd("ffn1_post_bwd", dh1, f1, gains["ffn1_post_g"], 0.5, t, d, tm)
    dgate1, dup1 = fly(["w_in", "w_o"], _ffn_dhid("ffn1", df1, full["ffn1_w_down"], g1, u1, t, d, f, tm,
                                                  carry=_scatter_spec([q_in, q_o])))
    dwd1, = fly(["w_out_a", "w_out_b", "pool_w"], _ffn_dwd("ffn1", hid1, df1, t, d, f,
                                                          carry=_scatter_spec([q_oa, q_ob, q_pool])))
    st, tok = _scatter_d2d("scatter3a", [dwd1])
    q_d1, = _scatter_pair("scatter3a", st, core, tok)
    dwg1, dwu1 = fly(["ffn1_w_down"], _ffn_dwgu("ffn1", dgate1, dup1, xn1, t, d, f, carry=_scatter_spec([q_d1])))
    st, tok = _scatter_d2d("scatter3b", [dwg1, dwu1])
    q_g1, q_u1 = _scatter_pair("scatter3b", st, core, tok)
    grad_x, dg_f1_pre = fly(["ffn1_w_gate", "ffn1_w_up"], _ffn_dx(
        "ffn1", dh1, xs, dgate1, dup1, full["ffn1_w_gate"], full["ffn1_w_up"], gains["ffn1_pre_g"], t, d, f, tm,
        carry=_scatter_spec([q_g1, q_u1])))

    grad, delta, new_m, new_v = {}, {}, {}, {}

    def update(group, staged):
        done = []
        for k, st in zip(group, staged):
            if k in col_sharded:
                grad[k] = _reduce_only("sum_" + k, st).T
                delta[k], new_m[k], new_v[k] = _adam_only("adam_" + k, w[k], grad[k], mom[k], var[k])
            else:
                shape = w[k].shape
                flat = (-1, shape[-1])
                res = _reduce_adam("adam_" + k, st, w[k].reshape(flat), mom[k].reshape(flat), var[k].reshape(flat))
                grad[k], delta[k], new_m[k], new_v[k] = [r.reshape(shape) for r in res]
            done.append(delta[k])
        return done

    small_grads = {
        "ffn1_pre_g": dg_f1_pre, "ffn1_post_g": dg_f1_post, "mix_pre_g": dg_mix_pre, "sgu_norm_g": dg_sgu,
        "sgu_w": dsgu_w, "sgu_b": dsgu_b, "pool_scale": dg_pool, "mix_post_g": dg_mix_post, "ffn2_pre_g": dg_f2_pre,
        "ffn2_post_g": dg_f2_post, "ple_pre_g": dg_ple_pre, "ple_post_g": dg_ple_post,
    }
    small_names = [k for k in names if k in small_grads]
    small_packed = _pack_rows([small_grads[k] for k in small_names], d)
    plan_s, local_s = _broadcast_plan()
    st_small, done = _start("small", dict(
        bufs=[small_packed, lax.empty((N_DEV,) + small_packed.shape, F32)], plan=plan_s, n_copies=N_DEV - 1, local=local_s))
    for keys, st in flying[:-2]:
        done = update(keys, _scatter_finish(st, done))
    staged_small = _wait(st_small, done)[1]
    small_like = [w[k] for k in small_names]
    res = _reduce_adam(
        "adam_small", staged_small, _pack_rows(small_like, d), _pack_rows([mom[k] for k in small_names], d),
        _pack_rows([var[k] for k in small_names], d))
    for dst, packed in zip((grad, delta, new_m, new_v), res):
        dst.update(zip(small_names, _unpack_rows(packed, small_like)))
    done = [res[1]]
    for keys, st in flying[-2:]:
        done = update(keys, _scatter_finish(st, done))

    out = [loss, grad_x[None]]
    for group in (grad, delta, new_m, new_v):
        out += [group[k][None] for k in names]
    return tuple(out)
```

```python
import math

import jax
import jax.numpy as jnp
from jax import lax
from jax.experimental import pallas as pl
from jax.experimental.pallas import tpu as pltpu
from jax.experimental.pallas import tpu_sc as plsc

EPS = 1e-6
CHUNK = 128
N_SGU_GROUPS = 4
POOL_WINDOWS = (2, 4, 8, 16)
ADAM_LR = 0.001
ADAM_B1 = 0.9
ADAM_B2 = 0.999
ADAM_EPS = 1e-08
ADAM_WD = 0.01
ADAM_STEP = 10

N_DEV = 8
AXES = ("x", "y", "c")
MESH = pl.DeviceIdType.MESH
V7X_VMEM_BYTES = 64 * 1024 * 1024
VMEM_LIMIT = V7X_VMEM_BYTES - 8 * 1024 * 1024
BF = jnp.bfloat16
F32 = jnp.float32

_DN = {
    "nn": (((1,), (0,)), ((), ())),
    "nt": (((1,), (1,)), ((), ())),
    "tn": (((0,), (0,)), ((), ())),
}


def _params(sem=None):
    return pltpu.CompilerParams(dimension_semantics=sem, vmem_limit_bytes=VMEM_LIMIT)


def _tile(n, target, align=128):
    t = min(n, target)
    t -= t % align
    while t >= align:
        if n % t == 0:
            return t
        t -= align
    return n


def _sig(x):
    return 1.0 / (1.0 + jnp.exp(-x))


_GELU_K = math.sqrt(2.0 / math.pi)
_GELU_C = 0.044715


def _gelu(x):
    return 0.5 * x * (1.0 + jnp.tanh(_GELU_K * (x + _GELU_C * x * x * x)))


def _gelu_grad(x):
    t = jnp.tanh(_GELU_K * (x + _GELU_C * x * x * x))
    return 0.5 * (1.0 + t) + 0.5 * x * (1.0 - t * t) * _GELU_K * (1.0 + 3.0 * _GELU_C * x * x)


def _rms_r(x):
    return lax.rsqrt(jnp.mean(x * x, axis=-1, keepdims=True) + EPS)


def _rms_bwd(x, g, dy):
    r = _rms_r(x)
    xh = x * r
    gy = dy * g
    dx = r * (gy - xh * jnp.mean(xh * gy, axis=-1, keepdims=True))
    return dx, jnp.sum(dy * xh, axis=0, keepdims=True)


def _matmul(name, mats, *, m, n, tm, tn, nk=1, extras=(), outs, epilogue, n_acc, j_outer=False, deps=()):
    ni, nj = m // tm, n // tn
    assert ni * tm == m and nj * tn == n, (name, m, n, tm, tn)
    if j_outer:
        grid = (nj, ni, nk)

        def ij(g0, g1):
            return g1, g0
    else:
        grid = (ni, nj, nk)

        def ij(g0, g1):
            return g0, g1

    in_specs, args = [], []
    for mt in mats:
        mode, tk = mt["mode"], mt["tk"]
        ao, bo = mt.get("a_off", (0, 0)), mt.get("b_off", (0, 0))
        if mode == "tn":
            sa = pl.BlockSpec((tk, tm), lambda g0, g1, kk, ao=ao: (ao[0] + kk, ao[1] + ij(g0, g1)[0]))
        else:
            sa = pl.BlockSpec((tm, tk), lambda g0, g1, kk, ao=ao: (ao[0] + ij(g0, g1)[0], ao[1] + kk))
        if mode == "nt":
            sb = pl.BlockSpec((tn, tk), lambda g0, g1, kk, bo=bo: (bo[0] + ij(g0, g1)[1], bo[1] + kk))
        else:
            sb = pl.BlockSpec((tk, tn), lambda g0, g1, kk, bo=bo: (bo[0] + kk, bo[1] + ij(g0, g1)[1]))
        in_specs += [sa, sb]
        args += [mt["a"], mt["b"]]
    n_mat_refs = len(args)
    for ex in extras:
        if ex[0] == "mn":
            in_specs.append(pl.BlockSpec((tm, tn), lambda g0, g1, kk, c=ex[2]: (ij(g0, g1)[0], c + ij(g0, g1)[1])))
        else:
            in_specs.append(pl.BlockSpec((1, tn), lambda g0, g1, kk: (0, ij(g0, g1)[1])))
        args.append(ex[1])
    n_ex_end = len(args)
    in_specs += [pl.BlockSpec(memory_space=pl.ANY)] * len(deps)
    args += list(deps)
    n_in = len(args)
    out_specs, out_shape = [], []
    for o in outs:
        if o[0] == "mn":
            out_specs.append(pl.BlockSpec((tm, tn), lambda g0, g1, kk, c=o[3]: (ij(g0, g1)[0], c + ij(g0, g1)[1])))
        else:
            assert nj == 1, name
            out_specs.append(pl.BlockSpec((1, tn), lambda g0, g1, kk: (0, 0)))
        out_shape.append(jax.ShapeDtypeStruct(o[1], o[2]))
    n_out = len(outs)

    def body(*refs):
        mat_refs = refs[:n_mat_refs]
        ex_refs = refs[n_mat_refs:n_ex_end]
        out_refs = refs[n_in:n_in + n_out]
        acc_refs = refs[n_in + n_out:]
        i = ij(pl.program_id(0), pl.program_id(1))[0]
        kk = pl.program_id(2)

        def products():
            res = [None] * n_acc
            for idx, mt in enumerate(mats):
                a = mat_refs[2 * idx][...].astype(BF)
                b = mat_refs[2 * idx + 1][...].astype(BF)
                p = lax.dot_general(a, b, _DN[mt["mode"]], preferred_element_type=F32)
                q = mt["acc"]
                res[q] = p if res[q] is None else res[q] + p
            return res

        def finish(accs):
            vals = epilogue(accs, [r[...] for r in ex_refs])
            for o, r, v in zip(outs, out_refs, vals):
                if o[0] == "mn":
                    r[...] = v.astype(o[2])
                else:
                    @pl.when(i == 0)
                    def _():
                        r[...] = v

                    @pl.when(i != 0)
                    def _():
                        r[...] += v

        if nk == 1:
            finish(products())
        else:
            res = products()

            @pl.when(kk == 0)
            def _():
                for q in range(n_acc):
                    acc_refs[q][...] = res[q]

            @pl.when(kk != 0)
            def _():
                for q in range(n_acc):
                    acc_refs[q][...] += res[q]

            @pl.when(kk == nk - 1)
            def _():
                finish([r[...] for r in acc_refs])

    scratch = [pltpu.VMEM((tm, tn), F32) for _ in range(n_acc)] if nk > 1 else []
    return pl.pallas_call(
        body,
        name=name,
        grid=grid,
        in_specs=in_specs,
        out_specs=out_specs,
        out_shape=out_shape,
        scratch_shapes=scratch,
        compiler_params=_params(("arbitrary", "arbitrary", "arbitrary")),
    )(*args)


def _rowwise(name, ins, outs, fn, *, t, tm, deps=()):
    ni = t // tm
    assert ni * tm == t, (name, t, tm)
    in_specs, args = [], []
    for s in ins:
        if s[0] == "row":
            in_specs.append(pl.BlockSpec((tm, s[2]), lambda i, c=s[3]: (i, c)))
        else:
            nd = s[1].ndim
            in_specs.append(pl.BlockSpec(s[1].shape, lambda i, nd=nd: (0,) * nd))
        args.append(s[1])
    out_specs, out_shape = [], []
    for o in outs:
        if o[0] == "row":
            out_specs.append(pl.BlockSpec((tm, o[3]), lambda i, c=o[4]: (i, c)))
        else:
            nd = len(o[1])
            out_specs.append(pl.BlockSpec(o[1], lambda i, nd=nd: (0,) * nd))
        out_shape.append(jax.ShapeDtypeStruct(o[1], o[2]))
    n_read = len(args)
    in_specs += [pl.BlockSpec(memory_space=pl.ANY)] * len(deps)
    args += list(deps)
    n_in = len(args)

    def body(*refs):
        i = pl.program_id(0)
        vals = fn([r[...] for r in refs[:n_read]])
        for o, r, v in zip(outs, refs[n_in:], vals):
            if o[0] == "row":
                r[...] = v.astype(o[2])
            else:
                @pl.when(i == 0)
                def _():
                    r[...] = v

                @pl.when(i != 0)
                def _():
                    r[...] += v

    return pl.pallas_call(
        body,
        name=name,
        grid=(ni,),
        in_specs=in_specs,
        out_specs=out_specs,
        out_shape=out_shape,
        compiler_params=_params(("arbitrary",)),
    )(*args)


def _place():
    return lax.axis_index("x"), lax.axis_index("y"), lax.axis_index("c")


def _remote(src, dst, send_sems, recv_sems, k, to):
    return pltpu.make_async_remote_copy(
        src_ref=src, dst_ref=dst, send_sem=send_sems.at[k], recv_sem=recv_sems.at[k],
        device_id=to, device_id_type=MESH)


def _chips(x, y):
    return [(1 - x, y), (x, 1 - y), (1 - x, 1 - y)]


def _broadcast_plan():
    def plan(refs, send_sems, recv_sems):
        x, y, c = _place()
        cps = []
        for dd in range(1, N_DEV):
            peer = (1 - x if dd & 4 else x, 1 - y if dd & 2 else y, 1 - c if dd & 1 else c)
            cps.append(_remote(refs[0], refs[1].at[4 * x + 2 * y + c], send_sems, recv_sems, dd - 1, peer))
        return cps

    def local(refs):
        x, y, c = _place()
        return [(refs[0], refs[1].at[4 * x + 2 * y + c])]

    return plan, local


def _handshake(peers):
    barrier = pltpu.get_barrier_semaphore()
    for peer in peers:
        pl.semaphore_signal(barrier, inc=1, device_id=peer, device_id_type=MESH)
    pl.semaphore_wait(barrier, len(peers))


def _sequencer_call(name, collective_id, body, args, out_type, scratch_types):
    return pl.kernel(
        body,
        out_type=out_type,
        mesh=plsc.ScalarSubcoreMesh(axis_name="sequencer", num_cores=1),
        scratch_types=scratch_types,
        compiler_params=pltpu.CompilerParams(collective_id=collective_id),
        name=name,
    )(*args)


def _sc_gather(name, collective_id, shards):
    n = len(shards)

    def body(*refs):
        ins, outs = refs[:n], refs[n:2 * n]
        send_sems, recv_sems, local_sems = refs[2 * n:]
        x, y, c = _place()
        me, sibling = (x, y, c), (x, y, 1 - c)
        chips = _chips(x, y)
        _handshake([sibling] + [(*chip, c) for chip in chips])

        def slot(i, px, py, pc):
            return outs[i].at[4 * px + 2 * py + pc]

        def copy(i, k, block, to, src=None):
            return _remote(slot(i, *block) if src is None else src, slot(i, *block), send_sems, recv_sems, 7 * i + k, to)

        mine = [pltpu.make_async_copy(ins[i], slot(i, *me), local_sems.at[i]) for i in range(n)]
        for cp in mine:
            cp.start()
        first = []
        for i in range(n):
            first += [copy(i, 1 + j, me, (*chip, c), src=ins[i]) for j, chip in enumerate(chips)]
            first.append(copy(i, 0, me, sibling, src=ins[i]))
        for cp in first:
            cp.start()
        passed = []
        for i in range(n):
            for j, chip in enumerate(chips):
                copy(i, 1 + j, (*chip, c), me).wait_recv()
                fwd = copy(i, 4 + j, (*chip, c), sibling)
                fwd.start()
                passed.append(fwd)
        for i in range(n):
            copy(i, 0, sibling, me).wait_recv()
            for j, chip in enumerate(chips):
                copy(i, 4 + j, (*chip, 1 - c), me).wait_recv()
        for cp in first + passed:
            cp.wait_send()
        for cp in mine:
            cp.wait()

    outs = _sequencer_call(
        name, collective_id, body, shards,
        [jax.ShapeDtypeStruct((N_DEV,) + s.shape, s.dtype) for s in shards],
        [pltpu.SemaphoreType.DMA((7 * n,)), pltpu.SemaphoreType.DMA((7 * n,)), pltpu.SemaphoreType.DMA((n,))])
    return [o.reshape((N_DEV * s.shape[0],) + s.shape[1:]) for o, s in zip(outs, shards)]


def _sc_exchange(name, collective_id, srcs, land_shapes, plan, n_copies, peers, local=None):
    n = len(srcs)

    def body(*refs):
        bufs, send_sems, recv_sems = refs[:2 * n], refs[2 * n], refs[2 * n + 1]
        _handshake(peers())
        if local is not None:
            mine = [pltpu.make_async_copy(a, b, refs[2 * n + 2].at[k]) for k, (a, b) in enumerate(local(bufs))]
            for cp in mine:
                cp.start()
        cps = plan(bufs, send_sems, recv_sems)
        for cp in cps:
            cp.start()
        for cp in cps:
            cp.wait_send()
            cp.wait_recv()
        if local is not None:
            for cp in mine:
                cp.wait()

    scratch = [pltpu.SemaphoreType.DMA((n_copies,)), pltpu.SemaphoreType.DMA((n_copies,))]
    if local is not None:
        scratch.append(pltpu.SemaphoreType.DMA((n,)))
    return list(_sequencer_call(name, collective_id, body, srcs, land_shapes, scratch))


def _adam_vals(w, g, m, v):
    m = ADAM_B1 * m + (1.0 - ADAM_B1) * g
    v = ADAM_B2 * v + (1.0 - ADAM_B2) * (g * g)
    m_hat = m / (1.0 - ADAM_B1 ** ADAM_STEP)
    v_hat = v / (1.0 - ADAM_B2 ** ADAM_STEP)
    delta = -ADAM_LR * (m_hat / (jnp.sqrt(v_hat) + ADAM_EPS) + ADAM_WD * w)
    return delta, m, v


def _sum_parts(st_ref):
    g = st_ref[0].astype(F32)
    for k in range(1, st_ref.shape[0]):
        g = g + st_ref[k].astype(F32)
    return g


def _reduce_adam(name, st, w, m, v, deps=()):
    rows, cols = w.shape
    tr = _tile(rows, 512, 16)

    def body(st_ref, w_ref, m_ref, v_ref, *rest):
        g_out, d_out, m_out, v_out = rest[len(deps):]
        g = _sum_parts(st_ref)
        d, mm, vv = _adam_vals(w_ref[...], g, m_ref[...], v_ref[...])
        g_out[...] = g
        d_out[...] = d
        m_out[...] = mm
        v_out[...] = vv

    blk = pl.BlockSpec((tr, cols), lambda i: (i, 0))
    return pl.pallas_call(
        body,
        name=name,
        grid=(rows // tr,),
        in_specs=[pl.BlockSpec((st.shape[0], tr, cols), lambda i: (0, i, 0)), blk, blk, blk]
        + [pl.BlockSpec(memory_space=pl.ANY)] * len(deps),
        out_specs=[blk] * 4,
        out_shape=[jax.ShapeDtypeStruct(w.shape, F32)] * 4,
        compiler_params=_params(("arbitrary",)),
    )(st, w, m, v, *deps)


def _reduce_only(name, st, deps=()):
    _, rows, cols = st.shape
    tr = _tile(rows, 512, 16)

    def body(st_ref, *rest):
        rest[-1][...] = _sum_parts(st_ref)

    return pl.pallas_call(
        body,
        name=name,
        grid=(rows // tr,),
        in_specs=[pl.BlockSpec((st.shape[0], tr, cols), lambda i: (0, i, 0))]
        + [pl.BlockSpec(memory_space=pl.ANY)] * len(deps),
        out_specs=pl.BlockSpec((tr, cols), lambda i: (i, 0)),
        out_shape=jax.ShapeDtypeStruct((rows, cols), F32),
        compiler_params=_params(("arbitrary",)),
    )(st, *deps)


def _adam_only(name, w, g, m, v):
    rows, cols = w.shape
    tr = _tile(rows, 512, 16)

    def body(w_ref, g_ref, m_ref, v_ref, d_out, m_out, v_out):
        d, mm, vv = _adam_vals(w_ref[...], g_ref[...], m_ref[...], v_ref[...])
        d_out[...] = d
        m_out[...] = mm
        v_out[...] = vv

    blk = pl.BlockSpec((tr, cols), lambda i: (i, 0))
    return pl.pallas_call(
        body,
        name=name,
        grid=(rows // tr,),
        in_specs=[blk] * 4,
        out_specs=[blk] * 3,
        out_shape=[jax.ShapeDtypeStruct(w.shape, F32)] * 3,
        compiler_params=_params(("arbitrary",)),
    )(w, g, m, v)


def _prenorm(name, h, g, t, d, tm, deps=()):
    def fn(v):
        x, gg = v
        return [x * _rms_r(x) * gg]

    return _rowwise(name, [("row", h, d, 0), ("full", g)], [("row", (t, d), BF, d, 0)], fn, t=t, tm=tm, deps=deps)[0]


def _ffn_up(tag, xn, wgt, wut, t, d, f, tm):
    def up_epi(accs, ex):
        gg, uu = accs
        return [gg, uu, gg * _sig(gg) * uu]

    mats = [dict(a=xn, b=wgt, mode="nt", acc=0, tk=d), dict(a=xn, b=wut, mode="nt", acc=1, tk=d)]
    return _matmul(
        tag + "_up", mats, m=t, n=f, tm=tm, tn=_tile(f, 1408),
        outs=[("mn", (t, f), BF, 0)] * 3, epilogue=up_epi, n_acc=2, j_outer=True)


def _ffn_down(tag, hid, wd, h, g_post, t, d, f, tm):
    def down_epi(accs, ex):
        ff = accs[0]
        hh, gg = ex
        return [ff, hh + 0.5 * ff * _rms_r(ff) * gg]

    return _matmul(
        tag + "_down", [dict(a=hid, b=wd, mode="nn", acc=0, tk=f)], m=t, n=d, tm=tm, tn=d,
        extras=[("mn", h, 0), ("n", g_post)],
        outs=[("mn", (t, d), F32, 0)] * 2, epilogue=down_epi, n_acc=1)


def _postnorm_bwd(name, dh, fo, g, scale, t, d, tm, deps=()):
    def fn(v):
        dy, ff, gg = v
        dx, dg = _rms_bwd(ff, gg, dy * scale)
        return [dx, dg]

    return _rowwise(
        name, [("row", dh, d, 0), ("row", fo, d, 0), ("full", g)],
        [("row", (t, d), BF, d, 0), ("acc", (1, d), F32)], fn, t=t, tm=tm, deps=deps)


def _prenorm_bwd_epi(accs, ex):
    hh, dh, gg = ex
    dx, dg = _rms_bwd(hh, gg, accs[0])
    return [dh + dx, dg]


def _ffn_dhid(tag, df, wd, gate, up, t, d, f, tm):
    def hid_epi(accs, ex):
        dhid = accs[0]
        gg, uu = ex[0].astype(F32), ex[1].astype(F32)
        s = _sig(gg)
        return [dhid * uu * s * (1.0 + gg * (1.0 - s)), dhid * gg * s]

    return _matmul(
        tag + "_dhid", [dict(a=df, b=wd, mode="nt", acc=0, tk=d)], m=t, n=f, tm=tm, tn=_tile(f, 1408),
        extras=[("mn", gate, 0), ("mn", up, 0)],
        outs=[("mn", (t, f), BF, 0)] * 2, epilogue=hid_epi, n_acc=1, j_outer=True)


def _ffn_dwd(tag, hid, df, t, d, f):
    tk_t = _tile(t, 512)
    return _matmul(
        tag + "_dwd", [dict(a=hid, b=df, mode="tn", acc=0, tk=tk_t)], m=f, n=d, tm=_tile(f, 1408), tn=d,
        nk=t // tk_t, outs=[("mn", (f, d), BF, 0)], epilogue=lambda accs, ex: accs, n_acc=1)


def _ffn_dwgu(tag, dgate, dup, xn, t, d, f, deps=()):
    tk_t = _tile(t, 512)
    return _matmul(
        tag + "_dwgu",
        [dict(a=dgate, b=xn, mode="tn", acc=0, tk=tk_t), dict(a=dup, b=xn, mode="tn", acc=1, tk=tk_t)],
        m=f, n=d, tm=_tile(f, 1408), tn=d, nk=t // tk_t,
        outs=[("mn", (f, d), BF, 0)] * 2, epilogue=lambda accs, ex: accs, n_acc=2, deps=deps)


def _ffn_dx(tag, dh, h_in, dgate, dup, wgt, wut, g_pre, t, d, f, tm, deps=()):
    tf = _tile(f, 1408)
    return _matmul(
        tag + "_dx",
        [dict(a=dgate, b=wgt, mode="nn", acc=0, tk=tf), dict(a=dup, b=wut, mode="nn", acc=0, tk=tf)],
        m=t, n=d, tm=tm, tn=d, nk=f // tf,
        extras=[("mn", h_in, 0), ("mn", dh, 0), ("n", g_pre)],
        outs=[("mn", (t, d), F32, 0), ("acc", (1, d), F32)], epilogue=_prenorm_bwd_epi, n_acc=1, deps=deps)


def _causal_mask():
    r = lax.broadcasted_iota(jnp.int32, (CHUNK, CHUNK), 0)
    c = lax.broadcasted_iota(jnp.int32, (CHUNK, CHUNK), 1)
    return r >= c


def _layernorm_parts(v):
    mu = jnp.mean(v, axis=-1, keepdims=True)
    vc = v - mu
    rstd = lax.rsqrt(jnp.mean(vc * vc, axis=-1, keepdims=True) + EPS)
    return vc * rstd, rstd


def _sgu_fwd(z, g_sgu, w_s, b_col, t, d, tm):
    dg = d // N_SGU_GROUPS
    n_chunk = tm // CHUNK

    def body(zu_ref, zv_ref, g_ref, w_ref, b_ref, a_ref):
        u = _gelu(zu_ref[...])
        vhat, _ = _layernorm_parts(_gelu(zv_ref[...]))
        vn = (vhat * g_ref[...]).astype(BF)
        mask = _causal_mask()
        for gi in range(N_SGU_GROUPS):
            ws = jnp.where(mask, w_ref[gi], 0.0).astype(BF)
            bias = b_ref[gi]
            for ci in range(n_chunk):
                rows, cols = slice(ci * CHUNK, (ci + 1) * CHUNK), slice(gi * dg, (gi + 1) * dg)
                sv = jnp.dot(ws, vn[rows, cols], preferred_element_type=F32) + bias
                a_ref[rows, cols] = (u[rows, cols] * sv).astype(BF)

    return pl.pallas_call(
        body,
        name="sgu_fwd",
        grid=(t // tm,),
        in_specs=[
            pl.BlockSpec((tm, d), lambda i: (i, 0)),
            pl.BlockSpec((tm, d), lambda i: (i, 1)),
            pl.BlockSpec((1, d), lambda i: (0, 0)),
            pl.BlockSpec(w_s.shape, lambda i: (0, 0, 0)),
            pl.BlockSpec(b_col.shape, lambda i: (0, 0, 0)),
        ],
        out_specs=pl.BlockSpec((tm, d), lambda i: (i, 0)),
        out_shape=jax.ShapeDtypeStruct((t, d), BF),
        compiler_params=_params(("arbitrary",)),
    )(z, z, g_sgu, w_s, b_col)


def _sgu_bwd(z, da, dz, g_sgu, w_s, b_col, t, d, tm):
    dg = d // N_SGU_GROUPS
    n_chunk = tm // CHUNK

    def body(zu_ref, zv_ref, da_ref, dz_in, g_ref, w_ref, b_ref, dz_ref, dw_ref, db_ref, dgn_ref, dvn_ref):
        del dz_in
        dzu_ref, dzv_ref = dz_ref.at[:, pl.ds(0, d)], dz_ref.at[:, pl.ds(d, d)]
        i = pl.program_id(0)
        zu, zv = zu_ref[...], zv_ref[...]
        u = _gelu(zu)
        vhat, rstd = _layernorm_parts(_gelu(zv))
        gn = g_ref[...]
        vn = (vhat * gn).astype(BF)
        da_v = da_ref[...].astype(F32)
        dsv_all = (da_v * u).astype(BF)
        mask = _causal_mask()
        for gi in range(N_SGU_GROUPS):
            ws = jnp.where(mask, w_ref[gi], 0.0).astype(BF)
            bias = b_ref[gi]
            dw = jnp.zeros((CHUNK, CHUNK), F32)
            dbias = jnp.zeros((CHUNK, 1), F32)
            for ci in range(n_chunk):
                rows, cols = slice(ci * CHUNK, (ci + 1) * CHUNK), slice(gi * dg, (gi + 1) * dg)
                vn_c, dsv = vn[rows, cols], dsv_all[rows, cols]
                sv = jnp.dot(ws, vn_c, preferred_element_type=F32) + bias
                dzu_ref[rows, cols] = (da_v[rows, cols] * sv * _gelu_grad(zu[rows, cols])).astype(BF)
                dw = dw + lax.dot_general(dsv, vn_c, _DN["nt"], preferred_element_type=F32)
                dbias = dbias + jnp.sum(dsv.astype(F32), axis=1, keepdims=True)
                dvn_ref[rows, cols] = lax.dot_general(ws, dsv, _DN["tn"], preferred_element_type=F32)
            dw = jnp.where(mask, dw, 0.0)

            @pl.when(i == 0)
            def _():
                dw_ref[gi] = dw
                db_ref[gi] = dbias

            @pl.when(i != 0)
            def _():
                dw_ref[gi] += dw
                db_ref[gi] += dbias

        dvn = dvn_ref[...]
        dgn = jnp.sum(dvn * vhat, axis=0, keepdims=True)

        @pl.when(i == 0)
        def _():
            dgn_ref[...] = dgn

        @pl.when(i != 0)
        def _():
            dgn_ref[...] += dgn

        dvh = dvn * gn
        dv = rstd * (dvh - jnp.mean(dvh, axis=-1, keepdims=True) - vhat * jnp.mean(dvh * vhat, axis=-1, keepdims=True))
        dzv_ref[...] = (dv * _gelu_grad(zv)).astype(BF)

    return pl.pallas_call(
        body,
        name="sgu_bwd",
        grid=(t // tm,),
        in_specs=[
            pl.BlockSpec((tm, d), lambda i: (i, 0)),
            pl.BlockSpec((tm, d), lambda i: (i, 1)),
            pl.BlockSpec((tm, d), lambda i: (i, 0)),
            pl.BlockSpec(memory_space=pl.ANY),
            pl.BlockSpec((1, d), lambda i: (0, 0)),
            pl.BlockSpec(w_s.shape, lambda i: (0, 0, 0)),
            pl.BlockSpec(b_col.shape, lambda i: (0, 0, 0)),
        ],
        out_specs=[
            pl.BlockSpec((tm, 2 * d), lambda i: (i, 0)),
            pl.BlockSpec(w_s.shape, lambda i: (0, 0, 0)),
            pl.BlockSpec(b_col.shape, lambda i: (0, 0, 0)),
            pl.BlockSpec((1, d), lambda i: (0, 0)),
        ],
        out_shape=[
            jax.ShapeDtypeStruct(dz.shape, BF),
            jax.ShapeDtypeStruct(w_s.shape, F32),
            jax.ShapeDtypeStruct(b_col.shape, F32),
            jax.ShapeDtypeStruct((1, d), F32),
        ],
        scratch_shapes=[pltpu.VMEM((tm, d), F32)],
        input_output_aliases={3: 0},
        compiler_params=_params(("arbitrary",)),
    )(z, z, da, dz, g_sgu, w_s, b_col)


def _shift_down(x, k, row):
    return jnp.where(row >= k, pltpu.roll(x, k, 0), 0.0)


def _shift_up(x, k, row, t):
    return jnp.where(row < t - k, pltpu.roll(x, t - k, 0), 0.0)


def _doublings(window):
    steps = int(math.log2(window))
    assert 2 ** steps == window
    return [2 ** s for s in range(steps)]


def _pool_diff(c, window, row):
    s = c
    for k in _doublings(window):
        s = s + _shift_down(s, k, row)
    count = jnp.minimum(row + 1, window).astype(F32)
    return s / count - c, count


def _pool_fwd(z, pool_w, pool_scale, t, d):
    dgp = d // len(POOL_WINDOWS)
    cblk = (2 * d) // dgp

    def body(zc_ref, w_ref, s_ref, b_ref):
        row = lax.broadcasted_iota(jnp.int32, (t, 1), 0)
        for gi, window in enumerate(POOL_WINDOWS):
            @pl.when(pl.program_id(0) == gi)
            def _(window=window):
                diff, _ = _pool_diff(zc_ref[...], window, row)
                out = jnp.dot(diff.astype(BF), w_ref[...], preferred_element_type=F32)
                b_ref[...] = (out * s_ref[...]).astype(BF)

    return pl.pallas_call(
        body,
        name="pool_fwd",
        grid=(len(POOL_WINDOWS),),
        in_specs=[
            pl.BlockSpec((t, dgp), lambda g: (0, cblk + g)),
            pl.BlockSpec((None, dgp, dgp), lambda g: (g, 0, 0)),
            pl.BlockSpec((1, dgp), lambda g: (0, g)),
        ],
        out_specs=pl.BlockSpec((t, dgp), lambda g: (0, g)),
        out_shape=jax.ShapeDtypeStruct((t, d), BF),
        compiler_params=_params(("arbitrary",)),
    )(z, pool_w, pool_scale)


def _pool_bwd(z, db, dz, pool_w, pool_scale, t, d):
    dgp = d // len(POOL_WINDOWS)
    cblk = (2 * d) // dgp

    def body(zc_ref, db_ref, dz_in, w_ref, s_ref, dzc_ref, dw_ref, ds_ref):
        del dz_in
        row = lax.broadcasted_iota(jnp.int32, (t, 1), 0)
        for gi, window in enumerate(POOL_WINDOWS):
            @pl.when(pl.program_id(0) == gi)
            def _(window=window):
                diff, count = _pool_diff(zc_ref[...], window, row)
                diff = diff.astype(BF)
                w = w_ref[...]
                dbv = db_ref[...].astype(F32)
                out = jnp.dot(diff, w, preferred_element_type=F32)
                ds_ref[...] = jnp.sum(dbv * out, axis=0, keepdims=True)
                dout = (dbv * s_ref[...]).astype(BF)
                dw_ref[...] = lax.dot_general(diff, dout, _DN["tn"], preferred_element_type=F32).astype(BF)
                ddiff = lax.dot_general(dout, w, _DN["nt"], preferred_element_type=F32)
                s = ddiff / count
                for k in _doublings(window):
                    s = s + _shift_up(s, k, row, t)
                dzc_ref[...] = (s - ddiff).astype(BF)

    return pl.pallas_call(
        body,
        name="pool_bwd",
        grid=(len(POOL_WINDOWS),),
        in_specs=[
            pl.BlockSpec((t, dgp), lambda g: (0, cblk + g)),
            pl.BlockSpec((t, dgp), lambda g: (0, g)),
            pl.BlockSpec(memory_space=pl.ANY),
            pl.BlockSpec((None, dgp, dgp), lambda g: (g, 0, 0)),
            pl.BlockSpec((1, dgp), lambda g: (0, g)),
        ],
        out_specs=[
            pl.BlockSpec((t, dgp), lambda g: (0, cblk + g)),
            pl.BlockSpec((None, dgp, dgp), lambda g: (g, 0, 0)),
            pl.BlockSpec((1, dgp), lambda g: (0, g)),
        ],
        out_shape=[
            jax.ShapeDtypeStruct(dz.shape, BF),
            jax.ShapeDtypeStruct(pool_w.shape, BF),
            jax.ShapeDtypeStruct((1, d), F32),
        ],
        input_output_aliases={2: 0},
        compiler_params=_params(("arbitrary",)),
    )(z, db, dz, pool_w, pool_scale)


def _mix_dy(dm, w_o, z, ya, yb, t, d, tm):
    def body(dm_ref, w_ref, gate_ref, ya_ref, yb_ref, dyab_ref, dz_ref, dy_ref):
        j = pl.program_id(1)

        @pl.when(j == 0)
        def _():
            dy_ref[...] = lax.dot_general(dm_ref[...], w_ref[...], _DN["nt"], preferred_element_type=F32)

        dy = dy_ref[...]
        s = _sig(gate_ref[...])
        yv = jnp.where(j == 0, ya_ref[...], yb_ref[...]).astype(F32)
        dyab_ref[...] = (dy * s).astype(BF)
        dz_ref[...] = (dy * yv * s * (1.0 - s)).astype(BF)

    row = pl.BlockSpec((tm, d), lambda i, j: (i, 0))
    return pl.pallas_call(
        body,
        name="mix_dy",
        grid=(t // tm, 2),
        in_specs=[row, pl.BlockSpec((d, d), lambda i, j: (0, 0)), pl.BlockSpec((tm, d), lambda i, j: (i, 3 + j)), row, row],
        out_specs=[pl.BlockSpec((tm, d), lambda i, j: (i, j)), pl.BlockSpec((tm, d), lambda i, j: (i, 3 + j))],
        out_shape=[jax.ShapeDtypeStruct((t, 2 * d), BF), jax.ShapeDtypeStruct((t, 5 * d), BF)],
        scratch_shapes=[pltpu.VMEM((tm, d), F32)],
        compiler_params=_params(("arbitrary", "arbitrary")),
    )(dm, w_o, z, ya, yb)


def _everyone():
    x, y, c = _place()
    return [(1 - x if dd & 4 else x, 1 - y if dd & 2 else y, 1 - c if dd & 1 else c) for dd in range(1, N_DEV)]


def _direct_plan(n):
    def plan(refs, send_sems, recv_sems):
        x, y, c = _place()
        me = 4 * x + 2 * y + c
        cps = []
        for i in range(n):
            for k, (px, py, pc) in enumerate(_everyone()):
                cps.append(_remote(refs[i].at[4 * px + 2 * py + pc], refs[n + i].at[me], send_sems, recv_sems,
                                   7 * i + k, (px, py, pc)))
        return cps

    def local(refs):
        x, y, c = _place()
        me = 4 * x + 2 * y + c
        return [(refs[i].at[me], refs[n + i].at[me]) for i in range(n)]

    return plan, local


def _sc_scatter_direct(name, collective_id, parts):
    n = len(parts)
    p8 = [a.reshape(N_DEV, a.shape[0] // N_DEV, a.shape[1]) for a in parts]
    plan, local = _direct_plan(n)
    return _sc_exchange(name, collective_id, p8, [jax.ShapeDtypeStruct(a.shape, a.dtype) for a in p8], plan,
                        7 * n, _everyone, local=local)


def _pack_rows(flat_list, width):
    flat = jnp.concatenate([a.reshape(-1) for a in flat_list])
    rows = -(-flat.shape[0] // (8 * width)) * 8
    return jnp.pad(flat, (0, rows * width - flat.shape[0])).reshape(rows, width)


def _unpack_rows(packed, like):
    flat, out, pos = packed.reshape(-1), [], 0
    for a in like:
        out.append(flat[pos:pos + a.size].reshape(a.shape))
        pos += a.size
    return out


def kernel(x, p, ffn1_pre_g, ffn1_w_gate, ffn1_w_up, ffn1_w_down, ffn1_post_g, mix_pre_g, w_in, sgu_norm_g, sgu_w, sgu_b, pool_w, pool_scale, w_out_a, w_out_b, w_o, mix_post_g, ffn2_pre_g, ffn2_w_gate, ffn2_w_up, ffn2_w_down, ffn2_post_g, ple_pre_g, ple_w_gate, ple_w_proj, ple_post_g, loss_target, m_ffn1_pre_g, m_ffn1_w_gate, m_ffn1_w_up, m_ffn1_w_down, m_ffn1_post_g, m_mix_pre_g, m_w_in, m_sgu_norm_g, m_sgu_w, m_sgu_b, m_pool_w, m_pool_scale, m_w_out_a, m_w_out_b, m_w_o, m_mix_post_g, m_ffn2_pre_g, m_ffn2_w_gate, m_ffn2_w_up, m_ffn2_w_down, m_ffn2_post_g, m_ple_pre_g, m_ple_w_gate, m_ple_w_proj, m_ple_post_g, v_ffn1_pre_g, v_ffn1_w_gate, v_ffn1_w_up, v_ffn1_w_down, v_ffn1_post_g, v_mix_pre_g, v_w_in, v_sgu_norm_g, v_sgu_w, v_sgu_b, v_pool_w, v_pool_scale, v_w_out_a, v_w_out_b, v_w_o, v_mix_post_g, v_ffn2_pre_g, v_ffn2_w_gate, v_ffn2_w_up, v_ffn2_w_down, v_ffn2_post_g, v_ple_pre_g, v_ple_w_gate, v_ple_w_proj, v_ple_post_g):
    args = dict(locals())
    names = ["ffn1_pre_g", "ffn1_w_gate", "ffn1_w_up", "ffn1_w_down", "ffn1_post_g", "mix_pre_g", "w_in",
             "sgu_norm_g", "sgu_w", "sgu_b", "pool_w", "pool_scale", "w_out_a", "w_out_b", "w_o", "mix_post_g",
             "ffn2_pre_g", "ffn2_w_gate", "ffn2_w_up", "ffn2_w_down", "ffn2_post_g", "ple_pre_g", "ple_w_gate",
             "ple_w_proj", "ple_post_g"]
    w = {k: args[k][0] for k in names}
    mom = {k: args["m_" + k][0] for k in names}
    var = {k: args["v_" + k][0] for k in names}

    assert x.shape[0] == 1 and p.shape[:2] == (1, 1)
    t, d = x.shape[1], x.shape[2]
    f = ffn1_w_gate.shape[2] * N_DEV
    d_in = w_in.shape[2] * N_DEV
    d_ple = p.shape[3]
    n_pool = len(POOL_WINDOWS)
    dgp = d // n_pool
    assert d_in == 5 * d and t % CHUNK == 0
    tm = _tile(t, 512, CHUNK)
    xs, ps, target = x[0], p[0, 0], loss_target[0]

    col_sharded = ("ffn1_w_gate", "ffn1_w_up", "ffn2_w_gate", "ffn2_w_up", "w_in", "ple_w_proj")

    def shard_of(k):
        if k in col_sharded:
            return w[k].T.astype(BF)
        if k == "pool_w":
            return w[k].reshape(-1, dgp).astype(BF)
        return w[k].astype(BF)

    rows_pw = dgp // N_DEV
    b_col = w["sgu_b"][:, :, None]
    gains = {k: w[k][None, :] for k in names if w[k].ndim == 1}
    full = {}

    groups = [["ffn1_w_gate", "ffn1_w_up"], ["ffn1_w_down", "w_in"], ["pool_w", "w_out_a", "w_out_b", "w_o"],
              ["ffn2_w_gate", "ffn2_w_up", "ffn2_w_down", "ple_w_gate", "ple_w_proj"]]
    def gather(gi, behind=None):
        shards = [shard_of(k) for k in groups[gi]]
        if behind is not None:
            shards = lax.optimization_barrier((shards, behind))[0]
        full.update(zip(groups[gi], _sc_gather("gather%d" % gi, gi, shards)))

    gather(0)
    xn1 = _prenorm("ffn1_prenorm", xs, gains["ffn1_pre_g"], t, d, tm)
    gather(1, xn1)
    g1, u1, hid1 = _ffn_up("ffn1", xn1, full["ffn1_w_gate"], full["ffn1_w_up"], t, d, f, tm)
    gather(2, hid1)
    f1, h1 = _ffn_down("ffn1", hid1, full["ffn1_w_down"], xs, gains["ffn1_post_g"], t, d, f, tm)
    gather(3, h1)

    xn2 = _prenorm("mix_prenorm", h1, gains["mix_pre_g"], t, d, tm)
    pool_full = full["pool_w"].reshape(N_DEV, n_pool, rows_pw, dgp).transpose(1, 0, 2, 3).reshape(n_pool, dgp, dgp)
    z = _matmul("mix_in", [dict(a=xn2, b=full["w_in"], mode="nt", acc=0, tk=d)], m=t, n=d_in, tm=tm, tn=d,
                outs=[("mn", (t, d_in), F32, 0)], epilogue=lambda accs, ex: accs, n_acc=1, j_outer=True)[0]
    a_br = _sgu_fwd(z, gains["sgu_norm_g"], w["sgu_w"], b_col, t, d, tm)
    b_br = _pool_fwd(z, pool_full, gains["pool_scale"], t, d)

    def merge_epi(accs, ex):
        ya, yb = accs
        ga, gb = ex
        return [ya, yb, _sig(ga) * ya + _sig(gb) * yb]

    ya, yb, y = _matmul(
        "mix_merge",
        [dict(a=a_br, b=full["w_out_a"], mode="nn", acc=0, tk=d), dict(a=b_br, b=full["w_out_b"], mode="nn", acc=1, tk=d)],
        m=t, n=d, tm=tm, tn=d, extras=[("mn", z, 3), ("mn", z, 4)],
        outs=[("mn", (t, d), BF, 0)] * 3, epilogue=merge_epi, n_acc=2)

    def proj_epi(accs, ex):
        mm = accs[0]
        hh, gg = ex
        return [mm, hh + mm * _rms_r(mm) * gg]

    m_out, h2 = _matmul(
        "mix_proj", [dict(a=y, b=full["w_o"], mode="nn", acc=0, tk=d)], m=t, n=d, tm=tm, tn=d,
        extras=[("mn", h1, 0), ("n", gains["mix_post_g"])],
        outs=[("mn", (t, d), F32, 0)] * 2, epilogue=proj_epi, n_acc=1)

    xn3 = _prenorm("ffn2_prenorm", h2, gains["ffn2_pre_g"], t, d, tm)
    g2, u2, hid2 = _ffn_up("ffn2", xn3, full["ffn2_w_gate"], full["ffn2_w_up"], t, d, f, tm)
    f2, h3 = _ffn_down("ffn2", hid2, full["ffn2_w_down"], h2, gains["ffn2_post_g"], t, d, f, tm)

    xn4 = _prenorm("ple_prenorm", h3, gains["ple_pre_g"], t, d, tm)

    def ple_epi(accs, ex):
        gl, e = accs
        hh, tgt, gg = ex
        q = _sig(gl) * e
        err = hh + q * _rms_r(q) * gg - tgt
        return [gl, e, err * (1.0 / d), jnp.sum(err * err, axis=0, keepdims=True)]

    gl, e_ple, dh4, loss_vec = _matmul(
        "ple_fwd",
        [dict(a=xn4, b=full["ple_w_gate"], mode="nn", acc=0, tk=d), dict(a=ps, b=full["ple_w_proj"], mode="nt", acc=1, tk=d_ple)],
        m=t, n=d, tm=tm, tn=d, extras=[("mn", h3, 0), ("mn", target, 0), ("n", gains["ple_post_g"])],
        outs=[("mn", (t, d), F32, 0)] * 3 + [("acc", (1, d), F32)], epilogue=ple_epi, n_acc=2)
    loss = lax.psum(jnp.sum(loss_vec) * (0.5 / d), AXES)

    tk_t = _tile(t, 512)

    def ple_post_fn(v):
        dy, gl_v, e_v, gg = v
        s = _sig(gl_v)
        dq, dg = _rms_bwd(s * e_v, gg, dy)
        return [dq * e_v * s * (1.0 - s), dq * s, dg]

    dgl, de, dg_ple_post = _rowwise(
        "ple_post_bwd", [("row", dh4, d, 0), ("row", gl, d, 0), ("row", e_ple, d, 0), ("full", gains["ple_post_g"])],
        [("row", (t, d), BF, d, 0), ("row", (t, d), BF, d, 0), ("acc", (1, d), F32)], ple_post_fn, t=t, tm=tm)
    ident = lambda accs, ex: accs
    dw_ple_gate = _matmul("ple_dwg", [dict(a=xn4, b=dgl, mode="tn", acc=0, tk=tk_t)], m=d, n=d, tm=d, tn=d,
                          nk=t // tk_t, outs=[("mn", (d, d), BF, 0)], epilogue=ident, n_acc=1)[0]
    dw_ple_proj_t = _matmul("ple_dwp", [dict(a=de, b=ps, mode="tn", acc=0, tk=tk_t)], m=d, n=d_ple, tm=d, tn=d_ple,
                            nk=t // tk_t, outs=[("mn", (d, d_ple), BF, 0)], epilogue=ident, n_acc=1)[0]
    dh3, dg_ple_pre = _matmul(
        "ple_dx", [dict(a=dgl, b=full["ple_w_gate"], mode="nt", acc=0, tk=d)], m=t, n=d, tm=tm, tn=d,
        extras=[("mn", h3, 0), ("mn", dh4, 0), ("n", gains["ple_pre_g"])],
        outs=[("mn", (t, d), F32, 0), ("acc", (1, d), F32)], epilogue=_prenorm_bwd_epi, n_acc=1)

    staged = []
    n_gather = len(groups)

    def scatter(keys, parts):
        k = len(staged)
        staged.append((keys, _sc_scatter_direct("scatter%d" % k, n_gather + k, parts)))

    df2, dg_f2_post = _postnorm_bwd("ffn2_post_bwd", dh3, f2, gains["ffn2_post_g"], 0.5, t, d, tm)
    dgate2, dup2 = _ffn_dhid("ffn2", df2, full["ffn2_w_down"], g2, u2, t, d, f, tm)
    dwd2 = _ffn_dwd("ffn2", hid2, df2, t, d, f)[0]
    dwg2, dwu2 = _ffn_dwgu("ffn2", dgate2, dup2, xn3, t, d, f)
    scatter(["ple_w_gate", "ple_w_proj", "ffn2_w_down", "ffn2_w_gate", "ffn2_w_up"],
            [dw_ple_gate, dw_ple_proj_t, dwd2, dwg2, dwu2])
    dh2, dg_f2_pre = _ffn_dx("ffn2", dh3, h2, dgate2, dup2, full["ffn2_w_gate"], full["ffn2_w_up"],
                             gains["ffn2_pre_g"], t, d, f, tm, deps=[dw_ple_gate, dw_ple_proj_t, dwd2, dwg2])

    dm, dg_mix_post = _postnorm_bwd("mix_post_bwd", dh2, m_out, gains["mix_post_g"], 1.0, t, d, tm)
    dyab, dz = _mix_dy(dm, full["w_o"], z, ya, yb, t, d, tm)
    da, db = _matmul(
        "mix_dab",
        [dict(a=dyab, b=full["w_out_a"], mode="nt", acc=0, tk=d),
         dict(a=dyab, b=full["w_out_b"], mode="nt", acc=1, tk=d, a_off=(0, 1))],
        m=t, n=d, tm=tm, tn=d, outs=[("mn", (t, d), BF, 0)] * 2, epilogue=ident, n_acc=2)
    dz, dsgu_w, dsgu_b, dg_sgu = _sgu_bwd(z, da, dz, gains["sgu_norm_g"], w["sgu_w"], b_col, t, d, _tile(t, 256, CHUNK))
    dz, dpool_w, dg_pool = _pool_bwd(z, db, dz, pool_full, gains["pool_scale"], t, d)
    dw_in_t = _matmul(
        "mix_dwin", [dict(a=dz, b=xn2, mode="tn", acc=0, tk=tk_t)], m=d_in, n=d, tm=d, tn=d,
        nk=t // tk_t, outs=[("mn", (d_in, d), BF, 0)], epilogue=ident, n_acc=1)[0]
    dpool_rows = dpool_w.reshape(n_pool, N_DEV, rows_pw, dgp).transpose(1, 0, 2, 3).reshape(N_DEV * n_pool * rows_pw, dgp)
    scatter(["pool_w", "w_in"], [dpool_rows, dw_in_t])
    dh1, dg_mix_pre = _matmul(
        "mix_dx", [dict(a=dz, b=full["w_in"], mode="nn", acc=0, tk=d)], m=t, n=d, tm=tm, tn=d, nk=d_in // d,
        extras=[("mn", h1, 0), ("mn", dh2, 0), ("n", gains["mix_pre_g"])],
        outs=[("mn", (t, d), F32, 0), ("acc", (1, d), F32)], epilogue=_prenorm_bwd_epi, n_acc=1,
        deps=[dpool_rows, dw_in_t])

    df1, dg_f1_post = _postnorm_bwd("ffn1_post_bwd", dh1, f1, gains["ffn1_post_g"], 0.5, t, d, tm)
    dgate1, dup1 = _ffn_dhid("ffn1", df1, full["ffn1_w_down"], g1, u1, t, d, f, tm)
    dwd1 = _ffn_dwd("ffn1", hid1, df1, t, d, f)[0]
    scatter(["ffn1_w_down"], [dwd1])
    dwg1, dwu1 = _ffn_dwgu("ffn1", dgate1, dup1, xn1, t, d, f, deps=[dwd1])
    scatter(["ffn1_w_gate", "ffn1_w_up"], [dwg1, dwu1])
    grad_x, dg_f1_pre = _ffn_dx("ffn1", dh1, xs, dgate1, dup1, full["ffn1_w_gate"], full["ffn1_w_up"],
                                gains["ffn1_pre_g"], t, d, f, tm, deps=[dwg1])
    dw_o = _matmul("mix_dwo", [dict(a=y, b=dm, mode="tn", acc=0, tk=tk_t)], m=d, n=d, tm=d, tn=d, nk=t // tk_t,
                   outs=[("mn", (d, d), BF, 0)], epilogue=ident, n_acc=1, deps=[grad_x])[0]
    dw_out_a, dw_out_b = _matmul(
        "mix_dwab",
        [dict(a=a_br, b=dyab, mode="tn", acc=0, tk=tk_t),
         dict(a=b_br, b=dyab, mode="tn", acc=1, tk=tk_t, b_off=(0, 1))],
        m=d, n=d, tm=d, tn=d, nk=t // tk_t, outs=[("mn", (d, d), BF, 0)] * 2, epilogue=ident, n_acc=2, deps=[dw_o])
    scatter(["w_o", "w_out_a", "w_out_b"], [dw_o, dw_out_a, dw_out_b])

    grad, delta, new_m, new_v = {}, {}, {}, {}

    def update(group, partial_sums, behind):
        done = []
        for k, st in zip(group, partial_sums):
            if k in col_sharded:
                grad[k] = _reduce_only("sum_" + k, st, deps=behind).T
                delta[k], new_m[k], new_v[k] = _adam_only("adam_" + k, w[k], grad[k], mom[k], var[k])
            else:
                shape = w[k].shape
                flat = (-1, shape[-1])
                res = _reduce_adam("adam_" + k, st, w[k].reshape(flat), mom[k].reshape(flat), var[k].reshape(flat),
                                   deps=behind)
                grad[k], delta[k], new_m[k], new_v[k] = [r.reshape(shape) for r in res]
            done.append(delta[k])
        return done

    small_grads = {
        "ffn1_pre_g": dg_f1_pre, "ffn1_post_g": dg_f1_post, "mix_pre_g": dg_mix_pre, "sgu_norm_g": dg_sgu,
        "sgu_w": dsgu_w, "sgu_b": dsgu_b, "pool_scale": dg_pool, "mix_post_g": dg_mix_post, "ffn2_pre_g": dg_f2_pre,
        "ffn2_post_g": dg_f2_post, "ple_pre_g": dg_ple_pre, "ple_post_g": dg_ple_post,
    }
    small_names = [k for k in names if k in small_grads]
    small_packed = _pack_rows([small_grads[k] for k in small_names], d)
    plan_s, local_s = _broadcast_plan()
    staged_small = _sc_exchange(
        "small", n_gather + len(staged), [small_packed],
        [jax.ShapeDtypeStruct((N_DEV,) + small_packed.shape, F32)], plan_s, N_DEV - 1, _everyone, local=local_s)[0]
    done = [dw_out_a]
    for keys, partial_sums in staged:
        done = update(keys, partial_sums, done)
    small_like = [w[k] for k in small_names]
    res = _reduce_adam(
        "adam_small", staged_small, _pack_rows(small_like, d), _pack_rows([mom[k] for k in small_names], d),
        _pack_rows([var[k] for k in small_names], d), deps=done)
    for dst, packed in zip((grad, delta, new_m, new_v), res):
        dst.update(zip(small_names, _unpack_rows(packed, small_like)))

    out = [loss, grad_x[None]]
    for group in (grad, delta, new_m, new_v):
        out += [group[k][None] for k in names]
    return tuple(out)
```

```python
import math

import jax
import jax.numpy as jnp
from jax import lax
from jax.experimental import pallas as pl
from jax.experimental.pallas import tpu as pltpu
from jax.experimental.pallas import tpu_sc as plsc

EPS = 1e-6
CHUNK = 128
N_SGU_GROUPS = 4
POOL_WINDOWS = (2, 4, 8, 16)
ADAM_LR = 0.001
ADAM_B1 = 0.9
ADAM_B2 = 0.999
ADAM_EPS = 1e-08
ADAM_WD = 0.01
ADAM_STEP = 10

N_DEV = 8
AXES = ("x", "y", "c")
MESH = pl.DeviceIdType.MESH
V7X_VMEM_BYTES = 64 * 1024 * 1024
VMEM_LIMIT = V7X_VMEM_BYTES - 8 * 1024 * 1024
TOKEN_K_TILE = 1024
BF = jnp.bfloat16
F32 = jnp.float32

_DN = {
    "nn": (((1,), (0,)), ((), ())),
    "nt": (((1,), (1,)), ((), ())),
    "tn": (((0,), (0,)), ((), ())),
}


def _params(sem=None):
    return pltpu.CompilerParams(dimension_semantics=sem, vmem_limit_bytes=VMEM_LIMIT)


def _tile(n, target, align=128):
    t = min(n, target)
    t -= t % align
    while t >= align:
        if n % t == 0:
            return t
        t -= align
    return n


def _sig(x):
    return 0.5 + 0.5 * jnp.tanh(0.5 * x)


_GELU_K = math.sqrt(2.0 / math.pi)
_GELU_C = 0.044715


def _gelu(x):
    return 0.5 * x * (1.0 + jnp.tanh(_GELU_K * (x + _GELU_C * x * x * x)))


def _gelu_grad(x):
    t = jnp.tanh(_GELU_K * (x + _GELU_C * x * x * x))
    return 0.5 * (1.0 + t) + 0.5 * x * (1.0 - t * t) * _GELU_K * (1.0 + 3.0 * _GELU_C * x * x)


def _rms_r(x):
    return lax.rsqrt(jnp.mean(x * x, axis=-1, keepdims=True) + EPS)


def _rms_bwd(x, g, dy):
    r = _rms_r(x)
    xh = x * r
    gy = dy * g
    dx = r * (gy - xh * jnp.mean(xh * gy, axis=-1, keepdims=True))
    return dx, jnp.sum(dy * xh, axis=0, keepdims=True)


def _matmul(name, mats, *, m, n, tm, tn, nk=1, extras=(), outs, epilogue, n_acc, j_outer=False, deps=()):
    ni, nj = m // tm, n // tn
    assert ni * tm == m and nj * tn == n, (name, m, n, tm, tn)
    if j_outer:
        grid = (nj, ni, nk)

        def ij(g0, g1):
            return g1, g0
    else:
        grid = (ni, nj, nk)

        def ij(g0, g1):
            return g0, g1

    in_specs, args = [], []
    for mt in mats:
        mode, tk = mt["mode"], mt["tk"]
        ao, bo = mt.get("a_off", (0, 0)), mt.get("b_off", (0, 0))
        if mode == "tn":
            sa = pl.BlockSpec((tk, tm), lambda g0, g1, kk, ao=ao: (ao[0] + kk, ao[1] + ij(g0, g1)[0]))
        else:
            sa = pl.BlockSpec((tm, tk), lambda g0, g1, kk, ao=ao: (ao[0] + ij(g0, g1)[0], ao[1] + kk))
        if mode == "nt":
            sb = pl.BlockSpec((tn, tk), lambda g0, g1, kk, bo=bo: (bo[0] + ij(g0, g1)[1], bo[1] + kk))
        else:
            sb = pl.BlockSpec((tk, tn), lambda g0, g1, kk, bo=bo: (bo[0] + kk, bo[1] + ij(g0, g1)[1]))
        in_specs += [sa, sb]
        args += [mt["a"], mt["b"]]
    n_mat_refs = len(args)
    for ex in extras:
        if ex[0] == "mn":
            in_specs.append(pl.BlockSpec((tm, tn), lambda g0, g1, kk, c=ex[2]: (ij(g0, g1)[0], c + ij(g0, g1)[1])))
        else:
            in_specs.append(pl.BlockSpec((1, tn), lambda g0, g1, kk: (0, ij(g0, g1)[1])))
        args.append(ex[1])
    n_ex_end = len(args)
    in_specs += [pl.BlockSpec(memory_space=pl.ANY)] * len(deps)
    args += list(deps)
    n_in = len(args)
    out_specs, out_shape = [], []
    for o in outs:
        if o[0] == "mn":
            out_specs.append(pl.BlockSpec((tm, tn), lambda g0, g1, kk, c=o[3]: (ij(g0, g1)[0], c + ij(g0, g1)[1])))
        else:
            assert nj == 1, name
            out_specs.append(pl.BlockSpec((1, tn), lambda g0, g1, kk: (0, 0)))
        out_shape.append(jax.ShapeDtypeStruct(o[1], o[2]))
    n_out = len(outs)

    def body(*refs):
        mat_refs = refs[:n_mat_refs]
        ex_refs = refs[n_mat_refs:n_ex_end]
        out_refs = refs[n_in:n_in + n_out]
        acc_refs = refs[n_in + n_out:]
        i = ij(pl.program_id(0), pl.program_id(1))[0]
        kk = pl.program_id(2)

        def products():
            res = [None] * n_acc
            for idx, mt in enumerate(mats):
                a = mat_refs[2 * idx][...].astype(BF)
                b = mat_refs[2 * idx + 1][...].astype(BF)
                p = lax.dot_general(a, b, _DN[mt["mode"]], preferred_element_type=F32)
                q = mt["acc"]
                res[q] = p if res[q] is None else res[q] + p
            return res

        def finish(accs):
            vals = epilogue(accs, [r[...] for r in ex_refs])
            for o, r, v in zip(outs, out_refs, vals):
                if o[0] == "mn":
                    r[...] = v.astype(o[2])
                else:
                    @pl.when(i == 0)
                    def _():
                        r[...] = v

                    @pl.when(i != 0)
                    def _():
                        r[...] += v

        if nk == 1:
            finish(products())
        else:
            res = products()

            @pl.when(kk == 0)
            def _():
                for q in range(n_acc):
                    acc_refs[q][...] = res[q]

            @pl.when(kk != 0)
            def _():
                for q in range(n_acc):
                    acc_refs[q][...] += res[q]

            @pl.when(kk == nk - 1)
            def _():
                finish([r[...] for r in acc_refs])

    scratch = [pltpu.VMEM((tm, tn), F32) for _ in range(n_acc)] if nk > 1 else []
    return pl.pallas_call(
        body,
        name=name,
        grid=grid,
        in_specs=in_specs,
        out_specs=out_specs,
        out_shape=out_shape,
        scratch_shapes=scratch,
        compiler_params=_params(("arbitrary", "arbitrary", "arbitrary")),
    )(*args)


def _rowwise(name, ins, outs, fn, *, t, tm, deps=()):
    ni = t // tm
    assert ni * tm == t, (name, t, tm)
    in_specs, args = [], []
    for s in ins:
        if s[0] == "row":
            in_specs.append(pl.BlockSpec((tm, s[2]), lambda i, c=s[3]: (i, c)))
        else:
            nd = s[1].ndim
            in_specs.append(pl.BlockSpec(s[1].shape, lambda i, nd=nd: (0,) * nd))
        args.append(s[1])
    out_specs, out_shape = [], []
    for o in outs:
        if o[0] == "row":
            out_specs.append(pl.BlockSpec((tm, o[3]), lambda i, c=o[4]: (i, c)))
        else:
            nd = len(o[1])
            out_specs.append(pl.BlockSpec(o[1], lambda i, nd=nd: (0,) * nd))
        out_shape.append(jax.ShapeDtypeStruct(o[1], o[2]))
    n_read = len(args)
    in_specs += [pl.BlockSpec(memory_space=pl.ANY)] * len(deps)
    args += list(deps)
    n_in = len(args)

    def body(*refs):
        i = pl.program_id(0)
        vals = fn([r[...] for r in refs[:n_read]])
        for o, r, v in zip(outs, refs[n_in:], vals):
            if o[0] == "row":
                r[...] = v.astype(o[2])
            else:
                @pl.when(i == 0)
                def _():
                    r[...] = v

                @pl.when(i != 0)
                def _():
                    r[...] += v

    return pl.pallas_call(
        body,
        name=name,
        grid=(ni,),
        in_specs=in_specs,
        out_specs=out_specs,
        out_shape=out_shape,
        compiler_params=_params(("arbitrary",)),
    )(*args)


def _place():
    return lax.axis_index("x"), lax.axis_index("y"), lax.axis_index("c")


def _remote(src, dst, send_sems, recv_sems, k, to):
    return pltpu.make_async_remote_copy(
        src_ref=src, dst_ref=dst, send_sem=send_sems.at[k], recv_sem=recv_sems.at[k],
        device_id=to, device_id_type=MESH)


def _chips(x, y):
    return [(1 - x, y), (x, 1 - y), (1 - x, 1 - y)]


def _broadcast_plan():
    def plan(refs, send_sems, recv_sems):
        x, y, c = _place()
        cps = []
        for dd in range(1, N_DEV):
            peer = (1 - x if dd & 4 else x, 1 - y if dd & 2 else y, 1 - c if dd & 1 else c)
            cps.append(_remote(refs[0], refs[1].at[4 * x + 2 * y + c], send_sems, recv_sems, dd - 1, peer))
        return cps

    def local(refs):
        x, y, c = _place()
        return [(refs[0], refs[1].at[4 * x + 2 * y + c])]

    return plan, local


def _handshake(peers):
    barrier = pltpu.get_barrier_semaphore()
    for peer in peers:
        pl.semaphore_signal(barrier, inc=1, device_id=peer, device_id_type=MESH)
    pl.semaphore_wait(barrier, len(peers))


def _sequencer_call(name, collective_id, body, args, out_type, scratch_types):
    return pl.kernel(
        body,
        out_type=out_type,
        mesh=plsc.ScalarSubcoreMesh(axis_name="sequencer", num_cores=1),
        scratch_types=scratch_types,
        compiler_params=pltpu.CompilerParams(collective_id=collective_id),
        name=name,
    )(*args)


def _sc_gather(name, collective_id, shards):
    n = len(shards)

    def body(*refs):
        ins, outs = refs[:n], refs[n:2 * n]
        send_sems, recv_sems, local_sems = refs[2 * n:]
        x, y, c = _place()
        me, sibling = (x, y, c), (x, y, 1 - c)
        chips = _chips(x, y)
        _handshake([sibling] + [(*chip, c) for chip in chips])

        def slot(i, px, py, pc):
            return outs[i].at[4 * px + 2 * py + pc]

        def copy(i, k, block, to, src=None):
            return _remote(slot(i, *block) if src is None else src, slot(i, *block), send_sems, recv_sems, 7 * i + k, to)

        mine = [pltpu.make_async_copy(ins[i], slot(i, *me), local_sems.at[i]) for i in range(n)]
        for cp in mine:
            cp.start()
        first = []
        for i in range(n):
            first += [copy(i, 1 + j, me, (*chip, c), src=ins[i]) for j, chip in enumerate(chips)]
            first.append(copy(i, 0, me, sibling, src=ins[i]))
        for cp in first:
            cp.start()
        passed = []
        for i in range(n):
            for j, chip in enumerate(chips):
                copy(i, 1 + j, (*chip, c), me).wait_recv()
                fwd = copy(i, 4 + j, (*chip, c), sibling)
                fwd.start()
                passed.append(fwd)
        for i in range(n):
            copy(i, 0, sibling, me).wait_recv()
            for j, chip in enumerate(chips):
                copy(i, 4 + j, (*chip, 1 - c), me).wait_recv()
        for cp in first + passed:
            cp.wait_send()
        for cp in mine:
            cp.wait()

    outs = _sequencer_call(
        name, collective_id, body, shards,
        [jax.ShapeDtypeStruct((N_DEV,) + s.shape, s.dtype) for s in shards],
        [pltpu.SemaphoreType.DMA((7 * n,)), pltpu.SemaphoreType.DMA((7 * n,)), pltpu.SemaphoreType.DMA((n,))])
    return [o.reshape((N_DEV * s.shape[0],) + s.shape[1:]) for o, s in zip(outs, shards)]


def _sc_exchange(name, collective_id, srcs, land_shapes, plan, n_copies, peers, local=None):
    n = len(srcs)

    def body(*refs):
        bufs, send_sems, recv_sems = refs[:2 * n], refs[2 * n], refs[2 * n + 1]
        _handshake(peers())
        if local is not None:
            mine = [pltpu.make_async_copy(a, b, refs[2 * n + 2].at[k]) for k, (a, b) in enumerate(local(bufs))]
            for cp in mine:
                cp.start()
        cps = plan(bufs, send_sems, recv_sems)
        for cp in cps:
            cp.start()
        for cp in cps:
            cp.wait_send()
            cp.wait_recv()
        if local is not None:
            for cp in mine:
                cp.wait()

    scratch = [pltpu.SemaphoreType.DMA((n_copies,)), pltpu.SemaphoreType.DMA((n_copies,))]
    if local is not None:
        scratch.append(pltpu.SemaphoreType.DMA((n,)))
    return list(_sequencer_call(name, collective_id, body, srcs, land_shapes, scratch))


def _adam_vals(w, g, m, v):
    m = ADAM_B1 * m + (1.0 - ADAM_B1) * g
    v = ADAM_B2 * v + (1.0 - ADAM_B2) * (g * g)
    m_hat = m / (1.0 - ADAM_B1 ** ADAM_STEP)
    v_hat = v / (1.0 - ADAM_B2 ** ADAM_STEP)
    delta = -ADAM_LR * (m_hat / (jnp.sqrt(v_hat) + ADAM_EPS) + ADAM_WD * w)
    return delta, m, v


def _sum_parts(st_ref):
    g = st_ref[0].astype(F32)
    for k in range(1, st_ref.shape[0]):
        g = g + st_ref[k].astype(F32)
    return g


def _reduce_adam(name, st, w, m, v, deps=()):
    rows, cols = w.shape
    tr = _tile(rows, 512, 16)

    def body(st_ref, w_ref, m_ref, v_ref, *rest):
        g_out, d_out, m_out, v_out = rest[len(deps):]
        g = _sum_parts(st_ref)
        d, mm, vv = _adam_vals(w_ref[...], g, m_ref[...], v_ref[...])
        g_out[...] = g
        d_out[...] = d
        m_out[...] = mm
        v_out[...] = vv

    blk = pl.BlockSpec((tr, cols), lambda i: (i, 0))
    return pl.pallas_call(
        body,
        name=name,
        grid=(rows // tr,),
        in_specs=[pl.BlockSpec((st.shape[0], tr, cols), lambda i: (0, i, 0)), blk, blk, blk]
        + [pl.BlockSpec(memory_space=pl.ANY)] * len(deps),
        out_specs=[blk] * 4,
        out_shape=[jax.ShapeDtypeStruct(w.shape, F32)] * 4,
        compiler_params=_params(("arbitrary",)),
    )(st, w, m, v, *deps)


def _reduce_only(name, st, deps=()):
    _, rows, cols = st.shape
    tr = _tile(rows, 512, 16)

    def body(st_ref, *rest):
        rest[-1][...] = _sum_parts(st_ref)

    return pl.pallas_call(
        body,
        name=name,
        grid=(rows // tr,),
        in_specs=[pl.BlockSpec((st.shape[0], tr, cols), lambda i: (0, i, 0))]
        + [pl.BlockSpec(memory_space=pl.ANY)] * len(deps),
        out_specs=pl.BlockSpec((tr, cols), lambda i: (i, 0)),
        out_shape=jax.ShapeDtypeStruct((rows, cols), F32),
        compiler_params=_params(("arbitrary",)),
    )(st, *deps)


def _adam_only(name, w, g, m, v):
    rows, cols = w.shape
    tr = _tile(rows, 512, 16)

    def body(w_ref, g_ref, m_ref, v_ref, d_out, m_out, v_out):
        d, mm, vv = _adam_vals(w_ref[...], g_ref[...], m_ref[...], v_ref[...])
        d_out[...] = d
        m_out[...] = mm
        v_out[...] = vv

    blk = pl.BlockSpec((tr, cols), lambda i: (i, 0))
    return pl.pallas_call(
        body,
        name=name,
        grid=(rows // tr,),
        in_specs=[blk] * 4,
        out_specs=[blk] * 3,
        out_shape=[jax.ShapeDtypeStruct(w.shape, F32)] * 3,
        compiler_params=_params(("arbitrary",)),
    )(w, g, m, v)


def _prenorm(name, h, g, t, d, tm, deps=()):
    def fn(v):
        x, gg = v
        return [x * _rms_r(x) * gg]

    return _rowwise(name, [("row", h, d, 0), ("full", g)], [("row", (t, d), BF, d, 0)], fn, t=t, tm=tm, deps=deps)[0]


def _ffn_up(tag, xn, wgt, wut, t, d, f, tm):
    def up_epi(accs, ex):
        gg, uu = accs
        return [gg, uu, gg * _sig(gg) * uu]

    mats = [dict(a=xn, b=wgt, mode="nt", acc=0, tk=d), dict(a=xn, b=wut, mode="nt", acc=1, tk=d)]
    return _matmul(
        tag + "_up", mats, m=t, n=f, tm=tm, tn=_tile(f, 1408),
        outs=[("mn", (t, f), BF, 0)] * 3, epilogue=up_epi, n_acc=2, j_outer=True)


def _ffn_down(tag, hid, wd, h, g_post, g_next, t, d, f, tm):
    def down_epi(accs, ex):
        ff = accs[0]
        hh, gg, gn = ex
        h_new = hh + 0.5 * ff * _rms_r(ff) * gg
        return [ff, h_new, h_new * _rms_r(h_new) * gn]

    return _matmul(
        tag + "_down", [dict(a=hid, b=wd, mode="nn", acc=0, tk=f)], m=t, n=d, tm=tm, tn=d,
        extras=[("mn", h, 0), ("n", g_post), ("n", g_next)],
        outs=[("mn", (t, d), F32, 0)] * 2 + [("mn", (t, d), BF, 0)], epilogue=down_epi, n_acc=1)


def _prenorm_bwd_epi(accs, ex):
    hh, dh, gg = ex
    dx, dg = _rms_bwd(hh, gg, accs[0])
    return [dh + dx, dg]


def _dx_extras(h_in, dh, g_pre, fo, g_post):
    return [("mn", h_in, 0), ("mn", dh, 0), ("n", g_pre), ("mn", fo, 0), ("n", g_post)]


def _dx_outs(t, d):
    return [("mn", (t, d), F32, 0), ("acc", (1, d), F32), ("mn", (t, d), BF, 0), ("acc", (1, d), F32)]


def _dx_epi(scale):
    def epi(accs, ex):
        hh, dh, gg, fo, gp = ex
        dx, dg = _rms_bwd(hh, gg, accs[0])
        dh_new = dh + dx
        dfo, dgp = _rms_bwd(fo, gp, dh_new * scale)
        return [dh_new, dg, dfo, dgp]

    return epi


def _ffn_dhid(tag, df, wd, gate, up, t, d, f, tm):
    def hid_epi(accs, ex):
        dhid = accs[0]
        gg, uu = ex[0].astype(F32), ex[1].astype(F32)
        s = _sig(gg)
        return [dhid * uu * s * (1.0 + gg * (1.0 - s)), dhid * gg * s]

    return _matmul(
        tag + "_dhid", [dict(a=df, b=wd, mode="nt", acc=0, tk=d)], m=t, n=f, tm=tm, tn=_tile(f, 1408),
        extras=[("mn", gate, 0), ("mn", up, 0)],
        outs=[("mn", (t, f), BF, 0)] * 2, epilogue=hid_epi, n_acc=1, j_outer=True)


def _ffn_dwd(tag, hid, df, t, d, f):
    tk_t = _tile(t, TOKEN_K_TILE)
    return _matmul(
        tag + "_dwd", [dict(a=hid, b=df, mode="tn", acc=0, tk=tk_t)], m=f, n=d, tm=_tile(f, 1408), tn=d,
        nk=t // tk_t, outs=[("mn", (f, d), BF, 0)], epilogue=lambda accs, ex: accs, n_acc=1)


def _ffn_dwgu(tag, dgate, dup, xn, t, d, f, deps=()):
    tk_t = _tile(t, TOKEN_K_TILE)
    return _matmul(
        tag + "_dwgu",
        [dict(a=dgate, b=xn, mode="tn", acc=0, tk=tk_t), dict(a=dup, b=xn, mode="tn", acc=1, tk=tk_t)],
        m=f, n=d, tm=_tile(f, 1408), tn=d, nk=t // tk_t,
        outs=[("mn", (f, d), BF, 0)] * 2, epilogue=lambda accs, ex: accs, n_acc=2, deps=deps)


def _ffn_dx(tag, dh, h_in, dgate, dup, wgt, wut, g_pre, t, d, f, tm, deps=(), before=None):
    tf = _tile(f, 1408)
    mats = [dict(a=dgate, b=wgt, mode="nn", acc=0, tk=tf), dict(a=dup, b=wut, mode="nn", acc=0, tk=tf)]
    if before is None:
        return _matmul(
            tag + "_dx", mats, m=t, n=d, tm=tm, tn=d, nk=f // tf,
            extras=[("mn", h_in, 0), ("mn", dh, 0), ("n", g_pre)],
            outs=[("mn", (t, d), F32, 0), ("acc", (1, d), F32)], epilogue=_prenorm_bwd_epi, n_acc=1, deps=deps)
    fo, g_post, scale = before
    return _matmul(
        tag + "_dx", mats, m=t, n=d, tm=tm, tn=d, nk=f // tf, extras=_dx_extras(h_in, dh, g_pre, fo, g_post),
        outs=_dx_outs(t, d), epilogue=_dx_epi(scale), n_acc=1, deps=deps)


def _causal_mask():
    r = lax.broadcasted_iota(jnp.int32, (CHUNK, CHUNK), 0)
    c = lax.broadcasted_iota(jnp.int32, (CHUNK, CHUNK), 1)
    return r >= c


def _layernorm_parts(v):
    mu = jnp.mean(v, axis=-1, keepdims=True)
    vc = v - mu
    rstd = lax.rsqrt(jnp.mean(vc * vc, axis=-1, keepdims=True) + EPS)
    return vc * rstd, rstd


def _sgu_fwd(z, g_sgu, w_s, b_col, t, d, tm):
    dg = d // N_SGU_GROUPS
    n_chunk = tm // CHUNK

    def body(zu_ref, zv_ref, g_ref, w_ref, b_ref, a_ref):
        u = _gelu(zu_ref[...])
        vhat, _ = _layernorm_parts(_gelu(zv_ref[...]))
        vn = (vhat * g_ref[...]).astype(BF)
        mask = _causal_mask()
        for gi in range(N_SGU_GROUPS):
            ws = jnp.where(mask, w_ref[gi], 0.0).astype(BF)
            bias = b_ref[gi]
            for ci in range(n_chunk):
                rows, cols = slice(ci * CHUNK, (ci + 1) * CHUNK), slice(gi * dg, (gi + 1) * dg)
                sv = jnp.dot(ws, vn[rows, cols], preferred_element_type=F32) + bias
                a_ref[rows, cols] = (u[rows, cols] * sv).astype(BF)

    return pl.pallas_call(
        body,
        name="sgu_fwd",
        grid=(t // tm,),
        in_specs=[
            pl.BlockSpec((tm, d), lambda i: (i, 0)),
            pl.BlockSpec((tm, d), lambda i: (i, 1)),
            pl.BlockSpec((1, d), lambda i: (0, 0)),
            pl.BlockSpec(w_s.shape, lambda i: (0, 0, 0)),
            pl.BlockSpec(b_col.shape, lambda i: (0, 0, 0)),
        ],
        out_specs=pl.BlockSpec((tm, d), lambda i: (i, 0)),
        out_shape=jax.ShapeDtypeStruct((t, d), BF),
        compiler_params=_params(("arbitrary",)),
    )(z, z, g_sgu, w_s, b_col)


def _sgu_bwd(z, da, dz, g_sgu, w_s, b_col, t, d, tm):
    dg = d // N_SGU_GROUPS
    n_chunk = tm // CHUNK

    def body(zu_ref, zv_ref, da_ref, dz_in, g_ref, w_ref, b_ref, dz_ref, dw_ref, db_ref, dgn_ref, dvn_ref):
        del dz_in
        dzu_ref, dzv_ref = dz_ref.at[:, pl.ds(0, d)], dz_ref.at[:, pl.ds(d, d)]
        i = pl.program_id(0)
        zu, zv = zu_ref[...], zv_ref[...]
        u = _gelu(zu)
        vhat, rstd = _layernorm_parts(_gelu(zv))
        gn = g_ref[...]
        vn = (vhat * gn).astype(BF)
        da_v = da_ref[...].astype(F32)
        dsv_all = (da_v * u).astype(BF)
        mask = _causal_mask()
        for gi in range(N_SGU_GROUPS):
            ws = jnp.where(mask, w_ref[gi], 0.0).astype(BF)
            bias = b_ref[gi]
            dw = jnp.zeros((CHUNK, CHUNK), F32)
            dbias = jnp.zeros((CHUNK, 1), F32)
            for ci in range(n_chunk):
                rows, cols = slice(ci * CHUNK, (ci + 1) * CHUNK), slice(gi * dg, (gi + 1) * dg)
                vn_c, dsv = vn[rows, cols], dsv_all[rows, cols]
                sv = jnp.dot(ws, vn_c, preferred_element_type=F32) + bias
                dzu_ref[rows, cols] = (da_v[rows, cols] * sv * _gelu_grad(zu[rows, cols])).astype(BF)
                dw = dw + lax.dot_general(dsv, vn_c, _DN["nt"], preferred_element_type=F32)
                dbias = dbias + jnp.sum(dsv.astype(F32), axis=1, keepdims=True)
                dvn_ref[rows, cols] = lax.dot_general(ws, dsv, _DN["tn"], preferred_element_type=F32)
            dw = jnp.where(mask, dw, 0.0)

            @pl.when(i == 0)
            def _():
                dw_ref[gi] = dw
                db_ref[gi] = dbias

            @pl.when(i != 0)
            def _():
                dw_ref[gi] += dw
                db_ref[gi] += dbias

        dvn = dvn_ref[...]
        dgn = jnp.sum(dvn * vhat, axis=0, keepdims=True)

        @pl.when(i == 0)
        def _():
            dgn_ref[...] = dgn

        @pl.when(i != 0)
        def _():
            dgn_ref[...] += dgn

        dvh = dvn * gn
        dv = rstd * (dvh - jnp.mean(dvh, axis=-1, keepdims=True) - vhat * jnp.mean(dvh * vhat, axis=-1, keepdims=True))
        dzv_ref[...] = (dv * _gelu_grad(zv)).astype(BF)

    return pl.pallas_call(
        body,
        name="sgu_bwd",
        grid=(t // tm,),
        in_specs=[
            pl.BlockSpec((tm, d), lambda i: (i, 0)),
            pl.BlockSpec((tm, d), lambda i: (i, 1)),
            pl.BlockSpec((tm, d), lambda i: (i, 0)),
            pl.BlockSpec(memory_space=pl.ANY),
            pl.BlockSpec((1, d), lambda i: (0, 0)),
            pl.BlockSpec(w_s.shape, lambda i: (0, 0, 0)),
            pl.BlockSpec(b_col.shape, lambda i: (0, 0, 0)),
        ],
        out_specs=[
            pl.BlockSpec((tm, 2 * d), lambda i: (i, 0)),
            pl.BlockSpec(w_s.shape, lambda i: (0, 0, 0)),
            pl.BlockSpec(b_col.shape, lambda i: (0, 0, 0)),
            pl.BlockSpec((1, d), lambda i: (0, 0)),
        ],
        out_shape=[
            jax.ShapeDtypeStruct(dz.shape, BF),
            jax.ShapeDtypeStruct(w_s.shape, F32),
            jax.ShapeDtypeStruct(b_col.shape, F32),
            jax.ShapeDtypeStruct((1, d), F32),
        ],
        scratch_shapes=[pltpu.VMEM((tm, d), F32)],
        input_output_aliases={3: 0},
        compiler_params=_params(("arbitrary",)),
    )(z, z, da, dz, g_sgu, w_s, b_col)


def _shift_down(x, k, row):
    return jnp.where(row >= k, pltpu.roll(x, k, 0), 0.0)


def _shift_up(x, k, row, t):
    return jnp.where(row < t - k, pltpu.roll(x, t - k, 0), 0.0)


def _doublings(window):
    steps = int(math.log2(window))
    assert 2 ** steps == window
    return [2 ** s for s in range(steps)]


def _pool_diff(c, window, row):
    s = c
    for k in _doublings(window):
        s = s + _shift_down(s, k, row)
    count = jnp.minimum(row + 1, window).astype(F32)
    return s / count - c, count


def _pool_fwd(z, pool_w, pool_scale, t, d):
    dgp = d // len(POOL_WINDOWS)
    cblk = (2 * d) // dgp

    def body(zc_ref, w_ref, s_ref, b_ref):
        row = lax.broadcasted_iota(jnp.int32, (t, 1), 0)
        for gi, window in enumerate(POOL_WINDOWS):
            @pl.when(pl.program_id(0) == gi)
            def _(window=window):
                diff, _ = _pool_diff(zc_ref[...], window, row)
                out = jnp.dot(diff.astype(BF), w_ref[...], preferred_element_type=F32)
                b_ref[...] = (out * s_ref[...]).astype(BF)

    return pl.pallas_call(
        body,
        name="pool_fwd",
        grid=(len(POOL_WINDOWS),),
        in_specs=[
            pl.BlockSpec((t, dgp), lambda g: (0, cblk + g)),
            pl.BlockSpec((None, dgp, dgp), lambda g: (g, 0, 0)),
            pl.BlockSpec((1, dgp), lambda g: (0, g)),
        ],
        out_specs=pl.BlockSpec((t, dgp), lambda g: (0, g)),
        out_shape=jax.ShapeDtypeStruct((t, d), BF),
        compiler_params=_params(("arbitrary",)),
    )(z, pool_w, pool_scale)


def _pool_bwd(z, db, dz, pool_w, pool_scale, t, d):
    dgp = d // len(POOL_WINDOWS)
    cblk = (2 * d) // dgp

    def body(zc_ref, db_ref, dz_in, w_ref, s_ref, dzc_ref, dw_ref, ds_ref):
        del dz_in
        row = lax.broadcasted_iota(jnp.int32, (t, 1), 0)
        for gi, window in enumerate(POOL_WINDOWS):
            @pl.when(pl.program_id(0) == gi)
            def _(window=window):
                diff, count = _pool_diff(zc_ref[...], window, row)
                diff = diff.astype(BF)
                w = w_ref[...]
                dbv = db_ref[...].astype(F32)
                out = jnp.dot(diff, w, preferred_element_type=F32)
                ds_ref[...] = jnp.sum(dbv * out, axis=0, keepdims=True)
                dout = (dbv * s_ref[...]).astype(BF)
                dw_ref[...] = lax.dot_general(diff, dout, _DN["tn"], preferred_element_type=F32).astype(BF)
                ddiff = lax.dot_general(dout, w, _DN["nt"], preferred_element_type=F32)
                s = ddiff / count
                for k in _doublings(window):
                    s = s + _shift_up(s, k, row, t)
                dzc_ref[...] = (s - ddiff).astype(BF)

    return pl.pallas_call(
        body,
        name="pool_bwd",
        grid=(len(POOL_WINDOWS),),
        in_specs=[
            pl.BlockSpec((t, dgp), lambda g: (0, cblk + g)),
            pl.BlockSpec((t, dgp), lambda g: (0, g)),
            pl.BlockSpec(memory_space=pl.ANY),
            pl.BlockSpec((None, dgp, dgp), lambda g: (g, 0, 0)),
            pl.BlockSpec((1, dgp), lambda g: (0, g)),
        ],
        out_specs=[
            pl.BlockSpec((t, dgp), lambda g: (0, cblk + g)),
            pl.BlockSpec((None, dgp, dgp), lambda g: (g, 0, 0)),
            pl.BlockSpec((1, dgp), lambda g: (0, g)),
        ],
        out_shape=[
            jax.ShapeDtypeStruct(dz.shape, BF),
            jax.ShapeDtypeStruct(pool_w.shape, BF),
            jax.ShapeDtypeStruct((1, d), F32),
        ],
        input_output_aliases={2: 0},
        compiler_params=_params(("arbitrary",)),
    )(z, db, dz, pool_w, pool_scale)


def _mix_dy(dm, w_o, z, ya, yb, t, d, tm):
    def body(dm_ref, w_ref, gate_ref, ya_ref, yb_ref, dyab_ref, dz_ref, dy_ref):
        j = pl.program_id(1)

        @pl.when(j == 0)
        def _():
            dy_ref[...] = lax.dot_general(dm_ref[...], w_ref[...], _DN["nt"], preferred_element_type=F32)

        dy = dy_ref[...]
        s = _sig(gate_ref[...])
        yv = jnp.where(j == 0, ya_ref[...], yb_ref[...]).astype(F32)
        dyab_ref[...] = (dy * s).astype(BF)
        dz_ref[...] = (dy * yv * s * (1.0 - s)).astype(BF)

    row = pl.BlockSpec((tm, d), lambda i, j: (i, 0))
    return pl.pallas_call(
        body,
        name="mix_dy",
        grid=(t // tm, 2),
        in_specs=[row, pl.BlockSpec((d, d), lambda i, j: (0, 0)), pl.BlockSpec((tm, d), lambda i, j: (i, 3 + j)), row, row],
        out_specs=[pl.BlockSpec((tm, d), lambda i, j: (i, j)), pl.BlockSpec((tm, d), lambda i, j: (i, 3 + j))],
        out_shape=[jax.ShapeDtypeStruct((t, 2 * d), BF), jax.ShapeDtypeStruct((t, 5 * d), BF)],
        scratch_shapes=[pltpu.VMEM((tm, d), F32)],
        compiler_params=_params(("arbitrary", "arbitrary")),
    )(dm, w_o, z, ya, yb)


def _everyone():
    x, y, c = _place()
    return [(1 - x if dd & 4 else x, 1 - y if dd & 2 else y, 1 - c if dd & 1 else c) for dd in range(1, N_DEV)]


def _direct_plan(n):
    def plan(refs, send_sems, recv_sems):
        x, y, c = _place()
        me = 4 * x + 2 * y + c
        cps = []
        for i in range(n):
            for k, (px, py, pc) in enumerate(_everyone()):
                cps.append(_remote(refs[i].at[4 * px + 2 * py + pc], refs[n + i].at[me], send_sems, recv_sems,
                                   7 * i + k, (px, py, pc)))
        return cps

    def local(refs):
        x, y, c = _place()
        me = 4 * x + 2 * y + c
        return [(refs[i].at[me], refs[n + i].at[me]) for i in range(n)]

    return plan, local


def _sc_scatter_direct(name, collective_id, parts):
    n = len(parts)
    p8 = [a.reshape(N_DEV, a.shape[0] // N_DEV, a.shape[1]) for a in parts]
    plan, local = _direct_plan(n)
    return _sc_exchange(name, collective_id, p8, [jax.ShapeDtypeStruct(a.shape, a.dtype) for a in p8], plan,
                        7 * n, _everyone, local=local)


def _pack_rows(flat_list, width):
    flat = jnp.concatenate([a.reshape(-1) for a in flat_list])
    rows = -(-flat.shape[0] // (8 * width)) * 8
    return jnp.pad(flat, (0, rows * width - flat.shape[0])).reshape(rows, width)


def _unpack_rows(packed, like):
    flat, out, pos = packed.reshape(-1), [], 0
    for a in like:
        out.append(flat[pos:pos + a.size].reshape(a.shape))
        pos += a.size
    return out


def kernel(x, p, ffn1_pre_g, ffn1_w_gate, ffn1_w_up, ffn1_w_down, ffn1_post_g, mix_pre_g, w_in, sgu_norm_g, sgu_w, sgu_b, pool_w, pool_scale, w_out_a, w_out_b, w_o, mix_post_g, ffn2_pre_g, ffn2_w_gate, ffn2_w_up, ffn2_w_down, ffn2_post_g, ple_pre_g, ple_w_gate, ple_w_proj, ple_post_g, loss_target, m_ffn1_pre_g, m_ffn1_w_gate, m_ffn1_w_up, m_ffn1_w_down, m_ffn1_post_g, m_mix_pre_g, m_w_in, m_sgu_norm_g, m_sgu_w, m_sgu_b, m_pool_w, m_pool_scale, m_w_out_a, m_w_out_b, m_w_o, m_mix_post_g, m_ffn2_pre_g, m_ffn2_w_gate, m_ffn2_w_up, m_ffn2_w_down, m_ffn2_post_g, m_ple_pre_g, m_ple_w_gate, m_ple_w_proj, m_ple_post_g, v_ffn1_pre_g, v_ffn1_w_gate, v_ffn1_w_up, v_ffn1_w_down, v_ffn1_post_g, v_mix_pre_g, v_w_in, v_sgu_norm_g, v_sgu_w, v_sgu_b, v_pool_w, v_pool_scale, v_w_out_a, v_w_out_b, v_w_o, v_mix_post_g, v_ffn2_pre_g, v_ffn2_w_gate, v_ffn2_w_up, v_ffn2_w_down, v_ffn2_post_g, v_ple_pre_g, v_ple_w_gate, v_ple_w_proj, v_ple_post_g):
    args = dict(locals())
    names = ["ffn1_pre_g", "ffn1_w_gate", "ffn1_w_up", "ffn1_w_down", "ffn1_post_g", "mix_pre_g", "w_in",
             "sgu_norm_g", "sgu_w", "sgu_b", "pool_w", "pool_scale", "w_out_a", "w_out_b", "w_o", "mix_post_g",
             "ffn2_pre_g", "ffn2_w_gate", "ffn2_w_up", "ffn2_w_down", "ffn2_post_g", "ple_pre_g", "ple_w_gate",
             "ple_w_proj", "ple_post_g"]
    w = {k: args[k][0] for k in names}
    mom = {k: args["m_" + k][0] for k in names}
    var = {k: args["v_" + k][0] for k in names}

    assert x.shape[0] == 1 and p.shape[:2] == (1, 1)
    t, d = x.shape[1], x.shape[2]
    f = ffn1_w_gate.shape[2] * N_DEV
    d_in = w_in.shape[2] * N_DEV
    d_ple = p.shape[3]
    n_pool = len(POOL_WINDOWS)
    dgp = d // n_pool
    assert d_in == 5 * d and t % CHUNK == 0
    tm = _tile(t, 512, CHUNK)
    xs, ps, target = x[0], p[0, 0], loss_target[0]

    col_sharded = ("ffn1_w_gate", "ffn1_w_up", "ffn2_w_gate", "ffn2_w_up", "w_in", "ple_w_proj")

    def shard_of(k):
        if k in col_sharded:
            return w[k].T.astype(BF)
        if k == "pool_w":
            return w[k].reshape(-1, dgp).astype(BF)
        return w[k].astype(BF)

    rows_pw = dgp // N_DEV
    b_col = w["sgu_b"][:, :, None]
    gains = {k: w[k][None, :] for k in names if w[k].ndim == 1}
    full = {}

    groups = [["ffn1_w_gate", "ffn1_w_up"], ["ffn1_w_down", "w_in"], ["pool_w", "w_out_a", "w_out_b", "w_o"],
              ["ffn2_w_gate", "ffn2_w_up", "ffn2_w_down", "ple_w_gate", "ple_w_proj"]]
    def gather(gi, behind=None):
        shards = [shard_of(k) for k in groups[gi]]
        if behind is not None:
            shards = lax.optimization_barrier((shards, behind))[0]
        full.update(zip(groups[gi], _sc_gather("gather%d" % gi, gi, shards)))

    gather(0)
    xn1 = _prenorm("ffn1_prenorm", xs, gains["ffn1_pre_g"], t, d, tm)
    gather(1, xn1)
    g1, u1, hid1 = _ffn_up("ffn1", xn1, full["ffn1_w_gate"], full["ffn1_w_up"], t, d, f, tm)
    gather(2, hid1)
    f1, h1, xn2 = _ffn_down("ffn1", hid1, full["ffn1_w_down"], xs, gains["ffn1_post_g"], gains["mix_pre_g"], t, d, f, tm)
    gather(3, h1)

    pool_full = full["pool_w"].reshape(N_DEV, n_pool, rows_pw, dgp).transpose(1, 0, 2, 3).reshape(n_pool, dgp, dgp)
    z = _matmul("mix_in", [dict(a=xn2, b=full["w_in"], mode="nt", acc=0, tk=d)], m=t, n=d_in, tm=tm, tn=d,
                outs=[("mn", (t, d_in), F32, 0)], epilogue=lambda accs, ex: accs, n_acc=1, j_outer=True)[0]
    a_br = _sgu_fwd(z, gains["sgu_norm_g"], w["sgu_w"], b_col, t, d, tm)
    b_br = _pool_fwd(z, pool_full, gains["pool_scale"], t, d)

    def merge_epi(accs, ex):
        ya, yb = accs
        ga, gb = ex
        return [ya, yb, _sig(ga) * ya + _sig(gb) * yb]

    ya, yb, y = _matmul(
        "mix_merge",
        [dict(a=a_br, b=full["w_out_a"], mode="nn", acc=0, tk=d), dict(a=b_br, b=full["w_out_b"], mode="nn", acc=1, tk=d)],
        m=t, n=d, tm=tm, tn=d, extras=[("mn", z, 3), ("mn", z, 4)],
        outs=[("mn", (t, d), BF, 0)] * 3, epilogue=merge_epi, n_acc=2)

    def proj_epi(accs, ex):
        mm = accs[0]
        hh, gg, gn = ex
        h_new = hh + mm * _rms_r(mm) * gg
        return [mm, h_new, h_new * _rms_r(h_new) * gn]

    m_out, h2, xn3 = _matmul(
        "mix_proj", [dict(a=y, b=full["w_o"], mode="nn", acc=0, tk=d)], m=t, n=d, tm=tm, tn=d,
        extras=[("mn", h1, 0), ("n", gains["mix_post_g"]), ("n", gains["ffn2_pre_g"])],
        outs=[("mn", (t, d), F32, 0)] * 2 + [("mn", (t, d), BF, 0)], epilogue=proj_epi, n_acc=1)

    g2, u2, hid2 = _ffn_up("ffn2", xn3, full["ffn2_w_gate"], full["ffn2_w_up"], t, d, f, tm)
    f2, h3, xn4 = _ffn_down("ffn2", hid2, full["ffn2_w_down"], h2, gains["ffn2_post_g"], gains["ple_pre_g"], t, d, f, tm)

    def ple_epi(accs, ex):
        gl, e = accs
        hh, tgt, gg = ex
        q = _sig(gl) * e
        err = hh + q * _rms_r(q) * gg - tgt
        return [gl, e, err * (1.0 / d), jnp.sum(err * err, axis=0, keepdims=True)]

    gl, e_ple, dh4, loss_vec = _matmul(
        "ple_fwd",
        [dict(a=xn4, b=full["ple_w_gate"], mode="nn", acc=0, tk=d), dict(a=ps, b=full["ple_w_proj"], mode="nt", acc=1, tk=d_ple)],
        m=t, n=d, tm=tm, tn=d, extras=[("mn", h3, 0), ("mn", target, 0), ("n", gains["ple_post_g"])],
        outs=[("mn", (t, d), F32, 0)] * 3 + [("acc", (1, d), F32)], epilogue=ple_epi, n_acc=2)
    loss = lax.psum(jnp.sum(loss_vec) * (0.5 / d), AXES)

    tk_t = _tile(t, TOKEN_K_TILE)

    def ple_post_fn(v):
        dy, gl_v, e_v, gg = v
        s = _sig(gl_v)
        dq, dg = _rms_bwd(s * e_v, gg, dy)
        return [dq * e_v * s * (1.0 - s), dq * s, dg]

    dgl, de, dg_ple_post = _rowwise(
        "ple_post_bwd", [("row", dh4, d, 0), ("row", gl, d, 0), ("row", e_ple, d, 0), ("full", gains["ple_post_g"])],
        [("row", (t, d), BF, d, 0), ("row", (t, d), BF, d, 0), ("acc", (1, d), F32)], ple_post_fn, t=t, tm=tm)
    ident = lambda accs, ex: accs
    dw_ple_gate = _matmul("ple_dwg", [dict(a=xn4, b=dgl, mode="tn", acc=0, tk=tk_t)], m=d, n=d, tm=d, tn=d,
                          nk=t // tk_t, outs=[("mn", (d, d), BF, 0)], epilogue=ident, n_acc=1)[0]
    dw_ple_proj_t = _matmul("ple_dwp", [dict(a=de, b=ps, mode="tn", acc=0, tk=tk_t)], m=d, n=d_ple, tm=d, tn=d_ple,
                            nk=t // tk_t, outs=[("mn", (d, d_ple), BF, 0)], epilogue=ident, n_acc=1)[0]
    dh3, dg_ple_pre, df2, dg_f2_post = _matmul(
        "ple_dx", [dict(a=dgl, b=full["ple_w_gate"], mode="nt", acc=0, tk=d)], m=t, n=d, tm=tm, tn=d,
        extras=_dx_extras(h3, dh4, gains["ple_pre_g"], f2, gains["ffn2_post_g"]),
        outs=_dx_outs(t, d), epilogue=_dx_epi(0.5), n_acc=1)

    staged = []
    n_gather = len(groups)

    def scatter(keys, parts):
        k = len(staged)
        staged.append((keys, _sc_scatter_direct("scatter%d" % k, n_gather + k, parts)))

    dgate2, dup2 = _ffn_dhid("ffn2", df2, full["ffn2_w_down"], g2, u2, t, d, f, tm)
    dwd2 = _ffn_dwd("ffn2", hid2, df2, t, d, f)[0]
    dwg2, dwu2 = _ffn_dwgu("ffn2", dgate2, dup2, xn3, t, d, f)
    scatter(["ple_w_gate", "ple_w_proj", "ffn2_w_down", "ffn2_w_gate", "ffn2_w_up"],
            [dw_ple_gate, dw_ple_proj_t, dwd2, dwg2, dwu2])
    dh2, dg_f2_pre, dm, dg_mix_post = _ffn_dx(
        "ffn2", dh3, h2, dgate2, dup2, full["ffn2_w_gate"], full["ffn2_w_up"], gains["ffn2_pre_g"], t, d, f, tm,
        deps=[dw_ple_gate, dw_ple_proj_t, dwd2, dwg2], before=(m_out, gains["mix_post_g"], 1.0))

    dyab, dz = _mix_dy(dm, full["w_o"], z, ya, yb, t, d, tm)
    da, db = _matmul(
        "mix_dab",
        [dict(a=dyab, b=full["w_out_a"], mode="nt", acc=0, tk=d),
         dict(a=dyab, b=full["w_out_b"], mode="nt", acc=1, tk=d, a_off=(0, 1))],
        m=t, n=d, tm=tm, tn=d, outs=[("mn", (t, d), BF, 0)] * 2, epilogue=ident, n_acc=2)
    dz, dsgu_w, dsgu_b, dg_sgu = _sgu_bwd(z, da, dz, gains["sgu_norm_g"], w["sgu_w"], b_col, t, d, _tile(t, 256, CHUNK))
    dz, dpool_w, dg_pool = _pool_bwd(z, db, dz, pool_full, gains["pool_scale"], t, d)
    dw_in_t = _matmul(
        "mix_dwin", [dict(a=dz, b=xn2, mode="tn", acc=0, tk=tk_t)], m=d_in, n=d, tm=d, tn=d,
        nk=t // tk_t, outs=[("mn", (d_in, d), BF, 0)], epilogue=ident, n_acc=1)[0]
    dpool_rows = dpool_w.reshape(n_pool, N_DEV, rows_pw, dgp).transpose(1, 0, 2, 3).reshape(N_DEV * n_pool * rows_pw, dgp)
    scatter(["pool_w", "w_in"], [dpool_rows, dw_in_t])
    dh1, dg_mix_pre, df1, dg_f1_post = _matmul(
        "mix_dx", [dict(a=dz, b=full["w_in"], mode="nn", acc=0, tk=d)], m=t, n=d, tm=tm, tn=d, nk=d_in // d,
        extras=_dx_extras(h1, dh2, gains["mix_pre_g"], f1, gains["ffn1_post_g"]),
        outs=_dx_outs(t, d), epilogue=_dx_epi(0.5), n_acc=1, deps=[dpool_rows, dw_in_t])

    dgate1, dup1 = _ffn_dhid("ffn1", df1, full["ffn1_w_down"], g1, u1, t, d, f, tm)
    dwd1 = _ffn_dwd("ffn1", hid1, df1, t, d, f)[0]
    scatter(["ffn1_w_down"], [dwd1])
    dwg1, dwu1 = _ffn_dwgu("ffn1", dgate1, dup1, xn1, t, d, f, deps=[dwd1])
    scatter(["ffn1_w_gate", "ffn1_w_up"], [dwg1, dwu1])
    grad_x, dg_f1_pre = _ffn_dx("ffn1", dh1, xs, dgate1, dup1, full["ffn1_w_gate"], full["ffn1_w_up"],
                                gains["ffn1_pre_g"], t, d, f, tm, deps=[dwg1])
    dw_o = _matmul("mix_dwo", [dict(a=y, b=dm, mode="tn", acc=0, tk=tk_t)], m=d, n=d, tm=d, tn=d, nk=t // tk_t,
                   outs=[("mn", (d, d), BF, 0)], epilogue=ident, n_acc=1, deps=[grad_x])[0]
    dw_out_a, dw_out_b = _matmul(
        "mix_dwab",
        [dict(a=a_br, b=dyab, mode="tn", acc=0, tk=tk_t),
         dict(a=b_br, b=dyab, mode="tn", acc=1, tk=tk_t, b_off=(0, 1))],
        m=d, n=d, tm=d, tn=d, nk=t // tk_t, outs=[("mn", (d, d), BF, 0)] * 2, epilogue=ident, n_acc=2, deps=[dw_o])
    scatter(["w_o", "w_out_a", "w_out_b"], [dw_o, dw_out_a, dw_out_b])

    grad, delta, new_m, new_v = {}, {}, {}, {}

    def update(group, partial_sums, behind):
        done = []
        for k, st in zip(group, partial_sums):
            if k in col_sharded:
                grad[k] = _reduce_only("sum_" + k, st, deps=behind).T
                delta[k], new_m[k], new_v[k] = _adam_only("adam_" + k, w[k], grad[k], mom[k], var[k])
            else:
                shape = w[k].shape
                flat = (-1, shape[-1])
                res = _reduce_adam("adam_" + k, st, w[k].reshape(flat), mom[k].reshape(flat), var[k].reshape(flat),
                                   deps=behind)
                grad[k], delta[k], new_m[k], new_v[k] = [r.reshape(shape) for r in res]
            done.append(delta[k])
        return done

    small_grads = {
        "ffn1_pre_g": dg_f1_pre, "ffn1_post_g": dg_f1_post, "mix_pre_g": dg_mix_pre, "sgu_norm_g": dg_sgu,
        "sgu_w": dsgu_w, "sgu_b": dsgu_b, "pool_scale": dg_pool, "mix_post_g": dg_mix_post, "ffn2_pre_g": dg_f2_pre,
        "ffn2_post_g": dg_f2_post, "ple_pre_g": dg_ple_pre, "ple_post_g": dg_ple_post,
    }
    small_names = [k for k in names if k in small_grads]
    small_packed = _pack_rows([small_grads[k] for k in small_names], d)
    plan_s, local_s = _broadcast_plan()
    staged_small = _sc_exchange(
        "small", n_gather + len(staged), [small_packed],
        [jax.ShapeDtypeStruct((N_DEV,) + small_packed.shape, F32)], plan_s, N_DEV - 1, _everyone, local=local_s)[0]
    done = [dw_out_a]
    for keys, partial_sums in staged:
        done = update(keys, partial_sums, done)
    small_like = [w[k] for k in small_names]
    res = _reduce_adam(
        "adam_small", staged_small, _pack_rows(small_like, d), _pack_rows([mom[k] for k in small_names], d),
        _pack_rows([var[k] for k in small_names], d), deps=done)
    for dst, packed in zip((grad, delta, new_m, new_v), res):
        dst.update(zip(small_names, _unpack_rows(packed, small_like)))

    out = [loss, grad_x[None]]
    for group in (grad, delta, new_m, new_v):
        out += [group[k][None] for k in names]
    return tuple(out)
```

```python
import math

import jax
import jax.numpy as jnp
from jax import lax
from jax.experimental import pallas as pl
from jax.experimental.pallas import tpu as pltpu
from jax.experimental.pallas import tpu_sc as plsc

EPS = 1e-6
CHUNK = 128
N_SGU_GROUPS = 4
POOL_WINDOWS = (2, 4, 8, 16)
ADAM_LR = 0.001
ADAM_B1 = 0.9
ADAM_B2 = 0.999
ADAM_EPS = 1e-08
ADAM_WD = 0.01
ADAM_STEP = 10

N_DEV = 8
MESH = pl.DeviceIdType.MESH
V7X_VMEM_BYTES = 64 * 1024 * 1024
VMEM_LIMIT = V7X_VMEM_BYTES - 8 * 1024 * 1024
TOKEN_K_TILE = 1024
LANES = 128
BF = jnp.bfloat16
F32 = jnp.float32

_DN = {
    "nn": (((1,), (0,)), ((), ())),
    "nt": (((1,), (1,)), ((), ())),
    "tn": (((0,), (0,)), ((), ())),
}


def _params(sem=None):
    return pltpu.CompilerParams(dimension_semantics=sem, vmem_limit_bytes=VMEM_LIMIT)


def _tile(n, target, align=128):
    t = min(n, target)
    t -= t % align
    while t >= align:
        if n % t == 0:
            return t
        t -= align
    return n


def _sig(x):
    return 0.5 + 0.5 * jnp.tanh(0.5 * x)


_GELU_K = math.sqrt(2.0 / math.pi)
_GELU_C = 0.044715


def _gelu(x):
    return 0.5 * x * (1.0 + jnp.tanh(_GELU_K * (x + _GELU_C * x * x * x)))


def _gelu_grad(x):
    t = jnp.tanh(_GELU_K * (x + _GELU_C * x * x * x))
    return 0.5 * (1.0 + t) + 0.5 * x * (1.0 - t * t) * _GELU_K * (1.0 + 3.0 * _GELU_C * x * x)


def _rms_r(x):
    return lax.rsqrt(jnp.mean(x * x, axis=-1, keepdims=True) + EPS)


def _rms_bwd(x, g, dy):
    r = _rms_r(x)
    xh = x * r
    gy = dy * g
    dx = r * (gy - xh * jnp.mean(xh * gy, axis=-1, keepdims=True))
    return dx, jnp.sum(dy * xh, axis=0, keepdims=True)


def _matmul(name, mats, *, m, n, tm, tn, nk=1, extras=(), outs, epilogue, n_acc, j_outer=False, deps=()):
    ni, nj = m // tm, n // tn
    assert ni * tm == m and nj * tn == n, (name, m, n, tm, tn)
    if j_outer:
        grid = (nj, ni, nk)

        def ij(g0, g1):
            return g1, g0
    else:
        grid = (ni, nj, nk)

        def ij(g0, g1):
            return g0, g1

    in_specs, args = [], []
    for mt in mats:
        mode, tk = mt["mode"], mt["tk"]
        ao, bo = mt.get("a_off", (0, 0)), mt.get("b_off", (0, 0))
        if mode == "tn":
            sa = pl.BlockSpec((tk, tm), lambda g0, g1, kk, ao=ao: (ao[0] + kk, ao[1] + ij(g0, g1)[0]))
        else:
            sa = pl.BlockSpec((tm, tk), lambda g0, g1, kk, ao=ao: (ao[0] + ij(g0, g1)[0], ao[1] + kk))
        if mode == "nt":
            sb = pl.BlockSpec((tn, tk), lambda g0, g1, kk, bo=bo: (bo[0] + ij(g0, g1)[1], bo[1] + kk))
        else:
            sb = pl.BlockSpec((tk, tn), lambda g0, g1, kk, bo=bo: (bo[0] + kk, bo[1] + ij(g0, g1)[1]))
        in_specs += [sa, sb]
        args += [mt["a"], mt["b"]]
    n_mat_refs = len(args)
    for ex in extras:
        if ex[0] == "mn":
            in_specs.append(pl.BlockSpec((tm, tn), lambda g0, g1, kk, c=ex[2]: (ij(g0, g1)[0], c + ij(g0, g1)[1])))
        else:
            in_specs.append(pl.BlockSpec((1, tn), lambda g0, g1, kk: (0, ij(g0, g1)[1])))
        args.append(ex[1])
    n_ex_end = len(args)
    in_specs += [pl.BlockSpec(memory_space=pl.ANY)] * len(deps)
    args += list(deps)
    n_in = len(args)
    out_specs, out_shape = [], []
    for o in outs:
        if o[0] == "mn":
            out_specs.append(pl.BlockSpec((tm, tn), lambda g0, g1, kk, c=o[3]: (ij(g0, g1)[0], c + ij(g0, g1)[1])))
        else:
            assert nj == 1, name
            out_specs.append(pl.BlockSpec((1, tn), lambda g0, g1, kk: (0, 0)))
        out_shape.append(jax.ShapeDtypeStruct(o[1], o[2]))
    n_out = len(outs)

    def body(*refs):
        mat_refs = refs[:n_mat_refs]
        ex_refs = refs[n_mat_refs:n_ex_end]
        out_refs = refs[n_in:n_in + n_out]
        acc_refs = refs[n_in + n_out:]
        i = ij(pl.program_id(0), pl.program_id(1))[0]
        kk = pl.program_id(2)

        def products():
            res = [None] * n_acc
            for idx, mt in enumerate(mats):
                a = mat_refs[2 * idx][...].astype(BF)
                b = mat_refs[2 * idx + 1][...].astype(BF)
                p = lax.dot_general(a, b, _DN[mt["mode"]], preferred_element_type=F32)
                q = mt["acc"]
                res[q] = p if res[q] is None else res[q] + p
            return res

        def finish(accs):
            vals = epilogue(accs, [r[...] for r in ex_refs])
            for o, r, v in zip(outs, out_refs, vals):
                if o[0] == "mn":
                    r[...] = v.astype(o[2])
                else:
                    @pl.when(i == 0)
                    def _():
                        r[...] = v

                    @pl.when(i != 0)
                    def _():
                        r[...] += v

        if nk == 1:
            finish(products())
        else:
            res = products()

            @pl.when(kk == 0)
            def _():
                for q in range(n_acc):
                    acc_refs[q][...] = res[q]

            @pl.when(kk != 0)
            def _():
                for q in range(n_acc):
                    acc_refs[q][...] += res[q]

            @pl.when(kk == nk - 1)
            def _():
                finish([r[...] for r in acc_refs])

    scratch = [pltpu.VMEM((tm, tn), F32) for _ in range(n_acc)] if nk > 1 else []
    return pl.pallas_call(
        body,
        name=name,
        grid=grid,
        in_specs=in_specs,
        out_specs=out_specs,
        out_shape=out_shape,
        scratch_shapes=scratch,
        compiler_params=_params(("arbitrary", "arbitrary", "arbitrary")),
    )(*args)


def _rowwise(name, ins, outs, fn, *, t, tm, deps=()):
    ni = t // tm
    assert ni * tm == t, (name, t, tm)
    in_specs, args = [], []
    for s in ins:
        if s[0] == "row":
            in_specs.append(pl.BlockSpec((tm, s[2]), lambda i, c=s[3]: (i, c)))
        else:
            nd = s[1].ndim
            in_specs.append(pl.BlockSpec(s[1].shape, lambda i, nd=nd: (0,) * nd))
        args.append(s[1])
    out_specs, out_shape = [], []
    for o in outs:
        if o[0] == "row":
            out_specs.append(pl.BlockSpec((tm, o[3]), lambda i, c=o[4]: (i, c)))
        else:
            nd = len(o[1])
            out_specs.append(pl.BlockSpec(o[1], lambda i, nd=nd: (0,) * nd))
        out_shape.append(jax.ShapeDtypeStruct(o[1], o[2]))
    n_read = len(args)
    in_specs += [pl.BlockSpec(memory_space=pl.ANY)] * len(deps)
    args += list(deps)
    n_in = len(args)

    def body(*refs):
        i = pl.program_id(0)
        vals = fn([r[...] for r in refs[:n_read]])
        for o, r, v in zip(outs, refs[n_in:], vals):
            if o[0] == "row":
                r[...] = v.astype(o[2])
            else:
                @pl.when(i == 0)
                def _():
                    r[...] = v

                @pl.when(i != 0)
                def _():
                    r[...] += v

    return pl.pallas_call(
        body,
        name=name,
        grid=(ni,),
        in_specs=in_specs,
        out_specs=out_specs,
        out_shape=out_shape,
        compiler_params=_params(("arbitrary",)),
    )(*args)


def _place():
    return lax.axis_index("x"), lax.axis_index("y"), lax.axis_index("c")


def _remote(src, dst, send_sems, recv_sems, k, to):
    return pltpu.make_async_remote_copy(
        src_ref=src, dst_ref=dst, send_sem=send_sems.at[k], recv_sem=recv_sems.at[k],
        device_id=to, device_id_type=MESH)


def _chips(x, y):
    return [(1 - x, y), (x, 1 - y), (1 - x, 1 - y)]


def _broadcast_plan(n):
    def plan(refs, send_sems, recv_sems):
        x, y, c = _place()
        cps = []
        for i in range(n):
            for dd in range(1, N_DEV):
                peer = (1 - x if dd & 4 else x, 1 - y if dd & 2 else y, 1 - c if dd & 1 else c)
                cps.append(_remote(refs[i], refs[n + i].at[4 * x + 2 * y + c], send_sems, recv_sems, 7 * i + dd - 1, peer))
        return cps

    def local(refs):
        x, y, c = _place()
        return [(refs[i], refs[n + i].at[4 * x + 2 * y + c]) for i in range(n)]

    return plan, local


def _handshake(peers):
    barrier = pltpu.get_barrier_semaphore()
    for peer in peers:
        pl.semaphore_signal(barrier, inc=1, device_id=peer, device_id_type=MESH)
    pl.semaphore_wait(barrier, len(peers))


def _sequencer_call(name, collective_id, body, args, out_type, scratch_types):
    return pl.kernel(
        body,
        out_type=out_type,
        mesh=plsc.ScalarSubcoreMesh(axis_name="sequencer", num_cores=1),
        scratch_types=scratch_types,
        compiler_params=pltpu.CompilerParams(collective_id=collective_id),
        name=name,
    )(*args)


def _sc_gather(name, collective_id, shards):
    n = len(shards)

    def body(*refs):
        ins, outs = refs[:n], refs[n:2 * n]
        send_sems, recv_sems, local_sems = refs[2 * n:]
        x, y, c = _place()
        me, sibling = (x, y, c), (x, y, 1 - c)
        chips = _chips(x, y)
        _handshake([sibling] + [(*chip, c) for chip in chips])

        def slot(i, px, py, pc):
            return outs[i].at[4 * px + 2 * py + pc]

        def copy(i, k, block, to, src=None):
            return _remote(slot(i, *block) if src is None else src, slot(i, *block), send_sems, recv_sems, 7 * i + k, to)

        mine = [pltpu.make_async_copy(ins[i], slot(i, *me), local_sems.at[i]) for i in range(n)]
        for cp in mine:
            cp.start()
        first = []
        for i in range(n):
            first += [copy(i, 1 + j, me, (*chip, c), src=ins[i]) for j, chip in enumerate(chips)]
            first.append(copy(i, 0, me, sibling, src=ins[i]))
        for cp in first:
            cp.start()
        passed = []
        for i in range(n):
            for j, chip in enumerate(chips):
                copy(i, 1 + j, (*chip, c), me).wait_recv()
                fwd = copy(i, 4 + j, (*chip, c), sibling)
                fwd.start()
                passed.append(fwd)
        for i in range(n):
            copy(i, 0, sibling, me).wait_recv()
            for j, chip in enumerate(chips):
                copy(i, 4 + j, (*chip, 1 - c), me).wait_recv()
        for cp in first + passed:
            cp.wait_send()
        for cp in mine:
            cp.wait()

    outs = _sequencer_call(
        name, collective_id, body, shards,
        [jax.ShapeDtypeStruct((N_DEV,) + s.shape, s.dtype) for s in shards],
        [pltpu.SemaphoreType.DMA((7 * n,)), pltpu.SemaphoreType.DMA((7 * n,)), pltpu.SemaphoreType.DMA((n,))])
    return [o.reshape((N_DEV * s.shape[0],) + s.shape[1:]) for o, s in zip(outs, shards)]


def _sc_exchange(name, collective_id, srcs, land_shapes, plan, n_copies, peers, local=None):
    n = len(srcs)

    def body(*refs):
        bufs, send_sems, recv_sems = refs[:2 * n], refs[2 * n], refs[2 * n + 1]
        _handshake(peers())
        if local is not None:
            mine = [pltpu.make_async_copy(a, b, refs[2 * n + 2].at[k]) for k, (a, b) in enumerate(local(bufs))]
            for cp in mine:
                cp.start()
        cps = plan(bufs, send_sems, recv_sems)
        for cp in cps:
            cp.start()
        for cp in cps:
            cp.wait_send()
            cp.wait_recv()
        if local is not None:
            for cp in mine:
                cp.wait()

    scratch = [pltpu.SemaphoreType.DMA((n_copies,)), pltpu.SemaphoreType.DMA((n_copies,))]
    if local is not None:
        scratch.append(pltpu.SemaphoreType.DMA((n,)))
    return list(_sequencer_call(name, collective_id, body, srcs, land_shapes, scratch))


def _adam_vals(w, g, m, v):
    m = ADAM_B1 * m + (1.0 - ADAM_B1) * g
    v = ADAM_B2 * v + (1.0 - ADAM_B2) * (g * g)
    m_hat = m / (1.0 - ADAM_B1 ** ADAM_STEP)
    v_hat = v / (1.0 - ADAM_B2 ** ADAM_STEP)
    delta = -ADAM_LR * (m_hat / (jnp.sqrt(v_hat) + ADAM_EPS) + ADAM_WD * w)
    return delta, m, v


def _sum_parts(st_ref):
    g = st_ref[0].astype(F32)
    for k in range(1, st_ref.shape[0]):
        g = g + st_ref[k].astype(F32)
    return g


def _reduce_adam(name, st, w, m, v, deps=()):
    rows, cols = w.shape
    tr = _tile(rows, 512, 16)

    def body(st_ref, w_ref, m_ref, v_ref, *rest):
        g_out, d_out, m_out, v_out = rest[len(deps):]
        g = _sum_parts(st_ref)
        d, mm, vv = _adam_vals(w_ref[...], g, m_ref[...], v_ref[...])
        g_out[...] = g
        d_out[...] = d
        m_out[...] = mm
        v_out[...] = vv

    blk = pl.BlockSpec((tr, cols), lambda i: (i, 0))
    return pl.pallas_call(
        body,
        name=name,
        grid=(rows // tr,),
        in_specs=[pl.BlockSpec((st.shape[0], tr, cols), lambda i: (0, i, 0)), blk, blk, blk]
        + [pl.BlockSpec(memory_space=pl.ANY)] * len(deps),
        out_specs=[blk] * 4,
        out_shape=[jax.ShapeDtypeStruct(w.shape, F32)] * 4,
        compiler_params=_params(("arbitrary",)),
    )(st, w, m, v, *deps)


def _reduce_only(name, st, deps=()):
    _, rows, cols = st.shape
    tr = _tile(rows, 512, 16)

    def body(st_ref, *rest):
        rest[-1][...] = _sum_parts(st_ref)

    return pl.pallas_call(
        body,
        name=name,
        grid=(rows // tr,),
        in_specs=[pl.BlockSpec((st.shape[0], tr, cols), lambda i: (0, i, 0))]
        + [pl.BlockSpec(memory_space=pl.ANY)] * len(deps),
        out_specs=pl.BlockSpec((tr, cols), lambda i: (i, 0)),
        out_shape=jax.ShapeDtypeStruct((rows, cols), F32),
        compiler_params=_params(("arbitrary",)),
    )(st, *deps)


def _adam_only(name, w, g, m, v):
    rows, cols = w.shape
    tr = _tile(rows, 512, 16)

    def body(w_ref, g_ref, m_ref, v_ref, d_out, m_out, v_out):
        d, mm, vv = _adam_vals(w_ref[...], g_ref[...], m_ref[...], v_ref[...])
        d_out[...] = d
        m_out[...] = mm
        v_out[...] = vv

    blk = pl.BlockSpec((tr, cols), lambda i: (i, 0))
    return pl.pallas_call(
        body,
        name=name,
        grid=(rows // tr,),
        in_specs=[blk] * 4,
        out_specs=[blk] * 3,
        out_shape=[jax.ShapeDtypeStruct(w.shape, F32)] * 3,
        compiler_params=_params(("arbitrary",)),
    )(w, g, m, v)


def _prenorm(name, h, g, t, d, tm, deps=()):
    def fn(v):
        x, gg = v
        return [x * _rms_r(x) * gg]

    return _rowwise(name, [("row", h, d, 0), ("full", g)], [("row", (t, d), BF, d, 0)], fn, t=t, tm=tm, deps=deps)[0]


def _ffn_up(tag, xn, wgt, wut, t, d, f, tm):
    def up_epi(accs, ex):
        gg, uu = accs
        return [gg, uu, gg * _sig(gg) * uu]

    mats = [dict(a=xn, b=wgt, mode="nt", acc=0, tk=d), dict(a=xn, b=wut, mode="nt", acc=1, tk=d)]
    return _matmul(
        tag + "_up", mats, m=t, n=f, tm=tm, tn=_tile(f, 1408),
        outs=[("mn", (t, f), BF, 0)] * 3, epilogue=up_epi, n_acc=2, j_outer=True)


def _ffn_down(tag, hid, wd, h, g_post, g_next, t, d, f, tm):
    def down_epi(accs, ex):
        ff = accs[0]
        hh, gg, gn = ex
        h_new = hh + 0.5 * ff * _rms_r(ff) * gg
        return [ff, h_new, h_new * _rms_r(h_new) * gn]

    return _matmul(
        tag + "_down", [dict(a=hid, b=wd, mode="nn", acc=0, tk=f)], m=t, n=d, tm=tm, tn=d,
        extras=[("mn", h, 0), ("n", g_post), ("n", g_next)],
        outs=[("mn", (t, d), F32, 0)] * 2 + [("mn", (t, d), BF, 0)], epilogue=down_epi, n_acc=1)


def _prenorm_bwd_epi(accs, ex):
    hh, dh, gg = ex
    dx, dg = _rms_bwd(hh, gg, accs[0])
    return [dh + dx, dg]


def _dx_extras(h_in, dh, g_pre, fo, g_post):
    return [("mn", h_in, 0), ("mn", dh, 0), ("n", g_pre), ("mn", fo, 0), ("n", g_post)]


def _dx_outs(t, d):
    return [("mn", (t, d), F32, 0), ("acc", (1, d), F32), ("mn", (t, d), BF, 0), ("acc", (1, d), F32)]


def _dx_epi(scale):
    def epi(accs, ex):
        hh, dh, gg, fo, gp = ex
        dx, dg = _rms_bwd(hh, gg, accs[0])
        dh_new = dh + dx
        dfo, dgp = _rms_bwd(fo, gp, dh_new * scale)
        return [dh_new, dg, dfo, dgp]

    return epi


def _ffn_dhid(tag, df, wd, gate, up, t, d, f, tm):
    def hid_epi(accs, ex):
        dhid = accs[0]
        gg, uu = ex[0].astype(F32), ex[1].astype(F32)
        s = _sig(gg)
        return [dhid * uu * s * (1.0 + gg * (1.0 - s)), dhid * gg * s]

    return _matmul(
        tag + "_dhid", [dict(a=df, b=wd, mode="nt", acc=0, tk=d)], m=t, n=f, tm=tm, tn=_tile(f, 1408),
        extras=[("mn", gate, 0), ("mn", up, 0)],
        outs=[("mn", (t, f), BF, 0)] * 2, epilogue=hid_epi, n_acc=1, j_outer=True)


def _ffn_dwd(tag, hid, df, t, d, f):
    tk_t = _tile(t, TOKEN_K_TILE)
    return _matmul(
        tag + "_dwd", [dict(a=hid, b=df, mode="tn", acc=0, tk=tk_t)], m=f, n=d, tm=_tile(f, 1408), tn=d,
        nk=t // tk_t, outs=[("mn", (f, d), BF, 0)], epilogue=lambda accs, ex: accs, n_acc=1)


def _ffn_dwgu(tag, dgate, dup, xn, t, d, f, deps=()):
    tk_t = _tile(t, TOKEN_K_TILE)
    return _matmul(
        tag + "_dwgu",
        [dict(a=dgate, b=xn, mode="tn", acc=0, tk=tk_t), dict(a=dup, b=xn, mode="tn", acc=1, tk=tk_t)],
        m=f, n=d, tm=_tile(f, 1408), tn=d, nk=t // tk_t,
        outs=[("mn", (f, d), BF, 0)] * 2, epilogue=lambda accs, ex: accs, n_acc=2, deps=deps)


def _ffn_dx(tag, dh, h_in, dgate, dup, wgt, wut, g_pre, t, d, f, tm, deps=(), before=None):
    tf = _tile(f, 1408)
    mats = [dict(a=dgate, b=wgt, mode="nn", acc=0, tk=tf), dict(a=dup, b=wut, mode="nn", acc=0, tk=tf)]
    if before is None:
        return _matmul(
            tag + "_dx", mats, m=t, n=d, tm=tm, tn=d, nk=f // tf,
            extras=[("mn", h_in, 0), ("mn", dh, 0), ("n", g_pre)],
            outs=[("mn", (t, d), F32, 0), ("acc", (1, d), F32)], epilogue=_prenorm_bwd_epi, n_acc=1, deps=deps)
    fo, g_post, scale = before
    return _matmul(
        tag + "_dx", mats, m=t, n=d, tm=tm, tn=d, nk=f // tf, extras=_dx_extras(h_in, dh, g_pre, fo, g_post),
        outs=_dx_outs(t, d), epilogue=_dx_epi(scale), n_acc=1, deps=deps)


def _causal_mask():
    r = lax.broadcasted_iota(jnp.int32, (CHUNK, CHUNK), 0)
    c = lax.broadcasted_iota(jnp.int32, (CHUNK, CHUNK), 1)
    return r >= c


def _layernorm_parts(v):
    mu = jnp.mean(v, axis=-1, keepdims=True)
    vc = v - mu
    rstd = lax.rsqrt(jnp.mean(vc * vc, axis=-1, keepdims=True) + EPS)
    return vc * rstd, rstd


def _sgu_fwd(z, g_sgu, w_s, b_col, t, d, tm):
    dg = d // N_SGU_GROUPS
    n_chunk = tm // CHUNK

    def body(zu_ref, zv_ref, g_ref, w_ref, b_ref, a_ref):
        u = _gelu(zu_ref[...])
        vhat, _ = _layernorm_parts(_gelu(zv_ref[...]))
        vn = (vhat * g_ref[...]).astype(BF)
        mask = _causal_mask()
        for gi in range(N_SGU_GROUPS):
            ws = jnp.where(mask, w_ref[gi], 0.0).astype(BF)
            bias = b_ref[gi]
            for ci in range(n_chunk):
                rows, cols = slice(ci * CHUNK, (ci + 1) * CHUNK), slice(gi * dg, (gi + 1) * dg)
                sv = jnp.dot(ws, vn[rows, cols], preferred_element_type=F32) + bias
                a_ref[rows, cols] = (u[rows, cols] * sv).astype(BF)

    return pl.pallas_call(
        body,
        name="sgu_fwd",
        grid=(t // tm,),
        in_specs=[
            pl.BlockSpec((tm, d), lambda i: (i, 0)),
            pl.BlockSpec((tm, d), lambda i: (i, 1)),
            pl.BlockSpec((1, d), lambda i: (0, 0)),
            pl.BlockSpec(w_s.shape, lambda i: (0, 0, 0)),
            pl.BlockSpec(b_col.shape, lambda i: (0, 0, 0)),
        ],
        out_specs=pl.BlockSpec((tm, d), lambda i: (i, 0)),
        out_shape=jax.ShapeDtypeStruct((t, d), BF),
        compiler_params=_params(("arbitrary",)),
    )(z, z, g_sgu, w_s, b_col)


def _sgu_bwd(z, da, dz, g_sgu, w_s, b_col, t, d, tm):
    dg = d // N_SGU_GROUPS
    n_chunk = tm // CHUNK

    def body(zu_ref, zv_ref, da_ref, dz_in, g_ref, w_ref, b_ref, dz_ref, dw_ref, db_ref, dgn_ref, dvn_ref):
        del dz_in
        dzu_ref, dzv_ref = dz_ref.at[:, pl.ds(0, d)], dz_ref.at[:, pl.ds(d, d)]
        i = pl.program_id(0)
        zu, zv = zu_ref[...], zv_ref[...]
        u = _gelu(zu)
        vhat, rstd = _layernorm_parts(_gelu(zv))
        gn = g_ref[...]
        vn = (vhat * gn).astype(BF)
        da_v = da_ref[...].astype(F32)
        dsv_all = (da_v * u).astype(BF)
        mask = _causal_mask()
        for gi in range(N_SGU_GROUPS):
            ws = jnp.where(mask, w_ref[gi], 0.0).astype(BF)
            bias = b_ref[gi]
            dw = jnp.zeros((CHUNK, CHUNK), F32)
            dbias = jnp.zeros((CHUNK, 1), F32)
            for ci in range(n_chunk):
                rows, cols = slice(ci * CHUNK, (ci + 1) * CHUNK), slice(gi * dg, (gi + 1) * dg)
                vn_c, dsv = vn[rows, cols], dsv_all[rows, cols]
                sv = jnp.dot(ws, vn_c, preferred_element_type=F32) + bias
                dzu_ref[rows, cols] = (da_v[rows, cols] * sv * _gelu_grad(zu[rows, cols])).astype(BF)
                dw = dw + lax.dot_general(dsv, vn_c, _DN["nt"], preferred_element_type=F32)
                dbias = dbias + jnp.sum(dsv.astype(F32), axis=1, keepdims=True)
                dvn_ref[rows, cols] = lax.dot_general(ws, dsv, _DN["tn"], preferred_element_type=F32)
            dw = jnp.where(mask, dw, 0.0)

            @pl.when(i == 0)
            def _():
                dw_ref[gi] = dw
                db_ref[gi] = dbias

            @pl.when(i != 0)
            def _():
                dw_ref[gi] += dw
                db_ref[gi] += dbias

        dvn = dvn_ref[...]
        dgn = jnp.sum(dvn * vhat, axis=0, keepdims=True)

        @pl.when(i == 0)
        def _():
            dgn_ref[...] = dgn

        @pl.when(i != 0)
        def _():
            dgn_ref[...] += dgn

        dvh = dvn * gn
        dv = rstd * (dvh - jnp.mean(dvh, axis=-1, keepdims=True) - vhat * jnp.mean(dvh * vhat, axis=-1, keepdims=True))
        dzv_ref[...] = (dv * _gelu_grad(zv)).astype(BF)

    return pl.pallas_call(
        body,
        name="sgu_bwd",
        grid=(t // tm,),
        in_specs=[
            pl.BlockSpec((tm, d), lambda i: (i, 0)),
            pl.BlockSpec((tm, d), lambda i: (i, 1)),
            pl.BlockSpec((tm, d), lambda i: (i, 0)),
            pl.BlockSpec(memory_space=pl.ANY),
            pl.BlockSpec((1, d), lambda i: (0, 0)),
            pl.BlockSpec(w_s.shape, lambda i: (0, 0, 0)),
            pl.BlockSpec(b_col.shape, lambda i: (0, 0, 0)),
        ],
        out_specs=[
            pl.BlockSpec((tm, 2 * d), lambda i: (i, 0)),
            pl.BlockSpec(w_s.shape, lambda i: (0, 0, 0)),
            pl.BlockSpec(b_col.shape, lambda i: (0, 0, 0)),
            pl.BlockSpec((1, d), lambda i: (0, 0)),
        ],
        out_shape=[
            jax.ShapeDtypeStruct(dz.shape, BF),
            jax.ShapeDtypeStruct(w_s.shape, F32),
            jax.ShapeDtypeStruct(b_col.shape, F32),
            jax.ShapeDtypeStruct((1, d), F32),
        ],
        scratch_shapes=[pltpu.VMEM((tm, d), F32)],
        input_output_aliases={3: 0},
        compiler_params=_params(("arbitrary",)),
    )(z, z, da, dz, g_sgu, w_s, b_col)


def _shift_down(x, k, row):
    return jnp.where(row >= k, pltpu.roll(x, k, 0), 0.0)


def _shift_up(x, k, row, t):
    return jnp.where(row < t - k, pltpu.roll(x, t - k, 0), 0.0)


def _doublings(window):
    steps = int(math.log2(window))
    assert 2 ** steps == window
    return [2 ** s for s in range(steps)]


def _pool_diff(c, window, row):
    s = c
    for k in _doublings(window):
        s = s + _shift_down(s, k, row)
    count = jnp.minimum(row + 1, window).astype(F32)
    return s / count - c, count


def _pool_fwd(z, pool_w, pool_scale, t, d):
    dgp = d // len(POOL_WINDOWS)
    cblk = (2 * d) // dgp

    def body(zc_ref, w_ref, s_ref, b_ref):
        row = lax.broadcasted_iota(jnp.int32, (t, 1), 0)
        for gi, window in enumerate(POOL_WINDOWS):
            @pl.when(pl.program_id(0) == gi)
            def _(window=window):
                diff, _ = _pool_diff(zc_ref[...], window, row)
                out = jnp.dot(diff.astype(BF), w_ref[...], preferred_element_type=F32)
                b_ref[...] = (out * s_ref[...]).astype(BF)

    return pl.pallas_call(
        body,
        name="pool_fwd",
        grid=(len(POOL_WINDOWS),),
        in_specs=[
            pl.BlockSpec((t, dgp), lambda g: (0, cblk + g)),
            pl.BlockSpec((None, dgp, dgp), lambda g: (g, 0, 0)),
            pl.BlockSpec((1, dgp), lambda g: (0, g)),
        ],
        out_specs=pl.BlockSpec((t, dgp), lambda g: (0, g)),
        out_shape=jax.ShapeDtypeStruct((t, d), BF),
        compiler_params=_params(("arbitrary",)),
    )(z, pool_w, pool_scale)


def _pool_bwd(z, db, dz, pool_w, pool_scale, t, d):
    dgp = d // len(POOL_WINDOWS)
    cblk = (2 * d) // dgp

    def body(zc_ref, db_ref, dz_in, w_ref, s_ref, dzc_ref, dw_ref, ds_ref):
        del dz_in
        row = lax.broadcasted_iota(jnp.int32, (t, 1), 0)
        for gi, window in enumerate(POOL_WINDOWS):
            @pl.when(pl.program_id(0) == gi)
            def _(window=window):
                diff, count = _pool_diff(zc_ref[...], window, row)
                diff = diff.astype(BF)
                w = w_ref[...]
                dbv = db_ref[...].astype(F32)
                out = jnp.dot(diff, w, preferred_element_type=F32)
                ds_ref[...] = jnp.sum(dbv * out, axis=0, keepdims=True)
                dout = (dbv * s_ref[...]).astype(BF)
                dw_ref[...] = lax.dot_general(diff, dout, _DN["tn"], preferred_element_type=F32).astype(BF)
                ddiff = lax.dot_general(dout, w, _DN["nt"], preferred_element_type=F32)
                s = ddiff / count
                for k in _doublings(window):
                    s = s + _shift_up(s, k, row, t)
                dzc_ref[...] = (s - ddiff).astype(BF)

    return pl.pallas_call(
        body,
        name="pool_bwd",
        grid=(len(POOL_WINDOWS),),
        in_specs=[
            pl.BlockSpec((t, dgp), lambda g: (0, cblk + g)),
            pl.BlockSpec((t, dgp), lambda g: (0, g)),
            pl.BlockSpec(memory_space=pl.ANY),
            pl.BlockSpec((None, dgp, dgp), lambda g: (g, 0, 0)),
            pl.BlockSpec((1, dgp), lambda g: (0, g)),
        ],
        out_specs=[
            pl.BlockSpec((t, dgp), lambda g: (0, cblk + g)),
            pl.BlockSpec((None, dgp, dgp), lambda g: (g, 0, 0)),
            pl.BlockSpec((1, dgp), lambda g: (0, g)),
        ],
        out_shape=[
            jax.ShapeDtypeStruct(dz.shape, BF),
            jax.ShapeDtypeStruct(pool_w.shape, BF),
            jax.ShapeDtypeStruct((1, d), F32),
        ],
        input_output_aliases={2: 0},
        compiler_params=_params(("arbitrary",)),
    )(z, db, dz, pool_w, pool_scale)


def _mix_dy(dm, w_o, z, ya, yb, t, d, tm):
    def body(dm_ref, w_ref, gate_ref, ya_ref, yb_ref, dyab_ref, dz_ref, dy_ref):
        j = pl.program_id(1)

        @pl.when(j == 0)
        def _():
            dy_ref[...] = lax.dot_general(dm_ref[...], w_ref[...], _DN["nt"], preferred_element_type=F32)

        dy = dy_ref[...]
        s = _sig(gate_ref[...])
        yv = jnp.where(j == 0, ya_ref[...], yb_ref[...]).astype(F32)
        dyab_ref[...] = (dy * s).astype(BF)
        dz_ref[...] = (dy * yv * s * (1.0 - s)).astype(BF)

    row = pl.BlockSpec((tm, d), lambda i, j: (i, 0))
    return pl.pallas_call(
        body,
        name="mix_dy",
        grid=(t // tm, 2),
        in_specs=[row, pl.BlockSpec((d, d), lambda i, j: (0, 0)), pl.BlockSpec((tm, d), lambda i, j: (i, 3 + j)), row, row],
        out_specs=[pl.BlockSpec((tm, d), lambda i, j: (i, j)), pl.BlockSpec((tm, d), lambda i, j: (i, 3 + j))],
        out_shape=[jax.ShapeDtypeStruct((t, 2 * d), BF), jax.ShapeDtypeStruct((t, 5 * d), BF)],
        scratch_shapes=[pltpu.VMEM((tm, d), F32)],
        compiler_params=_params(("arbitrary", "arbitrary")),
    )(dm, w_o, z, ya, yb)


def _everyone():
    x, y, c = _place()
    return [(1 - x if dd & 4 else x, 1 - y if dd & 2 else y, 1 - c if dd & 1 else c) for dd in range(1, N_DEV)]


def _direct_plan(n):
    def plan(refs, send_sems, recv_sems):
        x, y, c = _place()
        me = 4 * x + 2 * y + c
        cps = []
        for i in range(n):
            for k, (px, py, pc) in enumerate(_everyone()):
                cps.append(_remote(refs[i].at[4 * px + 2 * py + pc], refs[n + i].at[me], send_sems, recv_sems,
                                   7 * i + k, (px, py, pc)))
        return cps

    def local(refs):
        x, y, c = _place()
        me = 4 * x + 2 * y + c
        return [(refs[i].at[me], refs[n + i].at[me]) for i in range(n)]

    return plan, local


def _sc_scatter_direct(name, collective_id, parts):
    n = len(parts)
    p8 = [a.reshape(N_DEV, a.shape[0] // N_DEV, a.shape[1]) for a in parts]
    plan, local = _direct_plan(n)
    return _sc_exchange(name, collective_id, p8, [jax.ShapeDtypeStruct(a.shape, a.dtype) for a in p8], plan,
                        7 * n, _everyone, local=local)


def _stack_rows(arrays):
    parts, starts, row = [], [], 0
    for a in arrays:
        pad = -a.shape[0] % 8
        starts.append(row)
        parts += [a, jnp.zeros((pad, a.shape[1]), a.dtype)] if pad else [a]
        row += a.shape[0] + pad
    return jnp.concatenate(parts, axis=0), starts


def _adam_replicated(name, stacks, layout, weights, deps=()):
    n_st, n_par = len(stacks), len(weights)

    def body(*refs):
        st_refs = refs[:n_st]
        w_refs = refs[n_st:n_st + 3 * n_par]
        out_refs = refs[n_st + 3 * n_par + len(deps):]

        def summed(which, row, rows):
            g = st_refs[which][0, pl.ds(row, rows), :]
            for k in range(1, N_DEV):
                g = g + st_refs[which][k, pl.ds(row, rows), :]
            return g

        for i in range(n_par):
            w_ref, m_ref, v_ref = w_refs[3 * i:3 * i + 3]
            g = summed(layout[i][0], layout[i][1], w_ref.shape[0])
            dlt, mm, vv = _adam_vals(w_ref[...], g, m_ref[...], v_ref[...])
            for r, val in zip(out_refs[4 * i:4 * i + 4], (g, dlt, mm, vv)):
                r[...] = val
        out_refs[4 * n_par][...] = jnp.sum(summed(layout[n_par][0], layout[n_par][1], 1), axis=1, keepdims=True)

    flat_w = [a for wmv in weights for a in wmv]
    out_shape = [jax.ShapeDtypeStruct(wmv[0].shape, F32) for wmv in weights for _ in range(4)]
    out_shape.append(jax.ShapeDtypeStruct((1, 1), F32))
    vmem = pl.BlockSpec(memory_space=pltpu.VMEM)
    res = pl.pallas_call(
        body,
        name=name,
        in_specs=[vmem] * (n_st + len(flat_w)) + [pl.BlockSpec(memory_space=pl.ANY)] * len(deps),
        out_specs=[vmem] * len(out_shape),
        out_shape=out_shape,
        compiler_params=_params(),
    )(*stacks, *flat_w, *deps)
    return [res[4 * i:4 * i + 4] for i in range(n_par)], res[4 * n_par]


def kernel(x, p, ffn1_pre_g, ffn1_w_gate, ffn1_w_up, ffn1_w_down, ffn1_post_g, mix_pre_g, w_in, sgu_norm_g, sgu_w, sgu_b, pool_w, pool_scale, w_out_a, w_out_b, w_o, mix_post_g, ffn2_pre_g, ffn2_w_gate, ffn2_w_up, ffn2_w_down, ffn2_post_g, ple_pre_g, ple_w_gate, ple_w_proj, ple_post_g, loss_target, m_ffn1_pre_g, m_ffn1_w_gate, m_ffn1_w_up, m_ffn1_w_down, m_ffn1_post_g, m_mix_pre_g, m_w_in, m_sgu_norm_g, m_sgu_w, m_sgu_b, m_pool_w, m_pool_scale, m_w_out_a, m_w_out_b, m_w_o, m_mix_post_g, m_ffn2_pre_g, m_ffn2_w_gate, m_ffn2_w_up, m_ffn2_w_down, m_ffn2_post_g, m_ple_pre_g, m_ple_w_gate, m_ple_w_proj, m_ple_post_g, v_ffn1_pre_g, v_ffn1_w_gate, v_ffn1_w_up, v_ffn1_w_down, v_ffn1_post_g, v_mix_pre_g, v_w_in, v_sgu_norm_g, v_sgu_w, v_sgu_b, v_pool_w, v_pool_scale, v_w_out_a, v_w_out_b, v_w_o, v_mix_post_g, v_ffn2_pre_g, v_ffn2_w_gate, v_ffn2_w_up, v_ffn2_w_down, v_ffn2_post_g, v_ple_pre_g, v_ple_w_gate, v_ple_w_proj, v_ple_post_g):
    args = dict(locals())
    names = ["ffn1_pre_g", "ffn1_w_gate", "ffn1_w_up", "ffn1_w_down", "ffn1_post_g", "mix_pre_g", "w_in",
             "sgu_norm_g", "sgu_w", "sgu_b", "pool_w", "pool_scale", "w_out_a", "w_out_b", "w_o", "mix_post_g",
             "ffn2_pre_g", "ffn2_w_gate", "ffn2_w_up", "ffn2_w_down", "ffn2_post_g", "ple_pre_g", "ple_w_gate",
             "ple_w_proj", "ple_post_g"]
    w = {k: args[k][0] for k in names}
    mom = {k: args["m_" + k][0] for k in names}
    var = {k: args["v_" + k][0] for k in names}

    assert x.shape[0] == 1 and p.shape[:2] == (1, 1)
    t, d = x.shape[1], x.shape[2]
    f = ffn1_w_gate.shape[2] * N_DEV
    d_in = w_in.shape[2] * N_DEV
    d_ple = p.shape[3]
    n_pool = len(POOL_WINDOWS)
    dgp = d // n_pool
    assert d_in == 5 * d and t % CHUNK == 0
    tm = _tile(t, 512, CHUNK)
    xs, ps, target = x[0], p[0, 0], loss_target[0]

    col_sharded = ("ffn1_w_gate", "ffn1_w_up", "ffn2_w_gate", "ffn2_w_up", "w_in", "ple_w_proj")

    def shard_of(k):
        if k in col_sharded:
            return w[k].T.astype(BF)
        if k == "pool_w":
            return w[k].reshape(-1, dgp).astype(BF)
        return w[k].astype(BF)

    rows_pw = dgp // N_DEV
    b_col = w["sgu_b"][:, :, None]
    gains = {k: w[k][None, :] for k in names if w[k].ndim == 1}
    full = {}

    groups = [["ffn1_w_gate", "ffn1_w_up"], ["ffn1_w_down", "w_in"], ["pool_w", "w_out_a", "w_out_b", "w_o"],
              ["ffn2_w_gate", "ffn2_w_up", "ffn2_w_down", "ple_w_gate", "ple_w_proj"]]
    def gather(gi, behind=None):
        shards = [shard_of(k) for k in groups[gi]]
        if behind is not None:
            shards = lax.optimization_barrier((shards, behind))[0]
        full.update(zip(groups[gi], _sc_gather("gather%d" % gi, gi, shards)))

    gather(0)
    xn1 = _prenorm("ffn1_prenorm", xs, gains["ffn1_pre_g"], t, d, tm)
    gather(1, xn1)
    g1, u1, hid1 = _ffn_up("ffn1", xn1, full["ffn1_w_gate"], full["ffn1_w_up"], t, d, f, tm)
    gather(2, hid1)
    f1, h1, xn2 = _ffn_down("ffn1", hid1, full["ffn1_w_down"], xs, gains["ffn1_post_g"], gains["mix_pre_g"], t, d, f, tm)
    gather(3, h1)

    pool_full = full["pool_w"].reshape(N_DEV, n_pool, rows_pw, dgp).transpose(1, 0, 2, 3).reshape(n_pool, dgp, dgp)
    z = _matmul("mix_in", [dict(a=xn2, b=full["w_in"], mode="nt", acc=0, tk=d)], m=t, n=d_in, tm=tm, tn=d,
                outs=[("mn", (t, d_in), F32, 0)], epilogue=lambda accs, ex: accs, n_acc=1, j_outer=True)[0]
    a_br = _sgu_fwd(z, gains["sgu_norm_g"], w["sgu_w"], b_col, t, d, tm)
    b_br = _pool_fwd(z, pool_full, gains["pool_scale"], t, d)

    def merge_epi(accs, ex):
        ya, yb = accs
        ga, gb = ex
        return [ya, yb, _sig(ga) * ya + _sig(gb) * yb]

    ya, yb, y = _matmul(
        "mix_merge",
        [dict(a=a_br, b=full["w_out_a"], mode="nn", acc=0, tk=d), dict(a=b_br, b=full["w_out_b"], mode="nn", acc=1, tk=d)],
        m=t, n=d, tm=tm, tn=d, extras=[("mn", z, 3), ("mn", z, 4)],
        outs=[("mn", (t, d), BF, 0)] * 3, epilogue=merge_epi, n_acc=2)

    def proj_epi(accs, ex):
        mm = accs[0]
        hh, gg, gn = ex
        h_new = hh + mm * _rms_r(mm) * gg
        return [mm, h_new, h_new * _rms_r(h_new) * gn]

    m_out, h2, xn3 = _matmul(
        "mix_proj", [dict(a=y, b=full["w_o"], mode="nn", acc=0, tk=d)], m=t, n=d, tm=tm, tn=d,
        extras=[("mn", h1, 0), ("n", gains["mix_post_g"]), ("n", gains["ffn2_pre_g"])],
        outs=[("mn", (t, d), F32, 0)] * 2 + [("mn", (t, d), BF, 0)], epilogue=proj_epi, n_acc=1)

    g2, u2, hid2 = _ffn_up("ffn2", xn3, full["ffn2_w_gate"], full["ffn2_w_up"], t, d, f, tm)
    f2, h3, xn4 = _ffn_down("ffn2", hid2, full["ffn2_w_down"], h2, gains["ffn2_post_g"], gains["ple_pre_g"], t, d, f, tm)

    def ple_epi(accs, ex):
        gl, e = accs
        hh, tgt, gg = ex
        q = _sig(gl) * e
        err = hh + q * _rms_r(q) * gg - tgt
        return [gl, e, err * (1.0 / d), jnp.sum(err * err, axis=0, keepdims=True)]

    gl, e_ple, dh4, loss_vec = _matmul(
        "ple_fwd",
        [dict(a=xn4, b=full["ple_w_gate"], mode="nn", acc=0, tk=d), dict(a=ps, b=full["ple_w_proj"], mode="nt", acc=1, tk=d_ple)],
        m=t, n=d, tm=tm, tn=d, extras=[("mn", h3, 0), ("mn", target, 0), ("n", gains["ple_post_g"])],
        outs=[("mn", (t, d), F32, 0)] * 3 + [("acc", (1, d), F32)], epilogue=ple_epi, n_acc=2)

    tk_t = _tile(t, TOKEN_K_TILE)

    def ple_post_fn(v):
        dy, gl_v, e_v, gg = v
        s = _sig(gl_v)
        dq, dg = _rms_bwd(s * e_v, gg, dy)
        return [dq * e_v * s * (1.0 - s), dq * s, dg]

    dgl, de, dg_ple_post = _rowwise(
        "ple_post_bwd", [("row", dh4, d, 0), ("row", gl, d, 0), ("row", e_ple, d, 0), ("full", gains["ple_post_g"])],
        [("row", (t, d), BF, d, 0), ("row", (t, d), BF, d, 0), ("acc", (1, d), F32)], ple_post_fn, t=t, tm=tm)
    ident = lambda accs, ex: accs
    dw_ple_gate = _matmul("ple_dwg", [dict(a=xn4, b=dgl, mode="tn", acc=0, tk=tk_t)], m=d, n=d, tm=d, tn=d,
                          nk=t // tk_t, outs=[("mn", (d, d), BF, 0)], epilogue=ident, n_acc=1)[0]
    dw_ple_proj_t = _matmul("ple_dwp", [dict(a=de, b=ps, mode="tn", acc=0, tk=tk_t)], m=d, n=d_ple, tm=d, tn=d_ple,
                            nk=t // tk_t, outs=[("mn", (d, d_ple), BF, 0)], epilogue=ident, n_acc=1)[0]
    dh3, dg_ple_pre, df2, dg_f2_post = _matmul(
        "ple_dx", [dict(a=dgl, b=full["ple_w_gate"], mode="nt", acc=0, tk=d)], m=t, n=d, tm=tm, tn=d,
        extras=_dx_extras(h3, dh4, gains["ple_pre_g"], f2, gains["ffn2_post_g"]),
        outs=_dx_outs(t, d), epilogue=_dx_epi(0.5), n_acc=1)

    staged = []
    n_gather = len(groups)

    def scatter(keys, parts):
        k = len(staged)
        staged.append((keys, _sc_scatter_direct("scatter%d" % k, n_gather + k, parts)))

    dgate2, dup2 = _ffn_dhid("ffn2", df2, full["ffn2_w_down"], g2, u2, t, d, f, tm)
    dwd2 = _ffn_dwd("ffn2", hid2, df2, t, d, f)[0]
    dwg2, dwu2 = _ffn_dwgu("ffn2", dgate2, dup2, xn3, t, d, f)
    scatter(["ple_w_gate", "ple_w_proj", "ffn2_w_down", "ffn2_w_gate", "ffn2_w_up"],
            [dw_ple_gate, dw_ple_proj_t, dwd2, dwg2, dwu2])
    dh2, dg_f2_pre, dm, dg_mix_post = _ffn_dx(
        "ffn2", dh3, h2, dgate2, dup2, full["ffn2_w_gate"], full["ffn2_w_up"], gains["ffn2_pre_g"], t, d, f, tm,
        deps=[dw_ple_gate, dw_ple_proj_t, dwd2, dwg2], before=(m_out, gains["mix_post_g"], 1.0))

    dyab, dz = _mix_dy(dm, full["w_o"], z, ya, yb, t, d, tm)
    da, db = _matmul(
        "mix_dab",
        [dict(a=dyab, b=full["w_out_a"], mode="nt", acc=0, tk=d),
         dict(a=dyab, b=full["w_out_b"], mode="nt", acc=1, tk=d, a_off=(0, 1))],
        m=t, n=d, tm=tm, tn=d, outs=[("mn", (t, d), BF, 0)] * 2, epilogue=ident, n_acc=2)
    dz, dsgu_w, dsgu_b, dg_sgu = _sgu_bwd(z, da, dz, gains["sgu_norm_g"], w["sgu_w"], b_col, t, d, _tile(t, 256, CHUNK))
    dz, dpool_w, dg_pool = _pool_bwd(z, db, dz, pool_full, gains["pool_scale"], t, d)
    dw_in_t = _matmul(
        "mix_dwin", [dict(a=dz, b=xn2, mode="tn", acc=0, tk=tk_t)], m=d_in, n=d, tm=d, tn=d,
        nk=t // tk_t, outs=[("mn", (d_in, d), BF, 0)], epilogue=ident, n_acc=1)[0]
    dpool_rows = dpool_w.reshape(n_pool, N_DEV, rows_pw, dgp).transpose(1, 0, 2, 3).reshape(N_DEV * n_pool * rows_pw, dgp)
    scatter(["pool_w", "w_in"], [dpool_rows, dw_in_t])
    dh1, dg_mix_pre, df1, dg_f1_post = _matmul(
        "mix_dx", [dict(a=dz, b=full["w_in"], mode="nn", acc=0, tk=d)], m=t, n=d, tm=tm, tn=d, nk=d_in // d,
        extras=_dx_extras(h1, dh2, gains["mix_pre_g"], f1, gains["ffn1_post_g"]),
        outs=_dx_outs(t, d), epilogue=_dx_epi(0.5), n_acc=1, deps=[dpool_rows, dw_in_t])

    dgate1, dup1 = _ffn_dhid("ffn1", df1, full["ffn1_w_down"], g1, u1, t, d, f, tm)
    dwd1 = _ffn_dwd("ffn1", hid1, df1, t, d, f)[0]
    scatter(["ffn1_w_down"], [dwd1])
    dwg1, dwu1 = _ffn_dwgu("ffn1", dgate1, dup1, xn1, t, d, f, deps=[dwd1])
    scatter(["ffn1_w_gate", "ffn1_w_up"], [dwg1, dwu1])
    grad_x, dg_f1_pre = _ffn_dx("ffn1", dh1, xs, dgate1, dup1, full["ffn1_w_gate"], full["ffn1_w_up"],
                                gains["ffn1_pre_g"], t, d, f, tm, deps=[dwg1])
    dw_o = _matmul("mix_dwo", [dict(a=y, b=dm, mode="tn", acc=0, tk=tk_t)], m=d, n=d, tm=d, tn=d, nk=t // tk_t,
                   outs=[("mn", (d, d), BF, 0)], epilogue=ident, n_acc=1, deps=[grad_x])[0]
    dw_out_a, dw_out_b = _matmul(
        "mix_dwab",
        [dict(a=a_br, b=dyab, mode="tn", acc=0, tk=tk_t),
         dict(a=b_br, b=dyab, mode="tn", acc=1, tk=tk_t, b_off=(0, 1))],
        m=d, n=d, tm=d, tn=d, nk=t // tk_t, outs=[("mn", (d, d), BF, 0)] * 2, epilogue=ident, n_acc=2, deps=[dw_o])
    scatter(["w_o", "w_out_a", "w_out_b"], [dw_o, dw_out_a, dw_out_b])

    grad, delta, new_m, new_v = {}, {}, {}, {}

    def update(group, partial_sums, behind):
        done = []
        for k, st in zip(group, partial_sums):
            if k in col_sharded and w[k].shape[1] % LANES:
                res = _reduce_adam("adam_" + k, st, w[k].T, mom[k].T, var[k].T, deps=behind)
                grad[k], delta[k], new_m[k], new_v[k] = [r.T for r in res]
            elif k in col_sharded:
                grad[k] = _reduce_only("sum_" + k, st, deps=behind).T
                delta[k], new_m[k], new_v[k] = _adam_only("adam_" + k, w[k], grad[k], mom[k], var[k])
            else:
                shape = w[k].shape
                flat = (-1, shape[-1])
                res = _reduce_adam("adam_" + k, st, w[k].reshape(flat), mom[k].reshape(flat), var[k].reshape(flat),
                                   deps=behind)
                grad[k], delta[k], new_m[k], new_v[k] = [r.reshape(shape) for r in res]
            done.append(delta[k])
        return done

    gain_grads = {
        "ffn1_pre_g": dg_f1_pre, "ffn1_post_g": dg_f1_post, "mix_pre_g": dg_mix_pre, "sgu_norm_g": dg_sgu,
        "pool_scale": dg_pool, "mix_post_g": dg_mix_post, "ffn2_pre_g": dg_f2_pre, "ffn2_post_g": dg_f2_post,
        "ple_pre_g": dg_ple_pre, "ple_post_g": dg_ple_post,
    }
    gain_names = list(gain_grads)
    n_mix = N_SGU_GROUPS * CHUNK
    wide, wide_rows = _stack_rows([gain_grads[k] for k in gain_names] + [loss_vec * (0.5 / d)])
    narrow, narrow_rows = _stack_rows([dsgu_b.reshape(N_SGU_GROUPS, CHUNK), dsgu_w.reshape(n_mix, CHUNK)])
    plan_s, local_s = _broadcast_plan(2)
    stacks = _sc_exchange(
        "small", n_gather + len(staged), [wide, narrow],
        [jax.ShapeDtypeStruct((N_DEV,) + wide.shape, F32), jax.ShapeDtypeStruct((N_DEV,) + narrow.shape, F32)],
        plan_s, 2 * (N_DEV - 1), _everyone, local=local_s)
    done = [dw_out_a]
    for keys, partial_sums in staged:
        done = update(keys, partial_sums, done)
    rep_names = gain_names + ["sgu_b", "sgu_w"]
    rep_shape = {k: (1, d) for k in gain_names}
    rep_shape.update(sgu_b=(N_SGU_GROUPS, CHUNK), sgu_w=(n_mix, CHUNK))
    layout = [(0, r) for r in wide_rows[:-1]] + [(1, narrow_rows[0]), (1, narrow_rows[1]), (0, wide_rows[-1])]
    res, loss = _adam_replicated(
        "adam_replicated", stacks, layout,
        [tuple(src[k].reshape(rep_shape[k]) for src in (w, mom, var)) for k in rep_names], deps=done)
    for k, quad in zip(rep_names, res):
        for dst, val in zip((grad, delta, new_m, new_v), quad):
            dst[k] = val.reshape(w[k].shape)

    out = [loss[0, 0], grad_x[None]]
    for group in (grad, delta, new_m, new_v):
        out += [group[k][None] for k in names]
    return tuple(out)
```

```python
import math

import jax
import jax.numpy as jnp
from jax import lax
from jax.experimental import pallas as pl
from jax.experimental.pallas import tpu as pltpu
from jax.experimental.pallas import tpu_sc as plsc

EPS = 1e-6
CHUNK = 128
N_SGU_GROUPS = 4
POOL_WINDOWS = (2, 4, 8, 16)
ADAM_LR = 0.001
ADAM_B1 = 0.9
ADAM_B2 = 0.999
ADAM_EPS = 1e-08
ADAM_WD = 0.01
ADAM_STEP = 10

N_DEV = 8
MESH = pl.DeviceIdType.MESH
V7X_VMEM_BYTES = 64 * 1024 * 1024
VMEM_LIMIT = V7X_VMEM_BYTES - 8 * 1024 * 1024
TOKEN_K_TILE = 1024
LANES = 128
UPDATE_ROWS = 256
BF = jnp.bfloat16
F32 = jnp.float32

_DN = {
    "nn": (((1,), (0,)), ((), ())),
    "nt": (((1,), (1,)), ((), ())),
    "tn": (((0,), (0,)), ((), ())),
}


def _params(sem=None):
    return pltpu.CompilerParams(dimension_semantics=sem, vmem_limit_bytes=VMEM_LIMIT)


def _tile(n, target, align=128):
    t = min(n, target)
    t -= t % align
    while t >= align:
        if n % t == 0:
            return t
        t -= align
    return n


def _sig(x):
    return 0.5 + 0.5 * jnp.tanh(0.5 * x)


_GELU_K = math.sqrt(2.0 / math.pi)
_GELU_C = 0.044715


def _gelu(x):
    return 0.5 * x * (1.0 + jnp.tanh(_GELU_K * (x + _GELU_C * x * x * x)))


def _gelu_grad(x):
    t = jnp.tanh(_GELU_K * (x + _GELU_C * x * x * x))
    return 0.5 * (1.0 + t) + 0.5 * x * (1.0 - t * t) * _GELU_K * (1.0 + 3.0 * _GELU_C * x * x)


def _rms_r(x):
    return lax.rsqrt(jnp.mean(x * x, axis=-1, keepdims=True) + EPS)


def _rms_bwd(x, g, dy):
    r = _rms_r(x)
    xh = x * r
    gy = dy * g
    dx = r * (gy - xh * jnp.mean(xh * gy, axis=-1, keepdims=True))
    return dx, jnp.sum(dy * xh, axis=0, keepdims=True)


def _matmul(name, mats, *, m, n, tm, tn, nk=1, extras=(), outs, epilogue, n_acc, j_outer=False, deps=()):
    ni, nj = m // tm, n // tn
    assert ni * tm == m and nj * tn == n, (name, m, n, tm, tn)
    if j_outer:
        grid = (nj, ni, nk)

        def ij(g0, g1):
            return g1, g0
    else:
        grid = (ni, nj, nk)

        def ij(g0, g1):
            return g0, g1

    in_specs, args = [], []
    for mt in mats:
        mode, tk = mt["mode"], mt["tk"]
        ao, bo = mt.get("a_off", (0, 0)), mt.get("b_off", (0, 0))
        if mode == "tn":
            sa = pl.BlockSpec((tk, tm), lambda g0, g1, kk, ao=ao: (ao[0] + kk, ao[1] + ij(g0, g1)[0]))
        else:
            sa = pl.BlockSpec((tm, tk), lambda g0, g1, kk, ao=ao: (ao[0] + ij(g0, g1)[0], ao[1] + kk))
        if mode == "nt":
            sb = pl.BlockSpec((tn, tk), lambda g0, g1, kk, bo=bo: (bo[0] + ij(g0, g1)[1], bo[1] + kk))
        else:
            sb = pl.BlockSpec((tk, tn), lambda g0, g1, kk, bo=bo: (bo[0] + kk, bo[1] + ij(g0, g1)[1]))
        in_specs += [sa, sb]
        args += [mt["a"], mt["b"]]
    n_mat_refs = len(args)
    for ex in extras:
        if ex[0] == "mn":
            in_specs.append(pl.BlockSpec((tm, tn), lambda g0, g1, kk, c=ex[2]: (ij(g0, g1)[0], c + ij(g0, g1)[1])))
        else:
            in_specs.append(pl.BlockSpec((1, tn), lambda g0, g1, kk: (0, ij(g0, g1)[1])))
        args.append(ex[1])
    n_ex_end = len(args)
    in_specs += [pl.BlockSpec(memory_space=pl.ANY)] * len(deps)
    args += list(deps)
    n_in = len(args)
    out_specs, out_shape = [], []
    for o in outs:
        if o[0] == "mn":
            out_specs.append(pl.BlockSpec((tm, tn), lambda g0, g1, kk, c=o[3]: (ij(g0, g1)[0], c + ij(g0, g1)[1])))
        else:
            assert nj == 1, name
            out_specs.append(pl.BlockSpec((1, tn), lambda g0, g1, kk: (0, 0)))
        out_shape.append(jax.ShapeDtypeStruct(o[1], o[2]))
    n_out = len(outs)

    def body(*refs):
        mat_refs = refs[:n_mat_refs]
        ex_refs = refs[n_mat_refs:n_ex_end]
        out_refs = refs[n_in:n_in + n_out]
        acc_refs = refs[n_in + n_out:]
        i = ij(pl.program_id(0), pl.program_id(1))[0]
        kk = pl.program_id(2)

        def products():
            res = [None] * n_acc
            for idx, mt in enumerate(mats):
                a = mat_refs[2 * idx][...].astype(BF)
                b = mat_refs[2 * idx + 1][...].astype(BF)
                p = lax.dot_general(a, b, _DN[mt["mode"]], preferred_element_type=F32)
                q = mt["acc"]
                res[q] = p if res[q] is None else res[q] + p
            return res

        def finish(accs):
            vals = epilogue(accs, [r[...] for r in ex_refs])
            for o, r, v in zip(outs, out_refs, vals):
                if o[0] == "mn":
                    r[...] = v.astype(o[2])
                else:
                    @pl.when(i == 0)
                    def _():
                        r[...] = v

                    @pl.when(i != 0)
                    def _():
                        r[...] += v

        if nk == 1:
            finish(products())
        else:
            res = products()

            @pl.when(kk == 0)
            def _():
                for q in range(n_acc):
                    acc_refs[q][...] = res[q]

            @pl.when(kk != 0)
            def _():
                for q in range(n_acc):
                    acc_refs[q][...] += res[q]

            @pl.when(kk == nk - 1)
            def _():
                finish([r[...] for r in acc_refs])

    scratch = [pltpu.VMEM((tm, tn), F32) for _ in range(n_acc)] if nk > 1 else []
    return pl.pallas_call(
        body,
        name=name,
        grid=grid,
        in_specs=in_specs,
        out_specs=out_specs,
        out_shape=out_shape,
        scratch_shapes=scratch,
        compiler_params=_params(("arbitrary", "arbitrary", "arbitrary")),
    )(*args)


def _rowwise(name, ins, outs, fn, *, t, tm, deps=()):
    ni = t // tm
    assert ni * tm == t, (name, t, tm)
    in_specs, args = [], []
    for s in ins:
        if s[0] == "row":
            in_specs.append(pl.BlockSpec((tm, s[2]), lambda i, c=s[3]: (i, c)))
        else:
            nd = s[1].ndim
            in_specs.append(pl.BlockSpec(s[1].shape, lambda i, nd=nd: (0,) * nd))
        args.append(s[1])
    out_specs, out_shape = [], []
    for o in outs:
        if o[0] == "row":
            out_specs.append(pl.BlockSpec((tm, o[3]), lambda i, c=o[4]: (i, c)))
        else:
            nd = len(o[1])
            out_specs.append(pl.BlockSpec(o[1], lambda i, nd=nd: (0,) * nd))
        out_shape.append(jax.ShapeDtypeStruct(o[1], o[2]))
    n_read = len(args)
    in_specs += [pl.BlockSpec(memory_space=pl.ANY)] * len(deps)
    args += list(deps)
    n_in = len(args)

    def body(*refs):
        i = pl.program_id(0)
        vals = fn([r[...] for r in refs[:n_read]])
        for o, r, v in zip(outs, refs[n_in:], vals):
            if o[0] == "row":
                r[...] = v.astype(o[2])
            else:
                @pl.when(i == 0)
                def _():
                    r[...] = v

                @pl.when(i != 0)
                def _():
                    r[...] += v

    return pl.pallas_call(
        body,
        name=name,
        grid=(ni,),
        in_specs=in_specs,
        out_specs=out_specs,
        out_shape=out_shape,
        compiler_params=_params(("arbitrary",)),
    )(*args)


def _place():
    return lax.axis_index("x"), lax.axis_index("y"), lax.axis_index("c")


def _remote(src, dst, send_sems, recv_sems, k, to):
    return pltpu.make_async_remote_copy(
        src_ref=src, dst_ref=dst, send_sem=send_sems.at[k], recv_sem=recv_sems.at[k],
        device_id=to, device_id_type=MESH)


def _chips(x, y):
    return [(1 - x, y), (x, 1 - y), (1 - x, 1 - y)]


def _broadcast_plan(n):
    def plan(refs, send_sems, recv_sems):
        x, y, c = _place()
        cps = []
        for i in range(n):
            for dd in range(1, N_DEV):
                peer = (1 - x if dd & 4 else x, 1 - y if dd & 2 else y, 1 - c if dd & 1 else c)
                cps.append(_remote(refs[i], refs[n + i].at[4 * x + 2 * y + c], send_sems, recv_sems, 7 * i + dd - 1, peer))
        return cps

    def local(refs):
        x, y, c = _place()
        return [(refs[i], refs[n + i].at[4 * x + 2 * y + c]) for i in range(n)]

    return plan, local


def _handshake(peers):
    barrier = pltpu.get_barrier_semaphore()
    for peer in peers:
        pl.semaphore_signal(barrier, inc=1, device_id=peer, device_id_type=MESH)
    pl.semaphore_wait(barrier, len(peers))


def _sequencer_call(name, collective_id, body, args, out_type, scratch_types):
    return pl.kernel(
        body,
        out_type=out_type,
        mesh=plsc.ScalarSubcoreMesh(axis_name="sequencer", num_cores=1),
        scratch_types=scratch_types,
        compiler_params=pltpu.CompilerParams(collective_id=collective_id),
        name=name,
    )(*args)


def _sc_gather(name, collective_id, shards):
    n = len(shards)

    def body(*refs):
        ins, outs = refs[:n], refs[n:2 * n]
        send_sems, recv_sems, local_sems = refs[2 * n:]
        x, y, c = _place()
        me, sibling = (x, y, c), (x, y, 1 - c)
        chips = _chips(x, y)
        _handshake([sibling] + [(*chip, c) for chip in chips])

        def slot(i, px, py, pc):
            return outs[i].at[4 * px + 2 * py + pc]

        def copy(i, k, block, to, src=None):
            return _remote(slot(i, *block) if src is None else src, slot(i, *block), send_sems, recv_sems, 7 * i + k, to)

        mine = [pltpu.make_async_copy(ins[i], slot(i, *me), local_sems.at[i]) for i in range(n)]
        for cp in mine:
            cp.start()
        first = []
        for i in range(n):
            first += [copy(i, 1 + j, me, (*chip, c), src=ins[i]) for j, chip in enumerate(chips)]
            first.append(copy(i, 0, me, sibling, src=ins[i]))
        for cp in first:
            cp.start()
        passed = []
        for i in range(n):
            for j, chip in enumerate(chips):
                copy(i, 1 + j, (*chip, c), me).wait_recv()
                fwd = copy(i, 4 + j, (*chip, c), sibling)
                fwd.start()
                passed.append(fwd)
        for i in range(n):
            copy(i, 0, sibling, me).wait_recv()
            for j, chip in enumerate(chips):
                copy(i, 4 + j, (*chip, 1 - c), me).wait_recv()
        for cp in first + passed:
            cp.wait_send()
        for cp in mine:
            cp.wait()

    outs = _sequencer_call(
        name, collective_id, body, shards,
        [jax.ShapeDtypeStruct((N_DEV,) + s.shape, s.dtype) for s in shards],
        [pltpu.SemaphoreType.DMA((7 * n,)), pltpu.SemaphoreType.DMA((7 * n,)), pltpu.SemaphoreType.DMA((n,))])
    return [o.reshape((N_DEV * s.shape[0],) + s.shape[1:]) for o, s in zip(outs, shards)]


def _sc_exchange(name, collective_id, srcs, land_shapes, plan, n_copies, peers, local=None):
    n = len(srcs)

    def body(*refs):
        bufs, send_sems, recv_sems = refs[:2 * n], refs[2 * n], refs[2 * n + 1]
        _handshake(peers())
        if local is not None:
            mine = [pltpu.make_async_copy(a, b, refs[2 * n + 2].at[k]) for k, (a, b) in enumerate(local(bufs))]
            for cp in mine:
                cp.start()
        cps = plan(bufs, send_sems, recv_sems)
        for cp in cps:
            cp.start()
        for cp in cps:
            cp.wait_send()
            cp.wait_recv()
        if local is not None:
            for cp in mine:
                cp.wait()

    scratch = [pltpu.SemaphoreType.DMA((n_copies,)), pltpu.SemaphoreType.DMA((n_copies,))]
    if local is not None:
        scratch.append(pltpu.SemaphoreType.DMA((n,)))
    return list(_sequencer_call(name, collective_id, body, srcs, land_shapes, scratch))


def _adam_vals(w, g, m, v):
    m = ADAM_B1 * m + (1.0 - ADAM_B1) * g
    v = ADAM_B2 * v + (1.0 - ADAM_B2) * (g * g)
    m_hat = m / (1.0 - ADAM_B1 ** ADAM_STEP)
    v_hat = v / (1.0 - ADAM_B2 ** ADAM_STEP)
    delta = -ADAM_LR * (m_hat / (jnp.sqrt(v_hat) + ADAM_EPS) + ADAM_WD * w)
    return delta, m, v


def _sum_parts(st_ref):
    g = st_ref[0].astype(F32)
    for k in range(1, st_ref.shape[0]):
        g = g + st_ref[k].astype(F32)
    return g


def _reduce_adam(name, st, w, m, v, deps=()):
    rows, cols = w.shape
    tr = _tile(rows, UPDATE_ROWS, 16)

    def body(st_ref, w_ref, m_ref, v_ref, *rest):
        g_out, d_out, m_out, v_out = rest[len(deps):]
        g = _sum_parts(st_ref)
        d, mm, vv = _adam_vals(w_ref[...], g, m_ref[...], v_ref[...])
        g_out[...] = g
        d_out[...] = d
        m_out[...] = mm
        v_out[...] = vv

    blk = pl.BlockSpec((tr, cols), lambda i: (i, 0))
    return pl.pallas_call(
        body,
        name=name,
        grid=(rows // tr,),
        in_specs=[pl.BlockSpec((st.shape[0], tr, cols), lambda i: (0, i, 0)), blk, blk, blk]
        + [pl.BlockSpec(memory_space=pl.ANY)] * len(deps),
        out_specs=[blk] * 4,
        out_shape=[jax.ShapeDtypeStruct(w.shape, F32)] * 4,
        compiler_params=_params(("arbitrary",)),
    )(st, w, m, v, *deps)


def _reduce_only(name, st, deps=()):
    _, rows, cols = st.shape
    tr = _tile(rows, UPDATE_ROWS, 16)

    def body(st_ref, *rest):
        rest[-1][...] = _sum_parts(st_ref)

    return pl.pallas_call(
        body,
        name=name,
        grid=(rows // tr,),
        in_specs=[pl.BlockSpec((st.shape[0], tr, cols), lambda i: (0, i, 0))]
        + [pl.BlockSpec(memory_space=pl.ANY)] * len(deps),
        out_specs=pl.BlockSpec((tr, cols), lambda i: (i, 0)),
        out_shape=jax.ShapeDtypeStruct((rows, cols), F32),
        compiler_params=_params(("arbitrary",)),
    )(st, *deps)


def _adam_only(name, w, g, m, v):
    rows, cols = w.shape
    tr = _tile(rows, UPDATE_ROWS, 16)

    def body(w_ref, g_ref, m_ref, v_ref, d_out, m_out, v_out):
        d, mm, vv = _adam_vals(w_ref[...], g_ref[...], m_ref[...], v_ref[...])
        d_out[...] = d
        m_out[...] = mm
        v_out[...] = vv

    blk = pl.BlockSpec((tr, cols), lambda i: (i, 0))
    return pl.pallas_call(
        body,
        name=name,
        grid=(rows // tr,),
        in_specs=[blk] * 4,
        out_specs=[blk] * 3,
        out_shape=[jax.ShapeDtypeStruct(w.shape, F32)] * 3,
        compiler_params=_params(("arbitrary",)),
    )(w, g, m, v)


def _prenorm(name, h, g, t, d, tm, deps=()):
    def fn(v):
        x, gg = v
        return [x * _rms_r(x) * gg]

    return _rowwise(name, [("row", h, d, 0), ("full", g)], [("row", (t, d), BF, d, 0)], fn, t=t, tm=tm, deps=deps)[0]


def _ffn_up(tag, xn, wgt, wut, t, d, f, tm):
    def up_epi(accs, ex):
        gg, uu = accs
        return [gg, uu, gg * _sig(gg) * uu]

    mats = [dict(a=xn, b=wgt, mode="nt", acc=0, tk=d), dict(a=xn, b=wut, mode="nt", acc=1, tk=d)]
    return _matmul(
        tag + "_up", mats, m=t, n=f, tm=tm, tn=_tile(f, 1408),
        outs=[("mn", (t, f), BF, 0)] * 3, epilogue=up_epi, n_acc=2, j_outer=True)


def _ffn_up_split(tag, xn, wgt, wut, t, d, f, tm):
    tf = _tile(f, 1408)
    gate = _matmul(
        tag + "_gate", [dict(a=xn, b=wgt, mode="nt", acc=0, tk=d)], m=t, n=f, tm=tm, tn=tf,
        outs=[("mn", (t, f), BF, 0)], epilogue=lambda accs, ex: accs, n_acc=1, j_outer=True)[0]

    def up_epi(accs, ex):
        gg = ex[0].astype(F32)
        return [accs[0], gg * _sig(gg) * accs[0]]

    up, hid = _matmul(
        tag + "_up", [dict(a=xn, b=wut, mode="nt", acc=0, tk=d)], m=t, n=f, tm=tm, tn=tf, extras=[("mn", gate, 0)],
        outs=[("mn", (t, f), BF, 0)] * 2, epilogue=up_epi, n_acc=1, j_outer=True)
    return gate, up, hid


def _ffn_down(tag, hid, wd, h, g_post, g_next, t, d, f, tm):
    def down_epi(accs, ex):
        ff = accs[0]
        hh, gg, gn = ex
        h_new = hh + 0.5 * ff * _rms_r(ff) * gg
        return [ff, h_new, h_new * _rms_r(h_new) * gn]

    return _matmul(
        tag + "_down", [dict(a=hid, b=wd, mode="nn", acc=0, tk=f)], m=t, n=d, tm=tm, tn=d,
        extras=[("mn", h, 0), ("n", g_post), ("n", g_next)],
        outs=[("mn", (t, d), F32, 0)] * 2 + [("mn", (t, d), BF, 0)], epilogue=down_epi, n_acc=1)


def _prenorm_bwd_epi(accs, ex):
    hh, dh, gg = ex
    dx, dg = _rms_bwd(hh, gg, accs[0])
    return [dh + dx, dg]


def _dx_extras(h_in, dh, g_pre, fo, g_post):
    return [("mn", h_in, 0), ("mn", dh, 0), ("n", g_pre), ("mn", fo, 0), ("n", g_post)]


def _dx_outs(t, d):
    return [("mn", (t, d), F32, 0), ("acc", (1, d), F32), ("mn", (t, d), BF, 0), ("acc", (1, d), F32)]


def _dx_epi(scale):
    def epi(accs, ex):
        hh, dh, gg, fo, gp = ex
        dx, dg = _rms_bwd(hh, gg, accs[0])
        dh_new = dh + dx
        dfo, dgp = _rms_bwd(fo, gp, dh_new * scale)
        return [dh_new, dg, dfo, dgp]

    return epi


def _ffn_dhid(tag, df, wd, gate, up, t, d, f, tm):
    def hid_epi(accs, ex):
        dhid = accs[0]
        gg, uu = ex[0].astype(F32), ex[1].astype(F32)
        s = _sig(gg)
        return [dhid * uu * s * (1.0 + gg * (1.0 - s)), dhid * gg * s]

    return _matmul(
        tag + "_dhid", [dict(a=df, b=wd, mode="nt", acc=0, tk=d)], m=t, n=f, tm=tm, tn=_tile(f, 1408),
        extras=[("mn", gate, 0), ("mn", up, 0)],
        outs=[("mn", (t, f), BF, 0)] * 2, epilogue=hid_epi, n_acc=1, j_outer=True)


def _ffn_dwd(tag, hid, df, t, d, f):
    tk_t = _tile(t, TOKEN_K_TILE)
    return _matmul(
        tag + "_dwd", [dict(a=hid, b=df, mode="tn", acc=0, tk=tk_t)], m=f, n=d, tm=_tile(f, 1408), tn=d,
        nk=t // tk_t, outs=[("mn", (f, d), BF, 0)], epilogue=lambda accs, ex: accs, n_acc=1)


def _ffn_dwgu(tag, dgate, dup, xn, t, d, f, deps=()):
    tk_t = _tile(t, TOKEN_K_TILE)
    return _matmul(
        tag + "_dwgu",
        [dict(a=dgate, b=xn, mode="tn", acc=0, tk=tk_t), dict(a=dup, b=xn, mode="tn", acc=1, tk=tk_t)],
        m=f, n=d, tm=_tile(f, 1408), tn=d, nk=t // tk_t,
        outs=[("mn", (f, d), BF, 0)] * 2, epilogue=lambda accs, ex: accs, n_acc=2, deps=deps)


def _ffn_dx(tag, dh, h_in, dgate, dup, wgt, wut, g_pre, t, d, f, tm, deps=(), before=None):
    tf = _tile(f, 1408)
    mats = [dict(a=dgate, b=wgt, mode="nn", acc=0, tk=tf), dict(a=dup, b=wut, mode="nn", acc=0, tk=tf)]
    if before is None:
        return _matmul(
            tag + "_dx", mats, m=t, n=d, tm=tm, tn=d, nk=f // tf,
            extras=[("mn", h_in, 0), ("mn", dh, 0), ("n", g_pre)],
            outs=[("mn", (t, d), F32, 0), ("acc", (1, d), F32)], epilogue=_prenorm_bwd_epi, n_acc=1, deps=deps)
    fo, g_post, scale = before
    return _matmul(
        tag + "_dx", mats, m=t, n=d, tm=tm, tn=d, nk=f // tf, extras=_dx_extras(h_in, dh, g_pre, fo, g_post),
        outs=_dx_outs(t, d), epilogue=_dx_epi(scale), n_acc=1, deps=deps)


def _causal_mask():
    r = lax.broadcasted_iota(jnp.int32, (CHUNK, CHUNK), 0)
    c = lax.broadcasted_iota(jnp.int32, (CHUNK, CHUNK), 1)
    return r >= c


def _layernorm_parts(v):
    mu = jnp.mean(v, axis=-1, keepdims=True)
    vc = v - mu
    rstd = lax.rsqrt(jnp.mean(vc * vc, axis=-1, keepdims=True) + EPS)
    return vc * rstd, rstd


def _sgu_fwd(z, g_sgu, w_s, b_col, t, d, tm):
    dg = d // N_SGU_GROUPS
    n_chunk = tm // CHUNK

    def body(zu_ref, zv_ref, g_ref, w_ref, b_ref, a_ref):
        u = _gelu(zu_ref[...])
        vhat, _ = _layernorm_parts(_gelu(zv_ref[...]))
        vn = (vhat * g_ref[...]).astype(BF)
        mask = _causal_mask()
        for gi in range(N_SGU_GROUPS):
            ws = jnp.where(mask, w_ref[gi], 0.0).astype(BF)
            bias = b_ref[gi]
            for ci in range(n_chunk):
                rows, cols = slice(ci * CHUNK, (ci + 1) * CHUNK), slice(gi * dg, (gi + 1) * dg)
                sv = jnp.dot(ws, vn[rows, cols], preferred_element_type=F32) + bias
                a_ref[rows, cols] = (u[rows, cols] * sv).astype(BF)

    return pl.pallas_call(
        body,
        name="sgu_fwd",
        grid=(t // tm,),
        in_specs=[
            pl.BlockSpec((tm, d), lambda i: (i, 0)),
            pl.BlockSpec((tm, d), lambda i: (i, 1)),
            pl.BlockSpec((1, d), lambda i: (0, 0)),
            pl.BlockSpec(w_s.shape, lambda i: (0, 0, 0)),
            pl.BlockSpec(b_col.shape, lambda i: (0, 0, 0)),
        ],
        out_specs=pl.BlockSpec((tm, d), lambda i: (i, 0)),
        out_shape=jax.ShapeDtypeStruct((t, d), BF),
        compiler_params=_params(("arbitrary",)),
    )(z, z, g_sgu, w_s, b_col)


def _sgu_bwd(z, da, dz, g_sgu, w_s, b_col, t, d, tm):
    dg = d // N_SGU_GROUPS
    n_chunk = tm // CHUNK

    def body(zu_ref, zv_ref, da_ref, dz_in, g_ref, w_ref, b_ref, dz_ref, dw_ref, db_ref, dgn_ref, dvn_ref):
        del dz_in
        dzu_ref, dzv_ref = dz_ref.at[:, pl.ds(0, d)], dz_ref.at[:, pl.ds(d, d)]
        i = pl.program_id(0)
        zu, zv = zu_ref[...], zv_ref[...]
        u = _gelu(zu)
        vhat, rstd = _layernorm_parts(_gelu(zv))
        gn = g_ref[...]
        vn = (vhat * gn).astype(BF)
        da_v = da_ref[...].astype(F32)
        dsv_all = (da_v * u).astype(BF)
        mask = _causal_mask()
        for gi in range(N_SGU_GROUPS):
            ws = jnp.where(mask, w_ref[gi], 0.0).astype(BF)
            bias = b_ref[gi]
            dw = jnp.zeros((CHUNK, CHUNK), F32)
            dbias = jnp.zeros((CHUNK, 1), F32)
            for ci in range(n_chunk):
                rows, cols = slice(ci * CHUNK, (ci + 1) * CHUNK), slice(gi * dg, (gi + 1) * dg)
                vn_c, dsv = vn[rows, cols], dsv_all[rows, cols]
                sv = jnp.dot(ws, vn_c, preferred_element_type=F32) + bias
                dzu_ref[rows, cols] = (da_v[rows, cols] * sv * _gelu_grad(zu[rows, cols])).astype(BF)
                dw = dw + lax.dot_general(dsv, vn_c, _DN["nt"], preferred_element_type=F32)
                dbias = dbias + jnp.sum(dsv.astype(F32), axis=1, keepdims=True)
                dvn_ref[rows, cols] = lax.dot_general(ws, dsv, _DN["tn"], preferred_element_type=F32)
            dw = jnp.where(mask, dw, 0.0)

            @pl.when(i == 0)
            def _():
                dw_ref[gi] = dw
                db_ref[gi] = dbias

            @pl.when(i != 0)
            def _():
                dw_ref[gi] += dw
                db_ref[gi] += dbias

        dvn = dvn_ref[...]
        dgn = jnp.sum(dvn * vhat, axis=0, keepdims=True)

        @pl.when(i == 0)
        def _():
            dgn_ref[...] = dgn

        @pl.when(i != 0)
        def _():
            dgn_ref[...] += dgn

        dvh = dvn * gn
        dv = rstd * (dvh - jnp.mean(dvh, axis=-1, keepdims=True) - vhat * jnp.mean(dvh * vhat, axis=-1, keepdims=True))
        dzv_ref[...] = (dv * _gelu_grad(zv)).astype(BF)

    return pl.pallas_call(
        body,
        name="sgu_bwd",
        grid=(t // tm,),
        in_specs=[
            pl.BlockSpec((tm, d), lambda i: (i, 0)),
            pl.BlockSpec((tm, d), lambda i: (i, 1)),
            pl.BlockSpec((tm, d), lambda i: (i, 0)),
            pl.BlockSpec(memory_space=pl.ANY),
            pl.BlockSpec((1, d), lambda i: (0, 0)),
            pl.BlockSpec(w_s.shape, lambda i: (0, 0, 0)),
            pl.BlockSpec(b_col.shape, lambda i: (0, 0, 0)),
        ],
        out_specs=[
            pl.BlockSpec((tm, 2 * d), lambda i: (i, 0)),
            pl.BlockSpec(w_s.shape, lambda i: (0, 0, 0)),
            pl.BlockSpec(b_col.shape, lambda i: (0, 0, 0)),
            pl.BlockSpec((1, d), lambda i: (0, 0)),
        ],
        out_shape=[
            jax.ShapeDtypeStruct(dz.shape, BF),
            jax.ShapeDtypeStruct(w_s.shape, F32),
            jax.ShapeDtypeStruct(b_col.shape, F32),
            jax.ShapeDtypeStruct((1, d), F32),
        ],
        scratch_shapes=[pltpu.VMEM((tm, d), F32)],
        input_output_aliases={3: 0},
        compiler_params=_params(("arbitrary",)),
    )(z, z, da, dz, g_sgu, w_s, b_col)


def _shift_down(x, k, row):
    return jnp.where(row >= k, pltpu.roll(x, k, 0), 0.0)


def _shift_up(x, k, row, t):
    return jnp.where(row < t - k, pltpu.roll(x, t - k, 0), 0.0)


def _doublings(window):
    steps = int(math.log2(window))
    assert 2 ** steps == window
    return [2 ** s for s in range(steps)]


def _pool_diff(c, window, row):
    s = c
    for k in _doublings(window):
        s = s + _shift_down(s, k, row)
    count = jnp.minimum(row + 1, window).astype(F32)
    return s / count - c, count


def _pool_fwd(z, pool_w, pool_scale, t, d):
    dgp = d // len(POOL_WINDOWS)
    cblk = (2 * d) // dgp

    def body(zc_ref, w_ref, s_ref, b_ref):
        row = lax.broadcasted_iota(jnp.int32, (t, 1), 0)
        for gi, window in enumerate(POOL_WINDOWS):
            @pl.when(pl.program_id(0) == gi)
            def _(window=window):
                diff, _ = _pool_diff(zc_ref[...], window, row)
                out = jnp.dot(diff.astype(BF), w_ref[...], preferred_element_type=F32)
                b_ref[...] = (out * s_ref[...]).astype(BF)

    return pl.pallas_call(
        body,
        name="pool_fwd",
        grid=(len(POOL_WINDOWS),),
        in_specs=[
            pl.BlockSpec((t, dgp), lambda g: (0, cblk + g)),
            pl.BlockSpec((None, dgp, dgp), lambda g: (g, 0, 0)),
            pl.BlockSpec((1, dgp), lambda g: (0, g)),
        ],
        out_specs=pl.BlockSpec((t, dgp), lambda g: (0, g)),
        out_shape=jax.ShapeDtypeStruct((t, d), BF),
        compiler_params=_params(("arbitrary",)),
    )(z, pool_w, pool_scale)


def _pool_bwd(z, db, dz, pool_w, pool_scale, t, d):
    dgp = d // len(POOL_WINDOWS)
    cblk = (2 * d) // dgp

    def body(zc_ref, db_ref, dz_in, w_ref, s_ref, dzc_ref, dw_ref, ds_ref):
        del dz_in
        row = lax.broadcasted_iota(jnp.int32, (t, 1), 0)
        for gi, window in enumerate(POOL_WINDOWS):
            @pl.when(pl.program_id(0) == gi)
            def _(window=window):
                diff, count = _pool_diff(zc_ref[...], window, row)
                diff = diff.astype(BF)
                w = w_ref[...]
                dbv = db_ref[...].astype(F32)
                out = jnp.dot(diff, w, preferred_element_type=F32)
                ds_ref[...] = jnp.sum(dbv * out, axis=0, keepdims=True)
                dout = (dbv * s_ref[...]).astype(BF)
                dw_ref[...] = lax.dot_general(diff, dout, _DN["tn"], preferred_element_type=F32).astype(BF)
                ddiff = lax.dot_general(dout, w, _DN["nt"], preferred_element_type=F32)
                s = ddiff / count
                for k in _doublings(window):
                    s = s + _shift_up(s, k, row, t)
                dzc_ref[...] = (s - ddiff).astype(BF)

    return pl.pallas_call(
        body,
        name="pool_bwd",
        grid=(len(POOL_WINDOWS),),
        in_specs=[
            pl.BlockSpec((t, dgp), lambda g: (0, cblk + g)),
            pl.BlockSpec((t, dgp), lambda g: (0, g)),
            pl.BlockSpec(memory_space=pl.ANY),
            pl.BlockSpec((None, dgp, dgp), lambda g: (g, 0, 0)),
            pl.BlockSpec((1, dgp), lambda g: (0, g)),
        ],
        out_specs=[
            pl.BlockSpec((t, dgp), lambda g: (0, cblk + g)),
            pl.BlockSpec((None, dgp, dgp), lambda g: (g, 0, 0)),
            pl.BlockSpec((1, dgp), lambda g: (0, g)),
        ],
        out_shape=[
            jax.ShapeDtypeStruct(dz.shape, BF),
            jax.ShapeDtypeStruct(pool_w.shape, BF),
            jax.ShapeDtypeStruct((1, d), F32),
        ],
        input_output_aliases={2: 0},
        compiler_params=_params(("arbitrary",)),
    )(z, db, dz, pool_w, pool_scale)


def _mix_dy(dm, w_o, z, ya, yb, t, d, tm):
    def body(dm_ref, w_ref, gate_ref, ya_ref, yb_ref, dyab_ref, dz_ref, dy_ref):
        j = pl.program_id(1)

        @pl.when(j == 0)
        def _():
            dy_ref[...] = lax.dot_general(dm_ref[...], w_ref[...], _DN["nt"], preferred_element_type=F32)

        dy = dy_ref[...]
        s = _sig(gate_ref[...])
        yv = jnp.where(j == 0, ya_ref[...], yb_ref[...]).astype(F32)
        dyab_ref[...] = (dy * s).astype(BF)
        dz_ref[...] = (dy * yv * s * (1.0 - s)).astype(BF)

    row = pl.BlockSpec((tm, d), lambda i, j: (i, 0))
    return pl.pallas_call(
        body,
        name="mix_dy",
        grid=(t // tm, 2),
        in_specs=[row, pl.BlockSpec((d, d), lambda i, j: (0, 0)), pl.BlockSpec((tm, d), lambda i, j: (i, 3 + j)), row, row],
        out_specs=[pl.BlockSpec((tm, d), lambda i, j: (i, j)), pl.BlockSpec((tm, d), lambda i, j: (i, 3 + j))],
        out_shape=[jax.ShapeDtypeStruct((t, 2 * d), BF), jax.ShapeDtypeStruct((t, 5 * d), BF)],
        scratch_shapes=[pltpu.VMEM((tm, d), F32)],
        compiler_params=_params(("arbitrary", "arbitrary")),
    )(dm, w_o, z, ya, yb)


def _everyone():
    x, y, c = _place()
    return [(1 - x if dd & 4 else x, 1 - y if dd & 2 else y, 1 - c if dd & 1 else c) for dd in range(1, N_DEV)]


def _direct_plan(n):
    def plan(refs, send_sems, recv_sems):
        x, y, c = _place()
        me = 4 * x + 2 * y + c
        cps = []
        for i in range(n):
            for k, (px, py, pc) in enumerate(_everyone()):
                cps.append(_remote(refs[i].at[4 * px + 2 * py + pc], refs[n + i].at[me], send_sems, recv_sems,
                                   7 * i + k, (px, py, pc)))
        return cps

    def local(refs):
        x, y, c = _place()
        me = 4 * x + 2 * y + c
        return [(refs[i].at[me], refs[n + i].at[me]) for i in range(n)]

    return plan, local


def _sc_scatter_direct(name, collective_id, parts):
    n = len(parts)
    p8 = [a.reshape(N_DEV, a.shape[0] // N_DEV, a.shape[1]) for a in parts]
    plan, local = _direct_plan(n)
    return _sc_exchange(name, collective_id, p8, [jax.ShapeDtypeStruct(a.shape, a.dtype) for a in p8], plan,
                        7 * n, _everyone, local=local)


def _stack_rows(arrays):
    parts, starts, row = [], [], 0
    for a in arrays:
        pad = -a.shape[0] % 8
        starts.append(row)
        parts += [a, jnp.zeros((pad, a.shape[1]), a.dtype)] if pad else [a]
        row += a.shape[0] + pad
    return jnp.concatenate(parts, axis=0), starts


def _adam_replicated(name, stacks, layout, weights, deps=()):
    n_st, n_par = len(stacks), len(weights)

    def body(*refs):
        st_refs = refs[:n_st]
        w_refs = refs[n_st:n_st + 3 * n_par]
        out_refs = refs[n_st + 3 * n_par + len(deps):]

        def summed(which, row, rows):
            g = st_refs[which][0, pl.ds(row, rows), :]
            for k in range(1, N_DEV):
                g = g + st_refs[which][k, pl.ds(row, rows), :]
            return g

        for i in range(n_par):
            w_ref, m_ref, v_ref = w_refs[3 * i:3 * i + 3]
            g = summed(layout[i][0], layout[i][1], w_ref.shape[0])
            dlt, mm, vv = _adam_vals(w_ref[...], g, m_ref[...], v_ref[...])
            for r, val in zip(out_refs[4 * i:4 * i + 4], (g, dlt, mm, vv)):
                r[...] = val
        out_refs[4 * n_par][...] = jnp.sum(summed(layout[n_par][0], layout[n_par][1], 1), axis=1, keepdims=True)

    flat_w = [a for wmv in weights for a in wmv]
    out_shape = [jax.ShapeDtypeStruct(wmv[0].shape, F32) for wmv in weights for _ in range(4)]
    out_shape.append(jax.ShapeDtypeStruct((1, 1), F32))
    vmem = pl.BlockSpec(memory_space=pltpu.VMEM)
    res = pl.pallas_call(
        body,
        name=name,
        in_specs=[vmem] * (n_st + len(flat_w)) + [pl.BlockSpec(memory_space=pl.ANY)] * len(deps),
        out_specs=[vmem] * len(out_shape),
        out_shape=out_shape,
        compiler_params=_params(),
    )(*stacks, *flat_w, *deps)
    return [res[4 * i:4 * i + 4] for i in range(n_par)], res[4 * n_par]


def kernel(x, p, ffn1_pre_g, ffn1_w_gate, ffn1_w_up, ffn1_w_down, ffn1_post_g, mix_pre_g, w_in, sgu_norm_g, sgu_w, sgu_b, pool_w, pool_scale, w_out_a, w_out_b, w_o, mix_post_g, ffn2_pre_g, ffn2_w_gate, ffn2_w_up, ffn2_w_down, ffn2_post_g, ple_pre_g, ple_w_gate, ple_w_proj, ple_post_g, loss_target, m_ffn1_pre_g, m_ffn1_w_gate, m_ffn1_w_up, m_ffn1_w_down, m_ffn1_post_g, m_mix_pre_g, m_w_in, m_sgu_norm_g, m_sgu_w, m_sgu_b, m_pool_w, m_pool_scale, m_w_out_a, m_w_out_b, m_w_o, m_mix_post_g, m_ffn2_pre_g, m_ffn2_w_gate, m_ffn2_w_up, m_ffn2_w_down, m_ffn2_post_g, m_ple_pre_g, m_ple_w_gate, m_ple_w_proj, m_ple_post_g, v_ffn1_pre_g, v_ffn1_w_gate, v_ffn1_w_up, v_ffn1_w_down, v_ffn1_post_g, v_mix_pre_g, v_w_in, v_sgu_norm_g, v_sgu_w, v_sgu_b, v_pool_w, v_pool_scale, v_w_out_a, v_w_out_b, v_w_o, v_mix_post_g, v_ffn2_pre_g, v_ffn2_w_gate, v_ffn2_w_up, v_ffn2_w_down, v_ffn2_post_g, v_ple_pre_g, v_ple_w_gate, v_ple_w_proj, v_ple_post_g):
    args = dict(locals())
    names = ["ffn1_pre_g", "ffn1_w_gate", "ffn1_w_up", "ffn1_w_down", "ffn1_post_g", "mix_pre_g", "w_in",
             "sgu_norm_g", "sgu_w", "sgu_b", "pool_w", "pool_scale", "w_out_a", "w_out_b", "w_o", "mix_post_g",
             "ffn2_pre_g", "ffn2_w_gate", "ffn2_w_up", "ffn2_w_down", "ffn2_post_g", "ple_pre_g", "ple_w_gate",
             "ple_w_proj", "ple_post_g"]
    w = {k: args[k][0] for k in names}
    mom = {k: args["m_" + k][0] for k in names}
    var = {k: args["v_" + k][0] for k in names}

    assert x.shape[0] == 1 and p.shape[:2] == (1, 1)
    t, d = x.shape[1], x.shape[2]
    f = ffn1_w_gate.shape[2] * N_DEV
    d_in = w_in.shape[2] * N_DEV
    d_ple = p.shape[3]
    n_pool = len(POOL_WINDOWS)
    dgp = d // n_pool
    assert d_in == 5 * d and t % CHUNK == 0
    tm = _tile(t, 512, CHUNK)
    xs, ps, target = x[0], p[0, 0], loss_target[0]

    col_sharded = ("ffn1_w_gate", "ffn1_w_up", "ffn2_w_gate", "ffn2_w_up", "w_in", "ple_w_proj")

    def shard_of(k):
        if k in col_sharded:
            return w[k].T.astype(BF)
        if k == "pool_w":
            return w[k].reshape(-1, dgp).astype(BF)
        return w[k].astype(BF)

    rows_pw = dgp // N_DEV
    b_col = w["sgu_b"][:, :, None]
    gains = {k: w[k][None, :] for k in names if w[k].ndim == 1}
    full = {}

    groups = [["ffn1_w_gate"], ["ffn1_w_up"], ["ffn1_w_down"], ["w_in"], ["pool_w", "w_out_a", "w_out_b", "w_o"],
              ["ffn2_w_gate", "ffn2_w_up", "ffn2_w_down", "ple_w_gate", "ple_w_proj"]]
    def gather(gi, behind=None):
        shards = [shard_of(k) for k in groups[gi]]
        if behind is not None:
            shards = lax.optimization_barrier((shards, behind))[0]
        full.update(zip(groups[gi], _sc_gather("gather%d" % gi, gi, shards)))

    gather(0)
    xn1 = _prenorm("ffn1_prenorm", xs, gains["ffn1_pre_g"], t, d, tm)
    gather(1, xn1)
    gather(2, xn1)
    gather(3, xn1)
    g1, u1, hid1 = _ffn_up_split("ffn1", xn1, full["ffn1_w_gate"], full["ffn1_w_up"], t, d, f, tm)
    gather(4, hid1)
    f1, h1, xn2 = _ffn_down("ffn1", hid1, full["ffn1_w_down"], xs, gains["ffn1_post_g"], gains["mix_pre_g"], t, d, f, tm)
    gather(5, h1)

    pool_full = full["pool_w"].reshape(N_DEV, n_pool, rows_pw, dgp).transpose(1, 0, 2, 3).reshape(n_pool, dgp, dgp)
    z = _matmul("mix_in", [dict(a=xn2, b=full["w_in"], mode="nt", acc=0, tk=d)], m=t, n=d_in, tm=_tile(t, 1024, CHUNK), tn=d,
                outs=[("mn", (t, d_in), F32, 0)], epilogue=lambda accs, ex: accs, n_acc=1, j_outer=True)[0]
    a_br = _sgu_fwd(z, gains["sgu_norm_g"], w["sgu_w"], b_col, t, d, tm)
    b_br = _pool_fwd(z, pool_full, gains["pool_scale"], t, d)

    def merge_epi(accs, ex):
        ya, yb = accs
        ga, gb = ex
        return [ya, yb, _sig(ga) * ya + _sig(gb) * yb]

    ya, yb, y = _matmul(
        "mix_merge",
        [dict(a=a_br, b=full["w_out_a"], mode="nn", acc=0, tk=d), dict(a=b_br, b=full["w_out_b"], mode="nn", acc=1, tk=d)],
        m=t, n=d, tm=tm, tn=d, extras=[("mn", z, 3), ("mn", z, 4)],
        outs=[("mn", (t, d), BF, 0)] * 3, epilogue=merge_epi, n_acc=2)

    def proj_epi(accs, ex):
        mm = accs[0]
        hh, gg, gn = ex
        h_new = hh + mm * _rms_r(mm) * gg
        return [mm, h_new, h_new * _rms_r(h_new) * gn]

    m_out, h2, xn3 = _matmul(
        "mix_proj", [dict(a=y, b=full["w_o"], mode="nn", acc=0, tk=d)], m=t, n=d, tm=tm, tn=d,
        extras=[("mn", h1, 0), ("n", gains["mix_post_g"]), ("n", gains["ffn2_pre_g"])],
        outs=[("mn", (t, d), F32, 0)] * 2 + [("mn", (t, d), BF, 0)], epilogue=proj_epi, n_acc=1)

    g2, u2, hid2 = _ffn_up("ffn2", xn3, full["ffn2_w_gate"], full["ffn2_w_up"], t, d, f, tm)
    f2, h3, xn4 = _ffn_down("ffn2", hid2, full["ffn2_w_down"], h2, gains["ffn2_post_g"], gains["ple_pre_g"], t, d, f, tm)

    def ple_epi(accs, ex):
        gl, e = accs
        hh, tgt, gg = ex
        q = _sig(gl) * e
        err = hh + q * _rms_r(q) * gg - tgt
        return [gl, e, err * (1.0 / d), jnp.sum(err * err, axis=0, keepdims=True)]

    gl, e_ple, dh4, loss_vec = _matmul(
        "ple_fwd",
        [dict(a=xn4, b=full["ple_w_gate"], mode="nn", acc=0, tk=d), dict(a=ps, b=full["ple_w_proj"], mode="nt", acc=1, tk=d_ple)],
        m=t, n=d, tm=tm, tn=d, extras=[("mn", h3, 0), ("mn", target, 0), ("n", gains["ple_post_g"])],
        outs=[("mn", (t, d), F32, 0)] * 3 + [("acc", (1, d), F32)], epilogue=ple_epi, n_acc=2)

    tk_t = _tile(t, TOKEN_K_TILE)

    def ple_post_fn(v):
        dy, gl_v, e_v, gg = v
        s = _sig(gl_v)
        dq, dg = _rms_bwd(s * e_v, gg, dy)
        return [dq * e_v * s * (1.0 - s), dq * s, dg]

    dgl, de, dg_ple_post = _rowwise(
        "ple_post_bwd", [("row", dh4, d, 0), ("row", gl, d, 0), ("row", e_ple, d, 0), ("full", gains["ple_post_g"])],
        [("row", (t, d), BF, d, 0), ("row", (t, d), BF, d, 0), ("acc", (1, d), F32)], ple_post_fn, t=t, tm=tm)
    ident = lambda accs, ex: accs
    dw_ple_gate = _matmul("ple_dwg", [dict(a=xn4, b=dgl, mode="tn", acc=0, tk=tk_t)], m=d, n=d, tm=d, tn=d,
                          nk=t // tk_t, outs=[("mn", (d, d), BF, 0)], epilogue=ident, n_acc=1)[0]
    dw_ple_proj_t = _matmul("ple_dwp", [dict(a=de, b=ps, mode="tn", acc=0, tk=tk_t)], m=d, n=d_ple, tm=d, tn=d_ple,
                            nk=t // tk_t, outs=[("mn", (d, d_ple), BF, 0)], epilogue=ident, n_acc=1)[0]
    dh3, dg_ple_pre, df2, dg_f2_post = _matmul(
        "ple_dx", [dict(a=dgl, b=full["ple_w_gate"], mode="nt", acc=0, tk=d)], m=t, n=d, tm=tm, tn=d,
        extras=_dx_extras(h3, dh4, gains["ple_pre_g"], f2, gains["ffn2_post_g"]),
        outs=_dx_outs(t, d), epilogue=_dx_epi(0.5), n_acc=1)

    staged = []
    n_gather = len(groups)

    def scatter(keys, parts):
        k = len(staged)
        staged.append((keys, _sc_scatter_direct("scatter%d" % k, n_gather + k, parts)))

    dgate2, dup2 = _ffn_dhid("ffn2", df2, full["ffn2_w_down"], g2, u2, t, d, f, tm)
    dwd2 = _ffn_dwd("ffn2", hid2, df2, t, d, f)[0]
    dwg2, dwu2 = _ffn_dwgu("ffn2", dgate2, dup2, xn3, t, d, f)
    scatter(["ple_w_gate", "ple_w_proj", "ffn2_w_down", "ffn2_w_gate", "ffn2_w_up"],
            [dw_ple_gate, dw_ple_proj_t, dwd2, dwg2, dwu2])
    dh2, dg_f2_pre, dm, dg_mix_post = _ffn_dx(
        "ffn2", dh3, h2, dgate2, dup2, full["ffn2_w_gate"], full["ffn2_w_up"], gains["ffn2_pre_g"], t, d, f, tm,
        deps=[dw_ple_gate, dw_ple_proj_t, dwd2, dwg2], before=(m_out, gains["mix_post_g"], 1.0))

    dyab, dz = _mix_dy(dm, full["w_o"], z, ya, yb, t, d, tm)
    dw_o = _matmul("mix_dwo", [dict(a=y, b=dm, mode="tn", acc=0, tk=tk_t)], m=d, n=d, tm=d, tn=d, nk=t // tk_t,
                   outs=[("mn", (d, d), BF, 0)], epilogue=ident, n_acc=1)[0]
    dw_out_a, dw_out_b = _matmul(
        "mix_dwab",
        [dict(a=a_br, b=dyab, mode="tn", acc=0, tk=tk_t),
         dict(a=b_br, b=dyab, mode="tn", acc=1, tk=tk_t, b_off=(0, 1))],
        m=d, n=d, tm=d, tn=d, nk=t // tk_t, outs=[("mn", (d, d), BF, 0)] * 2, epilogue=ident, n_acc=2)
    da, db = _matmul(
        "mix_dab",
        [dict(a=dyab, b=full["w_out_a"], mode="nt", acc=0, tk=d),
         dict(a=dyab, b=full["w_out_b"], mode="nt", acc=1, tk=d, a_off=(0, 1))],
        m=t, n=d, tm=tm, tn=d, outs=[("mn", (t, d), BF, 0)] * 2, epilogue=ident, n_acc=2)
    dz, dsgu_w, dsgu_b, dg_sgu = _sgu_bwd(z, da, dz, gains["sgu_norm_g"], w["sgu_w"], b_col, t, d, _tile(t, 256, CHUNK))
    dz, dpool_w, dg_pool = _pool_bwd(z, db, dz, pool_full, gains["pool_scale"], t, d)
    dw_in_t = _matmul(
        "mix_dwin", [dict(a=dz, b=xn2, mode="tn", acc=0, tk=tk_t)], m=d_in, n=d, tm=d, tn=d,
        nk=t // tk_t, outs=[("mn", (d_in, d), BF, 0)], epilogue=ident, n_acc=1)[0]
    dpool_rows = dpool_w.reshape(n_pool, N_DEV, rows_pw, dgp).transpose(1, 0, 2, 3).reshape(N_DEV * n_pool * rows_pw, dgp)
    scatter(["w_o", "w_out_a", "w_out_b", "pool_w", "w_in"], [dw_o, dw_out_a, dw_out_b, dpool_rows, dw_in_t])
    dh1, dg_mix_pre, df1, dg_f1_post = _matmul(
        "mix_dx", [dict(a=dz, b=full["w_in"], mode="nn", acc=0, tk=d_in // 2)], m=t, n=d, tm=tm, tn=d, nk=2,
        extras=_dx_extras(h1, dh2, gains["mix_pre_g"], f1, gains["ffn1_post_g"]),
        outs=_dx_outs(t, d), epilogue=_dx_epi(0.5), n_acc=1, deps=[dw_o, dw_out_a, dpool_rows, dw_in_t])

    dgate1, dup1 = _ffn_dhid("ffn1", df1, full["ffn1_w_down"], g1, u1, t, d, f, tm)
    dwd1 = _ffn_dwd("ffn1", hid1, df1, t, d, f)[0]
    scatter(["ffn1_w_down"], [dwd1])
    dwg1, dwu1 = _ffn_dwgu("ffn1", dgate1, dup1, xn1, t, d, f, deps=[dwd1])
    scatter(["ffn1_w_gate", "ffn1_w_up"], [dwg1, dwu1])
    grad_x, dg_f1_pre = _ffn_dx("ffn1", dh1, xs, dgate1, dup1, full["ffn1_w_gate"], full["ffn1_w_up"],
                                gains["ffn1_pre_g"], t, d, f, tm, deps=[dwg1])

    grad, delta, new_m, new_v = {}, {}, {}, {}

    def update(group, partial_sums, behind):
        done = []
        for k, st in zip(group, partial_sums):
            if k in col_sharded and w[k].shape[1] % LANES:
                res = _reduce_adam("adam_" + k, st, w[k].T, mom[k].T, var[k].T, deps=behind)
                grad[k], delta[k], new_m[k], new_v[k] = [r.T for r in res]
            elif k in col_sharded:
                grad[k] = _reduce_only("sum_" + k, st, deps=behind).T
                delta[k], new_m[k], new_v[k] = _adam_only("adam_" + k, w[k], grad[k], mom[k], var[k])
            else:
                shape = w[k].shape
                flat = (-1, shape[-1])
                res = _reduce_adam("adam_" + k, st, w[k].reshape(flat), mom[k].reshape(flat), var[k].reshape(flat),
                                   deps=behind)
                grad[k], delta[k], new_m[k], new_v[k] = [r.reshape(shape) for r in res]
            done.append(delta[k])
        return done

    gain_grads = {
        "ffn1_pre_g": dg_f1_pre, "ffn1_post_g": dg_f1_post, "mix_pre_g": dg_mix_pre, "sgu_norm_g": dg_sgu,
        "pool_scale": dg_pool, "mix_post_g": dg_mix_post, "ffn2_pre_g": dg_f2_pre, "ffn2_post_g": dg_f2_post,
        "ple_pre_g": dg_ple_pre, "ple_post_g": dg_ple_post,
    }
    gain_names = list(gain_grads)
    n_mix = N_SGU_GROUPS * CHUNK
    wide, wide_rows = _stack_rows([gain_grads[k] for k in gain_names] + [loss_vec * (0.5 / d)])
    narrow, narrow_rows = _stack_rows([dsgu_b.reshape(N_SGU_GROUPS, CHUNK), dsgu_w.reshape(n_mix, CHUNK)])
    plan_s, local_s = _broadcast_plan(2)
    stacks = _sc_exchange(
        "small", n_gather + len(staged), [wide, narrow],
        [jax.ShapeDtypeStruct((N_DEV,) + wide.shape, F32), jax.ShapeDtypeStruct((N_DEV,) + narrow.shape, F32)],
        plan_s, 2 * (N_DEV - 1), _everyone, local=local_s)
    done = [grad_x]
    for keys, partial_sums in staged:
        done = update(keys, partial_sums, done)
    rep_names = gain_names + ["sgu_b", "sgu_w"]
    rep_shape = {k: (1, d) for k in gain_names}
    rep_shape.update(sgu_b=(N_SGU_GROUPS, CHUNK), sgu_w=(n_mix, CHUNK))
    layout = [(0, r) for r in wide_rows[:-1]] + [(1, narrow_rows[0]), (1, narrow_rows[1]), (0, wide_rows[-1])]
    res, loss = _adam_replicated(
        "adam_replicated", stacks, layout,
        [tuple(src[k].reshape(rep_shape[k]) for src in (w, mom, var)) for k in rep_names], deps=done)
    for k, quad in zip(rep_names, res):
        for dst, val in zip((grad, delta, new_m, new_v), quad):
            dst[k] = val.reshape(w[k].shape)

    out = [loss[0, 0], grad_x[None]]
    for group in (grad, delta, new_m, new_v):
        out += [group[k][None] for k in names]
    return tuple(out)
```

```python
import math

import jax
import jax.numpy as jnp
from jax import lax
from jax.experimental import pallas as pl
from jax.experimental.pallas import tpu as pltpu
from jax.experimental.pallas import tpu_sc as plsc

EPS = 1e-6
CHUNK = 128
N_SGU_GROUPS = 4
POOL_WINDOWS = (2, 4, 8, 16)
ADAM_LR = 0.001
ADAM_B1 = 0.9
ADAM_B2 = 0.999
ADAM_EPS = 1e-08
ADAM_WD = 0.01
ADAM_STEP = 10

N_DEV = 8
MESH = pl.DeviceIdType.MESH
V7X_VMEM_BYTES = 64 * 1024 * 1024
VMEM_LIMIT = V7X_VMEM_BYTES - 8 * 1024 * 1024
TOKEN_K_TILE = 1024
LANES = 128
UPDATE_ROWS = 256
BF = jnp.bfloat16
F32 = jnp.float32

_DN = {
    "nn": (((1,), (0,)), ((), ())),
    "nt": (((1,), (1,)), ((), ())),
    "tn": (((0,), (0,)), ((), ())),
}


def _params(sem=None):
    return pltpu.CompilerParams(dimension_semantics=sem, vmem_limit_bytes=VMEM_LIMIT)


def _tile(n, target, align=128):
    t = min(n, target)
    t -= t % align
    while t >= align:
        if n % t == 0:
            return t
        t -= align
    return n


def _sig(x):
    return 0.5 + 0.5 * jnp.tanh(0.5 * x)


_GELU_K = math.sqrt(2.0 / math.pi)
_GELU_C = 0.044715


def _gelu(x):
    return 0.5 * x * (1.0 + jnp.tanh(_GELU_K * (x + _GELU_C * x * x * x)))


def _gelu_grad(x):
    t = jnp.tanh(_GELU_K * (x + _GELU_C * x * x * x))
    return 0.5 * (1.0 + t) + 0.5 * x * (1.0 - t * t) * _GELU_K * (1.0 + 3.0 * _GELU_C * x * x)


def _rms_r(x):
    return lax.rsqrt(jnp.mean(x * x, axis=-1, keepdims=True) + EPS)


def _rms_bwd(x, g, dy):
    r = _rms_r(x)
    xh = x * r
    gy = dy * g
    dx = r * (gy - xh * jnp.mean(xh * gy, axis=-1, keepdims=True))
    return dx, jnp.sum(dy * xh, axis=0, keepdims=True)


def _matmul(name, mats, *, m, n, tm, tn, nk=1, extras=(), outs, epilogue, n_acc, j_outer=False, deps=()):
    ni, nj = m // tm, n // tn
    assert ni * tm == m and nj * tn == n, (name, m, n, tm, tn)
    if j_outer:
        grid = (nj, ni, nk)

        def ij(g0, g1):
            return g1, g0
    else:
        grid = (ni, nj, nk)

        def ij(g0, g1):
            return g0, g1

    in_specs, args = [], []
    for mt in mats:
        mode, tk = mt["mode"], mt["tk"]
        ao, bo = mt.get("a_off", (0, 0)), mt.get("b_off", (0, 0))
        if mode == "tn":
            sa = pl.BlockSpec((tk, tm), lambda g0, g1, kk, ao=ao: (ao[0] + kk, ao[1] + ij(g0, g1)[0]))
        else:
            sa = pl.BlockSpec((tm, tk), lambda g0, g1, kk, ao=ao: (ao[0] + ij(g0, g1)[0], ao[1] + kk))
        if mode == "nt":
            sb = pl.BlockSpec((tn, tk), lambda g0, g1, kk, bo=bo: (bo[0] + ij(g0, g1)[1], bo[1] + kk))
        else:
            sb = pl.BlockSpec((tk, tn), lambda g0, g1, kk, bo=bo: (bo[0] + kk, bo[1] + ij(g0, g1)[1]))
        in_specs += [sa, sb]
        args += [mt["a"], mt["b"]]
    n_mat_refs = len(args)
    for ex in extras:
        if ex[0] == "mn":
            in_specs.append(pl.BlockSpec((tm, tn), lambda g0, g1, kk, c=ex[2]: (ij(g0, g1)[0], c + ij(g0, g1)[1])))
        else:
            in_specs.append(pl.BlockSpec((1, tn), lambda g0, g1, kk: (0, ij(g0, g1)[1])))
        args.append(ex[1])
    n_ex_end = len(args)
    in_specs += [pl.BlockSpec(memory_space=pl.ANY)] * len(deps)
    args += list(deps)
    n_in = len(args)
    out_specs, out_shape = [], []
    for o in outs:
        if o[0] == "mn":
            out_specs.append(pl.BlockSpec((tm, tn), lambda g0, g1, kk, c=o[3]: (ij(g0, g1)[0], c + ij(g0, g1)[1])))
        else:
            assert nj == 1, name
            out_specs.append(pl.BlockSpec((1, tn), lambda g0, g1, kk: (0, 0)))
        out_shape.append(jax.ShapeDtypeStruct(o[1], o[2]))
    n_out = len(outs)

    def body(*refs):
        mat_refs = refs[:n_mat_refs]
        ex_refs = refs[n_mat_refs:n_ex_end]
        out_refs = refs[n_in:n_in + n_out]
        acc_refs = refs[n_in + n_out:]
        i = ij(pl.program_id(0), pl.program_id(1))[0]
        kk = pl.program_id(2)

        def products():
            res = [None] * n_acc
            for idx, mt in enumerate(mats):
                a = mat_refs[2 * idx][...].astype(BF)
                b = mat_refs[2 * idx + 1][...].astype(BF)
                p = lax.dot_general(a, b, _DN[mt["mode"]], preferred_element_type=F32)
                q = mt["acc"]
                res[q] = p if res[q] is None else res[q] + p
            return res

        def finish(accs):
            vals = epilogue(accs, [r[...] for r in ex_refs])
            for o, r, v in zip(outs, out_refs, vals):
                if o[0] == "mn":
                    r[...] = v.astype(o[2])
                else:
                    @pl.when(i == 0)
                    def _():
                        r[...] = v

                    @pl.when(i != 0)
                    def _():
                        r[...] += v

        if nk == 1:
            finish(products())
        else:
            res = products()

            @pl.when(kk == 0)
            def _():
                for q in range(n_acc):
                    acc_refs[q][...] = res[q]

            @pl.when(kk != 0)
            def _():
                for q in range(n_acc):
                    acc_refs[q][...] += res[q]

            @pl.when(kk == nk - 1)
            def _():
                finish([r[...] for r in acc_refs])

    scratch = [pltpu.VMEM((tm, tn), F32) for _ in range(n_acc)] if nk > 1 else []
    return pl.pallas_call(
        body,
        name=name,
        grid=grid,
        in_specs=in_specs,
        out_specs=out_specs,
        out_shape=out_shape,
        scratch_shapes=scratch,
        compiler_params=_params(("arbitrary", "arbitrary", "arbitrary")),
    )(*args)


def _rowwise(name, ins, outs, fn, *, t, tm, deps=()):
    ni = t // tm
    assert ni * tm == t, (name, t, tm)
    in_specs, args = [], []
    for s in ins:
        if s[0] == "row":
            in_specs.append(pl.BlockSpec((tm, s[2]), lambda i, c=s[3]: (i, c)))
        else:
            nd = s[1].ndim
            in_specs.append(pl.BlockSpec(s[1].shape, lambda i, nd=nd: (0,) * nd))
        args.append(s[1])
    out_specs, out_shape = [], []
    for o in outs:
        if o[0] == "row":
            out_specs.append(pl.BlockSpec((tm, o[3]), lambda i, c=o[4]: (i, c)))
        else:
            nd = len(o[1])
            out_specs.append(pl.BlockSpec(o[1], lambda i, nd=nd: (0,) * nd))
        out_shape.append(jax.ShapeDtypeStruct(o[1], o[2]))
    n_read = len(args)
    in_specs += [pl.BlockSpec(memory_space=pl.ANY)] * len(deps)
    args += list(deps)
    n_in = len(args)

    def body(*refs):
        i = pl.program_id(0)
        vals = fn([r[...] for r in refs[:n_read]])
        for o, r, v in zip(outs, refs[n_in:], vals):
            if o[0] == "row":
                r[...] = v.astype(o[2])
            else:
                @pl.when(i == 0)
                def _():
                    r[...] = v

                @pl.when(i != 0)
                def _():
                    r[...] += v

    return pl.pallas_call(
        body,
        name=name,
        grid=(ni,),
        in_specs=in_specs,
        out_specs=out_specs,
        out_shape=out_shape,
        compiler_params=_params(("arbitrary",)),
    )(*args)


def _place():
    return lax.axis_index("x"), lax.axis_index("y"), lax.axis_index("c")


def _remote(src, dst, send_sems, recv_sems, k, to):
    return pltpu.make_async_remote_copy(
        src_ref=src, dst_ref=dst, send_sem=send_sems.at[k], recv_sem=recv_sems.at[k],
        device_id=to, device_id_type=MESH)


def _chips(x, y):
    return [(1 - x, y), (x, 1 - y), (1 - x, 1 - y)]


def _broadcast_plan(n):
    def plan(refs, send_sems, recv_sems):
        x, y, c = _place()
        cps = []
        for i in range(n):
            for dd in range(1, N_DEV):
                peer = (1 - x if dd & 4 else x, 1 - y if dd & 2 else y, 1 - c if dd & 1 else c)
                cps.append(_remote(refs[i], refs[n + i].at[4 * x + 2 * y + c], send_sems, recv_sems, 7 * i + dd - 1, peer))
        return cps

    def local(refs):
        x, y, c = _place()
        return [(refs[i], refs[n + i].at[4 * x + 2 * y + c]) for i in range(n)]

    return plan, local


def _handshake(peers):
    barrier = pltpu.get_barrier_semaphore()
    for peer in peers:
        pl.semaphore_signal(barrier, inc=1, device_id=peer, device_id_type=MESH)
    pl.semaphore_wait(barrier, len(peers))


def _sequencer_call(name, collective_id, body, args, out_type, scratch_types):
    return pl.kernel(
        body,
        out_type=out_type,
        mesh=plsc.ScalarSubcoreMesh(axis_name="sequencer", num_cores=1),
        scratch_types=scratch_types,
        compiler_params=pltpu.CompilerParams(collective_id=collective_id),
        name=name,
    )(*args)


def _sc_gather(name, collective_id, shards):
    n = len(shards)

    def body(*refs):
        ins, outs = refs[:n], refs[n:2 * n]
        send_sems, recv_sems, local_sems = refs[2 * n:]
        x, y, c = _place()
        me, sibling = (x, y, c), (x, y, 1 - c)
        chips = _chips(x, y)
        _handshake([sibling] + [(*chip, c) for chip in chips])

        def slot(i, px, py, pc):
            return outs[i].at[4 * px + 2 * py + pc]

        def copy(i, k, block, to, src=None):
            return _remote(slot(i, *block) if src is None else src, slot(i, *block), send_sems, recv_sems, 7 * i + k, to)

        mine = [pltpu.make_async_copy(ins[i], slot(i, *me), local_sems.at[i]) for i in range(n)]
        for cp in mine:
            cp.start()
        first = []
        for i in range(n):
            first += [copy(i, 1 + j, me, (*chip, c), src=ins[i]) for j, chip in enumerate(chips)]
            first.append(copy(i, 0, me, sibling, src=ins[i]))
        for cp in first:
            cp.start()
        passed = []
        for i in range(n):
            for j, chip in enumerate(chips):
                copy(i, 1 + j, (*chip, c), me).wait_recv()
                fwd = copy(i, 4 + j, (*chip, c), sibling)
                fwd.start()
                passed.append(fwd)
        for i in range(n):
            copy(i, 0, sibling, me).wait_recv()
            for j, chip in enumerate(chips):
                copy(i, 4 + j, (*chip, 1 - c), me).wait_recv()
        for cp in first + passed:
            cp.wait_send()
        for cp in mine:
            cp.wait()

    outs = _sequencer_call(
        name, collective_id, body, shards,
        [jax.ShapeDtypeStruct((N_DEV,) + s.shape, s.dtype) for s in shards],
        [pltpu.SemaphoreType.DMA((7 * n,)), pltpu.SemaphoreType.DMA((7 * n,)), pltpu.SemaphoreType.DMA((n,))])
    return [o.reshape((N_DEV * s.shape[0],) + s.shape[1:]) for o, s in zip(outs, shards)]


def _sc_exchange(name, collective_id, srcs, land_shapes, plan, n_copies, peers, local=None):
    n = len(srcs)

    def body(*refs):
        bufs, send_sems, recv_sems = refs[:2 * n], refs[2 * n], refs[2 * n + 1]
        _handshake(peers())
        if local is not None:
            mine = [pltpu.make_async_copy(a, b, refs[2 * n + 2].at[k]) for k, (a, b) in enumerate(local(bufs))]
            for cp in mine:
                cp.start()
        cps = plan(bufs, send_sems, recv_sems)
        for cp in cps:
            cp.start()
        for cp in cps:
            cp.wait_send()
            cp.wait_recv()
        if local is not None:
            for cp in mine:
                cp.wait()

    scratch = [pltpu.SemaphoreType.DMA((n_copies,)), pltpu.SemaphoreType.DMA((n_copies,))]
    if local is not None:
        scratch.append(pltpu.SemaphoreType.DMA((n,)))
    return list(_sequencer_call(name, collective_id, body, srcs, land_shapes, scratch))


def _adam_vals(w, g, m, v):
    m = ADAM_B1 * m + (1.0 - ADAM_B1) * g
    v = ADAM_B2 * v + (1.0 - ADAM_B2) * (g * g)
    m_hat = m / (1.0 - ADAM_B1 ** ADAM_STEP)
    v_hat = v / (1.0 - ADAM_B2 ** ADAM_STEP)
    delta = -ADAM_LR * (m_hat / (jnp.sqrt(v_hat) + ADAM_EPS) + ADAM_WD * w)
    return delta, m, v


def _sum_parts(st_ref):
    g = st_ref[0].astype(F32)
    for k in range(1, st_ref.shape[0]):
        g = g + st_ref[k].astype(F32)
    return g


def _reduce_adam(name, st, w, m, v, deps=()):
    rows, cols = w.shape
    tr = _tile(rows, UPDATE_ROWS, 16)

    def body(st_ref, w_ref, m_ref, v_ref, *rest):
        g_out, d_out, m_out, v_out = rest[len(deps):]
        g = _sum_parts(st_ref)
        d, mm, vv = _adam_vals(w_ref[...], g, m_ref[...], v_ref[...])
        g_out[...] = g
        d_out[...] = d
        m_out[...] = mm
        v_out[...] = vv

    blk = pl.BlockSpec((tr, cols), lambda i: (i, 0))
    return pl.pallas_call(
        body,
        name=name,
        grid=(rows // tr,),
        in_specs=[pl.BlockSpec((st.shape[0], tr, cols), lambda i: (0, i, 0)), blk, blk, blk]
        + [pl.BlockSpec(memory_space=pl.ANY)] * len(deps),
        out_specs=[blk] * 4,
        out_shape=[jax.ShapeDtypeStruct(w.shape, F32)] * 4,
        compiler_params=_params(("arbitrary",)),
    )(st, w, m, v, *deps)


def _reduce_only(name, st, deps=()):
    _, rows, cols = st.shape
    tr = _tile(rows, UPDATE_ROWS, 16)

    def body(st_ref, *rest):
        rest[-1][...] = _sum_parts(st_ref)

    return pl.pallas_call(
        body,
        name=name,
        grid=(rows // tr,),
        in_specs=[pl.BlockSpec((st.shape[0], tr, cols), lambda i: (0, i, 0))]
        + [pl.BlockSpec(memory_space=pl.ANY)] * len(deps),
        out_specs=pl.BlockSpec((tr, cols), lambda i: (i, 0)),
        out_shape=jax.ShapeDtypeStruct((rows, cols), F32),
        compiler_params=_params(("arbitrary",)),
    )(st, *deps)


def _adam_only(name, w, g, m, v):
    rows, cols = w.shape
    tr = _tile(rows, UPDATE_ROWS, 16)

    def body(w_ref, g_ref, m_ref, v_ref, d_out, m_out, v_out):
        d, mm, vv = _adam_vals(w_ref[...], g_ref[...], m_ref[...], v_ref[...])
        d_out[...] = d
        m_out[...] = mm
        v_out[...] = vv

    blk = pl.BlockSpec((tr, cols), lambda i: (i, 0))
    return pl.pallas_call(
        body,
        name=name,
        grid=(rows // tr,),
        in_specs=[blk] * 4,
        out_specs=[blk] * 3,
        out_shape=[jax.ShapeDtypeStruct(w.shape, F32)] * 3,
        compiler_params=_params(("arbitrary",)),
    )(w, g, m, v)


def _prenorm(name, h, g, t, d, tm, deps=()):
    def fn(v):
        x, gg = v
        return [x * _rms_r(x) * gg]

    return _rowwise(name, [("row", h, d, 0), ("full", g)], [("row", (t, d), BF, d, 0)], fn, t=t, tm=tm, deps=deps)[0]


def _ffn_up(tag, xn, wgt, wut, t, d, f, tm):
    def up_epi(accs, ex):
        gg, uu = accs
        return [gg, uu, gg * _sig(gg) * uu]

    mats = [dict(a=xn, b=wgt, mode="nt", acc=0, tk=d), dict(a=xn, b=wut, mode="nt", acc=1, tk=d)]
    return _matmul(
        tag + "_up", mats, m=t, n=f, tm=tm, tn=_tile(f, 1408),
        outs=[("mn", (t, f), BF, 0)] * 3, epilogue=up_epi, n_acc=2, j_outer=True)


def _ffn_up_split(tag, xn, wgt, wut, t, d, f, tm):
    tf = _tile(f, 1408)
    gate = _matmul(
        tag + "_gate", [dict(a=xn, b=wgt, mode="nt", acc=0, tk=d)], m=t, n=f, tm=tm, tn=tf,
        outs=[("mn", (t, f), BF, 0)], epilogue=lambda accs, ex: accs, n_acc=1, j_outer=True)[0]

    def up_epi(accs, ex):
        gg = ex[0].astype(F32)
        return [accs[0], gg * _sig(gg) * accs[0]]

    up, hid = _matmul(
        tag + "_up", [dict(a=xn, b=wut, mode="nt", acc=0, tk=d)], m=t, n=f, tm=tm, tn=tf, extras=[("mn", gate, 0)],
        outs=[("mn", (t, f), BF, 0)] * 2, epilogue=up_epi, n_acc=1, j_outer=True)
    return gate, up, hid


def _ffn_down(tag, hid, wd, h, g_post, g_next, t, d, f, tm):
    def down_epi(accs, ex):
        ff = accs[0]
        hh, gg, gn = ex
        h_new = hh + 0.5 * ff * _rms_r(ff) * gg
        return [ff, h_new, h_new * _rms_r(h_new) * gn]

    return _matmul(
        tag + "_down", [dict(a=hid, b=wd, mode="nn", acc=0, tk=f)], m=t, n=d, tm=tm, tn=d,
        extras=[("mn", h, 0), ("n", g_post), ("n", g_next)],
        outs=[("mn", (t, d), F32, 0)] * 2 + [("mn", (t, d), BF, 0)], epilogue=down_epi, n_acc=1)


def _prenorm_bwd_epi(accs, ex):
    hh, dh, gg = ex
    dx, dg = _rms_bwd(hh, gg, accs[0])
    return [dh + dx, dg]


def _dx_extras(h_in, dh, g_pre, fo, g_post):
    return [("mn", h_in, 0), ("mn", dh, 0), ("n", g_pre), ("mn", fo, 0), ("n", g_post)]


def _dx_outs(t, d):
    return [("mn", (t, d), F32, 0), ("acc", (1, d), F32), ("mn", (t, d), BF, 0), ("acc", (1, d), F32)]


def _dx_epi(scale):
    def epi(accs, ex):
        hh, dh, gg, fo, gp = ex
        dx, dg = _rms_bwd(hh, gg, accs[0])
        dh_new = dh + dx
        dfo, dgp = _rms_bwd(fo, gp, dh_new * scale)
        return [dh_new, dg, dfo, dgp]

    return epi


def _ffn_dhid(tag, df, wd, gate, up, t, d, f, tm):
    def hid_epi(accs, ex):
        dhid = accs[0]
        gg, uu = ex[0].astype(F32), ex[1].astype(F32)
        s = _sig(gg)
        return [dhid * uu * s * (1.0 + gg * (1.0 - s)), dhid * gg * s]

    return _matmul(
        tag + "_dhid", [dict(a=df, b=wd, mode="nt", acc=0, tk=d)], m=t, n=f, tm=tm, tn=_tile(f, 1408),
        extras=[("mn", gate, 0), ("mn", up, 0)],
        outs=[("mn", (t, f), BF, 0)] * 2, epilogue=hid_epi, n_acc=1, j_outer=True)


def _ffn_dwd(tag, hid, df, t, d, f):
    tk_t = _tile(t, TOKEN_K_TILE)
    return _matmul(
        tag + "_dwd", [dict(a=hid, b=df, mode="tn", acc=0, tk=tk_t)], m=f, n=d, tm=_tile(f, 1408), tn=d,
        nk=t // tk_t, outs=[("mn", (f, d), BF, 0)], epilogue=lambda accs, ex: accs, n_acc=1)


def _ffn_dwgu(tag, dgate, dup, xn, t, d, f, deps=()):
    tk_t = _tile(t, TOKEN_K_TILE)
    return _matmul(
        tag + "_dwgu",
        [dict(a=dgate, b=xn, mode="tn", acc=0, tk=tk_t), dict(a=dup, b=xn, mode="tn", acc=1, tk=tk_t)],
        m=f, n=d, tm=_tile(f, 1408), tn=d, nk=t // tk_t,
        outs=[("mn", (f, d), BF, 0)] * 2, epilogue=lambda accs, ex: accs, n_acc=2, deps=deps)


def _ffn_dx(tag, dh, h_in, dgate, dup, wgt, wut, g_pre, t, d, f, tm, deps=(), before=None):
    tf = _tile(f, 1408)
    mats = [dict(a=dgate, b=wgt, mode="nn", acc=0, tk=tf), dict(a=dup, b=wut, mode="nn", acc=0, tk=tf)]
    if before is None:
        return _matmul(
            tag + "_dx", mats, m=t, n=d, tm=tm, tn=d, nk=f // tf,
            extras=[("mn", h_in, 0), ("mn", dh, 0), ("n", g_pre)],
            outs=[("mn", (t, d), F32, 0), ("acc", (1, d), F32)], epilogue=_prenorm_bwd_epi, n_acc=1, deps=deps)
    fo, g_post, scale = before
    return _matmul(
        tag + "_dx", mats, m=t, n=d, tm=tm, tn=d, nk=f // tf, extras=_dx_extras(h_in, dh, g_pre, fo, g_post),
        outs=_dx_outs(t, d), epilogue=_dx_epi(scale), n_acc=1, deps=deps)


def _causal_mask():
    r = lax.broadcasted_iota(jnp.int32, (CHUNK, CHUNK), 0)
    c = lax.broadcasted_iota(jnp.int32, (CHUNK, CHUNK), 1)
    return r >= c


def _layernorm_parts(v):
    mu = jnp.mean(v, axis=-1, keepdims=True)
    vc = v - mu
    rstd = lax.rsqrt(jnp.mean(vc * vc, axis=-1, keepdims=True) + EPS)
    return vc * rstd, rstd


def _sgu_fwd(z, g_sgu, w_s, b_col, t, d, tm):
    dg = d // N_SGU_GROUPS
    n_chunk = tm // CHUNK

    def body(zu_ref, zv_ref, g_ref, w_ref, b_ref, a_ref):
        u = _gelu(zu_ref[...].astype(F32))
        vhat, _ = _layernorm_parts(_gelu(zv_ref[...].astype(F32)))
        vn = (vhat * g_ref[...]).astype(BF)
        mask = _causal_mask()
        for gi in range(N_SGU_GROUPS):
            ws = jnp.where(mask, w_ref[gi], 0.0).astype(BF)
            bias = b_ref[gi]
            for ci in range(n_chunk):
                rows, cols = slice(ci * CHUNK, (ci + 1) * CHUNK), slice(gi * dg, (gi + 1) * dg)
                sv = jnp.dot(ws, vn[rows, cols], preferred_element_type=F32) + bias
                a_ref[rows, cols] = (u[rows, cols] * sv).astype(BF)

    return pl.pallas_call(
        body,
        name="sgu_fwd",
        grid=(t // tm,),
        in_specs=[
            pl.BlockSpec((tm, d), lambda i: (i, 0)),
            pl.BlockSpec((tm, d), lambda i: (i, 1)),
            pl.BlockSpec((1, d), lambda i: (0, 0)),
            pl.BlockSpec(w_s.shape, lambda i: (0, 0, 0)),
            pl.BlockSpec(b_col.shape, lambda i: (0, 0, 0)),
        ],
        out_specs=pl.BlockSpec((tm, d), lambda i: (i, 0)),
        out_shape=jax.ShapeDtypeStruct((t, d), BF),
        compiler_params=_params(("arbitrary",)),
    )(z, z, g_sgu, w_s, b_col)


def _sgu_bwd(z, da, dz, g_sgu, w_s, b_col, t, d, tm):
    dg = d // N_SGU_GROUPS
    n_chunk = tm // CHUNK

    def body(zu_ref, zv_ref, da_ref, dz_in, g_ref, w_ref, b_ref, dz_ref, dw_ref, db_ref, dgn_ref, dvn_ref):
        del dz_in
        dzu_ref, dzv_ref = dz_ref.at[:, pl.ds(0, d)], dz_ref.at[:, pl.ds(d, d)]
        i = pl.program_id(0)
        zu, zv = zu_ref[...].astype(F32), zv_ref[...].astype(F32)
        u = _gelu(zu)
        vhat, rstd = _layernorm_parts(_gelu(zv))
        gn = g_ref[...]
        vn = (vhat * gn).astype(BF)
        da_v = da_ref[...].astype(F32)
        dsv_all = (da_v * u).astype(BF)
        mask = _causal_mask()
        for gi in range(N_SGU_GROUPS):
            ws = jnp.where(mask, w_ref[gi], 0.0).astype(BF)
            bias = b_ref[gi]
            dw = jnp.zeros((CHUNK, CHUNK), F32)
            dbias = jnp.zeros((CHUNK, 1), F32)
            for ci in range(n_chunk):
                rows, cols = slice(ci * CHUNK, (ci + 1) * CHUNK), slice(gi * dg, (gi + 1) * dg)
                vn_c, dsv = vn[rows, cols], dsv_all[rows, cols]
                sv = jnp.dot(ws, vn_c, preferred_element_type=F32) + bias
                dzu_ref[rows, cols] = (da_v[rows, cols] * sv * _gelu_grad(zu[rows, cols])).astype(BF)
                dw = dw + lax.dot_general(dsv, vn_c, _DN["nt"], preferred_element_type=F32)
                dbias = dbias + jnp.sum(dsv.astype(F32), axis=1, keepdims=True)
                dvn_ref[rows, cols] = lax.dot_general(ws, dsv, _DN["tn"], preferred_element_type=F32)
            dw = jnp.where(mask, dw, 0.0)

            @pl.when(i == 0)
            def _():
                dw_ref[gi] = dw
                db_ref[gi] = dbias

            @pl.when(i != 0)
            def _():
                dw_ref[gi] += dw
                db_ref[gi] += dbias

        dvn = dvn_ref[...]
        dgn = jnp.sum(dvn * vhat, axis=0, keepdims=True)

        @pl.when(i == 0)
        def _():
            dgn_ref[...] = dgn

        @pl.when(i != 0)
        def _():
            dgn_ref[...] += dgn

        dvh = dvn * gn
        dv = rstd * (dvh - jnp.mean(dvh, axis=-1, keepdims=True) - vhat * jnp.mean(dvh * vhat, axis=-1, keepdims=True))
        dzv_ref[...] = (dv * _gelu_grad(zv)).astype(BF)

    return pl.pallas_call(
        body,
        name="sgu_bwd",
        grid=(t // tm,),
        in_specs=[
            pl.BlockSpec((tm, d), lambda i: (i, 0)),
            pl.BlockSpec((tm, d), lambda i: (i, 1)),
            pl.BlockSpec((tm, d), lambda i: (i, 0)),
            pl.BlockSpec(memory_space=pl.ANY),
            pl.BlockSpec((1, d), lambda i: (0, 0)),
            pl.BlockSpec(w_s.shape, lambda i: (0, 0, 0)),
            pl.BlockSpec(b_col.shape, lambda i: (0, 0, 0)),
        ],
        out_specs=[
            pl.BlockSpec((tm, 2 * d), lambda i: (i, 0)),
            pl.BlockSpec(w_s.shape, lambda i: (0, 0, 0)),
            pl.BlockSpec(b_col.shape, lambda i: (0, 0, 0)),
            pl.BlockSpec((1, d), lambda i: (0, 0)),
        ],
        out_shape=[
            jax.ShapeDtypeStruct(dz.shape, BF),
            jax.ShapeDtypeStruct(w_s.shape, F32),
            jax.ShapeDtypeStruct(b_col.shape, F32),
            jax.ShapeDtypeStruct((1, d), F32),
        ],
        scratch_shapes=[pltpu.VMEM((tm, d), F32)],
        input_output_aliases={3: 0},
        compiler_params=_params(("arbitrary",)),
    )(z, z, da, dz, g_sgu, w_s, b_col)


def _shift_down(x, k, row):
    return jnp.where(row >= k, pltpu.roll(x, k, 0), 0.0)


def _shift_up(x, k, row, t):
    return jnp.where(row < t - k, pltpu.roll(x, t - k, 0), 0.0)


def _doublings(window):
    steps = int(math.log2(window))
    assert 2 ** steps == window
    return [2 ** s for s in range(steps)]


def _pool_diff(c, window, row):
    s = c
    for k in _doublings(window):
        s = s + _shift_down(s, k, row)
    count = jnp.minimum(row + 1, window).astype(F32)
    return s / count - c, count


def _pool_fwd(z, pool_w, pool_scale, t, d):
    dgp = d // len(POOL_WINDOWS)
    cblk = (2 * d) // dgp

    def body(zc_ref, w_ref, s_ref, b_ref):
        row = lax.broadcasted_iota(jnp.int32, (t, 1), 0)
        for gi, window in enumerate(POOL_WINDOWS):
            @pl.when(pl.program_id(0) == gi)
            def _(window=window):
                diff, _ = _pool_diff(zc_ref[...].astype(F32), window, row)
                out = jnp.dot(diff.astype(BF), w_ref[...], preferred_element_type=F32)
                b_ref[...] = (out * s_ref[...]).astype(BF)

    return pl.pallas_call(
        body,
        name="pool_fwd",
        grid=(len(POOL_WINDOWS),),
        in_specs=[
            pl.BlockSpec((t, dgp), lambda g: (0, cblk + g)),
            pl.BlockSpec((None, dgp, dgp), lambda g: (g, 0, 0)),
            pl.BlockSpec((1, dgp), lambda g: (0, g)),
        ],
        out_specs=pl.BlockSpec((t, dgp), lambda g: (0, g)),
        out_shape=jax.ShapeDtypeStruct((t, d), BF),
        compiler_params=_params(("arbitrary",)),
    )(z, pool_w, pool_scale)


def _pool_bwd(z, db, dz, pool_w, pool_scale, t, d):
    dgp = d // len(POOL_WINDOWS)
    cblk = (2 * d) // dgp

    def body(zc_ref, db_ref, dz_in, w_ref, s_ref, dzc_ref, dw_ref, ds_ref):
        del dz_in
        row = lax.broadcasted_iota(jnp.int32, (t, 1), 0)
        for gi, window in enumerate(POOL_WINDOWS):
            @pl.when(pl.program_id(0) == gi)
            def _(window=window):
                diff, count = _pool_diff(zc_ref[...].astype(F32), window, row)
                diff = diff.astype(BF)
                w = w_ref[...]
                dbv = db_ref[...].astype(F32)
                out = jnp.dot(diff, w, preferred_element_type=F32)
                ds_ref[...] = jnp.sum(dbv * out, axis=0, keepdims=True)
                dout = (dbv * s_ref[...]).astype(BF)
                dw_ref[...] = lax.dot_general(diff, dout, _DN["tn"], preferred_element_type=F32).astype(BF)
                ddiff = lax.dot_general(dout, w, _DN["nt"], preferred_element_type=F32)
                s = ddiff / count
                for k in _doublings(window):
                    s = s + _shift_up(s, k, row, t)
                dzc_ref[...] = (s - ddiff).astype(BF)

    return pl.pallas_call(
        body,
        name="pool_bwd",
        grid=(len(POOL_WINDOWS),),
        in_specs=[
            pl.BlockSpec((t, dgp), lambda g: (0, cblk + g)),
            pl.BlockSpec((t, dgp), lambda g: (0, g)),
            pl.BlockSpec(memory_space=pl.ANY),
            pl.BlockSpec((None, dgp, dgp), lambda g: (g, 0, 0)),
            pl.BlockSpec((1, dgp), lambda g: (0, g)),
        ],
        out_specs=[
            pl.BlockSpec((t, dgp), lambda g: (0, cblk + g)),
            pl.BlockSpec((None, dgp, dgp), lambda g: (g, 0, 0)),
            pl.BlockSpec((1, dgp), lambda g: (0, g)),
        ],
        out_shape=[
            jax.ShapeDtypeStruct(dz.shape, BF),
            jax.ShapeDtypeStruct(pool_w.shape, BF),
            jax.ShapeDtypeStruct((1, d), F32),
        ],
        input_output_aliases={2: 0},
        compiler_params=_params(("arbitrary",)),
    )(z, db, dz, pool_w, pool_scale)


def _mix_dy(dm, w_o, z, ya, yb, t, d, tm):
    def body(dm_ref, w_ref, gate_ref, ya_ref, yb_ref, dyab_ref, dz_ref, dy_ref):
        j = pl.program_id(1)

        @pl.when(j == 0)
        def _():
            dy_ref[...] = lax.dot_general(dm_ref[...], w_ref[...], _DN["nt"], preferred_element_type=F32)

        dy = dy_ref[...]
        s = _sig(gate_ref[...].astype(F32))
        yv = jnp.where(j == 0, ya_ref[...], yb_ref[...]).astype(F32)
        dyab_ref[...] = (dy * s).astype(BF)
        dz_ref[...] = (dy * yv * s * (1.0 - s)).astype(BF)

    row = pl.BlockSpec((tm, d), lambda i, j: (i, 0))
    return pl.pallas_call(
        body,
        name="mix_dy",
        grid=(t // tm, 2),
        in_specs=[row, pl.BlockSpec((d, d), lambda i, j: (0, 0)), pl.BlockSpec((tm, d), lambda i, j: (i, 3 + j)), row, row],
        out_specs=[pl.BlockSpec((tm, d), lambda i, j: (i, j)), pl.BlockSpec((tm, d), lambda i, j: (i, 3 + j))],
        out_shape=[jax.ShapeDtypeStruct((t, 2 * d), BF), jax.ShapeDtypeStruct((t, 5 * d), BF)],
        scratch_shapes=[pltpu.VMEM((tm, d), F32)],
        compiler_params=_params(("arbitrary", "arbitrary")),
    )(dm, w_o, z, ya, yb)


def _everyone():
    x, y, c = _place()
    return [(1 - x if dd & 4 else x, 1 - y if dd & 2 else y, 1 - c if dd & 1 else c) for dd in range(1, N_DEV)]


def _direct_plan(n):
    def plan(refs, send_sems, recv_sems):
        x, y, c = _place()
        me = 4 * x + 2 * y + c
        cps = []
        for i in range(n):
            for k, (px, py, pc) in enumerate(_everyone()):
                cps.append(_remote(refs[i].at[4 * px + 2 * py + pc], refs[n + i].at[me], send_sems, recv_sems,
                                   7 * i + k, (px, py, pc)))
        return cps

    def local(refs):
        x, y, c = _place()
        me = 4 * x + 2 * y + c
        return [(refs[i].at[me], refs[n + i].at[me]) for i in range(n)]

    return plan, local


def _sc_scatter_direct(name, collective_id, parts):
    n = len(parts)
    p8 = [a.reshape(N_DEV, a.shape[0] // N_DEV, a.shape[1]) for a in parts]
    plan, local = _direct_plan(n)
    return _sc_exchange(name, collective_id, p8, [jax.ShapeDtypeStruct(a.shape, a.dtype) for a in p8], plan,
                        7 * n, _everyone, local=local)


def _stack_rows(arrays):
    parts, starts, row = [], [], 0
    for a in arrays:
        pad = -a.shape[0] % 8
        starts.append(row)
        parts += [a, jnp.zeros((pad, a.shape[1]), a.dtype)] if pad else [a]
        row += a.shape[0] + pad
    return jnp.concatenate(parts, axis=0), starts


def _adam_replicated(name, stacks, layout, weights, deps=()):
    n_st, n_par = len(stacks), len(weights)

    def body(*refs):
        st_refs = refs[:n_st]
        w_refs = refs[n_st:n_st + 3 * n_par]
        out_refs = refs[n_st + 3 * n_par + len(deps):]

        def summed(which, row, rows):
            g = st_refs[which][0, pl.ds(row, rows), :]
            for k in range(1, N_DEV):
                g = g + st_refs[which][k, pl.ds(row, rows), :]
            return g

        for i in range(n_par):
            w_ref, m_ref, v_ref = w_refs[3 * i:3 * i + 3]
            g = summed(layout[i][0], layout[i][1], w_ref.shape[0])
            dlt, mm, vv = _adam_vals(w_ref[...], g, m_ref[...], v_ref[...])
            for r, val in zip(out_refs[4 * i:4 * i + 4], (g, dlt, mm, vv)):
                r[...] = val
        out_refs[4 * n_par][...] = jnp.sum(summed(layout[n_par][0], layout[n_par][1], 1), axis=1, keepdims=True)

    flat_w = [a for wmv in weights for a in wmv]
    out_shape = [jax.ShapeDtypeStruct(wmv[0].shape, F32) for wmv in weights for _ in range(4)]
    out_shape.append(jax.ShapeDtypeStruct((1, 1), F32))
    vmem = pl.BlockSpec(memory_space=pltpu.VMEM)
    res = pl.pallas_call(
        body,
        name=name,
        in_specs=[vmem] * (n_st + len(flat_w)) + [pl.BlockSpec(memory_space=pl.ANY)] * len(deps),
        out_specs=[vmem] * len(out_shape),
        out_shape=out_shape,
        compiler_params=_params(),
    )(*stacks, *flat_w, *deps)
    return [res[4 * i:4 * i + 4] for i in range(n_par)], res[4 * n_par]


def kernel(x, p, ffn1_pre_g, ffn1_w_gate, ffn1_w_up, ffn1_w_down, ffn1_post_g, mix_pre_g, w_in, sgu_norm_g, sgu_w, sgu_b, pool_w, pool_scale, w_out_a, w_out_b, w_o, mix_post_g, ffn2_pre_g, ffn2_w_gate, ffn2_w_up, ffn2_w_down, ffn2_post_g, ple_pre_g, ple_w_gate, ple_w_proj, ple_post_g, loss_target, m_ffn1_pre_g, m_ffn1_w_gate, m_ffn1_w_up, m_ffn1_w_down, m_ffn1_post_g, m_mix_pre_g, m_w_in, m_sgu_norm_g, m_sgu_w, m_sgu_b, m_pool_w, m_pool_scale, m_w_out_a, m_w_out_b, m_w_o, m_mix_post_g, m_ffn2_pre_g, m_ffn2_w_gate, m_ffn2_w_up, m_ffn2_w_down, m_ffn2_post_g, m_ple_pre_g, m_ple_w_gate, m_ple_w_proj, m_ple_post_g, v_ffn1_pre_g, v_ffn1_w_gate, v_ffn1_w_up, v_ffn1_w_down, v_ffn1_post_g, v_mix_pre_g, v_w_in, v_sgu_norm_g, v_sgu_w, v_sgu_b, v_pool_w, v_pool_scale, v_w_out_a, v_w_out_b, v_w_o, v_mix_post_g, v_ffn2_pre_g, v_ffn2_w_gate, v_ffn2_w_up, v_ffn2_w_down, v_ffn2_post_g, v_ple_pre_g, v_ple_w_gate, v_ple_w_proj, v_ple_post_g):
    args = dict(locals())
    names = ["ffn1_pre_g", "ffn1_w_gate", "ffn1_w_up", "ffn1_w_down", "ffn1_post_g", "mix_pre_g", "w_in",
             "sgu_norm_g", "sgu_w", "sgu_b", "pool_w", "pool_scale", "w_out_a", "w_out_b", "w_o", "mix_post_g",
             "ffn2_pre_g", "ffn2_w_gate", "ffn2_w_up", "ffn2_w_down", "ffn2_post_g", "ple_pre_g", "ple_w_gate",
             "ple_w_proj", "ple_post_g"]
    w = {k: args[k][0] for k in names}
    mom = {k: args["m_" + k][0] for k in names}
    var = {k: args["v_" + k][0] for k in names}

    assert x.shape[0] == 1 and p.shape[:2] == (1, 1)
    t, d = x.shape[1], x.shape[2]
    f = ffn1_w_gate.shape[2] * N_DEV
    d_in = w_in.shape[2] * N_DEV
    d_ple = p.shape[3]
    n_pool = len(POOL_WINDOWS)
    dgp = d // n_pool
    assert d_in == 5 * d and t % CHUNK == 0
    tm = _tile(t, 512, CHUNK)
    xs, ps, target = x[0], p[0, 0], loss_target[0]

    col_sharded = ("ffn1_w_gate", "ffn1_w_up", "ffn2_w_gate", "ffn2_w_up", "w_in", "ple_w_proj")

    def shard_of(k):
        if k in col_sharded:
            return w[k].T.astype(BF)
        if k == "pool_w":
            return w[k].reshape(-1, dgp).astype(BF)
        return w[k].astype(BF)

    rows_pw = dgp // N_DEV
    b_col = w["sgu_b"][:, :, None]
    gains = {k: w[k][None, :] for k in names if w[k].ndim == 1}
    full = {}

    groups = [["ffn1_w_gate"], ["ffn1_w_up"], ["ffn1_w_down"], ["w_in"], ["pool_w", "w_out_a", "w_out_b", "w_o"],
              ["ffn2_w_gate", "ffn2_w_up", "ffn2_w_down", "ple_w_gate", "ple_w_proj"]]
    def gather(gi, behind=None):
        shards = [shard_of(k) for k in groups[gi]]
        if behind is not None:
            shards = lax.optimization_barrier((shards, behind))[0]
        full.update(zip(groups[gi], _sc_gather("gather%d" % gi, gi, shards)))

    gather(0)
    xn1 = _prenorm("ffn1_prenorm", xs, gains["ffn1_pre_g"], t, d, tm)
    gather(1, xn1)
    gather(2, xn1)
    gather(3, xn1)
    g1, u1, hid1 = _ffn_up_split("ffn1", xn1, full["ffn1_w_gate"], full["ffn1_w_up"], t, d, f, tm)
    gather(4, hid1)
    f1, h1, xn2 = _ffn_down("ffn1", hid1, full["ffn1_w_down"], xs, gains["ffn1_post_g"], gains["mix_pre_g"], t, d, f, tm)
    gather(5, h1)

    pool_full = full["pool_w"].reshape(N_DEV, n_pool, rows_pw, dgp).transpose(1, 0, 2, 3).reshape(n_pool, dgp, dgp)
    z = _matmul("mix_in", [dict(a=xn2, b=full["w_in"], mode="nt", acc=0, tk=d)], m=t, n=d_in, tm=_tile(t, 1024, CHUNK), tn=d,
                outs=[("mn", (t, d_in), BF, 0)], epilogue=lambda accs, ex: accs, n_acc=1, j_outer=True)[0]
    a_br = _sgu_fwd(z, gains["sgu_norm_g"], w["sgu_w"], b_col, t, d, tm)
    b_br = _pool_fwd(z, pool_full, gains["pool_scale"], t, d)

    def merge_epi(accs, ex):
        ya, yb = accs
        ga, gb = ex[0].astype(F32), ex[1].astype(F32)
        return [ya, yb, _sig(ga) * ya + _sig(gb) * yb]

    ya, yb, y = _matmul(
        "mix_merge",
        [dict(a=a_br, b=full["w_out_a"], mode="nn", acc=0, tk=d), dict(a=b_br, b=full["w_out_b"], mode="nn", acc=1, tk=d)],
        m=t, n=d, tm=tm, tn=d, extras=[("mn", z, 3), ("mn", z, 4)],
        outs=[("mn", (t, d), BF, 0)] * 3, epilogue=merge_epi, n_acc=2)

    def proj_epi(accs, ex):
        mm = accs[0]
        hh, gg, gn = ex
        h_new = hh + mm * _rms_r(mm) * gg
        return [mm, h_new, h_new * _rms_r(h_new) * gn]

    m_out, h2, xn3 = _matmul(
        "mix_proj", [dict(a=y, b=full["w_o"], mode="nn", acc=0, tk=d)], m=t, n=d, tm=tm, tn=d,
        extras=[("mn", h1, 0), ("n", gains["mix_post_g"]), ("n", gains["ffn2_pre_g"])],
        outs=[("mn", (t, d), F32, 0)] * 2 + [("mn", (t, d), BF, 0)], epilogue=proj_epi, n_acc=1)

    g2, u2, hid2 = _ffn_up("ffn2", xn3, full["ffn2_w_gate"], full["ffn2_w_up"], t, d, f, tm)
    f2, h3, xn4 = _ffn_down("ffn2", hid2, full["ffn2_w_down"], h2, gains["ffn2_post_g"], gains["ple_pre_g"], t, d, f, tm)

    def ple_epi(accs, ex):
        gl, e = accs
        hh, tgt, gg = ex
        q = _sig(gl) * e
        err = hh + q * _rms_r(q) * gg - tgt
        return [gl, e, err * (1.0 / d), jnp.sum(err * err, axis=0, keepdims=True)]

    gl, e_ple, dh4, loss_vec = _matmul(
        "ple_fwd",
        [dict(a=xn4, b=full["ple_w_gate"], mode="nn", acc=0, tk=d), dict(a=ps, b=full["ple_w_proj"], mode="nt", acc=1, tk=d_ple)],
        m=t, n=d, tm=tm, tn=d, extras=[("mn", h3, 0), ("mn", target, 0), ("n", gains["ple_post_g"])],
        outs=[("mn", (t, d), F32, 0)] * 3 + [("acc", (1, d), F32)], epilogue=ple_epi, n_acc=2)

    tk_t = _tile(t, TOKEN_K_TILE)

    def ple_post_fn(v):
        dy, gl_v, e_v, gg = v
        s = _sig(gl_v)
        dq, dg = _rms_bwd(s * e_v, gg, dy)
        return [dq * e_v * s * (1.0 - s), dq * s, dg]

    dgl, de, dg_ple_post = _rowwise(
        "ple_post_bwd", [("row", dh4, d, 0), ("row", gl, d, 0), ("row", e_ple, d, 0), ("full", gains["ple_post_g"])],
        [("row", (t, d), BF, d, 0), ("row", (t, d), BF, d, 0), ("acc", (1, d), F32)], ple_post_fn, t=t, tm=tm)
    ident = lambda accs, ex: accs
    dw_ple_gate = _matmul("ple_dwg", [dict(a=xn4, b=dgl, mode="tn", acc=0, tk=tk_t)], m=d, n=d, tm=d, tn=d,
                          nk=t // tk_t, outs=[("mn", (d, d), BF, 0)], epilogue=ident, n_acc=1)[0]
    dw_ple_proj_t = _matmul("ple_dwp", [dict(a=de, b=ps, mode="tn", acc=0, tk=tk_t)], m=d, n=d_ple, tm=d, tn=d_ple,
                            nk=t // tk_t, outs=[("mn", (d, d_ple), BF, 0)], epilogue=ident, n_acc=1)[0]
    dh3, dg_ple_pre, df2, dg_f2_post = _matmul(
        "ple_dx", [dict(a=dgl, b=full["ple_w_gate"], mode="nt", acc=0, tk=d)], m=t, n=d, tm=tm, tn=d,
        extras=_dx_extras(h3, dh4, gains["ple_pre_g"], f2, gains["ffn2_post_g"]),
        outs=_dx_outs(t, d), epilogue=_dx_epi(0.5), n_acc=1)

    staged = []
    n_gather = len(groups)

    def scatter(keys, parts):
        k = len(staged)
        staged.append((keys, _sc_scatter_direct("scatter%d" % k, n_gather + k, parts)))

    dgate2, dup2 = _ffn_dhid("ffn2", df2, full["ffn2_w_down"], g2, u2, t, d, f, tm)
    dwd2 = _ffn_dwd("ffn2", hid2, df2, t, d, f)[0]
    dwg2, dwu2 = _ffn_dwgu("ffn2", dgate2, dup2, xn3, t, d, f)
    scatter(["ple_w_gate", "ple_w_proj", "ffn2_w_down", "ffn2_w_gate", "ffn2_w_up"],
            [dw_ple_gate, dw_ple_proj_t, dwd2, dwg2, dwu2])
    dh2, dg_f2_pre, dm, dg_mix_post = _ffn_dx(
        "ffn2", dh3, h2, dgate2, dup2, full["ffn2_w_gate"], full["ffn2_w_up"], gains["ffn2_pre_g"], t, d, f, tm,
        deps=[dw_ple_gate, dw_ple_proj_t, dwd2, dwg2], before=(m_out, gains["mix_post_g"], 1.0))

    dyab, dz = _mix_dy(dm, full["w_o"], z, ya, yb, t, d, tm)
    dw_o = _matmul("mix_dwo", [dict(a=y, b=dm, mode="tn", acc=0, tk=tk_t)], m=d, n=d, tm=d, tn=d, nk=t // tk_t,
                   outs=[("mn", (d, d), BF, 0)], epilogue=ident, n_acc=1)[0]
    dw_out_a, dw_out_b = _matmul(
        "mix_dwab",
        [dict(a=a_br, b=dyab, mode="tn", acc=0, tk=tk_t),
         dict(a=b_br, b=dyab, mode="tn", acc=1, tk=tk_t, b_off=(0, 1))],
        m=d, n=d, tm=d, tn=d, nk=t // tk_t, outs=[("mn", (d, d), BF, 0)] * 2, epilogue=ident, n_acc=2)
    da, db = _matmul(
        "mix_dab",
        [dict(a=dyab, b=full["w_out_a"], mode="nt", acc=0, tk=d),
         dict(a=dyab, b=full["w_out_b"], mode="nt", acc=1, tk=d, a_off=(0, 1))],
        m=t, n=d, tm=tm, tn=d, outs=[("mn", (t, d), BF, 0)] * 2, epilogue=ident, n_acc=2)
    dz, dsgu_w, dsgu_b, dg_sgu = _sgu_bwd(z, da, dz, gains["sgu_norm_g"], w["sgu_w"], b_col, t, d, _tile(t, 256, CHUNK))
    dz, dpool_w, dg_pool = _pool_bwd(z, db, dz, pool_full, gains["pool_scale"], t, d)
    dw_in_t = _matmul(
        "mix_dwin", [dict(a=dz, b=xn2, mode="tn", acc=0, tk=tk_t)], m=d_in, n=d, tm=d, tn=d,
        nk=t // tk_t, outs=[("mn", (d_in, d), BF, 0)], epilogue=ident, n_acc=1)[0]
    dpool_rows = dpool_w.reshape(n_pool, N_DEV, rows_pw, dgp).transpose(1, 0, 2, 3).reshape(N_DEV * n_pool * rows_pw, dgp)
    scatter(["w_o", "w_out_a", "w_out_b", "pool_w", "w_in"], [dw_o, dw_out_a, dw_out_b, dpool_rows, dw_in_t])
    dh1, dg_mix_pre, df1, dg_f1_post = _matmul(
        "mix_dx", [dict(a=dz, b=full["w_in"], mode="nn", acc=0, tk=d_in // 2)], m=t, n=d, tm=tm, tn=d, nk=2,
        extras=_dx_extras(h1, dh2, gains["mix_pre_g"], f1, gains["ffn1_post_g"]),
        outs=_dx_outs(t, d), epilogue=_dx_epi(0.5), n_acc=1, deps=[dw_o, dw_out_a, dpool_rows, dw_in_t])

    dgate1, dup1 = _ffn_dhid("ffn1", df1, full["ffn1_w_down"], g1, u1, t, d, f, tm)
    dwd1 = _ffn_dwd("ffn1", hid1, df1, t, d, f)[0]
    scatter(["ffn1_w_down"], [dwd1])
    dwg1, dwu1 = _ffn_dwgu("ffn1", dgate1, dup1, xn1, t, d, f, deps=[dwd1])
    scatter(["ffn1_w_gate", "ffn1_w_up"], [dwg1, dwu1])
    grad_x, dg_f1_pre = _ffn_dx("ffn1", dh1, xs, dgate1, dup1, full["ffn1_w_gate"], full["ffn1_w_up"],
                                gains["ffn1_pre_g"], t, d, f, tm, deps=[dwg1])

    grad, delta, new_m, new_v = {}, {}, {}, {}

    def update(group, partial_sums, behind):
        done = []
        for k, st in zip(group, partial_sums):
            if k in col_sharded and w[k].shape[1] % LANES:
                res = _reduce_adam("adam_" + k, st, w[k].T, mom[k].T, var[k].T, deps=behind)
                grad[k], delta[k], new_m[k], new_v[k] = [r.T for r in res]
                done.append(res[1])
                continue
            elif k in col_sharded:
                grad[k] = _reduce_only("sum_" + k, st, deps=behind).T
                delta[k], new_m[k], new_v[k] = _adam_only("adam_" + k, w[k], grad[k], mom[k], var[k])
            else:
                shape = w[k].shape
                flat = (-1, shape[-1])
                res = _reduce_adam("adam_" + k, st, w[k].reshape(flat), mom[k].reshape(flat), var[k].reshape(flat),
                                   deps=behind)
                grad[k], delta[k], new_m[k], new_v[k] = [r.reshape(shape) for r in res]
            done.append(delta[k])
        return done

    gain_grads = {
        "ffn1_pre_g": dg_f1_pre, "ffn1_post_g": dg_f1_post, "mix_pre_g": dg_mix_pre, "sgu_norm_g": dg_sgu,
        "pool_scale": dg_pool, "mix_post_g": dg_mix_post, "ffn2_pre_g": dg_f2_pre, "ffn2_post_g": dg_f2_post,
        "ple_pre_g": dg_ple_pre, "ple_post_g": dg_ple_post,
    }
    gain_names = list(gain_grads)
    n_mix = N_SGU_GROUPS * CHUNK
    wide, wide_rows = _stack_rows([gain_grads[k] for k in gain_names] + [loss_vec * (0.5 / d)])
    narrow, narrow_rows = _stack_rows([dsgu_b.reshape(N_SGU_GROUPS, CHUNK), dsgu_w.reshape(n_mix, CHUNK)])
    plan_s, local_s = _broadcast_plan(2)
    stacks = _sc_exchange(
        "small", n_gather + len(staged), [wide, narrow],
        [jax.ShapeDtypeStruct((N_DEV,) + wide.shape, F32), jax.ShapeDtypeStruct((N_DEV,) + narrow.shape, F32)],
        plan_s, 2 * (N_DEV - 1), _everyone, local=local_s)
    done = [grad_x]
    for keys, partial_sums in staged:
        done = update(keys, partial_sums, done)
    rep_names = gain_names + ["sgu_b", "sgu_w"]
    rep_shape = {k: (1, d) for k in gain_names}
    rep_shape.update(sgu_b=(N_SGU_GROUPS, CHUNK), sgu_w=(n_mix, CHUNK))
    layout = [(0, r) for r in wide_rows[:-1]] + [(1, narrow_rows[0]), (1, narrow_rows[1]), (0, wide_rows[-1])]
    res, loss = _adam_replicated(
        "adam_replicated", stacks, layout,
        [tuple(src[k].reshape(rep_shape[k]) for src in (w, mom, var)) for k in rep_names], deps=done)
    for k, quad in zip(rep_names, res):
        for dst, val in zip((grad, delta, new_m, new_v), quad):
            dst[k] = val.reshape(w[k].shape)

    out = [loss[0, 0], grad_x[None]]
    for group in (grad, delta, new_m, new_v):
        out += [group[k][None] for k in names]
    return tuple(out)
```

```python
import math

import jax
import jax.numpy as jnp
from jax import lax
from jax.experimental import pallas as pl
from jax.experimental.pallas import tpu as pltpu
from jax.experimental.pallas import tpu_sc as plsc

EPS = 1e-6
CHUNK = 128
N_SGU_GROUPS = 4
POOL_WINDOWS = (2, 4, 8, 16)
ADAM_LR = 0.001
ADAM_B1 = 0.9
ADAM_B2 = 0.999
ADAM_EPS = 1e-08
ADAM_WD = 0.01
ADAM_STEP = 10

N_DEV = 8
MESH = pl.DeviceIdType.MESH
V7X_VMEM_BYTES = 64 * 1024 * 1024
VMEM_LIMIT = V7X_VMEM_BYTES - 8 * 1024 * 1024
TOKEN_K_TILE = 1024
LANES = 128
UPDATE_ROWS = 256
BF = jnp.bfloat16
F32 = jnp.float32

_DN = {
    "nn": (((1,), (0,)), ((), ())),
    "nt": (((1,), (1,)), ((), ())),
    "tn": (((0,), (0,)), ((), ())),
}


def _params(sem=None):
    return pltpu.CompilerParams(dimension_semantics=sem, vmem_limit_bytes=VMEM_LIMIT)


def _tile(n, target, align=128):
    t = min(n, target)
    t -= t % align
    while t >= align:
        if n % t == 0:
            return t
        t -= align
    return n


def _sig(x):
    return 0.5 + 0.5 * jnp.tanh(0.5 * x)


_GELU_K = math.sqrt(2.0 / math.pi)
_GELU_C = 0.044715


def _gelu(x):
    return 0.5 * x * (1.0 + jnp.tanh(_GELU_K * (x + _GELU_C * x * x * x)))


def _gelu_grad(x):
    t = jnp.tanh(_GELU_K * (x + _GELU_C * x * x * x))
    return 0.5 * (1.0 + t) + 0.5 * x * (1.0 - t * t) * _GELU_K * (1.0 + 3.0 * _GELU_C * x * x)


def _rms_r(x):
    return lax.rsqrt(jnp.mean(x * x, axis=-1, keepdims=True) + EPS)


def _rms_bwd(x, g, dy):
    r = _rms_r(x)
    xh = x * r
    gy = dy * g
    dx = r * (gy - xh * jnp.mean(xh * gy, axis=-1, keepdims=True))
    return dx, jnp.sum(dy * xh, axis=0, keepdims=True)


def _matmul(name, mats, *, m, n, tm, tn, nk=1, extras=(), outs, epilogue, n_acc, j_outer=False, deps=()):
    ni, nj = m // tm, n // tn
    assert ni * tm == m and nj * tn == n, (name, m, n, tm, tn)
    if j_outer:
        grid = (nj, ni, nk)

        def ij(g0, g1):
            return g1, g0
    else:
        grid = (ni, nj, nk)

        def ij(g0, g1):
            return g0, g1

    in_specs, args = [], []
    for mt in mats:
        mode, tk = mt["mode"], mt["tk"]
        ao, bo = mt.get("a_off", (0, 0)), mt.get("b_off", (0, 0))
        if mode == "tn":
            sa = pl.BlockSpec((tk, tm), lambda g0, g1, kk, ao=ao: (ao[0] + kk, ao[1] + ij(g0, g1)[0]))
        else:
            sa = pl.BlockSpec((tm, tk), lambda g0, g1, kk, ao=ao: (ao[0] + ij(g0, g1)[0], ao[1] + kk))
        if mode == "nt":
            sb = pl.BlockSpec((tn, tk), lambda g0, g1, kk, bo=bo: (bo[0] + ij(g0, g1)[1], bo[1] + kk))
        else:
            sb = pl.BlockSpec((tk, tn), lambda g0, g1, kk, bo=bo: (bo[0] + kk, bo[1] + ij(g0, g1)[1]))
        in_specs += [sa, sb]
        args += [mt["a"], mt["b"]]
    n_mat_refs = len(args)
    for ex in extras:
        if ex[0] == "mn":
            in_specs.append(pl.BlockSpec((tm, tn), lambda g0, g1, kk, c=ex[2]: (ij(g0, g1)[0], c + ij(g0, g1)[1])))
        else:
            in_specs.append(pl.BlockSpec((1, tn), lambda g0, g1, kk: (0, ij(g0, g1)[1])))
        args.append(ex[1])
    n_ex_end = len(args)
    in_specs += [pl.BlockSpec(memory_space=pl.ANY)] * len(deps)
    args += list(deps)
    n_in = len(args)
    out_specs, out_shape = [], []
    for o in outs:
        if o[0] == "mn":
            out_specs.append(pl.BlockSpec((tm, tn), lambda g0, g1, kk, c=o[3]: (ij(g0, g1)[0], c + ij(g0, g1)[1])))
        else:
            assert nj == 1, name
            out_specs.append(pl.BlockSpec((1, tn), lambda g0, g1, kk: (0, 0)))
        out_shape.append(jax.ShapeDtypeStruct(o[1], o[2]))
    n_out = len(outs)

    def body(*refs):
        mat_refs = refs[:n_mat_refs]
        ex_refs = refs[n_mat_refs:n_ex_end]
        out_refs = refs[n_in:n_in + n_out]
        acc_refs = refs[n_in + n_out:]
        i = ij(pl.program_id(0), pl.program_id(1))[0]
        kk = pl.program_id(2)

        def products():
            res = [None] * n_acc
            for idx, mt in enumerate(mats):
                a = mat_refs[2 * idx][...].astype(BF)
                b = mat_refs[2 * idx + 1][...].astype(BF)
                p = lax.dot_general(a, b, _DN[mt["mode"]], preferred_element_type=F32)
                q = mt["acc"]
                res[q] = p if res[q] is None else res[q] + p
            return res

        def finish(accs):
            vals = epilogue(accs, [r[...] for r in ex_refs])
            for o, r, v in zip(outs, out_refs, vals):
                if o[0] == "mn":
                    r[...] = v.astype(o[2])
                else:
                    @pl.when(i == 0)
                    def _():
                        r[...] = v

                    @pl.when(i != 0)
                    def _():
                        r[...] += v

        if nk == 1:
            finish(products())
        else:
            res = products()

            @pl.when(kk == 0)
            def _():
                for q in range(n_acc):
                    acc_refs[q][...] = res[q]

            @pl.when(kk != 0)
            def _():
                for q in range(n_acc):
                    acc_refs[q][...] += res[q]

            @pl.when(kk == nk - 1)
            def _():
                finish([r[...] for r in acc_refs])

    scratch = [pltpu.VMEM((tm, tn), F32) for _ in range(n_acc)] if nk > 1 else []
    return pl.pallas_call(
        body,
        name=name,
        grid=grid,
        in_specs=in_specs,
        out_specs=out_specs,
        out_shape=out_shape,
        scratch_shapes=scratch,
        compiler_params=_params(("arbitrary", "arbitrary", "arbitrary")),
    )(*args)


def _rowwise(name, ins, outs, fn, *, t, tm, deps=()):
    ni = t // tm
    assert ni * tm == t, (name, t, tm)
    in_specs, args = [], []
    for s in ins:
        if s[0] == "row":
            in_specs.append(pl.BlockSpec((tm, s[2]), lambda i, c=s[3]: (i, c)))
        else:
            nd = s[1].ndim
            in_specs.append(pl.BlockSpec(s[1].shape, lambda i, nd=nd: (0,) * nd))
        args.append(s[1])
    out_specs, out_shape = [], []
    for o in outs:
        if o[0] == "row":
            out_specs.append(pl.BlockSpec((tm, o[3]), lambda i, c=o[4]: (i, c)))
        else:
            nd = len(o[1])
            out_specs.append(pl.BlockSpec(o[1], lambda i, nd=nd: (0,) * nd))
        out_shape.append(jax.ShapeDtypeStruct(o[1], o[2]))
    n_read = len(args)
    in_specs += [pl.BlockSpec(memory_space=pl.ANY)] * len(deps)
    args += list(deps)
    n_in = len(args)

    def body(*refs):
        i = pl.program_id(0)
        vals = fn([r[...] for r in refs[:n_read]])
        for o, r, v in zip(outs, refs[n_in:], vals):
            if o[0] == "row":
                r[...] = v.astype(o[2])
            else:
                @pl.when(i == 0)
                def _():
                    r[...] = v

                @pl.when(i != 0)
                def _():
                    r[...] += v

    return pl.pallas_call(
        body,
        name=name,
        grid=(ni,),
        in_specs=in_specs,
        out_specs=out_specs,
        out_shape=out_shape,
        compiler_params=_params(("arbitrary",)),
    )(*args)


def _place():
    return lax.axis_index("x"), lax.axis_index("y"), lax.axis_index("c")


def _remote(src, dst, send_sems, recv_sems, k, to):
    return pltpu.make_async_remote_copy(
        src_ref=src, dst_ref=dst, send_sem=send_sems.at[k], recv_sem=recv_sems.at[k],
        device_id=to, device_id_type=MESH)


def _chips(x, y):
    return [(1 - x, y), (x, 1 - y), (1 - x, 1 - y)]


def _broadcast_plan(n):
    def plan(refs, send_sems, recv_sems):
        x, y, c = _place()
        cps = []
        for i in range(n):
            for dd in range(1, N_DEV):
                peer = (1 - x if dd & 4 else x, 1 - y if dd & 2 else y, 1 - c if dd & 1 else c)
                cps.append(_remote(refs[i], refs[n + i].at[4 * x + 2 * y + c], send_sems, recv_sems, 7 * i + dd - 1, peer))
        return cps

    def local(refs):
        x, y, c = _place()
        return [(refs[i], refs[n + i].at[4 * x + 2 * y + c]) for i in range(n)]

    return plan, local


def _handshake(peers):
    barrier = pltpu.get_barrier_semaphore()
    for peer in peers:
        pl.semaphore_signal(barrier, inc=1, device_id=peer, device_id_type=MESH)
    pl.semaphore_wait(barrier, len(peers))


def _sequencer_call(name, collective_id, body, args, out_type, scratch_types):
    return pl.kernel(
        body,
        out_type=out_type,
        mesh=plsc.ScalarSubcoreMesh(axis_name="sequencer", num_cores=1),
        scratch_types=scratch_types,
        compiler_params=pltpu.CompilerParams(collective_id=collective_id),
        name=name,
    )(*args)


def _sc_gather(name, collective_id, shards):
    n = len(shards)

    def body(*refs):
        ins, outs = refs[:n], refs[n:2 * n]
        send_sems, recv_sems, local_sems = refs[2 * n:]
        x, y, c = _place()
        me, sibling = (x, y, c), (x, y, 1 - c)
        chips = _chips(x, y)
        _handshake([sibling] + [(*chip, c) for chip in chips])

        def slot(i, px, py, pc):
            return outs[i].at[4 * px + 2 * py + pc]

        def copy(i, k, block, to, src=None):
            return _remote(slot(i, *block) if src is None else src, slot(i, *block), send_sems, recv_sems, 7 * i + k, to)

        mine = [pltpu.make_async_copy(ins[i], slot(i, *me), local_sems.at[i]) for i in range(n)]
        for cp in mine:
            cp.start()
        first = []
        for i in range(n):
            first += [copy(i, 1 + j, me, (*chip, c), src=ins[i]) for j, chip in enumerate(chips)]
            first.append(copy(i, 0, me, sibling, src=ins[i]))
        for cp in first:
            cp.start()
        passed = []
        for i in range(n):
            for j, chip in enumerate(chips):
                copy(i, 1 + j, (*chip, c), me).wait_recv()
                fwd = copy(i, 4 + j, (*chip, c), sibling)
                fwd.start()
                passed.append(fwd)
        for i in range(n):
            copy(i, 0, sibling, me).wait_recv()
            for j, chip in enumerate(chips):
                copy(i, 4 + j, (*chip, 1 - c), me).wait_recv()
        for cp in first + passed:
            cp.wait_send()
        for cp in mine:
            cp.wait()

    outs = _sequencer_call(
        name, collective_id, body, shards,
        [jax.ShapeDtypeStruct((N_DEV,) + s.shape, s.dtype) for s in shards],
        [pltpu.SemaphoreType.DMA((7 * n,)), pltpu.SemaphoreType.DMA((7 * n,)), pltpu.SemaphoreType.DMA((n,))])
    return [o.reshape((N_DEV * s.shape[0],) + s.shape[1:]) for o, s in zip(outs, shards)]


def _sc_exchange(name, collective_id, srcs, land_shapes, plan, n_copies, peers, local=None):
    n = len(srcs)

    def body(*refs):
        bufs, send_sems, recv_sems = refs[:2 * n], refs[2 * n], refs[2 * n + 1]
        _handshake(peers())
        if local is not None:
            mine = [pltpu.make_async_copy(a, b, refs[2 * n + 2].at[k]) for k, (a, b) in enumerate(local(bufs))]
            for cp in mine:
                cp.start()
        cps = plan(bufs, send_sems, recv_sems)
        for cp in cps:
            cp.start()
        for cp in cps:
            cp.wait_send()
            cp.wait_recv()
        if local is not None:
            for cp in mine:
                cp.wait()

    scratch = [pltpu.SemaphoreType.DMA((n_copies,)), pltpu.SemaphoreType.DMA((n_copies,))]
    if local is not None:
        scratch.append(pltpu.SemaphoreType.DMA((n,)))
    return list(_sequencer_call(name, collective_id, body, srcs, land_shapes, scratch))


def _adam_vals(w, g, m, v):
    m = ADAM_B1 * m + (1.0 - ADAM_B1) * g
    v = ADAM_B2 * v + (1.0 - ADAM_B2) * (g * g)
    m_hat = m / (1.0 - ADAM_B1 ** ADAM_STEP)
    v_hat = v / (1.0 - ADAM_B2 ** ADAM_STEP)
    delta = -ADAM_LR * (m_hat / (jnp.sqrt(v_hat) + ADAM_EPS) + ADAM_WD * w)
    return delta, m, v


def _sum_parts(st_ref):
    g = st_ref[0].astype(F32)
    for k in range(1, st_ref.shape[0]):
        g = g + st_ref[k].astype(F32)
    return g


def _reduce_adam(name, st, w, m, v, deps=()):
    rows, cols = w.shape
    tr = _tile(rows, UPDATE_ROWS, 16)

    def body(st_ref, w_ref, m_ref, v_ref, *rest):
        g_out, d_out, m_out, v_out = rest[len(deps):]
        g = _sum_parts(st_ref)
        d, mm, vv = _adam_vals(w_ref[...], g, m_ref[...], v_ref[...])
        g_out[...] = g
        d_out[...] = d
        m_out[...] = mm
        v_out[...] = vv

    blk = pl.BlockSpec((tr, cols), lambda i: (i, 0))
    return pl.pallas_call(
        body,
        name=name,
        grid=(rows // tr,),
        in_specs=[pl.BlockSpec((st.shape[0], tr, cols), lambda i: (0, i, 0)), blk, blk, blk]
        + [pl.BlockSpec(memory_space=pl.ANY)] * len(deps),
        out_specs=[blk] * 4,
        out_shape=[jax.ShapeDtypeStruct(w.shape, F32)] * 4,
        compiler_params=_params(("arbitrary",)),
    )(st, w, m, v, *deps)


def _reduce_only(name, st, deps=()):
    _, rows, cols = st.shape
    tr = _tile(rows, UPDATE_ROWS, 16)

    def body(st_ref, *rest):
        rest[-1][...] = _sum_parts(st_ref)

    return pl.pallas_call(
        body,
        name=name,
        grid=(rows // tr,),
        in_specs=[pl.BlockSpec((st.shape[0], tr, cols), lambda i: (0, i, 0))]
        + [pl.BlockSpec(memory_space=pl.ANY)] * len(deps),
        out_specs=pl.BlockSpec((tr, cols), lambda i: (i, 0)),
        out_shape=jax.ShapeDtypeStruct((rows, cols), F32),
        compiler_params=_params(("arbitrary",)),
    )(st, *deps)


def _adam_only(name, w, g, m, v):
    rows, cols = w.shape
    tr = _tile(rows, UPDATE_ROWS, 16)

    def body(w_ref, g_ref, m_ref, v_ref, d_out, m_out, v_out):
        d, mm, vv = _adam_vals(w_ref[...], g_ref[...], m_ref[...], v_ref[...])
        d_out[...] = d
        m_out[...] = mm
        v_out[...] = vv

    blk = pl.BlockSpec((tr, cols), lambda i: (i, 0))
    return pl.pallas_call(
        body,
        name=name,
        grid=(rows // tr,),
        in_specs=[blk] * 4,
        out_specs=[blk] * 3,
        out_shape=[jax.ShapeDtypeStruct(w.shape, F32)] * 3,
        compiler_params=_params(("arbitrary",)),
    )(w, g, m, v)


def _prenorm(name, h, g, t, d, tm, deps=()):
    def fn(v):
        x, gg = v
        return [x * _rms_r(x) * gg]

    return _rowwise(name, [("row", h, d, 0), ("full", g)], [("row", (t, d), BF, d, 0)], fn, t=t, tm=tm, deps=deps)[0]


def _ffn_up(tag, xn, wgt, wut, t, d, f, tm):
    def up_epi(accs, ex):
        gg, uu = accs
        return [gg, uu, gg * _sig(gg) * uu]

    mats = [dict(a=xn, b=wgt, mode="nt", acc=0, tk=d), dict(a=xn, b=wut, mode="nt", acc=1, tk=d)]
    return _matmul(
        tag + "_up", mats, m=t, n=f, tm=tm, tn=_tile(f, 1408),
        outs=[("mn", (t, f), BF, 0)] * 3, epilogue=up_epi, n_acc=2, j_outer=True)


def _ffn_up_split(tag, xn, wgt, wut, t, d, f, tm):
    tf = _tile(f, 1408)
    gate = _matmul(
        tag + "_gate", [dict(a=xn, b=wgt, mode="nt", acc=0, tk=d)], m=t, n=f, tm=tm, tn=tf,
        outs=[("mn", (t, f), BF, 0)], epilogue=lambda accs, ex: accs, n_acc=1, j_outer=True)[0]

    def up_epi(accs, ex):
        gg = ex[0].astype(F32)
        return [accs[0], gg * _sig(gg) * accs[0]]

    up, hid = _matmul(
        tag + "_up", [dict(a=xn, b=wut, mode="nt", acc=0, tk=d)], m=t, n=f, tm=tm, tn=tf, extras=[("mn", gate, 0)],
        outs=[("mn", (t, f), BF, 0)] * 2, epilogue=up_epi, n_acc=1, j_outer=True)
    return gate, up, hid


def _ffn_down(tag, hid, wd, h, g_post, g_next, t, d, f, tm):
    def down_epi(accs, ex):
        ff = accs[0]
        hh, gg, gn = ex
        h_new = hh + 0.5 * ff * _rms_r(ff) * gg
        return [ff, h_new, h_new * _rms_r(h_new) * gn]

    return _matmul(
        tag + "_down", [dict(a=hid, b=wd, mode="nn", acc=0, tk=f)], m=t, n=d, tm=tm, tn=d,
        extras=[("mn", h, 0), ("n", g_post), ("n", g_next)],
        outs=[("mn", (t, d), F32, 0)] * 2 + [("mn", (t, d), BF, 0)], epilogue=down_epi, n_acc=1)


def _prenorm_bwd_epi(accs, ex):
    hh, dh, gg = ex
    dx, dg = _rms_bwd(hh, gg, accs[0])
    return [dh + dx, dg]


def _dx_extras(h_in, dh, g_pre, fo, g_post):
    return [("mn", h_in, 0), ("mn", dh, 0), ("n", g_pre), ("mn", fo, 0), ("n", g_post)]


def _dx_outs(t, d):
    return [("mn", (t, d), F32, 0), ("acc", (1, d), F32), ("mn", (t, d), BF, 0), ("acc", (1, d), F32)]


def _dx_epi(scale):
    def epi(accs, ex):
        hh, dh, gg, fo, gp = ex
        dx, dg = _rms_bwd(hh, gg, accs[0])
        dh_new = dh + dx
        dfo, dgp = _rms_bwd(fo, gp, dh_new * scale)
        return [dh_new, dg, dfo, dgp]

    return epi


def _ffn_dhid(tag, df, wd, gate, up, t, d, f, tm):
    def hid_epi(accs, ex):
        dhid = accs[0]
        gg, uu = ex[0].astype(F32), ex[1].astype(F32)
        s = _sig(gg)
        gs = gg * s
        return [dhid * uu * (s + gs * (1.0 - s)), dhid * gs]

    return _matmul(
        tag + "_dhid", [dict(a=df, b=wd, mode="nt", acc=0, tk=d)], m=t, n=f, tm=tm, tn=_tile(f, 1408),
        extras=[("mn", gate, 0), ("mn", up, 0)],
        outs=[("mn", (t, f), BF, 0)] * 2, epilogue=hid_epi, n_acc=1, j_outer=True)


def _ffn_dwd(tag, hid, df, t, d, f):
    tk_t = _tile(t, 2 * TOKEN_K_TILE)
    return _matmul(
        tag + "_dwd", [dict(a=hid, b=df, mode="tn", acc=0, tk=tk_t)], m=f, n=d, tm=_tile(f, 1408), tn=d,
        nk=t // tk_t, outs=[("mn", (f, d), BF, 0)], epilogue=lambda accs, ex: accs, n_acc=1)


def _ffn_dw_in(name, dact, xn, t, d, f, deps=()):
    tk_t = _tile(t, 2 * TOKEN_K_TILE)
    return _matmul(
        name, [dict(a=dact, b=xn, mode="tn", acc=0, tk=tk_t)], m=f, n=d, tm=_tile(f, 1408), tn=d, nk=t // tk_t,
        outs=[("mn", (f, d), BF, 0)], epilogue=lambda accs, ex: accs, n_acc=1, deps=deps)[0]


def _ffn_dx(tag, dh, h_in, dgate, dup, wgt, wut, g_pre, t, d, f, tm, deps=(), before=None):
    tf = _tile(f, 1408)
    mats = [dict(a=dgate, b=wgt, mode="nn", acc=0, tk=tf), dict(a=dup, b=wut, mode="nn", acc=0, tk=tf)]
    if before is None:
        return _matmul(
            tag + "_dx", mats, m=t, n=d, tm=tm, tn=d, nk=f // tf,
            extras=[("mn", h_in, 0), ("mn", dh, 0), ("n", g_pre)],
            outs=[("mn", (t, d), F32, 0), ("acc", (1, d), F32)], epilogue=_prenorm_bwd_epi, n_acc=1, deps=deps)
    fo, g_post, scale = before
    return _matmul(
        tag + "_dx", mats, m=t, n=d, tm=tm, tn=d, nk=f // tf, extras=_dx_extras(h_in, dh, g_pre, fo, g_post),
        outs=_dx_outs(t, d), epilogue=_dx_epi(scale), n_acc=1, deps=deps)


def _causal_mask():
    r = lax.broadcasted_iota(jnp.int32, (CHUNK, CHUNK), 0)
    c = lax.broadcasted_iota(jnp.int32, (CHUNK, CHUNK), 1)
    return r >= c


def _layernorm_parts(v):
    mu = jnp.mean(v, axis=-1, keepdims=True)
    vc = v - mu
    rstd = lax.rsqrt(jnp.mean(vc * vc, axis=-1, keepdims=True) + EPS)
    return vc * rstd, rstd


def _sgu_fwd(z, g_sgu, w_s, b_col, t, d, tm):
    dg = d // N_SGU_GROUPS
    n_chunk = tm // CHUNK

    def body(zu_ref, zv_ref, g_ref, w_ref, b_ref, a_ref):
        u = _gelu(zu_ref[...].astype(F32))
        vhat, _ = _layernorm_parts(_gelu(zv_ref[...].astype(F32)))
        vn = (vhat * g_ref[...]).astype(BF)
        mask = _causal_mask()
        for gi in range(N_SGU_GROUPS):
            ws = jnp.where(mask, w_ref[gi], 0.0).astype(BF)
            bias = b_ref[gi]
            for ci in range(n_chunk):
                rows, cols = slice(ci * CHUNK, (ci + 1) * CHUNK), slice(gi * dg, (gi + 1) * dg)
                sv = jnp.dot(ws, vn[rows, cols], preferred_element_type=F32) + bias
                a_ref[rows, cols] = (u[rows, cols] * sv).astype(BF)

    return pl.pallas_call(
        body,
        name="sgu_fwd",
        grid=(t // tm,),
        in_specs=[
            pl.BlockSpec((tm, d), lambda i: (i, 0)),
            pl.BlockSpec((tm, d), lambda i: (i, 1)),
            pl.BlockSpec((1, d), lambda i: (0, 0)),
            pl.BlockSpec(w_s.shape, lambda i: (0, 0, 0)),
            pl.BlockSpec(b_col.shape, lambda i: (0, 0, 0)),
        ],
        out_specs=pl.BlockSpec((tm, d), lambda i: (i, 0)),
        out_shape=jax.ShapeDtypeStruct((t, d), BF),
        compiler_params=_params(("arbitrary",)),
    )(z, z, g_sgu, w_s, b_col)


def _sgu_bwd(z, da, dz, g_sgu, w_s, b_col, t, d, tm):
    dg = d // N_SGU_GROUPS
    n_chunk = tm // CHUNK

    def body(zu_ref, zv_ref, da_ref, dz_in, g_ref, w_ref, b_ref, dz_ref, dw_ref, db_ref, dgn_ref, dvn_ref):
        del dz_in
        dzu_ref, dzv_ref = dz_ref.at[:, pl.ds(0, d)], dz_ref.at[:, pl.ds(d, d)]
        i = pl.program_id(0)
        zu, zv = zu_ref[...].astype(F32), zv_ref[...].astype(F32)
        u = _gelu(zu)
        vhat, rstd = _layernorm_parts(_gelu(zv))
        gn = g_ref[...]
        vn = (vhat * gn).astype(BF)
        da_v = da_ref[...].astype(F32)
        dsv_all = (da_v * u).astype(BF)
        mask = _causal_mask()
        for gi in range(N_SGU_GROUPS):
            ws = jnp.where(mask, w_ref[gi], 0.0).astype(BF)
            bias = b_ref[gi]
            dw = jnp.zeros((CHUNK, CHUNK), F32)
            dbias = jnp.zeros((CHUNK, 1), F32)
            for ci in range(n_chunk):
                rows, cols = slice(ci * CHUNK, (ci + 1) * CHUNK), slice(gi * dg, (gi + 1) * dg)
                vn_c, dsv = vn[rows, cols], dsv_all[rows, cols]
                sv = jnp.dot(ws, vn_c, preferred_element_type=F32) + bias
                dzu_ref[rows, cols] = (da_v[rows, cols] * sv * _gelu_grad(zu[rows, cols])).astype(BF)
                dw = dw + lax.dot_general(dsv, vn_c, _DN["nt"], preferred_element_type=F32)
                dbias = dbias + jnp.sum(dsv.astype(F32), axis=1, keepdims=True)
                dvn_ref[rows, cols] = lax.dot_general(ws, dsv, _DN["tn"], preferred_element_type=F32)
            dw = jnp.where(mask, dw, 0.0)

            @pl.when(i == 0)
            def _():
                dw_ref[gi] = dw
                db_ref[gi] = dbias

            @pl.when(i != 0)
            def _():
                dw_ref[gi] += dw
                db_ref[gi] += dbias

        dvn = dvn_ref[...]
        dgn = jnp.sum(dvn * vhat, axis=0, keepdims=True)

        @pl.when(i == 0)
        def _():
            dgn_ref[...] = dgn

        @pl.when(i != 0)
        def _():
            dgn_ref[...] += dgn

        dvh = dvn * gn
        dv = rstd * (dvh - jnp.mean(dvh, axis=-1, keepdims=True) - vhat * jnp.mean(dvh * vhat, axis=-1, keepdims=True))
        dzv_ref[...] = (dv * _gelu_grad(zv)).astype(BF)

    return pl.pallas_call(
        body,
        name="sgu_bwd",
        grid=(t // tm,),
        in_specs=[
            pl.BlockSpec((tm, d), lambda i: (i, 0)),
            pl.BlockSpec((tm, d), lambda i: (i, 1)),
            pl.BlockSpec((tm, d), lambda i: (i, 0)),
            pl.BlockSpec(memory_space=pl.ANY),
            pl.BlockSpec((1, d), lambda i: (0, 0)),
            pl.BlockSpec(w_s.shape, lambda i: (0, 0, 0)),
            pl.BlockSpec(b_col.shape, lambda i: (0, 0, 0)),
        ],
        out_specs=[
            pl.BlockSpec((tm, 2 * d), lambda i: (i, 0)),
            pl.BlockSpec(w_s.shape, lambda i: (0, 0, 0)),
            pl.BlockSpec(b_col.shape, lambda i: (0, 0, 0)),
            pl.BlockSpec((1, d), lambda i: (0, 0)),
        ],
        out_shape=[
            jax.ShapeDtypeStruct(dz.shape, BF),
            jax.ShapeDtypeStruct(w_s.shape, F32),
            jax.ShapeDtypeStruct(b_col.shape, F32),
            jax.ShapeDtypeStruct((1, d), F32),
        ],
        scratch_shapes=[pltpu.VMEM((tm, d), F32)],
        input_output_aliases={3: 0},
        compiler_params=_params(("arbitrary",)),
    )(z, z, da, dz, g_sgu, w_s, b_col)


def _shift_down(x, k, row):
    return jnp.where(row >= k, pltpu.roll(x, k, 0), 0.0)


def _shift_up(x, k, row, t):
    return jnp.where(row < t - k, pltpu.roll(x, t - k, 0), 0.0)


def _doublings(window):
    steps = int(math.log2(window))
    assert 2 ** steps == window
    return [2 ** s for s in range(steps)]


def _pool_diff(c, window, row):
    s = c
    for k in _doublings(window):
        s = s + _shift_down(s, k, row)
    count = jnp.minimum(row + 1, window).astype(F32)
    return s / count - c, count


def _pool_fwd(z, pool_w, pool_scale, t, d):
    dgp = d // len(POOL_WINDOWS)
    cblk = (2 * d) // dgp

    def body(zc_ref, w_ref, s_ref, b_ref):
        row = lax.broadcasted_iota(jnp.int32, (t, 1), 0)
        for gi, window in enumerate(POOL_WINDOWS):
            @pl.when(pl.program_id(0) == gi)
            def _(window=window):
                diff, _ = _pool_diff(zc_ref[...].astype(F32), window, row)
                out = jnp.dot(diff.astype(BF), w_ref[...], preferred_element_type=F32)
                b_ref[...] = (out * s_ref[...]).astype(BF)

    return pl.pallas_call(
        body,
        name="pool_fwd",
        grid=(len(POOL_WINDOWS),),
        in_specs=[
            pl.BlockSpec((t, dgp), lambda g: (0, cblk + g)),
            pl.BlockSpec((None, dgp, dgp), lambda g: (g, 0, 0)),
            pl.BlockSpec((1, dgp), lambda g: (0, g)),
        ],
        out_specs=pl.BlockSpec((t, dgp), lambda g: (0, g)),
        out_shape=jax.ShapeDtypeStruct((t, d), BF),
        compiler_params=_params(("arbitrary",)),
    )(z, pool_w, pool_scale)


def _pool_bwd(z, db, dz, pool_w, pool_scale, t, d):
    dgp = d // len(POOL_WINDOWS)
    cblk = (2 * d) // dgp

    def body(zc_ref, db_ref, dz_in, w_ref, s_ref, dzc_ref, dw_ref, ds_ref):
        del dz_in
        row = lax.broadcasted_iota(jnp.int32, (t, 1), 0)
        for gi, window in enumerate(POOL_WINDOWS):
            @pl.when(pl.program_id(0) == gi)
            def _(window=window):
                diff, count = _pool_diff(zc_ref[...].astype(F32), window, row)
                diff = diff.astype(BF)
                w = w_ref[...]
                dbv = db_ref[...].astype(F32)
                out = jnp.dot(diff, w, preferred_element_type=F32)
                ds_ref[...] = jnp.sum(dbv * out, axis=0, keepdims=True)
                dout = (dbv * s_ref[...]).astype(BF)
                dw_ref[...] = lax.dot_general(diff, dout, _DN["tn"], preferred_element_type=F32).astype(BF)
                ddiff = lax.dot_general(dout, w, _DN["nt"], preferred_element_type=F32)
                s = ddiff / count
                for k in _doublings(window):
                    s = s + _shift_up(s, k, row, t)
                dzc_ref[...] = (s - ddiff).astype(BF)

    return pl.pallas_call(
        body,
        name="pool_bwd",
        grid=(len(POOL_WINDOWS),),
        in_specs=[
            pl.BlockSpec((t, dgp), lambda g: (0, cblk + g)),
            pl.BlockSpec((t, dgp), lambda g: (0, g)),
            pl.BlockSpec(memory_space=pl.ANY),
            pl.BlockSpec((None, dgp, dgp), lambda g: (g, 0, 0)),
            pl.BlockSpec((1, dgp), lambda g: (0, g)),
        ],
        out_specs=[
            pl.BlockSpec((t, dgp), lambda g: (0, cblk + g)),
            pl.BlockSpec((None, dgp, dgp), lambda g: (g, 0, 0)),
            pl.BlockSpec((1, dgp), lambda g: (0, g)),
        ],
        out_shape=[
            jax.ShapeDtypeStruct(dz.shape, BF),
            jax.ShapeDtypeStruct(pool_w.shape, BF),
            jax.ShapeDtypeStruct((1, d), F32),
        ],
        input_output_aliases={2: 0},
        compiler_params=_params(("arbitrary",)),
    )(z, db, dz, pool_w, pool_scale)


def _mix_dy(dm, w_o, z, ya, yb, t, d, tm):
    def body(dm_ref, w_ref, gate_ref, ya_ref, yb_ref, dyab_ref, dz_ref, dy_ref):
        j = pl.program_id(1)

        @pl.when(j == 0)
        def _():
            dy_ref[...] = lax.dot_general(dm_ref[...], w_ref[...], _DN["nt"], preferred_element_type=F32)

        dy = dy_ref[...]
        s = _sig(gate_ref[...].astype(F32))
        yv = jnp.where(j == 0, ya_ref[...], yb_ref[...]).astype(F32)
        dyab_ref[...] = (dy * s).astype(BF)
        dz_ref[...] = (dy * yv * s * (1.0 - s)).astype(BF)

    row = pl.BlockSpec((tm, d), lambda i, j: (i, 0))
    return pl.pallas_call(
        body,
        name="mix_dy",
        grid=(t // tm, 2),
        in_specs=[row, pl.BlockSpec((d, d), lambda i, j: (0, 0)), pl.BlockSpec((tm, d), lambda i, j: (i, 3 + j)), row, row],
        out_specs=[pl.BlockSpec((tm, d), lambda i, j: (i, j)), pl.BlockSpec((tm, d), lambda i, j: (i, 3 + j))],
        out_shape=[jax.ShapeDtypeStruct((t, 2 * d), BF), jax.ShapeDtypeStruct((t, 5 * d), BF)],
        scratch_shapes=[pltpu.VMEM((tm, d), F32)],
        compiler_params=_params(("arbitrary", "arbitrary")),
    )(dm, w_o, z, ya, yb)


def _everyone():
    x, y, c = _place()
    return [(1 - x if dd & 4 else x, 1 - y if dd & 2 else y, 1 - c if dd & 1 else c) for dd in range(1, N_DEV)]


def _direct_plan(n):
    def plan(refs, send_sems, recv_sems):
        x, y, c = _place()
        me = 4 * x + 2 * y + c
        cps = []
        for i in range(n):
            for k, (px, py, pc) in enumerate(_everyone()):
                cps.append(_remote(refs[i].at[4 * px + 2 * py + pc], refs[n + i].at[me], send_sems, recv_sems,
                                   7 * i + k, (px, py, pc)))
        return cps

    def local(refs):
        x, y, c = _place()
        me = 4 * x + 2 * y + c
        return [(refs[i].at[me], refs[n + i].at[me]) for i in range(n)]

    return plan, local


def _sc_scatter_direct(name, collective_id, parts):
    n = len(parts)
    p8 = [a.reshape(N_DEV, a.shape[0] // N_DEV, a.shape[1]) for a in parts]
    plan, local = _direct_plan(n)
    return _sc_exchange(name, collective_id, p8, [jax.ShapeDtypeStruct(a.shape, a.dtype) for a in p8], plan,
                        7 * n, _everyone, local=local)


def _stack_rows(arrays):
    parts, starts, row = [], [], 0
    for a in arrays:
        pad = -a.shape[0] % 8
        starts.append(row)
        parts += [a, jnp.zeros((pad, a.shape[1]), a.dtype)] if pad else [a]
        row += a.shape[0] + pad
    return jnp.concatenate(parts, axis=0), starts


def _adam_replicated(name, stacks, layout, weights, deps=()):
    n_st, n_par = len(stacks), len(weights)

    def body(*refs):
        st_refs = refs[:n_st]
        w_refs = refs[n_st:n_st + 3 * n_par]
        out_refs = refs[n_st + 3 * n_par + len(deps):]

        def summed(which, row, rows):
            g = st_refs[which][0, pl.ds(row, rows), :]
            for k in range(1, N_DEV):
                g = g + st_refs[which][k, pl.ds(row, rows), :]
            return g

        for i in range(n_par):
            w_ref, m_ref, v_ref = w_refs[3 * i:3 * i + 3]
            g = summed(layout[i][0], layout[i][1], w_ref.shape[0])
            dlt, mm, vv = _adam_vals(w_ref[...], g, m_ref[...], v_ref[...])
            for r, val in zip(out_refs[4 * i:4 * i + 4], (g, dlt, mm, vv)):
                r[...] = val
        out_refs[4 * n_par][...] = jnp.sum(summed(layout[n_par][0], layout[n_par][1], 1), axis=1, keepdims=True)

    flat_w = [a for wmv in weights for a in wmv]
    out_shape = [jax.ShapeDtypeStruct(wmv[0].shape, F32) for wmv in weights for _ in range(4)]
    out_shape.append(jax.ShapeDtypeStruct((1, 1), F32))
    vmem = pl.BlockSpec(memory_space=pltpu.VMEM)
    res = pl.pallas_call(
        body,
        name=name,
        in_specs=[vmem] * (n_st + len(flat_w)) + [pl.BlockSpec(memory_space=pl.ANY)] * len(deps),
        out_specs=[vmem] * len(out_shape),
        out_shape=out_shape,
        compiler_params=_params(),
    )(*stacks, *flat_w, *deps)
    return [res[4 * i:4 * i + 4] for i in range(n_par)], res[4 * n_par]


def kernel(x, p, ffn1_pre_g, ffn1_w_gate, ffn1_w_up, ffn1_w_down, ffn1_post_g, mix_pre_g, w_in, sgu_norm_g, sgu_w, sgu_b, pool_w, pool_scale, w_out_a, w_out_b, w_o, mix_post_g, ffn2_pre_g, ffn2_w_gate, ffn2_w_up, ffn2_w_down, ffn2_post_g, ple_pre_g, ple_w_gate, ple_w_proj, ple_post_g, loss_target, m_ffn1_pre_g, m_ffn1_w_gate, m_ffn1_w_up, m_ffn1_w_down, m_ffn1_post_g, m_mix_pre_g, m_w_in, m_sgu_norm_g, m_sgu_w, m_sgu_b, m_pool_w, m_pool_scale, m_w_out_a, m_w_out_b, m_w_o, m_mix_post_g, m_ffn2_pre_g, m_ffn2_w_gate, m_ffn2_w_up, m_ffn2_w_down, m_ffn2_post_g, m_ple_pre_g, m_ple_w_gate, m_ple_w_proj, m_ple_post_g, v_ffn1_pre_g, v_ffn1_w_gate, v_ffn1_w_up, v_ffn1_w_down, v_ffn1_post_g, v_mix_pre_g, v_w_in, v_sgu_norm_g, v_sgu_w, v_sgu_b, v_pool_w, v_pool_scale, v_w_out_a, v_w_out_b, v_w_o, v_mix_post_g, v_ffn2_pre_g, v_ffn2_w_gate, v_ffn2_w_up, v_ffn2_w_down, v_ffn2_post_g, v_ple_pre_g, v_ple_w_gate, v_ple_w_proj, v_ple_post_g):
    args = dict(locals())
    names = ["ffn1_pre_g", "ffn1_w_gate", "ffn1_w_up", "ffn1_w_down", "ffn1_post_g", "mix_pre_g", "w_in",
             "sgu_norm_g", "sgu_w", "sgu_b", "pool_w", "pool_scale", "w_out_a", "w_out_b", "w_o", "mix_post_g",
             "ffn2_pre_g", "ffn2_w_gate", "ffn2_w_up", "ffn2_w_down", "ffn2_post_g", "ple_pre_g", "ple_w_gate",
             "ple_w_proj", "ple_post_g"]
    w = {k: args[k][0] for k in names}
    mom = {k: args["m_" + k][0] for k in names}
    var = {k: args["v_" + k][0] for k in names}

    assert x.shape[0] == 1 and p.shape[:2] == (1, 1)
    t, d = x.shape[1], x.shape[2]
    f = ffn1_w_gate.shape[2] * N_DEV
    d_in = w_in.shape[2] * N_DEV
    d_ple = p.shape[3]
    n_pool = len(POOL_WINDOWS)
    dgp = d // n_pool
    assert d_in == 5 * d and t % CHUNK == 0
    tm = _tile(t, 512, CHUNK)
    xs, ps, target = x[0], p[0, 0], loss_target[0]

    col_sharded = ("ffn1_w_gate", "ffn1_w_up", "ffn2_w_gate", "ffn2_w_up", "w_in", "ple_w_proj")

    def shard_of(k):
        if k in col_sharded:
            return w[k].T.astype(BF)
        if k == "pool_w":
            return w[k].reshape(-1, dgp).astype(BF)
        return w[k].astype(BF)

    rows_pw = dgp // N_DEV
    b_col = w["sgu_b"][:, :, None]
    gains = {k: w[k][None, :] for k in names if w[k].ndim == 1}
    full = {}

    groups = [["ffn1_w_gate"], ["ffn1_w_up"], ["ffn1_w_down"], ["w_in"], ["pool_w", "w_out_a", "w_out_b", "w_o"],
              ["ffn2_w_gate", "ffn2_w_up", "ffn2_w_down", "ple_w_gate", "ple_w_proj"]]
    def gather(gi, behind=None):
        shards = [shard_of(k) for k in groups[gi]]
        if behind is not None:
            shards = lax.optimization_barrier((shards, behind))[0]
        full.update(zip(groups[gi], _sc_gather("gather%d" % gi, gi, shards)))

    gather(0)
    xn1 = _prenorm("ffn1_prenorm", xs, gains["ffn1_pre_g"], t, d, tm)
    gather(1, xn1)
    gather(2, xn1)
    gather(3, xn1)
    g1, u1, hid1 = _ffn_up_split("ffn1", xn1, full["ffn1_w_gate"], full["ffn1_w_up"], t, d, f, tm)
    gather(4, hid1)
    f1, h1, xn2 = _ffn_down("ffn1", hid1, full["ffn1_w_down"], xs, gains["ffn1_post_g"], gains["mix_pre_g"], t, d, f, tm)
    gather(5, h1)

    pool_full = full["pool_w"].reshape(N_DEV, n_pool, rows_pw, dgp).transpose(1, 0, 2, 3).reshape(n_pool, dgp, dgp)
    z = _matmul("mix_in", [dict(a=xn2, b=full["w_in"], mode="nt", acc=0, tk=d)], m=t, n=d_in, tm=_tile(t, 1024, CHUNK), tn=d,
                outs=[("mn", (t, d_in), BF, 0)], epilogue=lambda accs, ex: accs, n_acc=1, j_outer=True)[0]
    a_br = _sgu_fwd(z, gains["sgu_norm_g"], w["sgu_w"], b_col, t, d, tm)
    b_br = _pool_fwd(z, pool_full, gains["pool_scale"], t, d)

    def merge_epi(accs, ex):
        ya, yb = accs
        ga, gb = ex[0].astype(F32), ex[1].astype(F32)
        return [ya, yb, _sig(ga) * ya + _sig(gb) * yb]

    ya, yb, y = _matmul(
        "mix_merge",
        [dict(a=a_br, b=full["w_out_a"], mode="nn", acc=0, tk=d), dict(a=b_br, b=full["w_out_b"], mode="nn", acc=1, tk=d)],
        m=t, n=d, tm=tm, tn=d, extras=[("mn", z, 3), ("mn", z, 4)],
        outs=[("mn", (t, d), BF, 0)] * 3, epilogue=merge_epi, n_acc=2)

    def proj_epi(accs, ex):
        mm = accs[0]
        hh, gg, gn = ex
        h_new = hh + mm * _rms_r(mm) * gg
        return [mm, h_new, h_new * _rms_r(h_new) * gn]

    m_out, h2, xn3 = _matmul(
        "mix_proj", [dict(a=y, b=full["w_o"], mode="nn", acc=0, tk=d)], m=t, n=d, tm=tm, tn=d,
        extras=[("mn", h1, 0), ("n", gains["mix_post_g"]), ("n", gains["ffn2_pre_g"])],
        outs=[("mn", (t, d), F32, 0)] * 2 + [("mn", (t, d), BF, 0)], epilogue=proj_epi, n_acc=1)

    g2, u2, hid2 = _ffn_up("ffn2", xn3, full["ffn2_w_gate"], full["ffn2_w_up"], t, d, f, tm)
    f2, h3, xn4 = _ffn_down("ffn2", hid2, full["ffn2_w_down"], h2, gains["ffn2_post_g"], gains["ple_pre_g"], t, d, f, tm)

    def ple_epi(accs, ex):
        gl, e = accs
        hh, tgt, gg = ex
        q = _sig(gl) * e
        err = hh + q * _rms_r(q) * gg - tgt
        return [gl, e, err * (1.0 / d), jnp.sum(err * err, axis=0, keepdims=True)]

    gl, e_ple, dh4, loss_vec = _matmul(
        "ple_fwd",
        [dict(a=xn4, b=full["ple_w_gate"], mode="nn", acc=0, tk=d), dict(a=ps, b=full["ple_w_proj"], mode="nt", acc=1, tk=d_ple)],
        m=t, n=d, tm=tm, tn=d, extras=[("mn", h3, 0), ("mn", target, 0), ("n", gains["ple_post_g"])],
        outs=[("mn", (t, d), F32, 0)] * 3 + [("acc", (1, d), F32)], epilogue=ple_epi, n_acc=2)

    tk_t = _tile(t, TOKEN_K_TILE)
    tk_1 = _tile(t, 2 * TOKEN_K_TILE)

    def ple_post_fn(v):
        dy, gl_v, e_v, gg = v
        s = _sig(gl_v)
        dq, dg = _rms_bwd(s * e_v, gg, dy)
        return [dq * e_v * s * (1.0 - s), dq * s, dg]

    dgl, de, dg_ple_post = _rowwise(
        "ple_post_bwd", [("row", dh4, d, 0), ("row", gl, d, 0), ("row", e_ple, d, 0), ("full", gains["ple_post_g"])],
        [("row", (t, d), BF, d, 0), ("row", (t, d), BF, d, 0), ("acc", (1, d), F32)], ple_post_fn, t=t, tm=tm)
    ident = lambda accs, ex: accs
    dw_ple_gate = _matmul("ple_dwg", [dict(a=xn4, b=dgl, mode="tn", acc=0, tk=tk_1)], m=d, n=d, tm=d, tn=d,
                          nk=t // tk_1, outs=[("mn", (d, d), BF, 0)], epilogue=ident, n_acc=1)[0]
    dw_ple_proj_t = _matmul("ple_dwp", [dict(a=de, b=ps, mode="tn", acc=0, tk=tk_1)], m=d, n=d_ple, tm=d, tn=d_ple,
                            nk=t // tk_1, outs=[("mn", (d, d_ple), BF, 0)], epilogue=ident, n_acc=1)[0]
    dh3, dg_ple_pre, df2, dg_f2_post = _matmul(
        "ple_dx", [dict(a=dgl, b=full["ple_w_gate"], mode="nt", acc=0, tk=d)], m=t, n=d, tm=tm, tn=d,
        extras=_dx_extras(h3, dh4, gains["ple_pre_g"], f2, gains["ffn2_post_g"]),
        outs=_dx_outs(t, d), epilogue=_dx_epi(0.5), n_acc=1)

    staged = []
    n_gather = len(groups)

    def scatter(keys, parts):
        k = len(staged)
        staged.append((keys, _sc_scatter_direct("scatter%d" % k, n_gather + k, parts)))

    dgate2, dup2 = _ffn_dhid("ffn2", df2, full["ffn2_w_down"], g2, u2, t, d, f, tm)
    dwd2 = _ffn_dwd("ffn2", hid2, df2, t, d, f)[0]
    dwg2 = _ffn_dw_in("ffn2_dwg", dgate2, xn3, t, d, f)
    dwu2 = _ffn_dw_in("ffn2_dwu", dup2, xn3, t, d, f)
    scatter(["ple_w_gate", "ple_w_proj", "ffn2_w_down", "ffn2_w_gate", "ffn2_w_up"],
            [dw_ple_gate, dw_ple_proj_t, dwd2, dwg2, dwu2])
    dh2, dg_f2_pre, dm, dg_mix_post = _ffn_dx(
        "ffn2", dh3, h2, dgate2, dup2, full["ffn2_w_gate"], full["ffn2_w_up"], gains["ffn2_pre_g"], t, d, f, tm,
        deps=[dw_ple_gate, dw_ple_proj_t, dwd2, dwg2, dwu2], before=(m_out, gains["mix_post_g"], 1.0))

    dyab, dz = _mix_dy(dm, full["w_o"], z, ya, yb, t, d, tm)
    dw_o = _matmul("mix_dwo", [dict(a=y, b=dm, mode="tn", acc=0, tk=tk_1)], m=d, n=d, tm=d, tn=d, nk=t // tk_1,
                   outs=[("mn", (d, d), BF, 0)], epilogue=ident, n_acc=1)[0]
    dw_out_a, dw_out_b = _matmul(
        "mix_dwab",
        [dict(a=a_br, b=dyab, mode="tn", acc=0, tk=tk_t),
         dict(a=b_br, b=dyab, mode="tn", acc=1, tk=tk_t, b_off=(0, 1))],
        m=d, n=d, tm=d, tn=d, nk=t // tk_t, outs=[("mn", (d, d), BF, 0)] * 2, epilogue=ident, n_acc=2)
    da, db = _matmul(
        "mix_dab",
        [dict(a=dyab, b=full["w_out_a"], mode="nt", acc=0, tk=d),
         dict(a=dyab, b=full["w_out_b"], mode="nt", acc=1, tk=d, a_off=(0, 1))],
        m=t, n=d, tm=tm, tn=d, outs=[("mn", (t, d), BF, 0)] * 2, epilogue=ident, n_acc=2)
    dz, dsgu_w, dsgu_b, dg_sgu = _sgu_bwd(z, da, dz, gains["sgu_norm_g"], w["sgu_w"], b_col, t, d, _tile(t, 256, CHUNK))
    dz, dpool_w, dg_pool = _pool_bwd(z, db, dz, pool_full, gains["pool_scale"], t, d)
    dw_in_t = _matmul(
        "mix_dwin", [dict(a=dz, b=xn2, mode="tn", acc=0, tk=tk_1)], m=d_in, n=d, tm=d, tn=d,
        nk=t // tk_1, outs=[("mn", (d_in, d), BF, 0)], epilogue=ident, n_acc=1)[0]
    dpool_rows = dpool_w.reshape(n_pool, N_DEV, rows_pw, dgp).transpose(1, 0, 2, 3).reshape(N_DEV * n_pool * rows_pw, dgp)
    scatter(["w_o", "w_out_a", "w_out_b", "pool_w", "w_in"], [dw_o, dw_out_a, dw_out_b, dpool_rows, dw_in_t])
    dh1, dg_mix_pre, df1, dg_f1_post = _matmul(
        "mix_dx", [dict(a=dz, b=full["w_in"], mode="nn", acc=0, tk=d_in // 2)], m=t, n=d, tm=tm, tn=d, nk=2,
        extras=_dx_extras(h1, dh2, gains["mix_pre_g"], f1, gains["ffn1_post_g"]),
        outs=_dx_outs(t, d), epilogue=_dx_epi(0.5), n_acc=1, deps=[dw_o, dw_out_a, dpool_rows, dw_in_t])

    dgate1, dup1 = _ffn_dhid("ffn1", df1, full["ffn1_w_down"], g1, u1, t, d, f, tm)
    dwd1 = _ffn_dwd("ffn1", hid1, df1, t, d, f)[0]
    scatter(["ffn1_w_down"], [dwd1])
    dwg1 = _ffn_dw_in("ffn1_dwg", dgate1, xn1, t, d, f, deps=[dwd1])
    scatter(["ffn1_w_gate"], [dwg1])
    dwu1 = _ffn_dw_in("ffn1_dwu", dup1, xn1, t, d, f, deps=[dwg1])
    scatter(["ffn1_w_up"], [dwu1])
    grad_x, dg_f1_pre = _ffn_dx("ffn1", dh1, xs, dgate1, dup1, full["ffn1_w_gate"], full["ffn1_w_up"],
                                gains["ffn1_pre_g"], t, d, f, tm, deps=[dwu1])

    grad, delta, new_m, new_v = {}, {}, {}, {}

    def update(group, partial_sums, behind):
        done = []
        for k, st in zip(group, partial_sums):
            if k in col_sharded and w[k].shape[1] % LANES:
                res = _reduce_adam("adam_" + k, st, w[k].T, mom[k].T, var[k].T, deps=behind)
                grad[k], delta[k], new_m[k], new_v[k] = [r.T for r in res]
                done.append(res[1])
                continue
            elif k in col_sharded:
                grad[k] = _reduce_only("sum_" + k, st, deps=behind).T
                delta[k], new_m[k], new_v[k] = _adam_only("adam_" + k, w[k], grad[k], mom[k], var[k])
            else:
                shape = w[k].shape
                flat = (-1, shape[-1])
                res = _reduce_adam("adam_" + k, st, w[k].reshape(flat), mom[k].reshape(flat), var[k].reshape(flat),
                                   deps=behind)
                grad[k], delta[k], new_m[k], new_v[k] = [r.reshape(shape) for r in res]
            done.append(delta[k])
        return done

    gain_grads = {
        "ffn1_pre_g": dg_f1_pre, "ffn1_post_g": dg_f1_post, "mix_pre_g": dg_mix_pre, "sgu_norm_g": dg_sgu,
        "pool_scale": dg_pool, "mix_post_g": dg_mix_post, "ffn2_pre_g": dg_f2_pre, "ffn2_post_g": dg_f2_post,
        "ple_pre_g": dg_ple_pre, "ple_post_g": dg_ple_post,
    }
    gain_names = list(gain_grads)
    n_mix = N_SGU_GROUPS * CHUNK
    wide, wide_rows = _stack_rows([gain_grads[k] for k in gain_names] + [loss_vec * (0.5 / d)])
    narrow, narrow_rows = _stack_rows([dsgu_b.reshape(N_SGU_GROUPS, CHUNK), dsgu_w.reshape(n_mix, CHUNK)])
    plan_s, local_s = _broadcast_plan(2)
    stacks = _sc_exchange(
        "small", n_gather + len(staged), [wide, narrow],
        [jax.ShapeDtypeStruct((N_DEV,) + wide.shape, F32), jax.ShapeDtypeStruct((N_DEV,) + narrow.shape, F32)],
        plan_s, 2 * (N_DEV - 1), _everyone, local=local_s)
    done = [grad_x]
    for keys, partial_sums in staged:
        done = update(keys, partial_sums, done)
    rep_names = gain_names + ["sgu_b", "sgu_w"]
    rep_shape = {k: (1, d) for k in gain_names}
    rep_shape.update(sgu_b=(N_SGU_GROUPS, CHUNK), sgu_w=(n_mix, CHUNK))
    layout = [(0, r) for r in wide_rows[:-1]] + [(1, narrow_rows[0]), (1, narrow_rows[1]), (0, wide_rows[-1])]
    res, loss = _adam_replicated(
        "adam_replicated", stacks, layout,
        [tuple(src[k].reshape(rep_shape[k]) for src in (w, mom, var)) for k in rep_names], deps=done)
    for k, quad in zip(rep_names, res):
        for dst, val in zip((grad, delta, new_m, new_v), quad):
            dst[k] = val.reshape(w[k].shape)

    out = [loss[0, 0], grad_x[None]]
    for group in (grad, delta, new_m, new_v):
        out += [group[k][None] for k in names]
    return tuple(out)
```

```python
import math

import jax
import jax.numpy as jnp
from jax import lax
from jax.experimental import pallas as pl
from jax.experimental.pallas import tpu as pltpu
from jax.experimental.pallas import tpu_sc as plsc

EPS = 1e-6
CHUNK = 128
N_SGU_GROUPS = 4
POOL_WINDOWS = (2, 4, 8, 16)
ADAM_LR = 0.001
ADAM_B1 = 0.9
ADAM_B2 = 0.999
ADAM_EPS = 1e-08
ADAM_WD = 0.01
ADAM_STEP = 10

N_DEV = 8
MESH = pl.DeviceIdType.MESH
V7X_VMEM_BYTES = 64 * 1024 * 1024
VMEM_LIMIT = V7X_VMEM_BYTES - 8 * 1024 * 1024
TOKEN_K_TILE = 1024
LANES = 128
UPDATE_ROWS = 256
BF = jnp.bfloat16
F32 = jnp.float32

_DN = {
    "nn": (((1,), (0,)), ((), ())),
    "nt": (((1,), (1,)), ((), ())),
    "tn": (((0,), (0,)), ((), ())),
}


def _params(sem=None):
    return pltpu.CompilerParams(dimension_semantics=sem, vmem_limit_bytes=VMEM_LIMIT)


def _tile(n, target, align=128):
    t = min(n, target)
    t -= t % align
    while t >= align:
        if n % t == 0:
            return t
        t -= align
    return n


def _sig(x):
    return 0.5 + 0.5 * jnp.tanh(0.5 * x)


_GELU_K = math.sqrt(2.0 / math.pi)
_GELU_C = 0.044715


def _gelu(x):
    return 0.5 * x * (1.0 + jnp.tanh(_GELU_K * (x + _GELU_C * x * x * x)))


def _gelu_grad(x):
    t = jnp.tanh(_GELU_K * (x + _GELU_C * x * x * x))
    return 0.5 * (1.0 + t) + 0.5 * x * (1.0 - t * t) * _GELU_K * (1.0 + 3.0 * _GELU_C * x * x)


def _rms_r(x):
    return lax.rsqrt(jnp.mean(x * x, axis=-1, keepdims=True) + EPS)


def _rms_bwd(x, g, dy):
    r = _rms_r(x)
    xh = x * r
    gy = dy * g
    dx = r * (gy - xh * jnp.mean(xh * gy, axis=-1, keepdims=True))
    return dx, jnp.sum(dy * xh, axis=0, keepdims=True)


def _matmul(name, mats, *, m, n, tm, tn, nk=1, extras=(), outs, epilogue, n_acc, j_outer=False, deps=()):
    ni, nj = m // tm, n // tn
    assert ni * tm == m and nj * tn == n, (name, m, n, tm, tn)
    if j_outer:
        grid = (nj, ni, nk)

        def ij(g0, g1):
            return g1, g0
    else:
        grid = (ni, nj, nk)

        def ij(g0, g1):
            return g0, g1

    in_specs, args = [], []
    for mt in mats:
        mode, tk = mt["mode"], mt["tk"]
        ao, bo = mt.get("a_off", (0, 0)), mt.get("b_off", (0, 0))
        if mode == "tn":
            sa = pl.BlockSpec((tk, tm), lambda g0, g1, kk, ao=ao: (ao[0] + kk, ao[1] + ij(g0, g1)[0]))
        else:
            sa = pl.BlockSpec((tm, tk), lambda g0, g1, kk, ao=ao: (ao[0] + ij(g0, g1)[0], ao[1] + kk))
        if mode == "nt":
            sb = pl.BlockSpec((tn, tk), lambda g0, g1, kk, bo=bo: (bo[0] + ij(g0, g1)[1], bo[1] + kk))
        else:
            sb = pl.BlockSpec((tk, tn), lambda g0, g1, kk, bo=bo: (bo[0] + kk, bo[1] + ij(g0, g1)[1]))
        in_specs += [sa, sb]
        args += [mt["a"], mt["b"]]
    n_mat_refs = len(args)
    for ex in extras:
        if ex[0] == "mn":
            in_specs.append(pl.BlockSpec((tm, tn), lambda g0, g1, kk, c=ex[2]: (ij(g0, g1)[0], c + ij(g0, g1)[1])))
        else:
            in_specs.append(pl.BlockSpec((1, tn), lambda g0, g1, kk: (0, ij(g0, g1)[1])))
        args.append(ex[1])
    n_ex_end = len(args)
    in_specs += [pl.BlockSpec(memory_space=pl.ANY)] * len(deps)
    args += list(deps)
    n_in = len(args)
    out_specs, out_shape = [], []
    for o in outs:
        if o[0] == "mn":
            out_specs.append(pl.BlockSpec((tm, tn), lambda g0, g1, kk, c=o[3]: (ij(g0, g1)[0], c + ij(g0, g1)[1])))
        else:
            assert nj == 1, name
            out_specs.append(pl.BlockSpec((1, tn), lambda g0, g1, kk: (0, 0)))
        out_shape.append(jax.ShapeDtypeStruct(o[1], o[2]))
    n_out = len(outs)

    def body(*refs):
        mat_refs = refs[:n_mat_refs]
        ex_refs = refs[n_mat_refs:n_ex_end]
        out_refs = refs[n_in:n_in + n_out]
        acc_refs = refs[n_in + n_out:]
        i = ij(pl.program_id(0), pl.program_id(1))[0]
        kk = pl.program_id(2)

        def products():
            res = [None] * n_acc
            for idx, mt in enumerate(mats):
                a = mat_refs[2 * idx][...].astype(BF)
                b = mat_refs[2 * idx + 1][...].astype(BF)
                p = lax.dot_general(a, b, _DN[mt["mode"]], preferred_element_type=F32)
                q = mt["acc"]
                res[q] = p if res[q] is None else res[q] + p
            return res

        def finish(accs):
            vals = epilogue(accs, [r[...] for r in ex_refs])
            for o, r, v in zip(outs, out_refs, vals):
                if o[0] == "mn":
                    r[...] = v.astype(o[2])
                else:
                    @pl.when(i == 0)
                    def _():
                        r[...] = v

                    @pl.when(i != 0)
                    def _():
                        r[...] += v

        if nk == 1:
            finish(products())
        else:
            res = products()

            @pl.when(kk == 0)
            def _():
                for q in range(n_acc):
                    acc_refs[q][...] = res[q]

            @pl.when(kk != 0)
            def _():
                for q in range(n_acc):
                    acc_refs[q][...] += res[q]

            @pl.when(kk == nk - 1)
            def _():
                finish([r[...] for r in acc_refs])

    scratch = [pltpu.VMEM((tm, tn), F32) for _ in range(n_acc)] if nk > 1 else []
    return pl.pallas_call(
        body,
        name=name,
        grid=grid,
        in_specs=in_specs,
        out_specs=out_specs,
        out_shape=out_shape,
        scratch_shapes=scratch,
        compiler_params=_params(("arbitrary", "arbitrary", "arbitrary")),
    )(*args)


def _rowwise(name, ins, outs, fn, *, t, tm, deps=()):
    ni = t // tm
    assert ni * tm == t, (name, t, tm)
    in_specs, args = [], []
    for s in ins:
        if s[0] == "row":
            in_specs.append(pl.BlockSpec((tm, s[2]), lambda i, c=s[3]: (i, c)))
        else:
            nd = s[1].ndim
            in_specs.append(pl.BlockSpec(s[1].shape, lambda i, nd=nd: (0,) * nd))
        args.append(s[1])
    out_specs, out_shape = [], []
    for o in outs:
        if o[0] == "row":
            out_specs.append(pl.BlockSpec((tm, o[3]), lambda i, c=o[4]: (i, c)))
        else:
            nd = len(o[1])
            out_specs.append(pl.BlockSpec(o[1], lambda i, nd=nd: (0,) * nd))
        out_shape.append(jax.ShapeDtypeStruct(o[1], o[2]))
    n_read = len(args)
    in_specs += [pl.BlockSpec(memory_space=pl.ANY)] * len(deps)
    args += list(deps)
    n_in = len(args)

    def body(*refs):
        i = pl.program_id(0)
        vals = fn([r[...] for r in refs[:n_read]])
        for o, r, v in zip(outs, refs[n_in:], vals):
            if o[0] == "row":
                r[...] = v.astype(o[2])
            else:
                @pl.when(i == 0)
                def _():
                    r[...] = v

                @pl.when(i != 0)
                def _():
                    r[...] += v

    return pl.pallas_call(
        body,
        name=name,
        grid=(ni,),
        in_specs=in_specs,
        out_specs=out_specs,
        out_shape=out_shape,
        compiler_params=_params(("arbitrary",)),
    )(*args)


def _place():
    return lax.axis_index("x"), lax.axis_index("y"), lax.axis_index("c")


def _remote(src, dst, send_sems, recv_sems, k, to):
    return pltpu.make_async_remote_copy(
        src_ref=src, dst_ref=dst, send_sem=send_sems.at[k], recv_sem=recv_sems.at[k],
        device_id=to, device_id_type=MESH)


def _chips(x, y):
    return [(1 - x, y), (x, 1 - y), (1 - x, 1 - y)]


def _broadcast_plan(n):
    def plan(refs, send_sems, recv_sems):
        x, y, c = _place()
        cps = []
        for i in range(n):
            for dd in range(1, N_DEV):
                peer = (1 - x if dd & 4 else x, 1 - y if dd & 2 else y, 1 - c if dd & 1 else c)
                cps.append(_remote(refs[i], refs[n + i].at[4 * x + 2 * y + c], send_sems, recv_sems, 7 * i + dd - 1, peer))
        return cps

    def local(refs):
        x, y, c = _place()
        return [(refs[i], refs[n + i].at[4 * x + 2 * y + c]) for i in range(n)]

    return plan, local


def _handshake(peers):
    barrier = pltpu.get_barrier_semaphore()
    for peer in peers:
        pl.semaphore_signal(barrier, inc=1, device_id=peer, device_id_type=MESH)
    pl.semaphore_wait(barrier, len(peers))


def _sequencer_call(name, collective_id, body, args, out_type, scratch_types):
    return pl.kernel(
        body,
        out_type=out_type,
        mesh=plsc.ScalarSubcoreMesh(axis_name="sequencer", num_cores=1),
        scratch_types=scratch_types,
        compiler_params=pltpu.CompilerParams(collective_id=collective_id),
        name=name,
    )(*args)


def _sc_gather(name, collective_id, shards):
    n = len(shards)

    def body(*refs):
        ins, outs = refs[:n], refs[n:2 * n]
        send_sems, recv_sems, local_sems = refs[2 * n:]
        x, y, c = _place()
        me, sibling = (x, y, c), (x, y, 1 - c)
        chips = _chips(x, y)
        _handshake([sibling] + [(*chip, c) for chip in chips])

        def slot(i, px, py, pc):
            return outs[i].at[4 * px + 2 * py + pc]

        def copy(i, k, block, to, src=None):
            return _remote(slot(i, *block) if src is None else src, slot(i, *block), send_sems, recv_sems, 7 * i + k, to)

        mine = [pltpu.make_async_copy(ins[i], slot(i, *me), local_sems.at[i]) for i in range(n)]
        for cp in mine:
            cp.start()
        first = []
        for i in range(n):
            first += [copy(i, 1 + j, me, (*chip, c), src=ins[i]) for j, chip in enumerate(chips)]
            first.append(copy(i, 0, me, sibling, src=ins[i]))
        for cp in first:
            cp.start()
        passed = []
        for i in range(n):
            for j, chip in enumerate(chips):
                copy(i, 1 + j, (*chip, c), me).wait_recv()
                fwd = copy(i, 4 + j, (*chip, c), sibling)
                fwd.start()
                passed.append(fwd)
        for i in range(n):
            copy(i, 0, sibling, me).wait_recv()
            for j, chip in enumerate(chips):
                copy(i, 4 + j, (*chip, 1 - c), me).wait_recv()
        for cp in first + passed:
            cp.wait_send()
        for cp in mine:
            cp.wait()

    outs = _sequencer_call(
        name, collective_id, body, shards,
        [jax.ShapeDtypeStruct((N_DEV,) + s.shape, s.dtype) for s in shards],
        [pltpu.SemaphoreType.DMA((7 * n,)), pltpu.SemaphoreType.DMA((7 * n,)), pltpu.SemaphoreType.DMA((n,))])
    return [o.reshape((N_DEV * s.shape[0],) + s.shape[1:]) for o, s in zip(outs, shards)]


def _sc_exchange(name, collective_id, srcs, land_shapes, plan, n_copies, peers, local=None):
    n = len(srcs)

    def body(*refs):
        bufs, send_sems, recv_sems = refs[:2 * n], refs[2 * n], refs[2 * n + 1]
        _handshake(peers())
        if local is not None:
            mine = [pltpu.make_async_copy(a, b, refs[2 * n + 2].at[k]) for k, (a, b) in enumerate(local(bufs))]
            for cp in mine:
                cp.start()
        cps = plan(bufs, send_sems, recv_sems)
        for cp in cps:
            cp.start()
        for cp in cps:
            cp.wait_send()
            cp.wait_recv()
        if local is not None:
            for cp in mine:
                cp.wait()

    scratch = [pltpu.SemaphoreType.DMA((n_copies,)), pltpu.SemaphoreType.DMA((n_copies,))]
    if local is not None:
        scratch.append(pltpu.SemaphoreType.DMA((n,)))
    return list(_sequencer_call(name, collective_id, body, srcs, land_shapes, scratch))


def _adam_vals(w, g, m, v):
    m = ADAM_B1 * m + (1.0 - ADAM_B1) * g
    v = ADAM_B2 * v + (1.0 - ADAM_B2) * (g * g)
    m_hat = m / (1.0 - ADAM_B1 ** ADAM_STEP)
    v_hat = v / (1.0 - ADAM_B2 ** ADAM_STEP)
    delta = -ADAM_LR * (m_hat / (jnp.sqrt(v_hat) + ADAM_EPS) + ADAM_WD * w)
    return delta, m, v


def _sum_parts(st_ref):
    g = st_ref[0].astype(F32)
    for k in range(1, st_ref.shape[0]):
        g = g + st_ref[k].astype(F32)
    return g


def _reduce_adam(name, st, w, m, v, deps=()):
    rows, cols = w.shape
    tr = _tile(rows, UPDATE_ROWS, 16)

    def body(st_ref, w_ref, m_ref, v_ref, *rest):
        g_out, d_out, m_out, v_out = rest[len(deps):]
        g = _sum_parts(st_ref)
        d, mm, vv = _adam_vals(w_ref[...], g, m_ref[...], v_ref[...])
        g_out[...] = g
        d_out[...] = d
        m_out[...] = mm
        v_out[...] = vv

    blk = pl.BlockSpec((tr, cols), lambda i: (i, 0))
    return pl.pallas_call(
        body,
        name=name,
        grid=(rows // tr,),
        in_specs=[pl.BlockSpec((st.shape[0], tr, cols), lambda i: (0, i, 0)), blk, blk, blk]
        + [pl.BlockSpec(memory_space=pl.ANY)] * len(deps),
        out_specs=[blk] * 4,
        out_shape=[jax.ShapeDtypeStruct(w.shape, F32)] * 4,
        compiler_params=_params(("arbitrary",)),
    )(st, w, m, v, *deps)


def _reduce_only(name, st, deps=()):
    _, rows, cols = st.shape
    tr = _tile(rows, UPDATE_ROWS, 16)

    def body(st_ref, *rest):
        rest[-1][...] = _sum_parts(st_ref)

    return pl.pallas_call(
        body,
        name=name,
        grid=(rows // tr,),
        in_specs=[pl.BlockSpec((st.shape[0], tr, cols), lambda i: (0, i, 0))]
        + [pl.BlockSpec(memory_space=pl.ANY)] * len(deps),
        out_specs=pl.BlockSpec((tr, cols), lambda i: (i, 0)),
        out_shape=jax.ShapeDtypeStruct((rows, cols), F32),
        compiler_params=_params(("arbitrary",)),
    )(st, *deps)


def _adam_only(name, w, g, m, v):
    rows, cols = w.shape
    tr = _tile(rows, UPDATE_ROWS, 16)

    def body(w_ref, g_ref, m_ref, v_ref, d_out, m_out, v_out):
        d, mm, vv = _adam_vals(w_ref[...], g_ref[...], m_ref[...], v_ref[...])
        d_out[...] = d
        m_out[...] = mm
        v_out[...] = vv

    blk = pl.BlockSpec((tr, cols), lambda i: (i, 0))
    return pl.pallas_call(
        body,
        name=name,
        grid=(rows // tr,),
        in_specs=[blk] * 4,
        out_specs=[blk] * 3,
        out_shape=[jax.ShapeDtypeStruct(w.shape, F32)] * 3,
        compiler_params=_params(("arbitrary",)),
    )(w, g, m, v)


def _prenorm(name, h, g, t, d, tm, deps=()):
    def fn(v):
        x, gg = v
        return [x * _rms_r(x) * gg]

    return _rowwise(name, [("row", h, d, 0), ("full", g)], [("row", (t, d), BF, d, 0)], fn, t=t, tm=tm, deps=deps)[0]


def _ffn_up(tag, xn, wgt, wut, t, d, f, tm):
    def up_epi(accs, ex):
        gg, uu = accs
        return [gg, uu, gg * _sig(gg) * uu]

    mats = [dict(a=xn, b=wgt, mode="nt", acc=0, tk=d), dict(a=xn, b=wut, mode="nt", acc=1, tk=d)]
    return _matmul(
        tag + "_up", mats, m=t, n=f, tm=tm, tn=_tile(f, 1408),
        outs=[("mn", (t, f), BF, 0)] * 3, epilogue=up_epi, n_acc=2, j_outer=True)


def _ffn_up_split(tag, xn, wgt, wut, t, d, f, tm):
    tf = _tile(f, 1408)
    gate = _matmul(
        tag + "_gate", [dict(a=xn, b=wgt, mode="nt", acc=0, tk=d)], m=t, n=f, tm=tm, tn=tf,
        outs=[("mn", (t, f), BF, 0)], epilogue=lambda accs, ex: accs, n_acc=1, j_outer=True)[0]

    def up_epi(accs, ex):
        gg = ex[0].astype(F32)
        return [accs[0], gg * _sig(gg) * accs[0]]

    up, hid = _matmul(
        tag + "_up", [dict(a=xn, b=wut, mode="nt", acc=0, tk=d)], m=t, n=f, tm=tm, tn=tf, extras=[("mn", gate, 0)],
        outs=[("mn", (t, f), BF, 0)] * 2, epilogue=up_epi, n_acc=1, j_outer=True)
    return gate, up, hid


def _ffn_down(tag, hid, wd, h, g_post, g_next, t, d, f, tm):
    def down_epi(accs, ex):
        ff = accs[0]
        hh, gg, gn = ex
        h_new = hh + 0.5 * ff * _rms_r(ff) * gg
        return [ff, h_new, h_new * _rms_r(h_new) * gn]

    return _matmul(
        tag + "_down", [dict(a=hid, b=wd, mode="nn", acc=0, tk=f)], m=t, n=d, tm=tm, tn=d,
        extras=[("mn", h, 0), ("n", g_post), ("n", g_next)],
        outs=[("mn", (t, d), F32, 0)] * 2 + [("mn", (t, d), BF, 0)], epilogue=down_epi, n_acc=1)


def _prenorm_bwd_epi(accs, ex):
    hh, dh, gg = ex
    dx, dg = _rms_bwd(hh, gg, accs[0])
    return [dh + dx, dg]


def _dx_extras(h_in, dh, g_pre, fo, g_post):
    return [("mn", h_in, 0), ("mn", dh, 0), ("n", g_pre), ("mn", fo, 0), ("n", g_post)]


def _dx_outs(t, d):
    return [("mn", (t, d), F32, 0), ("acc", (1, d), F32), ("mn", (t, d), BF, 0), ("acc", (1, d), F32)]


def _dx_epi(scale):
    def epi(accs, ex):
        hh, dh, gg, fo, gp = ex
        dx, dg = _rms_bwd(hh, gg, accs[0])
        dh_new = dh + dx
        dfo, dgp = _rms_bwd(fo, gp, dh_new * scale)
        return [dh_new, dg, dfo, dgp]

    return epi


def _ffn_dhid(tag, df, wd, gate, up, t, d, f, tm):
    def hid_epi(accs, ex):
        dhid = accs[0]
        gg, uu = ex[0].astype(F32), ex[1].astype(F32)
        s = _sig(gg)
        gs = gg * s
        return [dhid * uu * (s + gs * (1.0 - s)), dhid * gs]

    return _matmul(
        tag + "_dhid", [dict(a=df, b=wd, mode="nt", acc=0, tk=d)], m=t, n=f, tm=tm, tn=_tile(f, 1408),
        extras=[("mn", gate, 0), ("mn", up, 0)],
        outs=[("mn", (t, f), BF, 0)] * 2, epilogue=hid_epi, n_acc=1, j_outer=True)


def _ffn_dwd(tag, hid, df, t, d, f):
    tk_t = _tile(t, 2 * TOKEN_K_TILE)
    return _matmul(
        tag + "_dwd", [dict(a=hid, b=df, mode="tn", acc=0, tk=tk_t)], m=f, n=d, tm=_tile(f, 1408), tn=d,
        nk=t // tk_t, outs=[("mn", (f, d), BF, 0)], epilogue=lambda accs, ex: accs, n_acc=1)


def _ffn_dw_in(name, dact, xn, t, d, f, deps=()):
    tk_t = _tile(t, 2 * TOKEN_K_TILE)
    return _matmul(
        name, [dict(a=dact, b=xn, mode="tn", acc=0, tk=tk_t)], m=f, n=d, tm=_tile(f, 1408), tn=d, nk=t // tk_t,
        outs=[("mn", (f, d), BF, 0)], epilogue=lambda accs, ex: accs, n_acc=1, deps=deps)[0]


def _ffn_dx(tag, dh, h_in, dgate, dup, wgt, wut, g_pre, t, d, f, tm, deps=(), before=None):
    tf = _tile(f, 1408)
    mats = [dict(a=dgate, b=wgt, mode="nn", acc=0, tk=tf), dict(a=dup, b=wut, mode="nn", acc=0, tk=tf)]
    if before is None:
        return _matmul(
            tag + "_dx", mats, m=t, n=d, tm=tm, tn=d, nk=f // tf,
            extras=[("mn", h_in, 0), ("mn", dh, 0), ("n", g_pre)],
            outs=[("mn", (t, d), F32, 0), ("acc", (1, d), F32)], epilogue=_prenorm_bwd_epi, n_acc=1, deps=deps)
    fo, g_post, scale = before
    return _matmul(
        tag + "_dx", mats, m=t, n=d, tm=tm, tn=d, nk=f // tf, extras=_dx_extras(h_in, dh, g_pre, fo, g_post),
        outs=_dx_outs(t, d), epilogue=_dx_epi(scale), n_acc=1, deps=deps)


def _causal_mask():
    r = lax.broadcasted_iota(jnp.int32, (CHUNK, CHUNK), 0)
    c = lax.broadcasted_iota(jnp.int32, (CHUNK, CHUNK), 1)
    return r >= c


def _layernorm_parts(v):
    mu = jnp.mean(v, axis=-1, keepdims=True)
    vc = v - mu
    rstd = lax.rsqrt(jnp.mean(vc * vc, axis=-1, keepdims=True) + EPS)
    return vc * rstd, rstd


def _sgu_fwd(z, g_sgu, w_s, b_col, t, d, tm):
    dg = d // N_SGU_GROUPS
    n_chunk = tm // CHUNK

    def body(zu_ref, zv_ref, g_ref, w_ref, b_ref, a_ref):
        u = _gelu(zu_ref[...].astype(F32))
        vhat, _ = _layernorm_parts(_gelu(zv_ref[...].astype(F32)))
        vn = (vhat * g_ref[...]).astype(BF)
        mask = _causal_mask()
        for gi in range(N_SGU_GROUPS):
            ws = jnp.where(mask, w_ref[gi], 0.0).astype(BF)
            bias = b_ref[gi]
            for ci in range(n_chunk):
                rows, cols = slice(ci * CHUNK, (ci + 1) * CHUNK), slice(gi * dg, (gi + 1) * dg)
                sv = jnp.dot(ws, vn[rows, cols], preferred_element_type=F32) + bias
                a_ref[rows, cols] = (u[rows, cols] * sv).astype(BF)

    return pl.pallas_call(
        body,
        name="sgu_fwd",
        grid=(t // tm,),
        in_specs=[
            pl.BlockSpec((tm, d), lambda i: (i, 0)),
            pl.BlockSpec((tm, d), lambda i: (i, 1)),
            pl.BlockSpec((1, d), lambda i: (0, 0)),
            pl.BlockSpec(w_s.shape, lambda i: (0, 0, 0)),
            pl.BlockSpec(b_col.shape, lambda i: (0, 0, 0)),
        ],
        out_specs=pl.BlockSpec((tm, d), lambda i: (i, 0)),
        out_shape=jax.ShapeDtypeStruct((t, d), BF),
        compiler_params=_params(("arbitrary",)),
    )(z, z, g_sgu, w_s, b_col)


def _sgu_bwd(z, da, dz, g_sgu, w_s, b_col, t, d, tm):
    dg = d // N_SGU_GROUPS
    n_chunk = tm // CHUNK

    def body(zu_ref, zv_ref, da_ref, dz_in, g_ref, w_ref, b_ref, dz_ref, dw_ref, db_ref, dgn_ref, dvn_ref):
        del dz_in
        dzu_ref, dzv_ref = dz_ref.at[:, pl.ds(0, d)], dz_ref.at[:, pl.ds(d, d)]
        i = pl.program_id(0)
        zu, zv = zu_ref[...].astype(F32), zv_ref[...].astype(F32)
        u = _gelu(zu)
        vhat, rstd = _layernorm_parts(_gelu(zv))
        gn = g_ref[...]
        vn = (vhat * gn).astype(BF)
        da_v = da_ref[...].astype(F32)
        dsv_all = (da_v * u).astype(BF)
        mask = _causal_mask()
        for gi in range(N_SGU_GROUPS):
            ws = jnp.where(mask, w_ref[gi], 0.0).astype(BF)
            bias = b_ref[gi]
            dw = jnp.zeros((CHUNK, CHUNK), F32)
            dbias = jnp.zeros((CHUNK, 1), F32)
            for ci in range(n_chunk):
                rows, cols = slice(ci * CHUNK, (ci + 1) * CHUNK), slice(gi * dg, (gi + 1) * dg)
                vn_c, dsv = vn[rows, cols], dsv_all[rows, cols]
                sv = jnp.dot(ws, vn_c, preferred_element_type=F32) + bias
                dzu_ref[rows, cols] = (da_v[rows, cols] * sv * _gelu_grad(zu[rows, cols])).astype(BF)
                dw = dw + lax.dot_general(dsv, vn_c, _DN["nt"], preferred_element_type=F32)
                dbias = dbias + jnp.sum(dsv.astype(F32), axis=1, keepdims=True)
                dvn_ref[rows, cols] = lax.dot_general(ws, dsv, _DN["tn"], preferred_element_type=F32)
            dw = jnp.where(mask, dw, 0.0)

            @pl.when(i == 0)
            def _():
                dw_ref[gi] = dw
                db_ref[gi] = dbias

            @pl.when(i != 0)
            def _():
                dw_ref[gi] += dw
                db_ref[gi] += dbias

        dvn = dvn_ref[...]
        dgn = jnp.sum(dvn * vhat, axis=0, keepdims=True)

        @pl.when(i == 0)
        def _():
            dgn_ref[...] = dgn

        @pl.when(i != 0)
        def _():
            dgn_ref[...] += dgn

        dvh = dvn * gn
        dv = rstd * (dvh - jnp.mean(dvh, axis=-1, keepdims=True) - vhat * jnp.mean(dvh * vhat, axis=-1, keepdims=True))
        dzv_ref[...] = (dv * _gelu_grad(zv)).astype(BF)

    return pl.pallas_call(
        body,
        name="sgu_bwd",
        grid=(t // tm,),
        in_specs=[
            pl.BlockSpec((tm, d), lambda i: (i, 0)),
            pl.BlockSpec((tm, d), lambda i: (i, 1)),
            pl.BlockSpec((tm, d), lambda i: (i, 0)),
            pl.BlockSpec(memory_space=pl.ANY),
            pl.BlockSpec((1, d), lambda i: (0, 0)),
            pl.BlockSpec(w_s.shape, lambda i: (0, 0, 0)),
            pl.BlockSpec(b_col.shape, lambda i: (0, 0, 0)),
        ],
        out_specs=[
            pl.BlockSpec((tm, 2 * d), lambda i: (i, 0)),
            pl.BlockSpec(w_s.shape, lambda i: (0, 0, 0)),
            pl.BlockSpec(b_col.shape, lambda i: (0, 0, 0)),
            pl.BlockSpec((1, d), lambda i: (0, 0)),
        ],
        out_shape=[
            jax.ShapeDtypeStruct(dz.shape, BF),
            jax.ShapeDtypeStruct(w_s.shape, F32),
            jax.ShapeDtypeStruct(b_col.shape, F32),
            jax.ShapeDtypeStruct((1, d), F32),
        ],
        scratch_shapes=[pltpu.VMEM((tm, d), F32)],
        input_output_aliases={3: 0},
        compiler_params=_params(("arbitrary",)),
    )(z, z, da, dz, g_sgu, w_s, b_col)


def _shift_down(x, k, row):
    return jnp.where(row >= k, pltpu.roll(x, k, 0), 0.0)


def _shift_up(x, k, row, t):
    return jnp.where(row < t - k, pltpu.roll(x, t - k, 0), 0.0)


def _doublings(window):
    steps = int(math.log2(window))
    assert 2 ** steps == window
    return [2 ** s for s in range(steps)]


def _pool_diff(c, window, row):
    s = c
    for k in _doublings(window):
        s = s + _shift_down(s, k, row)
    count = jnp.minimum(row + 1, window).astype(F32)
    return s / count - c, count


def _pool_fwd(z, pool_w, pool_scale, t, d):
    dgp = d // len(POOL_WINDOWS)
    cblk = (2 * d) // dgp

    def body(zc_ref, w_ref, s_ref, b_ref):
        row = lax.broadcasted_iota(jnp.int32, (t, 1), 0)
        for gi, window in enumerate(POOL_WINDOWS):
            @pl.when(pl.program_id(0) == gi)
            def _(window=window):
                diff, _ = _pool_diff(zc_ref[...].astype(F32), window, row)
                out = jnp.dot(diff.astype(BF), w_ref[...], preferred_element_type=F32)
                b_ref[...] = (out * s_ref[...]).astype(BF)

    return pl.pallas_call(
        body,
        name="pool_fwd",
        grid=(len(POOL_WINDOWS),),
        in_specs=[
            pl.BlockSpec((t, dgp), lambda g: (0, cblk + g)),
            pl.BlockSpec((None, dgp, dgp), lambda g: (g, 0, 0)),
            pl.BlockSpec((1, dgp), lambda g: (0, g)),
        ],
        out_specs=pl.BlockSpec((t, dgp), lambda g: (0, g)),
        out_shape=jax.ShapeDtypeStruct((t, d), BF),
        compiler_params=_params(("arbitrary",)),
    )(z, pool_w, pool_scale)


def _pool_bwd(z, db, dz, pool_w, pool_scale, t, d):
    dgp = d // len(POOL_WINDOWS)
    cblk = (2 * d) // dgp

    def body(zc_ref, db_ref, dz_in, w_ref, s_ref, dzc_ref, dw_ref, ds_ref):
        del dz_in
        row = lax.broadcasted_iota(jnp.int32, (t, 1), 0)
        for gi, window in enumerate(POOL_WINDOWS):
            @pl.when(pl.program_id(0) == gi)
            def _(window=window):
                diff, count = _pool_diff(zc_ref[...].astype(F32), window, row)
                diff = diff.astype(BF)
                w = w_ref[...]
                dbv = db_ref[...].astype(F32)
                out = jnp.dot(diff, w, preferred_element_type=F32)
                ds_ref[...] = jnp.sum(dbv * out, axis=0, keepdims=True)
                dout = (dbv * s_ref[...]).astype(BF)
                dw_ref[...] = lax.dot_general(diff, dout, _DN["tn"], preferred_element_type=F32).astype(BF)
                ddiff = lax.dot_general(dout, w, _DN["nt"], preferred_element_type=F32)
                s = ddiff / count
                for k in _doublings(window):
                    s = s + _shift_up(s, k, row, t)
                dzc_ref[...] = (s - ddiff).astype(BF)

    return pl.pallas_call(
        body,
        name="pool_bwd",
        grid=(len(POOL_WINDOWS),),
        in_specs=[
            pl.BlockSpec((t, dgp), lambda g: (0, cblk + g)),
            pl.BlockSpec((t, dgp), lambda g: (0, g)),
            pl.BlockSpec(memory_space=pl.ANY),
            pl.BlockSpec((None, dgp, dgp), lambda g: (g, 0, 0)),
            pl.BlockSpec((1, dgp), lambda g: (0, g)),
        ],
        out_specs=[
            pl.BlockSpec((t, dgp), lambda g: (0, cblk + g)),
            pl.BlockSpec((None, dgp, dgp), lambda g: (g, 0, 0)),
            pl.BlockSpec((1, dgp), lambda g: (0, g)),
        ],
        out_shape=[
            jax.ShapeDtypeStruct(dz.shape, BF),
            jax.ShapeDtypeStruct(pool_w.shape, BF),
            jax.ShapeDtypeStruct((1, d), F32),
        ],
        input_output_aliases={2: 0},
        compiler_params=_params(("arbitrary",)),
    )(z, db, dz, pool_w, pool_scale)


def _mix_dy(dm, w_o, z, ya, yb, t, d, tm):
    def body(dm_ref, w_ref, gate_ref, ya_ref, yb_ref, dyab_ref, dz_ref, dy_ref):
        j = pl.program_id(1)

        @pl.when(j == 0)
        def _():
            dy_ref[...] = lax.dot_general(dm_ref[...], w_ref[...], _DN["nt"], preferred_element_type=F32)

        dy = dy_ref[...]
        s = _sig(gate_ref[...].astype(F32))
        yv = jnp.where(j == 0, ya_ref[...], yb_ref[...]).astype(F32)
        dyab_ref[...] = (dy * s).astype(BF)
        dz_ref[...] = (dy * yv * s * (1.0 - s)).astype(BF)

    row = pl.BlockSpec((tm, d), lambda i, j: (i, 0))
    return pl.pallas_call(
        body,
        name="mix_dy",
        grid=(t // tm, 2),
        in_specs=[row, pl.BlockSpec((d, d), lambda i, j: (0, 0)), pl.BlockSpec((tm, d), lambda i, j: (i, 3 + j)), row, row],
        out_specs=[pl.BlockSpec((tm, d), lambda i, j: (i, j)), pl.BlockSpec((tm, d), lambda i, j: (i, 3 + j))],
        out_shape=[jax.ShapeDtypeStruct((t, 2 * d), BF), jax.ShapeDtypeStruct((t, 5 * d), BF)],
        scratch_shapes=[pltpu.VMEM((tm, d), F32)],
        compiler_params=_params(("arbitrary", "arbitrary")),
    )(dm, w_o, z, ya, yb)


def _everyone():
    x, y, c = _place()
    return [(1 - x if dd & 4 else x, 1 - y if dd & 2 else y, 1 - c if dd & 1 else c) for dd in range(1, N_DEV)]


def _direct_plan(n):
    def plan(refs, send_sems, recv_sems):
        x, y, c = _place()
        me = 4 * x + 2 * y + c
        cps = []
        for i in range(n):
            for k, (px, py, pc) in enumerate(_everyone()):
                cps.append(_remote(refs[i].at[4 * px + 2 * py + pc], refs[n + i].at[me], send_sems, recv_sems,
                                   7 * i + k, (px, py, pc)))
        return cps

    def local(refs):
        x, y, c = _place()
        me = 4 * x + 2 * y + c
        return [(refs[i].at[me], refs[n + i].at[me]) for i in range(n)]

    return plan, local


def _sc_scatter_direct(name, collective_id, parts):
    n = len(parts)
    p8 = [a.reshape(N_DEV, a.shape[0] // N_DEV, a.shape[1]) for a in parts]
    plan, local = _direct_plan(n)
    return _sc_exchange(name, collective_id, p8, [jax.ShapeDtypeStruct(a.shape, a.dtype) for a in p8], plan,
                        7 * n, _everyone, local=local)


def _stack_rows(arrays):
    parts, starts, row = [], [], 0
    for a in arrays:
        pad = -a.shape[0] % 8
        starts.append(row)
        parts += [a, jnp.zeros((pad, a.shape[1]), a.dtype)] if pad else [a]
        row += a.shape[0] + pad
    return jnp.concatenate(parts, axis=0), starts


def _adam_replicated(name, stacks, layout, weights, with_loss, deps=()):
    n_st, n_par = len(stacks), len(weights)

    def body(*refs):
        st_refs = refs[:n_st]
        w_refs = refs[n_st:n_st + 3 * n_par]
        out_refs = refs[n_st + 3 * n_par + len(deps):]

        def summed(which, row, rows):
            g = st_refs[which][0, pl.ds(row, rows), :]
            for k in range(1, N_DEV):
                g = g + st_refs[which][k, pl.ds(row, rows), :]
            return g

        for i in range(n_par):
            w_ref, m_ref, v_ref = w_refs[3 * i:3 * i + 3]
            g = summed(layout[i][0], layout[i][1], w_ref.shape[0])
            dlt, mm, vv = _adam_vals(w_ref[...], g, m_ref[...], v_ref[...])
            for r, val in zip(out_refs[4 * i:4 * i + 4], (g, dlt, mm, vv)):
                r[...] = val
        if with_loss:
            out_refs[4 * n_par][...] = jnp.sum(summed(layout[n_par][0], layout[n_par][1], 1), axis=1, keepdims=True)

    flat_w = [a for wmv in weights for a in wmv]
    out_shape = [jax.ShapeDtypeStruct(wmv[0].shape, F32) for wmv in weights for _ in range(4)]
    if with_loss:
        out_shape.append(jax.ShapeDtypeStruct((1, 1), F32))
    vmem = pl.BlockSpec(memory_space=pltpu.VMEM)
    res = pl.pallas_call(
        body,
        name=name,
        in_specs=[vmem] * (n_st + len(flat_w)) + [pl.BlockSpec(memory_space=pl.ANY)] * len(deps),
        out_specs=[vmem] * len(out_shape),
        out_shape=out_shape,
        compiler_params=_params(),
    )(*stacks, *flat_w, *deps)
    return [res[4 * i:4 * i + 4] for i in range(n_par)], res[4 * n_par] if with_loss else None


def kernel(x, p, ffn1_pre_g, ffn1_w_gate, ffn1_w_up, ffn1_w_down, ffn1_post_g, mix_pre_g, w_in, sgu_norm_g, sgu_w, sgu_b, pool_w, pool_scale, w_out_a, w_out_b, w_o, mix_post_g, ffn2_pre_g, ffn2_w_gate, ffn2_w_up, ffn2_w_down, ffn2_post_g, ple_pre_g, ple_w_gate, ple_w_proj, ple_post_g, loss_target, m_ffn1_pre_g, m_ffn1_w_gate, m_ffn1_w_up, m_ffn1_w_down, m_ffn1_post_g, m_mix_pre_g, m_w_in, m_sgu_norm_g, m_sgu_w, m_sgu_b, m_pool_w, m_pool_scale, m_w_out_a, m_w_out_b, m_w_o, m_mix_post_g, m_ffn2_pre_g, m_ffn2_w_gate, m_ffn2_w_up, m_ffn2_w_down, m_ffn2_post_g, m_ple_pre_g, m_ple_w_gate, m_ple_w_proj, m_ple_post_g, v_ffn1_pre_g, v_ffn1_w_gate, v_ffn1_w_up, v_ffn1_w_down, v_ffn1_post_g, v_mix_pre_g, v_w_in, v_sgu_norm_g, v_sgu_w, v_sgu_b, v_pool_w, v_pool_scale, v_w_out_a, v_w_out_b, v_w_o, v_mix_post_g, v_ffn2_pre_g, v_ffn2_w_gate, v_ffn2_w_up, v_ffn2_w_down, v_ffn2_post_g, v_ple_pre_g, v_ple_w_gate, v_ple_w_proj, v_ple_post_g):
    args = dict(locals())
    names = ["ffn1_pre_g", "ffn1_w_gate", "ffn1_w_up", "ffn1_w_down", "ffn1_post_g", "mix_pre_g", "w_in",
             "sgu_norm_g", "sgu_w", "sgu_b", "pool_w", "pool_scale", "w_out_a", "w_out_b", "w_o", "mix_post_g",
             "ffn2_pre_g", "ffn2_w_gate", "ffn2_w_up", "ffn2_w_down", "ffn2_post_g", "ple_pre_g", "ple_w_gate",
             "ple_w_proj", "ple_post_g"]
    w = {k: args[k][0] for k in names}
    mom = {k: args["m_" + k][0] for k in names}
    var = {k: args["v_" + k][0] for k in names}

    assert x.shape[0] == 1 and p.shape[:2] == (1, 1)
    t, d = x.shape[1], x.shape[2]
    f = ffn1_w_gate.shape[2] * N_DEV
    d_in = w_in.shape[2] * N_DEV
    d_ple = p.shape[3]
    n_pool = len(POOL_WINDOWS)
    dgp = d // n_pool
    assert d_in == 5 * d and t % CHUNK == 0
    tm = _tile(t, 512, CHUNK)
    xs, ps, target = x[0], p[0, 0], loss_target[0]

    col_sharded = ("ffn1_w_gate", "ffn1_w_up", "ffn2_w_gate", "ffn2_w_up", "w_in", "ple_w_proj")

    def shard_of(k):
        if k in col_sharded:
            return w[k].T.astype(BF)
        if k == "pool_w":
            return w[k].reshape(-1, dgp).astype(BF)
        return w[k].astype(BF)

    rows_pw = dgp // N_DEV
    b_col = w["sgu_b"][:, :, None]
    gains = {k: w[k][None, :] for k in names if w[k].ndim == 1}
    full = {}

    groups = [["ffn1_w_gate"], ["ffn1_w_up"], ["ffn1_w_down"], ["w_in"], ["pool_w", "w_out_a", "w_out_b", "w_o"],
              ["ffn2_w_gate", "ffn2_w_up", "ffn2_w_down", "ple_w_gate", "ple_w_proj"]]
    def gather(gi, behind=None):
        shards = [shard_of(k) for k in groups[gi]]
        if behind is not None:
            shards = lax.optimization_barrier((shards, behind))[0]
        full.update(zip(groups[gi], _sc_gather("gather%d" % gi, gi, shards)))

    gather(0)
    xn1 = _prenorm("ffn1_prenorm", xs, gains["ffn1_pre_g"], t, d, tm)
    gather(1, xn1)
    gather(2, xn1)
    gather(3, xn1)
    g1, u1, hid1 = _ffn_up_split("ffn1", xn1, full["ffn1_w_gate"], full["ffn1_w_up"], t, d, f, tm)
    gather(4, hid1)
    f1, h1, xn2 = _ffn_down("ffn1", hid1, full["ffn1_w_down"], xs, gains["ffn1_post_g"], gains["mix_pre_g"], t, d, f, tm)
    gather(5, h1)

    pool_full = full["pool_w"].reshape(N_DEV, n_pool, rows_pw, dgp).transpose(1, 0, 2, 3).reshape(n_pool, dgp, dgp)
    z = _matmul("mix_in", [dict(a=xn2, b=full["w_in"], mode="nt", acc=0, tk=d)], m=t, n=d_in, tm=_tile(t, 1024, CHUNK), tn=d,
                outs=[("mn", (t, d_in), BF, 0)], epilogue=lambda accs, ex: accs, n_acc=1, j_outer=True)[0]
    a_br = _sgu_fwd(z, gains["sgu_norm_g"], w["sgu_w"], b_col, t, d, tm)
    b_br = _pool_fwd(z, pool_full, gains["pool_scale"], t, d)

    def merge_epi(accs, ex):
        ya, yb = accs
        ga, gb = ex[0].astype(F32), ex[1].astype(F32)
        return [ya, yb, _sig(ga) * ya + _sig(gb) * yb]

    ya, yb, y = _matmul(
        "mix_merge",
        [dict(a=a_br, b=full["w_out_a"], mode="nn", acc=0, tk=d), dict(a=b_br, b=full["w_out_b"], mode="nn", acc=1, tk=d)],
        m=t, n=d, tm=tm, tn=d, extras=[("mn", z, 3), ("mn", z, 4)],
        outs=[("mn", (t, d), BF, 0)] * 3, epilogue=merge_epi, n_acc=2)

    def proj_epi(accs, ex):
        mm = accs[0]
        hh, gg, gn = ex
        h_new = hh + mm * _rms_r(mm) * gg
        return [mm, h_new, h_new * _rms_r(h_new) * gn]

    m_out, h2, xn3 = _matmul(
        "mix_proj", [dict(a=y, b=full["w_o"], mode="nn", acc=0, tk=d)], m=t, n=d, tm=tm, tn=d,
        extras=[("mn", h1, 0), ("n", gains["mix_post_g"]), ("n", gains["ffn2_pre_g"])],
        outs=[("mn", (t, d), F32, 0)] * 2 + [("mn", (t, d), BF, 0)], epilogue=proj_epi, n_acc=1)

    g2, u2, hid2 = _ffn_up("ffn2", xn3, full["ffn2_w_gate"], full["ffn2_w_up"], t, d, f, tm)
    f2, h3, xn4 = _ffn_down("ffn2", hid2, full["ffn2_w_down"], h2, gains["ffn2_post_g"], gains["ple_pre_g"], t, d, f, tm)

    def ple_epi(accs, ex):
        gl, e = accs
        hh, tgt, gg = ex
        q = _sig(gl) * e
        err = hh + q * _rms_r(q) * gg - tgt
        return [gl, e, err * (1.0 / d), jnp.sum(err * err, axis=0, keepdims=True)]

    gl, e_ple, dh4, loss_vec = _matmul(
        "ple_fwd",
        [dict(a=xn4, b=full["ple_w_gate"], mode="nn", acc=0, tk=d), dict(a=ps, b=full["ple_w_proj"], mode="nt", acc=1, tk=d_ple)],
        m=t, n=d, tm=tm, tn=d, extras=[("mn", h3, 0), ("mn", target, 0), ("n", gains["ple_post_g"])],
        outs=[("mn", (t, d), F32, 0)] * 3 + [("acc", (1, d), F32)], epilogue=ple_epi, n_acc=2)

    tk_t = _tile(t, TOKEN_K_TILE)
    tk_1 = _tile(t, 2 * TOKEN_K_TILE)

    def ple_post_fn(v):
        dy, gl_v, e_v, gg = v
        s = _sig(gl_v)
        dq, dg = _rms_bwd(s * e_v, gg, dy)
        return [dq * e_v * s * (1.0 - s), dq * s, dg]

    dgl, de, dg_ple_post = _rowwise(
        "ple_post_bwd", [("row", dh4, d, 0), ("row", gl, d, 0), ("row", e_ple, d, 0), ("full", gains["ple_post_g"])],
        [("row", (t, d), BF, d, 0), ("row", (t, d), BF, d, 0), ("acc", (1, d), F32)], ple_post_fn, t=t, tm=tm)
    ident = lambda accs, ex: accs
    dw_ple_gate = _matmul("ple_dwg", [dict(a=xn4, b=dgl, mode="tn", acc=0, tk=tk_1)], m=d, n=d, tm=d, tn=d,
                          nk=t // tk_1, outs=[("mn", (d, d), BF, 0)], epilogue=ident, n_acc=1)[0]
    dw_ple_proj_t = _matmul("ple_dwp", [dict(a=de, b=ps, mode="tn", acc=0, tk=tk_1)], m=d, n=d_ple, tm=d, tn=d_ple,
                            nk=t // tk_1, outs=[("mn", (d, d_ple), BF, 0)], epilogue=ident, n_acc=1)[0]
    dh3, dg_ple_pre, df2, dg_f2_post = _matmul(
        "ple_dx", [dict(a=dgl, b=full["ple_w_gate"], mode="nt", acc=0, tk=d)], m=t, n=d, tm=tm, tn=d,
        extras=_dx_extras(h3, dh4, gains["ple_pre_g"], f2, gains["ffn2_post_g"]),
        outs=_dx_outs(t, d), epilogue=_dx_epi(0.5), n_acc=1)

    staged = []
    n_gather = len(groups)

    def scatter(keys, parts):
        k = len(staged)
        staged.append((keys, _sc_scatter_direct("scatter%d" % k, n_gather + k, parts)))

    dgate2, dup2 = _ffn_dhid("ffn2", df2, full["ffn2_w_down"], g2, u2, t, d, f, tm)
    dwd2 = _ffn_dwd("ffn2", hid2, df2, t, d, f)[0]
    dwg2 = _ffn_dw_in("ffn2_dwg", dgate2, xn3, t, d, f)
    dwu2 = _ffn_dw_in("ffn2_dwu", dup2, xn3, t, d, f)
    scatter(["ple_w_gate", "ple_w_proj", "ffn2_w_down", "ffn2_w_gate", "ffn2_w_up"],
            [dw_ple_gate, dw_ple_proj_t, dwd2, dwg2, dwu2])
    dh2, dg_f2_pre, dm, dg_mix_post = _ffn_dx(
        "ffn2", dh3, h2, dgate2, dup2, full["ffn2_w_gate"], full["ffn2_w_up"], gains["ffn2_pre_g"], t, d, f, tm,
        deps=[dw_ple_gate, dw_ple_proj_t, dwd2, dwg2, dwu2], before=(m_out, gains["mix_post_g"], 1.0))

    dyab, dz = _mix_dy(dm, full["w_o"], z, ya, yb, t, d, tm)
    dw_o = _matmul("mix_dwo", [dict(a=y, b=dm, mode="tn", acc=0, tk=tk_1)], m=d, n=d, tm=d, tn=d, nk=t // tk_1,
                   outs=[("mn", (d, d), BF, 0)], epilogue=ident, n_acc=1)[0]
    dw_out_a, dw_out_b = _matmul(
        "mix_dwab",
        [dict(a=a_br, b=dyab, mode="tn", acc=0, tk=tk_t),
         dict(a=b_br, b=dyab, mode="tn", acc=1, tk=tk_t, b_off=(0, 1))],
        m=d, n=d, tm=d, tn=d, nk=t // tk_t, outs=[("mn", (d, d), BF, 0)] * 2, epilogue=ident, n_acc=2)
    da, db = _matmul(
        "mix_dab",
        [dict(a=dyab, b=full["w_out_a"], mode="nt", acc=0, tk=d),
         dict(a=dyab, b=full["w_out_b"], mode="nt", acc=1, tk=d, a_off=(0, 1))],
        m=t, n=d, tm=tm, tn=d, outs=[("mn", (t, d), BF, 0)] * 2, epilogue=ident, n_acc=2)
    dz, dsgu_w, dsgu_b, dg_sgu = _sgu_bwd(z, da, dz, gains["sgu_norm_g"], w["sgu_w"], b_col, t, d, _tile(t, 256, CHUNK))
    dz, dpool_w, dg_pool = _pool_bwd(z, db, dz, pool_full, gains["pool_scale"], t, d)
    dw_in_t = _matmul(
        "mix_dwin", [dict(a=dz, b=xn2, mode="tn", acc=0, tk=tk_1)], m=d_in, n=d, tm=d, tn=d,
        nk=t // tk_1, outs=[("mn", (d_in, d), BF, 0)], epilogue=ident, n_acc=1)[0]
    dpool_rows = dpool_w.reshape(n_pool, N_DEV, rows_pw, dgp).transpose(1, 0, 2, 3).reshape(N_DEV * n_pool * rows_pw, dgp)
    scatter(["w_o", "w_out_a", "w_out_b", "pool_w", "w_in"], [dw_o, dw_out_a, dw_out_b, dpool_rows, dw_in_t])
    dh1, dg_mix_pre, df1, dg_f1_post = _matmul(
        "mix_dx", [dict(a=dz, b=full["w_in"], mode="nn", acc=0, tk=d_in // 2)], m=t, n=d, tm=tm, tn=d, nk=2,
        extras=_dx_extras(h1, dh2, gains["mix_pre_g"], f1, gains["ffn1_post_g"]),
        outs=_dx_outs(t, d), epilogue=_dx_epi(0.5), n_acc=1, deps=[dw_o, dw_out_a, dpool_rows, dw_in_t])

    dgate1, dup1 = _ffn_dhid("ffn1", df1, full["ffn1_w_down"], g1, u1, t, d, f, tm)
    dwd1 = _ffn_dwd("ffn1", hid1, df1, t, d, f)[0]
    scatter(["ffn1_w_down"], [dwd1])
    dwg1 = _ffn_dw_in("ffn1_dwg", dgate1, xn1, t, d, f, deps=[dwd1])
    scatter(["ffn1_w_gate"], [dwg1])
    dwu1 = _ffn_dw_in("ffn1_dwu", dup1, xn1, t, d, f, deps=[dwg1])
    scatter(["ffn1_w_up"], [dwu1])
    grad_x, dg_f1_pre = _ffn_dx("ffn1", dh1, xs, dgate1, dup1, full["ffn1_w_gate"], full["ffn1_w_up"],
                                gains["ffn1_pre_g"], t, d, f, tm, deps=[dwu1])

    grad, delta, new_m, new_v = {}, {}, {}, {}

    def update(group, partial_sums, behind):
        done = []
        for k, st in zip(group, partial_sums):
            if k in col_sharded and w[k].shape[1] % LANES:
                res = _reduce_adam("adam_" + k, st, w[k].T, mom[k].T, var[k].T, deps=behind)
                grad[k], delta[k], new_m[k], new_v[k] = [r.T for r in res]
                done.append(res[1])
                continue
            elif k in col_sharded:
                grad[k] = _reduce_only("sum_" + k, st, deps=behind).T
                delta[k], new_m[k], new_v[k] = _adam_only("adam_" + k, w[k], grad[k], mom[k], var[k])
            else:
                shape = w[k].shape
                flat = (-1, shape[-1])
                res = _reduce_adam("adam_" + k, st, w[k].reshape(flat), mom[k].reshape(flat), var[k].reshape(flat),
                                   deps=behind)
                grad[k], delta[k], new_m[k], new_v[k] = [r.reshape(shape) for r in res]
            done.append(delta[k])
        return done

    gain_grads = {
        "ffn1_post_g": dg_f1_post, "mix_pre_g": dg_mix_pre, "sgu_norm_g": dg_sgu, "pool_scale": dg_pool,
        "mix_post_g": dg_mix_post, "ffn2_pre_g": dg_f2_pre, "ffn2_post_g": dg_f2_post, "ple_pre_g": dg_ple_pre,
        "ple_post_g": dg_ple_post,
    }
    gain_names = list(gain_grads)
    n_mix = N_SGU_GROUPS * CHUNK
    rep_shape = {k: (1, d) for k in gain_names + ["ffn1_pre_g"]}
    rep_shape.update(sgu_b=(N_SGU_GROUPS, CHUNK), sgu_w=(n_mix, CHUNK))

    def replicated(tag, cid, arrays, rep_names, layout, with_loss, behind):
        plan_s, local_s = _broadcast_plan(len(arrays))
        stacks = _sc_exchange(
            tag, cid, arrays, [jax.ShapeDtypeStruct((N_DEV,) + a.shape, F32) for a in arrays],
            plan_s, len(arrays) * (N_DEV - 1), _everyone, local=local_s)
        res, loss_sum = _adam_replicated(
            "adam_" + tag, stacks, layout,
            [tuple(src[k].reshape(rep_shape[k]) for src in (w, mom, var)) for k in rep_names], with_loss, deps=behind)
        for k, quad in zip(rep_names, res):
            for dst, val in zip((grad, delta, new_m, new_v), quad):
                dst[k] = val.reshape(w[k].shape)
        return loss_sum, [res[-1][1]]

    wide, wide_rows = _stack_rows([gain_grads[k] for k in gain_names] + [loss_vec * (0.5 / d)])
    narrow, narrow_rows = _stack_rows([dsgu_b.reshape(N_SGU_GROUPS, CHUNK), dsgu_w.reshape(n_mix, CHUNK)])
    layout = [(0, r) for r in wide_rows[:-1]] + [(1, narrow_rows[0]), (1, narrow_rows[1]), (0, wide_rows[-1])]
    done = [grad_x]
    for keys, partial_sums in staged[:2]:
        done = update(keys, partial_sums, done)
    loss, done = replicated("small", n_gather + len(staged), [wide, narrow], gain_names + ["sgu_b", "sgu_w"], layout,
                            True, done)
    for keys, partial_sums in staged[2:]:
        done = update(keys, partial_sums, done)
    late, late_rows = _stack_rows([dg_f1_pre])
    replicated("late", n_gather + len(staged) + 1, [late], ["ffn1_pre_g"], [(0, late_rows[0])], False, done)

    out = [loss[0, 0], grad_x[None]]
    for group in (grad, delta, new_m, new_v):
        out += [group[k][None] for k in names]
    return tuple(out)
```

```python
import math

import jax
import jax.numpy as jnp
from jax import lax
from jax.experimental import pallas as pl
from jax.experimental.pallas import tpu as pltpu
from jax.experimental.pallas import tpu_sc as plsc

EPS = 1e-6
CHUNK = 128
N_SGU_GROUPS = 4
POOL_WINDOWS = (2, 4, 8, 16)
ADAM_LR = 0.001
ADAM_B1 = 0.9
ADAM_B2 = 0.999
ADAM_EPS = 1e-08
ADAM_WD = 0.01
ADAM_STEP = 10

N_DEV = 8
MESH = pl.DeviceIdType.MESH
V7X_VMEM_BYTES = 64 * 1024 * 1024
VMEM_LIMIT = V7X_VMEM_BYTES - 8 * 1024 * 1024
TOKEN_K_TILE = 1024
LANES = 128
UPDATE_ROWS = 256
BF = jnp.bfloat16
F32 = jnp.float32

_DN = {
    "nn": (((1,), (0,)), ((), ())),
    "nt": (((1,), (1,)), ((), ())),
    "tn": (((0,), (0,)), ((), ())),
}


def _params(sem=None):
    return pltpu.CompilerParams(dimension_semantics=sem, vmem_limit_bytes=VMEM_LIMIT)


def _tile(n, target, align=128):
    t = min(n, target)
    t -= t % align
    while t >= align:
        if n % t == 0:
            return t
        t -= align
    return n


def _sig(x):
    return 0.5 + 0.5 * jnp.tanh(0.5 * x)


_GELU_K = math.sqrt(2.0 / math.pi)
_GELU_C = 0.044715


def _gelu(x):
    return 0.5 * x * (1.0 + jnp.tanh(_GELU_K * (x + _GELU_C * x * x * x)))


def _gelu_grad(x):
    t = jnp.tanh(_GELU_K * (x + _GELU_C * x * x * x))
    return 0.5 * (1.0 + t) + 0.5 * x * (1.0 - t * t) * _GELU_K * (1.0 + 3.0 * _GELU_C * x * x)


def _rms_r(x):
    return lax.rsqrt(jnp.mean(x * x, axis=-1, keepdims=True) + EPS)


def _rms_bwd(x, g, dy):
    r = _rms_r(x)
    xh = x * r
    gy = dy * g
    dx = r * (gy - xh * jnp.mean(xh * gy, axis=-1, keepdims=True))
    return dx, jnp.sum(dy * xh, axis=0, keepdims=True)


def _matmul(name, mats, *, m, n, tm, tn, nk=1, extras=(), outs, epilogue, n_acc, j_outer=False, deps=()):
    ni, nj = m // tm, n // tn
    assert ni * tm == m and nj * tn == n, (name, m, n, tm, tn)
    if j_outer:
        grid = (nj, ni, nk)

        def ij(g0, g1):
            return g1, g0
    else:
        grid = (ni, nj, nk)

        def ij(g0, g1):
            return g0, g1

    in_specs, args = [], []
    for mt in mats:
        mode, tk = mt["mode"], mt["tk"]
        ao, bo = mt.get("a_off", (0, 0)), mt.get("b_off", (0, 0))
        if mode == "tn":
            sa = pl.BlockSpec((tk, tm), lambda g0, g1, kk, ao=ao: (ao[0] + kk, ao[1] + ij(g0, g1)[0]))
        else:
            sa = pl.BlockSpec((tm, tk), lambda g0, g1, kk, ao=ao: (ao[0] + ij(g0, g1)[0], ao[1] + kk))
        if mode == "nt":
            sb = pl.BlockSpec((tn, tk), lambda g0, g1, kk, bo=bo: (bo[0] + ij(g0, g1)[1], bo[1] + kk))
        else:
            sb = pl.BlockSpec((tk, tn), lambda g0, g1, kk, bo=bo: (bo[0] + kk, bo[1] + ij(g0, g1)[1]))
        in_specs += [sa, sb]
        args += [mt["a"], mt["b"]]
    n_mat_refs = len(args)
    for ex in extras:
        if ex[0] == "mn":
            in_specs.append(pl.BlockSpec((tm, tn), lambda g0, g1, kk, c=ex[2]: (ij(g0, g1)[0], c + ij(g0, g1)[1])))
        else:
            in_specs.append(pl.BlockSpec((1, tn), lambda g0, g1, kk: (0, ij(g0, g1)[1])))
        args.append(ex[1])
    n_ex_end = len(args)
    in_specs += [pl.BlockSpec(memory_space=pl.ANY)] * len(deps)
    args += list(deps)
    n_in = len(args)
    out_specs, out_shape = [], []
    for o in outs:
        if o[0] == "mn":
            out_specs.append(pl.BlockSpec((tm, tn), lambda g0, g1, kk, c=o[3]: (ij(g0, g1)[0], c + ij(g0, g1)[1])))
        else:
            assert nj == 1, name
            out_specs.append(pl.BlockSpec((1, tn), lambda g0, g1, kk: (0, 0)))
        out_shape.append(jax.ShapeDtypeStruct(o[1], o[2]))
    n_out = len(outs)

    def body(*refs):
        mat_refs = refs[:n_mat_refs]
        ex_refs = refs[n_mat_refs:n_ex_end]
        out_refs = refs[n_in:n_in + n_out]
        acc_refs = refs[n_in + n_out:]
        i = ij(pl.program_id(0), pl.program_id(1))[0]
        kk = pl.program_id(2)

        def products():
            res = [None] * n_acc
            for idx, mt in enumerate(mats):
                a = mat_refs[2 * idx][...].astype(BF)
                b = mat_refs[2 * idx + 1][...].astype(BF)
                p = lax.dot_general(a, b, _DN[mt["mode"]], preferred_element_type=F32)
                q = mt["acc"]
                res[q] = p if res[q] is None else res[q] + p
            return res

        def finish(accs):
            vals = epilogue(accs, [r[...] for r in ex_refs])
            for o, r, v in zip(outs, out_refs, vals):
                if o[0] == "mn":
                    r[...] = v.astype(o[2])
                else:
                    @pl.when(i == 0)
                    def _():
                        r[...] = v

                    @pl.when(i != 0)
                    def _():
                        r[...] += v

        if nk == 1:
            finish(products())
        else:
            res = products()

            @pl.when(kk == 0)
            def _():
                for q in range(n_acc):
                    acc_refs[q][...] = res[q]

            @pl.when(kk != 0)
            def _():
                for q in range(n_acc):
                    acc_refs[q][...] += res[q]

            @pl.when(kk == nk - 1)
            def _():
                finish([r[...] for r in acc_refs])

    scratch = [pltpu.VMEM((tm, tn), F32) for _ in range(n_acc)] if nk > 1 else []
    return pl.pallas_call(
        body,
        name=name,
        grid=grid,
        in_specs=in_specs,
        out_specs=out_specs,
        out_shape=out_shape,
        scratch_shapes=scratch,
        compiler_params=_params(("arbitrary", "arbitrary", "arbitrary")),
    )(*args)


def _rowwise(name, ins, outs, fn, *, t, tm, deps=()):
    ni = t // tm
    assert ni * tm == t, (name, t, tm)
    in_specs, args = [], []
    for s in ins:
        if s[0] == "row":
            in_specs.append(pl.BlockSpec((tm, s[2]), lambda i, c=s[3]: (i, c)))
        else:
            nd = s[1].ndim
            in_specs.append(pl.BlockSpec(s[1].shape, lambda i, nd=nd: (0,) * nd))
        args.append(s[1])
    out_specs, out_shape = [], []
    for o in outs:
        if o[0] == "row":
            out_specs.append(pl.BlockSpec((tm, o[3]), lambda i, c=o[4]: (i, c)))
        else:
            nd = len(o[1])
            out_specs.append(pl.BlockSpec(o[1], lambda i, nd=nd: (0,) * nd))
        out_shape.append(jax.ShapeDtypeStruct(o[1], o[2]))
    n_read = len(args)
    in_specs += [pl.BlockSpec(memory_space=pl.ANY)] * len(deps)
    args += list(deps)
    n_in = len(args)

    def body(*refs):
        i = pl.program_id(0)
        vals = fn([r[...] for r in refs[:n_read]])
        for o, r, v in zip(outs, refs[n_in:], vals):
            if o[0] == "row":
                r[...] = v.astype(o[2])
            else:
                @pl.when(i == 0)
                def _():
                    r[...] = v

                @pl.when(i != 0)
                def _():
                    r[...] += v

    return pl.pallas_call(
        body,
        name=name,
        grid=(ni,),
        in_specs=in_specs,
        out_specs=out_specs,
        out_shape=out_shape,
        compiler_params=_params(("arbitrary",)),
    )(*args)


def _place():
    return lax.axis_index("x"), lax.axis_index("y"), lax.axis_index("c")


def _remote(src, dst, send_sems, recv_sems, k, to):
    return pltpu.make_async_remote_copy(
        src_ref=src, dst_ref=dst, send_sem=send_sems.at[k], recv_sem=recv_sems.at[k],
        device_id=to, device_id_type=MESH)


def _chips(x, y):
    return [(1 - x, y), (x, 1 - y), (1 - x, 1 - y)]


def _broadcast_plan(n):
    def plan(refs, send_sems, recv_sems):
        x, y, c = _place()
        cps = []
        for i in range(n):
            for dd in range(1, N_DEV):
                peer = (1 - x if dd & 4 else x, 1 - y if dd & 2 else y, 1 - c if dd & 1 else c)
                cps.append(_remote(refs[i], refs[n + i].at[4 * x + 2 * y + c], send_sems, recv_sems, 7 * i + dd - 1, peer))
        return cps

    def local(refs):
        x, y, c = _place()
        return [(refs[i], refs[n + i].at[4 * x + 2 * y + c]) for i in range(n)]

    return plan, local


def _handshake(peers):
    barrier = pltpu.get_barrier_semaphore()
    for peer in peers:
        pl.semaphore_signal(barrier, inc=1, device_id=peer, device_id_type=MESH)
    pl.semaphore_wait(barrier, len(peers))


def _sequencer_call(name, collective_id, body, args, out_type, scratch_types):
    return pl.kernel(
        body,
        out_type=out_type,
        mesh=plsc.ScalarSubcoreMesh(axis_name="sequencer", num_cores=1),
        scratch_types=scratch_types,
        compiler_params=pltpu.CompilerParams(collective_id=collective_id),
        name=name,
    )(*args)


def _sc_gather(name, collective_id, shards):
    n = len(shards)

    def body(*refs):
        ins, outs = refs[:n], refs[n:2 * n]
        send_sems, recv_sems, local_sems = refs[2 * n:]
        x, y, c = _place()
        me, sibling = (x, y, c), (x, y, 1 - c)
        chips = _chips(x, y)
        _handshake([sibling] + [(*chip, c) for chip in chips])

        def slot(i, px, py, pc):
            return outs[i].at[4 * px + 2 * py + pc]

        def copy(i, k, block, to, src=None):
            return _remote(slot(i, *block) if src is None else src, slot(i, *block), send_sems, recv_sems, 7 * i + k, to)

        mine = [pltpu.make_async_copy(ins[i], slot(i, *me), local_sems.at[i]) for i in range(n)]
        for cp in mine:
            cp.start()
        first = []
        for i in range(n):
            first += [copy(i, 1 + j, me, (*chip, c), src=ins[i]) for j, chip in enumerate(chips)]
            first.append(copy(i, 0, me, sibling, src=ins[i]))
        for cp in first:
            cp.start()
        passed = []
        for i in range(n):
            for j, chip in enumerate(chips):
                copy(i, 1 + j, (*chip, c), me).wait_recv()
                fwd = copy(i, 4 + j, (*chip, c), sibling)
                fwd.start()
                passed.append(fwd)
        for i in range(n):
            copy(i, 0, sibling, me).wait_recv()
            for j, chip in enumerate(chips):
                copy(i, 4 + j, (*chip, 1 - c), me).wait_recv()
        for cp in first + passed:
            cp.wait_send()
        for cp in mine:
            cp.wait()

    outs = _sequencer_call(
        name, collective_id, body, shards,
        [jax.ShapeDtypeStruct((N_DEV,) + s.shape, s.dtype) for s in shards],
        [pltpu.SemaphoreType.DMA((7 * n,)), pltpu.SemaphoreType.DMA((7 * n,)), pltpu.SemaphoreType.DMA((n,))])
    return [o.reshape((N_DEV * s.shape[0],) + s.shape[1:]) for o, s in zip(outs, shards)]


def _sc_exchange(name, collective_id, srcs, land_shapes, plan, n_copies, peers, local=None):
    n = len(srcs)

    def body(*refs):
        bufs, send_sems, recv_sems = refs[:2 * n], refs[2 * n], refs[2 * n + 1]
        _handshake(peers())
        if local is not None:
            mine = [pltpu.make_async_copy(a, b, refs[2 * n + 2].at[k]) for k, (a, b) in enumerate(local(bufs))]
            for cp in mine:
                cp.start()
        cps = plan(bufs, send_sems, recv_sems)
        for cp in cps:
            cp.start()
        for cp in cps:
            cp.wait_send()
            cp.wait_recv()
        if local is not None:
            for cp in mine:
                cp.wait()

    scratch = [pltpu.SemaphoreType.DMA((n_copies,)), pltpu.SemaphoreType.DMA((n_copies,))]
    if local is not None:
        scratch.append(pltpu.SemaphoreType.DMA((n,)))
    return list(_sequencer_call(name, collective_id, body, srcs, land_shapes, scratch))


def _adam_vals(w, g, m, v):
    m = ADAM_B1 * m + (1.0 - ADAM_B1) * g
    v = ADAM_B2 * v + (1.0 - ADAM_B2) * (g * g)
    m_hat = m / (1.0 - ADAM_B1 ** ADAM_STEP)
    v_hat = v / (1.0 - ADAM_B2 ** ADAM_STEP)
    delta = -ADAM_LR * (m_hat / (jnp.sqrt(v_hat) + ADAM_EPS) + ADAM_WD * w)
    return delta, m, v


def _sum_parts(st_ref):
    g = st_ref[0].astype(F32)
    for k in range(1, st_ref.shape[0]):
        g = g + st_ref[k].astype(F32)
    return g


def _reduce_adam(name, st, w, m, v, deps=()):
    rows, cols = w.shape
    tr = _tile(rows, UPDATE_ROWS, 16)

    def body(st_ref, w_ref, m_ref, v_ref, *rest):
        g_out, d_out, m_out, v_out = rest[len(deps):]
        g = _sum_parts(st_ref)
        d, mm, vv = _adam_vals(w_ref[...], g, m_ref[...], v_ref[...])
        g_out[...] = g
        d_out[...] = d
        m_out[...] = mm
        v_out[...] = vv

    blk = pl.BlockSpec((tr, cols), lambda i: (i, 0))
    return pl.pallas_call(
        body,
        name=name,
        grid=(rows // tr,),
        in_specs=[pl.BlockSpec((st.shape[0], tr, cols), lambda i: (0, i, 0)), blk, blk, blk]
        + [pl.BlockSpec(memory_space=pl.ANY)] * len(deps),
        out_specs=[blk] * 4,
        out_shape=[jax.ShapeDtypeStruct(w.shape, F32)] * 4,
        compiler_params=_params(("arbitrary",)),
    )(st, w, m, v, *deps)


def _reduce_only(name, st, deps=()):
    _, rows, cols = st.shape
    tr = _tile(rows, UPDATE_ROWS, 16)

    def body(st_ref, *rest):
        rest[-1][...] = _sum_parts(st_ref)

    return pl.pallas_call(
        body,
        name=name,
        grid=(rows // tr,),
        in_specs=[pl.BlockSpec((st.shape[0], tr, cols), lambda i: (0, i, 0))]
        + [pl.BlockSpec(memory_space=pl.ANY)] * len(deps),
        out_specs=pl.BlockSpec((tr, cols), lambda i: (i, 0)),
        out_shape=jax.ShapeDtypeStruct((rows, cols), F32),
        compiler_params=_params(("arbitrary",)),
    )(st, *deps)


def _adam_only(name, w, g, m, v):
    rows, cols = w.shape
    tr = _tile(rows, UPDATE_ROWS, 16)

    def body(w_ref, g_ref, m_ref, v_ref, d_out, m_out, v_out):
        d, mm, vv = _adam_vals(w_ref[...], g_ref[...], m_ref[...], v_ref[...])
        d_out[...] = d
        m_out[...] = mm
        v_out[...] = vv

    blk = pl.BlockSpec((tr, cols), lambda i: (i, 0))
    return pl.pallas_call(
        body,
        name=name,
        grid=(rows // tr,),
        in_specs=[blk] * 4,
        out_specs=[blk] * 3,
        out_shape=[jax.ShapeDtypeStruct(w.shape, F32)] * 3,
        compiler_params=_params(("arbitrary",)),
    )(w, g, m, v)


def _prenorm(name, h, g, t, d, tm, deps=()):
    def fn(v):
        x, gg = v
        return [x * _rms_r(x) * gg]

    return _rowwise(name, [("row", h, d, 0), ("full", g)], [("row", (t, d), BF, d, 0)], fn, t=t, tm=tm, deps=deps)[0]


def _ffn_up(tag, xn, wgt, wut, t, d, f, tm):
    def up_epi(accs, ex):
        gg, uu = accs
        return [gg, uu, gg * _sig(gg) * uu]

    mats = [dict(a=xn, b=wgt, mode="nt", acc=0, tk=d), dict(a=xn, b=wut, mode="nt", acc=1, tk=d)]
    return _matmul(
        tag + "_up", mats, m=t, n=f, tm=tm, tn=_tile(f, 1408),
        outs=[("mn", (t, f), BF, 0)] * 3, epilogue=up_epi, n_acc=2, j_outer=True)


def _ffn_up_split(tag, xn, wgt, wut, t, d, f, tm):
    tf = _tile(f, 1408)
    gate = _matmul(
        tag + "_gate", [dict(a=xn, b=wgt, mode="nt", acc=0, tk=d)], m=t, n=f, tm=tm, tn=tf,
        outs=[("mn", (t, f), BF, 0)], epilogue=lambda accs, ex: accs, n_acc=1, j_outer=True)[0]

    def up_epi(accs, ex):
        gg = ex[0].astype(F32)
        return [accs[0], gg * _sig(gg) * accs[0]]

    up, hid = _matmul(
        tag + "_up", [dict(a=xn, b=wut, mode="nt", acc=0, tk=d)], m=t, n=f, tm=tm, tn=tf, extras=[("mn", gate, 0)],
        outs=[("mn", (t, f), BF, 0)] * 2, epilogue=up_epi, n_acc=1, j_outer=True)
    return gate, up, hid


def _ffn_down(tag, hid, wd, h, g_post, g_next, t, d, f, tm):
    def down_epi(accs, ex):
        ff = accs[0]
        hh, gg, gn = ex
        h_new = hh + 0.5 * ff * _rms_r(ff) * gg
        return [ff, h_new, h_new * _rms_r(h_new) * gn]

    return _matmul(
        tag + "_down", [dict(a=hid, b=wd, mode="nn", acc=0, tk=f)], m=t, n=d, tm=tm, tn=d,
        extras=[("mn", h, 0), ("n", g_post), ("n", g_next)],
        outs=[("mn", (t, d), F32, 0)] * 2 + [("mn", (t, d), BF, 0)], epilogue=down_epi, n_acc=1)


def _prenorm_bwd_epi(accs, ex):
    hh, dh, gg = ex
    dx, dg = _rms_bwd(hh, gg, accs[0])
    return [dh + dx, dg]


def _dx_extras(h_in, dh, g_pre, fo, g_post):
    return [("mn", h_in, 0), ("mn", dh, 0), ("n", g_pre), ("mn", fo, 0), ("n", g_post)]


def _dx_outs(t, d):
    return [("mn", (t, d), F32, 0), ("acc", (1, d), F32), ("mn", (t, d), BF, 0), ("acc", (1, d), F32)]


def _dx_epi(scale):
    def epi(accs, ex):
        hh, dh, gg, fo, gp = ex
        dx, dg = _rms_bwd(hh, gg, accs[0])
        dh_new = dh + dx
        dfo, dgp = _rms_bwd(fo, gp, dh_new * scale)
        return [dh_new, dg, dfo, dgp]

    return epi


def _ffn_dhid(tag, df, wd, gate, up, t, d, f, tm):
    def hid_epi(accs, ex):
        dhid = accs[0]
        gg, uu = ex[0].astype(F32), ex[1].astype(F32)
        s = _sig(gg)
        gs = gg * s
        return [dhid * uu * (s + gs * (1.0 - s)), dhid * gs]

    return _matmul(
        tag + "_dhid", [dict(a=df, b=wd, mode="nt", acc=0, tk=d)], m=t, n=f, tm=tm, tn=_tile(f, 1408),
        extras=[("mn", gate, 0), ("mn", up, 0)],
        outs=[("mn", (t, f), BF, 0)] * 2, epilogue=hid_epi, n_acc=1, j_outer=True)


def _ffn_dwd(tag, hid, df, t, d, f):
    tk_t = _tile(t, 2 * TOKEN_K_TILE)
    return _matmul(
        tag + "_dwd", [dict(a=hid, b=df, mode="tn", acc=0, tk=tk_t)], m=f, n=d, tm=_tile(f, 1408), tn=d,
        nk=t // tk_t, outs=[("mn", (f, d), BF, 0)], epilogue=lambda accs, ex: accs, n_acc=1)


def _ffn_dw_in(name, dact, xn, t, d, f, deps=()):
    tk_t = _tile(t, 2 * TOKEN_K_TILE)
    return _matmul(
        name, [dict(a=dact, b=xn, mode="tn", acc=0, tk=tk_t)], m=f, n=d, tm=_tile(f, 1408), tn=d, nk=t // tk_t,
        outs=[("mn", (f, d), BF, 0)], epilogue=lambda accs, ex: accs, n_acc=1, deps=deps)[0]


def _ffn_dx(tag, dh, h_in, dgate, dup, wgt, wut, g_pre, t, d, f, tm, deps=(), before=None):
    tf = _tile(f, 1408)
    mats = [dict(a=dgate, b=wgt, mode="nn", acc=0, tk=tf), dict(a=dup, b=wut, mode="nn", acc=0, tk=tf)]
    if before is None:
        return _matmul(
            tag + "_dx", mats, m=t, n=d, tm=tm, tn=d, nk=f // tf,
            extras=[("mn", h_in, 0), ("mn", dh, 0), ("n", g_pre)],
            outs=[("mn", (t, d), F32, 0), ("acc", (1, d), F32)], epilogue=_prenorm_bwd_epi, n_acc=1, deps=deps)
    fo, g_post, scale = before
    return _matmul(
        tag + "_dx", mats, m=t, n=d, tm=tm, tn=d, nk=f // tf, extras=_dx_extras(h_in, dh, g_pre, fo, g_post),
        outs=_dx_outs(t, d), epilogue=_dx_epi(scale), n_acc=1, deps=deps)


def _causal_mask():
    r = lax.broadcasted_iota(jnp.int32, (CHUNK, CHUNK), 0)
    c = lax.broadcasted_iota(jnp.int32, (CHUNK, CHUNK), 1)
    return r >= c


def _layernorm_parts(v):
    mu = jnp.mean(v, axis=-1, keepdims=True)
    vc = v - mu
    rstd = lax.rsqrt(jnp.mean(vc * vc, axis=-1, keepdims=True) + EPS)
    return vc * rstd, rstd


def _sgu_fwd(z, g_sgu, w_s, b_col, t, d, tm):
    dg = d // N_SGU_GROUPS
    n_chunk = tm // CHUNK

    def body(zu_ref, zv_ref, g_ref, w_ref, b_ref, a_ref):
        u = _gelu(zu_ref[...].astype(F32))
        vhat, _ = _layernorm_parts(_gelu(zv_ref[...].astype(F32)))
        vn = (vhat * g_ref[...]).astype(BF)
        mask = _causal_mask()
        for gi in range(N_SGU_GROUPS):
            ws = jnp.where(mask, w_ref[gi], 0.0).astype(BF)
            bias = b_ref[gi]
            for ci in range(n_chunk):
                rows, cols = slice(ci * CHUNK, (ci + 1) * CHUNK), slice(gi * dg, (gi + 1) * dg)
                sv = jnp.dot(ws, vn[rows, cols], preferred_element_type=F32) + bias
                a_ref[rows, cols] = (u[rows, cols] * sv).astype(BF)

    return pl.pallas_call(
        body,
        name="sgu_fwd",
        grid=(t // tm,),
        in_specs=[
            pl.BlockSpec((tm, d), lambda i: (i, 0)),
            pl.BlockSpec((tm, d), lambda i: (i, 1)),
            pl.BlockSpec((1, d), lambda i: (0, 0)),
            pl.BlockSpec(w_s.shape, lambda i: (0, 0, 0)),
            pl.BlockSpec(b_col.shape, lambda i: (0, 0, 0)),
        ],
        out_specs=pl.BlockSpec((tm, d), lambda i: (i, 0)),
        out_shape=jax.ShapeDtypeStruct((t, d), BF),
        compiler_params=_params(("arbitrary",)),
    )(z, z, g_sgu, w_s, b_col)


def _sgu_bwd(z, da, dz, g_sgu, w_s, b_col, t, d, tm):
    dg = d // N_SGU_GROUPS
    n_chunk = tm // CHUNK

    def body(zu_ref, zv_ref, da_ref, dz_in, g_ref, w_ref, b_ref, dz_ref, dw_ref, db_ref, dgn_ref, dvn_ref):
        del dz_in
        dzu_ref, dzv_ref = dz_ref.at[:, pl.ds(0, d)], dz_ref.at[:, pl.ds(d, d)]
        i = pl.program_id(0)
        zu, zv = zu_ref[...].astype(F32), zv_ref[...].astype(F32)
        u = _gelu(zu)
        vhat, rstd = _layernorm_parts(_gelu(zv))
        gn = g_ref[...]
        vn = (vhat * gn).astype(BF)
        da_v = da_ref[...].astype(F32)
        dsv_all = (da_v * u).astype(BF)
        mask = _causal_mask()
        for gi in range(N_SGU_GROUPS):
            ws = jnp.where(mask, w_ref[gi], 0.0).astype(BF)
            bias = b_ref[gi]
            dw = jnp.zeros((CHUNK, CHUNK), F32)
            dbias = jnp.zeros((CHUNK, 1), F32)
            for ci in range(n_chunk):
                rows, cols = slice(ci * CHUNK, (ci + 1) * CHUNK), slice(gi * dg, (gi + 1) * dg)
                vn_c, dsv = vn[rows, cols], dsv_all[rows, cols]
                sv = jnp.dot(ws, vn_c, preferred_element_type=F32) + bias
                dzu_ref[rows, cols] = (da_v[rows, cols] * sv * _gelu_grad(zu[rows, cols])).astype(BF)
                dw = dw + lax.dot_general(dsv, vn_c, _DN["nt"], preferred_element_type=F32)
                dbias = dbias + jnp.sum(dsv.astype(F32), axis=1, keepdims=True)
                dvn_ref[rows, cols] = lax.dot_general(ws, dsv, _DN["tn"], preferred_element_type=F32)
            dw = jnp.where(mask, dw, 0.0)

            @pl.when(i == 0)
            def _():
                dw_ref[gi] = dw
                db_ref[gi] = dbias

            @pl.when(i != 0)
            def _():
                dw_ref[gi] += dw
                db_ref[gi] += dbias

        dvn = dvn_ref[...]
        dgn = jnp.sum(dvn * vhat, axis=0, keepdims=True)

        @pl.when(i == 0)
        def _():
            dgn_ref[...] = dgn

        @pl.when(i != 0)
        def _():
            dgn_ref[...] += dgn

        dvh = dvn * gn
        dv = rstd * (dvh - jnp.mean(dvh, axis=-1, keepdims=True) - vhat * jnp.mean(dvh * vhat, axis=-1, keepdims=True))
        dzv_ref[...] = (dv * _gelu_grad(zv)).astype(BF)

    return pl.pallas_call(
        body,
        name="sgu_bwd",
        grid=(t // tm,),
        in_specs=[
            pl.BlockSpec((tm, d), lambda i: (i, 0)),
            pl.BlockSpec((tm, d), lambda i: (i, 1)),
            pl.BlockSpec((tm, d), lambda i: (i, 0)),
            pl.BlockSpec(memory_space=pl.ANY),
            pl.BlockSpec((1, d), lambda i: (0, 0)),
            pl.BlockSpec(w_s.shape, lambda i: (0, 0, 0)),
            pl.BlockSpec(b_col.shape, lambda i: (0, 0, 0)),
        ],
        out_specs=[
            pl.BlockSpec((tm, 2 * d), lambda i: (i, 0)),
            pl.BlockSpec(w_s.shape, lambda i: (0, 0, 0)),
            pl.BlockSpec(b_col.shape, lambda i: (0, 0, 0)),
            pl.BlockSpec((1, d), lambda i: (0, 0)),
        ],
        out_shape=[
            jax.ShapeDtypeStruct(dz.shape, BF),
            jax.ShapeDtypeStruct(w_s.shape, F32),
            jax.ShapeDtypeStruct(b_col.shape, F32),
            jax.ShapeDtypeStruct((1, d), F32),
        ],
        scratch_shapes=[pltpu.VMEM((tm, d), F32)],
        input_output_aliases={3: 0},
        compiler_params=_params(("arbitrary",)),
    )(z, z, da, dz, g_sgu, w_s, b_col)


def _shift_down(x, k, row):
    return jnp.where(row >= k, pltpu.roll(x, k, 0), 0.0)


def _shift_up(x, k, row, t):
    return jnp.where(row < t - k, pltpu.roll(x, t - k, 0), 0.0)


def _doublings(window):
    steps = int(math.log2(window))
    assert 2 ** steps == window
    return [2 ** s for s in range(steps)]


def _pool_diff(c, window, row):
    s = c
    for k in _doublings(window):
        s = s + _shift_down(s, k, row)
    count = jnp.minimum(row + 1, window).astype(F32)
    return s / count - c, count


def _pool_fwd(z, pool_w, pool_scale, t, d):
    dgp = d // len(POOL_WINDOWS)
    cblk = (2 * d) // dgp

    def body(zc_ref, w_ref, s_ref, b_ref):
        row = lax.broadcasted_iota(jnp.int32, (t, 1), 0)
        for gi, window in enumerate(POOL_WINDOWS):
            @pl.when(pl.program_id(0) == gi)
            def _(window=window):
                diff, _ = _pool_diff(zc_ref[...].astype(F32), window, row)
                out = jnp.dot(diff.astype(BF), w_ref[...], preferred_element_type=F32)
                b_ref[...] = (out * s_ref[...]).astype(BF)

    return pl.pallas_call(
        body,
        name="pool_fwd",
        grid=(len(POOL_WINDOWS),),
        in_specs=[
            pl.BlockSpec((t, dgp), lambda g: (0, cblk + g)),
            pl.BlockSpec((None, dgp, dgp), lambda g: (g, 0, 0)),
            pl.BlockSpec((1, dgp), lambda g: (0, g)),
        ],
        out_specs=pl.BlockSpec((t, dgp), lambda g: (0, g)),
        out_shape=jax.ShapeDtypeStruct((t, d), BF),
        compiler_params=_params(("arbitrary",)),
    )(z, pool_w, pool_scale)


def _pool_bwd(z, db, dz, pool_w, pool_scale, t, d):
    dgp = d // len(POOL_WINDOWS)
    cblk = (2 * d) // dgp

    def body(zc_ref, db_ref, dz_in, w_ref, s_ref, dzc_ref, dw_ref, ds_ref):
        del dz_in
        row = lax.broadcasted_iota(jnp.int32, (t, 1), 0)
        for gi, window in enumerate(POOL_WINDOWS):
            @pl.when(pl.program_id(0) == gi)
            def _(window=window):
                diff, count = _pool_diff(zc_ref[...].astype(F32), window, row)
                diff = diff.astype(BF)
                w = w_ref[...]
                dbv = db_ref[...].astype(F32)
                out = jnp.dot(diff, w, preferred_element_type=F32)
                ds_ref[...] = jnp.sum(dbv * out, axis=0, keepdims=True)
                dout = (dbv * s_ref[...]).astype(BF)
                dw_ref[...] = lax.dot_general(diff, dout, _DN["tn"], preferred_element_type=F32).astype(BF)
                ddiff = lax.dot_general(dout, w, _DN["nt"], preferred_element_type=F32)
                s = ddiff / count
                for k in _doublings(window):
                    s = s + _shift_up(s, k, row, t)
                dzc_ref[...] = (s - ddiff).astype(BF)

    return pl.pallas_call(
        body,
        name="pool_bwd",
        grid=(len(POOL_WINDOWS),),
        in_specs=[
            pl.BlockSpec((t, dgp), lambda g: (0, cblk + g)),
            pl.BlockSpec((t, dgp), lambda g: (0, g)),
            pl.BlockSpec(memory_space=pl.ANY),
            pl.BlockSpec((None, dgp, dgp), lambda g: (g, 0, 0)),
            pl.BlockSpec((1, dgp), lambda g: (0, g)),
        ],
        out_specs=[
            pl.BlockSpec((t, dgp), lambda g: (0, cblk + g)),
            pl.BlockSpec((None, dgp, dgp), lambda g: (g, 0, 0)),
            pl.BlockSpec((1, dgp), lambda g: (0, g)),
        ],
        out_shape=[
            jax.ShapeDtypeStruct(dz.shape, BF),
            jax.ShapeDtypeStruct(pool_w.shape, BF),
            jax.ShapeDtypeStruct((1, d), F32),
        ],
        input_output_aliases={2: 0},
        compiler_params=_params(("arbitrary",)),
    )(z, db, dz, pool_w, pool_scale)


def _mix_dy(dm, w_o, z, ya, yb, t, d, tm):
    def body(dm_ref, w_ref, gate_ref, ya_ref, yb_ref, dyab_ref, dz_ref, dy_ref):
        j = pl.program_id(1)

        @pl.when(j == 0)
        def _():
            dy_ref[...] = lax.dot_general(dm_ref[...], w_ref[...], _DN["nt"], preferred_element_type=F32)

        dy = dy_ref[...]
        s = _sig(gate_ref[...].astype(F32))
        yv = jnp.where(j == 0, ya_ref[...], yb_ref[...]).astype(F32)
        dyab_ref[...] = (dy * s).astype(BF)
        dz_ref[...] = (dy * yv * s * (1.0 - s)).astype(BF)

    row = pl.BlockSpec((tm, d), lambda i, j: (i, 0))
    return pl.pallas_call(
        body,
        name="mix_dy",
        grid=(t // tm, 2),
        in_specs=[row, pl.BlockSpec((d, d), lambda i, j: (0, 0)), pl.BlockSpec((tm, d), lambda i, j: (i, 3 + j)), row, row],
        out_specs=[pl.BlockSpec((tm, d), lambda i, j: (i, j)), pl.BlockSpec((tm, d), lambda i, j: (i, 3 + j))],
        out_shape=[jax.ShapeDtypeStruct((t, 2 * d), BF), jax.ShapeDtypeStruct((t, 5 * d), BF)],
        scratch_shapes=[pltpu.VMEM((tm, d), F32)],
        compiler_params=_params(("arbitrary", "arbitrary")),
    )(dm, w_o, z, ya, yb)


def _everyone():
    x, y, c = _place()
    return [(1 - x if dd & 4 else x, 1 - y if dd & 2 else y, 1 - c if dd & 1 else c) for dd in range(1, N_DEV)]


def _direct_plan(n):
    def plan(refs, send_sems, recv_sems):
        x, y, c = _place()
        me = 4 * x + 2 * y + c
        cps = []
        for i in range(n):
            for k, (px, py, pc) in enumerate(_everyone()):
                cps.append(_remote(refs[i].at[4 * px + 2 * py + pc], refs[n + i].at[me], send_sems, recv_sems,
                                   7 * i + k, (px, py, pc)))
        return cps

    def local(refs):
        x, y, c = _place()
        me = 4 * x + 2 * y + c
        return [(refs[i].at[me], refs[n + i].at[me]) for i in range(n)]

    return plan, local


def _sc_scatter_direct(name, collective_id, parts):
    n = len(parts)
    p8 = [a.reshape(N_DEV, a.shape[0] // N_DEV, a.shape[1]) for a in parts]
    plan, local = _direct_plan(n)
    return _sc_exchange(name, collective_id, p8, [jax.ShapeDtypeStruct(a.shape, a.dtype) for a in p8], plan,
                        7 * n, _everyone, local=local)


def _stack_rows(arrays):
    parts, starts, row = [], [], 0
    for a in arrays:
        pad = -a.shape[0] % 8
        starts.append(row)
        parts += [a, jnp.zeros((pad, a.shape[1]), a.dtype)] if pad else [a]
        row += a.shape[0] + pad
    return jnp.concatenate(parts, axis=0), starts


def _adam_replicated(name, stacks, layout, weights, with_loss, deps=()):
    n_st, n_par = len(stacks), len(weights)

    def body(*refs):
        st_refs = refs[:n_st]
        w_refs = refs[n_st:n_st + 3 * n_par]
        out_refs = refs[n_st + 3 * n_par + len(deps):]

        def summed(which, row, rows):
            g = st_refs[which][0, pl.ds(row, rows), :]
            for k in range(1, N_DEV):
                g = g + st_refs[which][k, pl.ds(row, rows), :]
            return g

        for i in range(n_par):
            w_ref, m_ref, v_ref = w_refs[3 * i:3 * i + 3]
            g = summed(layout[i][0], layout[i][1], w_ref.shape[0])
            dlt, mm, vv = _adam_vals(w_ref[...], g, m_ref[...], v_ref[...])
            for r, val in zip(out_refs[4 * i:4 * i + 4], (g, dlt, mm, vv)):
                r[...] = val
        if with_loss:
            out_refs[4 * n_par][...] = jnp.sum(summed(layout[n_par][0], layout[n_par][1], 1), axis=1, keepdims=True)

    flat_w = [a for wmv in weights for a in wmv]
    out_shape = [jax.ShapeDtypeStruct(wmv[0].shape, F32) for wmv in weights for _ in range(4)]
    if with_loss:
        out_shape.append(jax.ShapeDtypeStruct((1, 1), F32))
    vmem = pl.BlockSpec(memory_space=pltpu.VMEM)
    res = pl.pallas_call(
        body,
        name=name,
        in_specs=[vmem] * (n_st + len(flat_w)) + [pl.BlockSpec(memory_space=pl.ANY)] * len(deps),
        out_specs=[vmem] * len(out_shape),
        out_shape=out_shape,
        compiler_params=_params(),
    )(*stacks, *flat_w, *deps)
    return [res[4 * i:4 * i + 4] for i in range(n_par)], res[4 * n_par] if with_loss else None


def kernel(x, p, ffn1_pre_g, ffn1_w_gate, ffn1_w_up, ffn1_w_down, ffn1_post_g, mix_pre_g, w_in, sgu_norm_g, sgu_w, sgu_b, pool_w, pool_scale, w_out_a, w_out_b, w_o, mix_post_g, ffn2_pre_g, ffn2_w_gate, ffn2_w_up, ffn2_w_down, ffn2_post_g, ple_pre_g, ple_w_gate, ple_w_proj, ple_post_g, loss_target, m_ffn1_pre_g, m_ffn1_w_gate, m_ffn1_w_up, m_ffn1_w_down, m_ffn1_post_g, m_mix_pre_g, m_w_in, m_sgu_norm_g, m_sgu_w, m_sgu_b, m_pool_w, m_pool_scale, m_w_out_a, m_w_out_b, m_w_o, m_mix_post_g, m_ffn2_pre_g, m_ffn2_w_gate, m_ffn2_w_up, m_ffn2_w_down, m_ffn2_post_g, m_ple_pre_g, m_ple_w_gate, m_ple_w_proj, m_ple_post_g, v_ffn1_pre_g, v_ffn1_w_gate, v_ffn1_w_up, v_ffn1_w_down, v_ffn1_post_g, v_mix_pre_g, v_w_in, v_sgu_norm_g, v_sgu_w, v_sgu_b, v_pool_w, v_pool_scale, v_w_out_a, v_w_out_b, v_w_o, v_mix_post_g, v_ffn2_pre_g, v_ffn2_w_gate, v_ffn2_w_up, v_ffn2_w_down, v_ffn2_post_g, v_ple_pre_g, v_ple_w_gate, v_ple_w_proj, v_ple_post_g):
    args = dict(locals())
    names = ["ffn1_pre_g", "ffn1_w_gate", "ffn1_w_up", "ffn1_w_down", "ffn1_post_g", "mix_pre_g", "w_in",
             "sgu_norm_g", "sgu_w", "sgu_b", "pool_w", "pool_scale", "w_out_a", "w_out_b", "w_o", "mix_post_g",
             "ffn2_pre_g", "ffn2_w_gate", "ffn2_w_up", "ffn2_w_down", "ffn2_post_g", "ple_pre_g", "ple_w_gate",
             "ple_w_proj", "ple_post_g"]
    w = {k: args[k][0] for k in names}
    mom = {k: args["m_" + k][0] for k in names}
    var = {k: args["v_" + k][0] for k in names}

    assert x.shape[0] == 1 and p.shape[:2] == (1, 1)
    t, d = x.shape[1], x.shape[2]
    f = ffn1_w_gate.shape[2] * N_DEV
    d_in = w_in.shape[2] * N_DEV
    d_ple = p.shape[3]
    n_pool = len(POOL_WINDOWS)
    dgp = d // n_pool
    assert d_in == 5 * d and t % CHUNK == 0
    tm = _tile(t, 512, CHUNK)
    xs, ps, target = x[0], p[0, 0], loss_target[0]

    col_sharded = ("ffn1_w_gate", "ffn1_w_up", "ffn2_w_gate", "ffn2_w_up", "w_in", "ple_w_proj")

    def shard_of(k):
        if k in col_sharded:
            return w[k].T.astype(BF)
        if k == "pool_w":
            return w[k].reshape(-1, dgp).astype(BF)
        return w[k].astype(BF)

    rows_pw = dgp // N_DEV
    b_col = w["sgu_b"][:, :, None]
    gains = {k: w[k][None, :] for k in names if w[k].ndim == 1}
    full = {}

    groups = [["ffn1_w_gate"], ["ffn1_w_up"], ["ffn1_w_down"], ["w_in"], ["pool_w", "w_out_a", "w_out_b", "w_o"],
              ["ffn2_w_gate", "ffn2_w_up", "ffn2_w_down", "ple_w_gate", "ple_w_proj"]]
    def gather(gi, behind=None):
        shards = [shard_of(k) for k in groups[gi]]
        if behind is not None:
            shards = lax.optimization_barrier((shards, behind))[0]
        full.update(zip(groups[gi], _sc_gather("gather%d" % gi, gi, shards)))

    gather(0)
    xn1 = _prenorm("ffn1_prenorm", xs, gains["ffn1_pre_g"], t, d, tm)
    gather(1, xn1)
    gather(2, xn1)
    gather(3, xn1)
    g1, u1, hid1 = _ffn_up_split("ffn1", xn1, full["ffn1_w_gate"], full["ffn1_w_up"], t, d, f, tm)
    gather(4, hid1)
    f1, h1, xn2 = _ffn_down("ffn1", hid1, full["ffn1_w_down"], xs, gains["ffn1_post_g"], gains["mix_pre_g"], t, d, f, tm)
    gather(5, h1)

    pool_full = full["pool_w"].reshape(N_DEV, n_pool, rows_pw, dgp).transpose(1, 0, 2, 3).reshape(n_pool, dgp, dgp)
    z = _matmul("mix_in", [dict(a=xn2, b=full["w_in"], mode="nt", acc=0, tk=d)], m=t, n=d_in, tm=_tile(t, 1024, CHUNK), tn=d,
                outs=[("mn", (t, d_in), BF, 0)], epilogue=lambda accs, ex: accs, n_acc=1, j_outer=True)[0]
    a_br = _sgu_fwd(z, gains["sgu_norm_g"], w["sgu_w"], b_col, t, d, tm)
    b_br = _pool_fwd(z, pool_full, gains["pool_scale"], t, d)

    def merge_epi(accs, ex):
        ya, yb = accs
        ga, gb = ex[0].astype(F32), ex[1].astype(F32)
        return [ya, yb, _sig(ga) * ya + _sig(gb) * yb]

    ya, yb, y = _matmul(
        "mix_merge",
        [dict(a=a_br, b=full["w_out_a"], mode="nn", acc=0, tk=d), dict(a=b_br, b=full["w_out_b"], mode="nn", acc=1, tk=d)],
        m=t, n=d, tm=tm, tn=d, extras=[("mn", z, 3), ("mn", z, 4)],
        outs=[("mn", (t, d), BF, 0)] * 3, epilogue=merge_epi, n_acc=2)

    def proj_epi(accs, ex):
        mm = accs[0]
        hh, gg, gn = ex
        h_new = hh + mm * _rms_r(mm) * gg
        return [mm, h_new, h_new * _rms_r(h_new) * gn]

    m_out, h2, xn3 = _matmul(
        "mix_proj", [dict(a=y, b=full["w_o"], mode="nn", acc=0, tk=d)], m=t, n=d, tm=tm, tn=d,
        extras=[("mn", h1, 0), ("n", gains["mix_post_g"]), ("n", gains["ffn2_pre_g"])],
        outs=[("mn", (t, d), F32, 0)] * 2 + [("mn", (t, d), BF, 0)], epilogue=proj_epi, n_acc=1)

    g2, u2, hid2 = _ffn_up("ffn2", xn3, full["ffn2_w_gate"], full["ffn2_w_up"], t, d, f, tm)
    f2, h3, xn4 = _ffn_down("ffn2", hid2, full["ffn2_w_down"], h2, gains["ffn2_post_g"], gains["ple_pre_g"], t, d, f, tm)

    def ple_epi(accs, ex):
        gl, e = accs
        hh, tgt, gg = ex
        q = _sig(gl) * e
        err = hh + q * _rms_r(q) * gg - tgt
        return [gl, e, err * (1.0 / d), jnp.sum(err * err, axis=0, keepdims=True)]

    gl, e_ple, dh4, loss_vec = _matmul(
        "ple_fwd",
        [dict(a=xn4, b=full["ple_w_gate"], mode="nn", acc=0, tk=d), dict(a=ps, b=full["ple_w_proj"], mode="nt", acc=1, tk=d_ple)],
        m=t, n=d, tm=tm, tn=d, extras=[("mn", h3, 0), ("mn", target, 0), ("n", gains["ple_post_g"])],
        outs=[("mn", (t, d), F32, 0)] * 3 + [("acc", (1, d), F32)], epilogue=ple_epi, n_acc=2)

    tk_t = _tile(t, TOKEN_K_TILE)
    tk_1 = _tile(t, 2 * TOKEN_K_TILE)

    def ple_post_fn(v):
        dy, gl_v, e_v, gg = v
        s = _sig(gl_v)
        dq, dg = _rms_bwd(s * e_v, gg, dy)
        return [dq * e_v * s * (1.0 - s), dq * s, dg]

    dgl, de, dg_ple_post = _rowwise(
        "ple_post_bwd", [("row", dh4, d, 0), ("row", gl, d, 0), ("row", e_ple, d, 0), ("full", gains["ple_post_g"])],
        [("row", (t, d), BF, d, 0), ("row", (t, d), BF, d, 0), ("acc", (1, d), F32)], ple_post_fn, t=t, tm=tm)
    ident = lambda accs, ex: accs
    dw_ple_gate = _matmul("ple_dwg", [dict(a=xn4, b=dgl, mode="tn", acc=0, tk=tk_1)], m=d, n=d, tm=d, tn=d,
                          nk=t // tk_1, outs=[("mn", (d, d), BF, 0)], epilogue=ident, n_acc=1)[0]
    dw_ple_proj_t = _matmul("ple_dwp", [dict(a=de, b=ps, mode="tn", acc=0, tk=tk_1)], m=d, n=d_ple, tm=d, tn=d_ple,
                            nk=t // tk_1, outs=[("mn", (d, d_ple), BF, 0)], epilogue=ident, n_acc=1)[0]
    dh3, dg_ple_pre, df2, dg_f2_post = _matmul(
        "ple_dx", [dict(a=dgl, b=full["ple_w_gate"], mode="nt", acc=0, tk=d)], m=t, n=d, tm=tm, tn=d,
        extras=_dx_extras(h3, dh4, gains["ple_pre_g"], f2, gains["ffn2_post_g"]),
        outs=_dx_outs(t, d), epilogue=_dx_epi(0.5), n_acc=1)

    staged = []
    n_gather = len(groups)

    def scatter(keys, parts):
        k = len(staged)
        staged.append((keys, _sc_scatter_direct("scatter%d" % k, n_gather + k, parts)))

    dgate2, dup2 = _ffn_dhid("ffn2", df2, full["ffn2_w_down"], g2, u2, t, d, f, tm)
    dwd2 = _ffn_dwd("ffn2", hid2, df2, t, d, f)[0]
    dwg2 = _ffn_dw_in("ffn2_dwg", dgate2, xn3, t, d, f)
    dwu2 = _ffn_dw_in("ffn2_dwu", dup2, xn3, t, d, f)
    scatter(["ple_w_gate", "ple_w_proj", "ffn2_w_down", "ffn2_w_gate", "ffn2_w_up"],
            [dw_ple_gate, dw_ple_proj_t, dwd2, dwg2, dwu2])
    dh2, dg_f2_pre, dm, dg_mix_post = _ffn_dx(
        "ffn2", dh3, h2, dgate2, dup2, full["ffn2_w_gate"], full["ffn2_w_up"], gains["ffn2_pre_g"], t, d, f, tm,
        deps=[dw_ple_gate, dw_ple_proj_t, dwd2, dwg2, dwu2], before=(m_out, gains["mix_post_g"], 1.0))

    dyab, dz = _mix_dy(dm, full["w_o"], z, ya, yb, t, d, tm)
    dw_o = _matmul("mix_dwo", [dict(a=y, b=dm, mode="tn", acc=0, tk=tk_1)], m=d, n=d, tm=d, tn=d, nk=t // tk_1,
                   outs=[("mn", (d, d), BF, 0)], epilogue=ident, n_acc=1)[0]
    dw_out_a, dw_out_b = _matmul(
        "mix_dwab",
        [dict(a=a_br, b=dyab, mode="tn", acc=0, tk=tk_t),
         dict(a=b_br, b=dyab, mode="tn", acc=1, tk=tk_t, b_off=(0, 1))],
        m=d, n=d, tm=d, tn=d, nk=t // tk_t, outs=[("mn", (d, d), BF, 0)] * 2, epilogue=ident, n_acc=2)
    scatter(["w_o", "w_out_a", "w_out_b"], [dw_o, dw_out_a, dw_out_b])
    da, db = _matmul(
        "mix_dab",
        [dict(a=dyab, b=full["w_out_a"], mode="nt", acc=0, tk=d),
         dict(a=dyab, b=full["w_out_b"], mode="nt", acc=1, tk=d, a_off=(0, 1))],
        m=t, n=d, tm=tm, tn=d, outs=[("mn", (t, d), BF, 0)] * 2, epilogue=ident, n_acc=2, deps=[dw_o, dw_out_a])
    dz, dsgu_w, dsgu_b, dg_sgu = _sgu_bwd(z, da, dz, gains["sgu_norm_g"], w["sgu_w"], b_col, t, d, _tile(t, 256, CHUNK))
    dz, dpool_w, dg_pool = _pool_bwd(z, db, dz, pool_full, gains["pool_scale"], t, d)
    dw_in_t = _matmul(
        "mix_dwin", [dict(a=dz, b=xn2, mode="tn", acc=0, tk=tk_1)], m=d_in, n=d, tm=d, tn=d,
        nk=t // tk_1, outs=[("mn", (d_in, d), BF, 0)], epilogue=ident, n_acc=1)[0]
    dpool_rows = dpool_w.reshape(n_pool, N_DEV, rows_pw, dgp).transpose(1, 0, 2, 3).reshape(N_DEV * n_pool * rows_pw, dgp)
    scatter(["pool_w", "w_in"], [dpool_rows, dw_in_t])
    dh1, dg_mix_pre, df1, dg_f1_post = _matmul(
        "mix_dx", [dict(a=dz, b=full["w_in"], mode="nn", acc=0, tk=d_in // 2)], m=t, n=d, tm=tm, tn=d, nk=2,
        extras=_dx_extras(h1, dh2, gains["mix_pre_g"], f1, gains["ffn1_post_g"]),
        outs=_dx_outs(t, d), epilogue=_dx_epi(0.5), n_acc=1, deps=[dpool_rows, dw_in_t])

    dgate1, dup1 = _ffn_dhid("ffn1", df1, full["ffn1_w_down"], g1, u1, t, d, f, tm)
    dwd1 = _ffn_dwd("ffn1", hid1, df1, t, d, f)[0]
    scatter(["ffn1_w_down"], [dwd1])
    dwg1 = _ffn_dw_in("ffn1_dwg", dgate1, xn1, t, d, f, deps=[dwd1])
    scatter(["ffn1_w_gate"], [dwg1])
    dwu1 = _ffn_dw_in("ffn1_dwu", dup1, xn1, t, d, f, deps=[dwg1])
    scatter(["ffn1_w_up"], [dwu1])
    grad_x, dg_f1_pre = _ffn_dx("ffn1", dh1, xs, dgate1, dup1, full["ffn1_w_gate"], full["ffn1_w_up"],
                                gains["ffn1_pre_g"], t, d, f, tm, deps=[dwu1])

    grad, delta, new_m, new_v = {}, {}, {}, {}

    def update(group, partial_sums, behind):
        done = []
        for k, st in zip(group, partial_sums):
            if k in col_sharded and w[k].shape[1] % LANES:
                res = _reduce_adam("adam_" + k, st, w[k].T, mom[k].T, var[k].T, deps=behind)
                grad[k], delta[k], new_m[k], new_v[k] = [r.T for r in res]
                done.append(res[1])
                continue
            elif k in col_sharded:
                grad[k] = _reduce_only("sum_" + k, st, deps=behind).T
                delta[k], new_m[k], new_v[k] = _adam_only("adam_" + k, w[k], grad[k], mom[k], var[k])
            else:
                shape = w[k].shape
                flat = (-1, shape[-1])
                res = _reduce_adam("adam_" + k, st, w[k].reshape(flat), mom[k].reshape(flat), var[k].reshape(flat),
                                   deps=behind)
                grad[k], delta[k], new_m[k], new_v[k] = [r.reshape(shape) for r in res]
            done.append(delta[k])
        return done

    gain_grads = {
        "ffn1_post_g": dg_f1_post, "mix_pre_g": dg_mix_pre, "sgu_norm_g": dg_sgu, "pool_scale": dg_pool,
        "mix_post_g": dg_mix_post, "ffn2_pre_g": dg_f2_pre, "ffn2_post_g": dg_f2_post, "ple_pre_g": dg_ple_pre,
        "ple_post_g": dg_ple_post,
    }
    gain_names = list(gain_grads)
    n_mix = N_SGU_GROUPS * CHUNK
    rep_shape = {k: (1, d) for k in gain_names + ["ffn1_pre_g"]}
    rep_shape.update(sgu_b=(N_SGU_GROUPS, CHUNK), sgu_w=(n_mix, CHUNK))

    def replicated(tag, cid, arrays, rep_names, layout, with_loss, behind):
        plan_s, local_s = _broadcast_plan(len(arrays))
        stacks = _sc_exchange(
            tag, cid, arrays, [jax.ShapeDtypeStruct((N_DEV,) + a.shape, F32) for a in arrays],
            plan_s, len(arrays) * (N_DEV - 1), _everyone, local=local_s)
        res, loss_sum = _adam_replicated(
            "adam_" + tag, stacks, layout,
            [tuple(src[k].reshape(rep_shape[k]) for src in (w, mom, var)) for k in rep_names], with_loss, deps=behind)
        for k, quad in zip(rep_names, res):
            for dst, val in zip((grad, delta, new_m, new_v), quad):
                dst[k] = val.reshape(w[k].shape)
        return loss_sum, [res[-1][1]]

    wide, wide_rows = _stack_rows([gain_grads[k] for k in gain_names] + [loss_vec * (0.5 / d)])
    narrow, narrow_rows = _stack_rows([dsgu_b.reshape(N_SGU_GROUPS, CHUNK), dsgu_w.reshape(n_mix, CHUNK)])
    layout = [(0, r) for r in wide_rows[:-1]] + [(1, narrow_rows[0]), (1, narrow_rows[1]), (0, wide_rows[-1])]
    done = [grad_x]
    for keys, partial_sums in staged[:3]:
        done = update(keys, partial_sums, done)
    loss, done = replicated("small", n_gather + len(staged), [wide, narrow], gain_names + ["sgu_b", "sgu_w"], layout,
                            True, done)
    for keys, partial_sums in staged[3:]:
        done = update(keys, partial_sums, done)
    late, late_rows = _stack_rows([dg_f1_pre])
    replicated("late", n_gather + len(staged) + 1, [late], ["ffn1_pre_g"], [(0, late_rows[0])], False, done)

    out = [loss[0, 0], grad_x[None]]
    for group in (grad, delta, new_m, new_v):
        out += [group[k][None] for k in names]
    return tuple(out)
```

```python
import math

import jax
import jax.numpy as jnp
from jax import lax
from jax.experimental import pallas as pl
from jax.experimental.pallas import tpu as pltpu
from jax.experimental.pallas import tpu_sc as plsc

EPS = 1e-6
CHUNK = 128
N_SGU_GROUPS = 4
POOL_WINDOWS = (2, 4, 8, 16)
ADAM_LR = 0.001
ADAM_B1 = 0.9
ADAM_B2 = 0.999
ADAM_EPS = 1e-08
ADAM_WD = 0.01
ADAM_STEP = 10

N_DEV = 8
MESH = pl.DeviceIdType.MESH
V7X_VMEM_BYTES = 64 * 1024 * 1024
VMEM_LIMIT = V7X_VMEM_BYTES - 8 * 1024 * 1024
TOKEN_K_TILE = 1024
LANES = 128
UPDATE_ROWS = 256
EPI_ROWS = 128
BF = jnp.bfloat16
F32 = jnp.float32

_DN = {
    "nn": (((1,), (0,)), ((), ())),
    "nt": (((1,), (1,)), ((), ())),
    "tn": (((0,), (0,)), ((), ())),
}


def _params(sem=None):
    return pltpu.CompilerParams(dimension_semantics=sem, vmem_limit_bytes=VMEM_LIMIT)


def _tile(n, target, align=128):
    t = min(n, target)
    t -= t % align
    while t >= align:
        if n % t == 0:
            return t
        t -= align
    return n


def _sig(x):
    return 0.5 + 0.5 * jnp.tanh(0.5 * x)


_GELU_K = math.sqrt(2.0 / math.pi)
_GELU_C = 0.044715


def _gelu(x):
    return 0.5 * x * (1.0 + jnp.tanh(_GELU_K * (x + _GELU_C * x * x * x)))


def _gelu_grad(x):
    t = jnp.tanh(_GELU_K * (x + _GELU_C * x * x * x))
    return 0.5 * (1.0 + t) + 0.5 * x * (1.0 - t * t) * _GELU_K * (1.0 + 3.0 * _GELU_C * x * x)


def _rms_r(x):
    return lax.rsqrt(jnp.mean(x * x, axis=-1, keepdims=True) + EPS)


def _rms_bwd(x, g, dy):
    r = _rms_r(x)
    xh = x * r
    gy = dy * g
    dx = r * (gy - xh * jnp.mean(xh * gy, axis=-1, keepdims=True))
    return dx, jnp.sum(dy * xh, axis=0, keepdims=True)


def _matmul(name, mats, *, m, n, tm, tn, nk=1, extras=(), outs, epilogue, n_acc, j_outer=False, deps=(), epi_rows=None):
    ni, nj = m // tm, n // tn
    assert ni * tm == m and nj * tn == n, (name, m, n, tm, tn)
    if j_outer:
        grid = (nj, ni, nk)

        def ij(g0, g1):
            return g1, g0
    else:
        grid = (ni, nj, nk)

        def ij(g0, g1):
            return g0, g1

    in_specs, args = [], []
    for mt in mats:
        mode, tk = mt["mode"], mt["tk"]
        ao, bo = mt.get("a_off", (0, 0)), mt.get("b_off", (0, 0))
        if mode == "tn":
            sa = pl.BlockSpec((tk, tm), lambda g0, g1, kk, ao=ao: (ao[0] + kk, ao[1] + ij(g0, g1)[0]))
        else:
            sa = pl.BlockSpec((tm, tk), lambda g0, g1, kk, ao=ao: (ao[0] + ij(g0, g1)[0], ao[1] + kk))
        if mode == "nt":
            sb = pl.BlockSpec((tn, tk), lambda g0, g1, kk, bo=bo: (bo[0] + ij(g0, g1)[1], bo[1] + kk))
        else:
            sb = pl.BlockSpec((tk, tn), lambda g0, g1, kk, bo=bo: (bo[0] + kk, bo[1] + ij(g0, g1)[1]))
        in_specs += [sa, sb]
        args += [mt["a"], mt["b"]]
    n_mat_refs = len(args)
    for ex in extras:
        if ex[0] == "mn":
            in_specs.append(pl.BlockSpec((tm, tn), lambda g0, g1, kk, c=ex[2]: (ij(g0, g1)[0], c + ij(g0, g1)[1])))
        else:
            in_specs.append(pl.BlockSpec((1, tn), lambda g0, g1, kk: (0, ij(g0, g1)[1])))
        args.append(ex[1])
    n_ex_end = len(args)
    in_specs += [pl.BlockSpec(memory_space=pl.ANY)] * len(deps)
    args += list(deps)
    n_in = len(args)
    out_specs, out_shape = [], []
    for o in outs:
        if o[0] == "mn":
            out_specs.append(pl.BlockSpec((tm, tn), lambda g0, g1, kk, c=o[3]: (ij(g0, g1)[0], c + ij(g0, g1)[1])))
        else:
            assert nj == 1, name
            out_specs.append(pl.BlockSpec((1, tn), lambda g0, g1, kk: (0, 0)))
        out_shape.append(jax.ShapeDtypeStruct(o[1], o[2]))
    n_out = len(outs)

    def body(*refs):
        mat_refs = refs[:n_mat_refs]
        ex_refs = refs[n_mat_refs:n_ex_end]
        out_refs = refs[n_in:n_in + n_out]
        acc_refs = refs[n_in + n_out:]
        i = ij(pl.program_id(0), pl.program_id(1))[0]
        kk = pl.program_id(2)

        def products():
            res = [None] * n_acc
            for idx, mt in enumerate(mats):
                a = mat_refs[2 * idx][...].astype(BF)
                b = mat_refs[2 * idx + 1][...].astype(BF)
                p = lax.dot_general(a, b, _DN[mt["mode"]], preferred_element_type=F32)
                q = mt["acc"]
                res[q] = p if res[q] is None else res[q] + p
            return res

        def finish(accs):
            n_chunks = tm // epi_rows if epi_rows else 1
            step = tm // n_chunks
            sums = [None] * len(outs)
            for c in range(n_chunks):
                rows = slice(c * step, (c + 1) * step)
                ex_vals = [r[rows, :] if ex[0] == "mn" else r[...] for ex, r in zip(extras, ex_refs)]
                vals = epilogue([a[rows, :] for a in accs], ex_vals)
                for k, (o, r, v) in enumerate(zip(outs, out_refs, vals)):
                    if o[0] == "mn":
                        r[rows, :] = v.astype(o[2])
                    else:
                        sums[k] = v if sums[k] is None else sums[k] + v
            for o, r, v in zip(outs, out_refs, sums):
                if o[0] == "acc":
                    @pl.when(i == 0)
                    def _(r=r, v=v):
                        r[...] = v

                    @pl.when(i != 0)
                    def _(r=r, v=v):
                        r[...] += v

        if nk == 1:
            finish(products())
        else:
            res = products()

            @pl.when(kk == 0)
            def _():
                for q in range(n_acc):
                    acc_refs[q][...] = res[q]

            @pl.when(kk != 0)
            def _():
                for q in range(n_acc):
                    acc_refs[q][...] += res[q]

            @pl.when(kk == nk - 1)
            def _():
                finish(acc_refs)

    scratch = [pltpu.VMEM((tm, tn), F32) for _ in range(n_acc)] if nk > 1 else []
    return pl.pallas_call(
        body,
        name=name,
        grid=grid,
        in_specs=in_specs,
        out_specs=out_specs,
        out_shape=out_shape,
        scratch_shapes=scratch,
        compiler_params=_params(("arbitrary", "arbitrary", "arbitrary")),
    )(*args)


def _rowwise(name, ins, outs, fn, *, t, tm, deps=()):
    ni = t // tm
    assert ni * tm == t, (name, t, tm)
    in_specs, args = [], []
    for s in ins:
        if s[0] == "row":
            in_specs.append(pl.BlockSpec((tm, s[2]), lambda i, c=s[3]: (i, c)))
        else:
            nd = s[1].ndim
            in_specs.append(pl.BlockSpec(s[1].shape, lambda i, nd=nd: (0,) * nd))
        args.append(s[1])
    out_specs, out_shape = [], []
    for o in outs:
        if o[0] == "row":
            out_specs.append(pl.BlockSpec((tm, o[3]), lambda i, c=o[4]: (i, c)))
        else:
            nd = len(o[1])
            out_specs.append(pl.BlockSpec(o[1], lambda i, nd=nd: (0,) * nd))
        out_shape.append(jax.ShapeDtypeStruct(o[1], o[2]))
    n_read = len(args)
    in_specs += [pl.BlockSpec(memory_space=pl.ANY)] * len(deps)
    args += list(deps)
    n_in = len(args)

    def body(*refs):
        i = pl.program_id(0)
        vals = fn([r[...] for r in refs[:n_read]])
        for o, r, v in zip(outs, refs[n_in:], vals):
            if o[0] == "row":
                r[...] = v.astype(o[2])
            else:
                @pl.when(i == 0)
                def _():
                    r[...] = v

                @pl.when(i != 0)
                def _():
                    r[...] += v

    return pl.pallas_call(
        body,
        name=name,
        grid=(ni,),
        in_specs=in_specs,
        out_specs=out_specs,
        out_shape=out_shape,
        compiler_params=_params(("arbitrary",)),
    )(*args)


def _place():
    return lax.axis_index("x"), lax.axis_index("y"), lax.axis_index("c")


def _remote(src, dst, send_sems, recv_sems, k, to):
    return pltpu.make_async_remote_copy(
        src_ref=src, dst_ref=dst, send_sem=send_sems.at[k], recv_sem=recv_sems.at[k],
        device_id=to, device_id_type=MESH)


def _chips(x, y):
    return [(1 - x, y), (x, 1 - y), (1 - x, 1 - y)]


def _broadcast_plan(n):
    def plan(refs, send_sems, recv_sems):
        x, y, c = _place()
        cps = []
        for i in range(n):
            for dd in range(1, N_DEV):
                peer = (1 - x if dd & 4 else x, 1 - y if dd & 2 else y, 1 - c if dd & 1 else c)
                cps.append(_remote(refs[i], refs[n + i].at[4 * x + 2 * y + c], send_sems, recv_sems, 7 * i + dd - 1, peer))
        return cps

    def local(refs):
        x, y, c = _place()
        return [(refs[i], refs[n + i].at[4 * x + 2 * y + c]) for i in range(n)]

    return plan, local


def _handshake(peers):
    barrier = pltpu.get_barrier_semaphore()
    for peer in peers:
        pl.semaphore_signal(barrier, inc=1, device_id=peer, device_id_type=MESH)
    pl.semaphore_wait(barrier, len(peers))


def _sequencer_call(name, collective_id, body, args, out_type, scratch_types):
    return pl.kernel(
        body,
        out_type=out_type,
        mesh=plsc.ScalarSubcoreMesh(axis_name="sequencer", num_cores=1),
        scratch_types=scratch_types,
        compiler_params=pltpu.CompilerParams(collective_id=collective_id),
        name=name,
    )(*args)


def _sc_gather(name, collective_id, shards):
    n = len(shards)

    def body(*refs):
        ins, outs = refs[:n], refs[n:2 * n]
        send_sems, recv_sems, local_sems = refs[2 * n:]
        x, y, c = _place()
        me, sibling = (x, y, c), (x, y, 1 - c)
        chips = _chips(x, y)
        _handshake([sibling] + [(*chip, c) for chip in chips])

        def slot(i, px, py, pc):
            return outs[i].at[4 * px + 2 * py + pc]

        def copy(i, k, block, to, src=None):
            return _remote(slot(i, *block) if src is None else src, slot(i, *block), send_sems, recv_sems, 7 * i + k, to)

        mine = [pltpu.make_async_copy(ins[i], slot(i, *me), local_sems.at[i]) for i in range(n)]
        for cp in mine:
            cp.start()
        first = []
        for i in range(n):
            first += [copy(i, 1 + j, me, (*chip, c), src=ins[i]) for j, chip in enumerate(chips)]
            first.append(copy(i, 0, me, sibling, src=ins[i]))
        for cp in first:
            cp.start()
        passed = []
        for i in range(n):
            for j, chip in enumerate(chips):
                copy(i, 1 + j, (*chip, c), me).wait_recv()
                fwd = copy(i, 4 + j, (*chip, c), sibling)
                fwd.start()
                passed.append(fwd)
        for i in range(n):
            copy(i, 0, sibling, me).wait_recv()
            for j, chip in enumerate(chips):
                copy(i, 4 + j, (*chip, 1 - c), me).wait_recv()
        for cp in first + passed:
            cp.wait_send()
        for cp in mine:
            cp.wait()

    outs = _sequencer_call(
        name, collective_id, body, shards,
        [jax.ShapeDtypeStruct((N_DEV,) + s.shape, s.dtype) for s in shards],
        [pltpu.SemaphoreType.DMA((7 * n,)), pltpu.SemaphoreType.DMA((7 * n,)), pltpu.SemaphoreType.DMA((n,))])
    return [o.reshape((N_DEV * s.shape[0],) + s.shape[1:]) for o, s in zip(outs, shards)]


def _sc_exchange(name, collective_id, srcs, land_shapes, plan, n_copies, peers, local=None):
    n = len(srcs)

    def body(*refs):
        bufs, send_sems, recv_sems = refs[:2 * n], refs[2 * n], refs[2 * n + 1]
        _handshake(peers())
        if local is not None:
            mine = [pltpu.make_async_copy(a, b, refs[2 * n + 2].at[k]) for k, (a, b) in enumerate(local(bufs))]
            for cp in mine:
                cp.start()
        cps = plan(bufs, send_sems, recv_sems)
        for cp in cps:
            cp.start()
        for cp in cps:
            cp.wait_send()
            cp.wait_recv()
        if local is not None:
            for cp in mine:
                cp.wait()

    scratch = [pltpu.SemaphoreType.DMA((n_copies,)), pltpu.SemaphoreType.DMA((n_copies,))]
    if local is not None:
        scratch.append(pltpu.SemaphoreType.DMA((n,)))
    return list(_sequencer_call(name, collective_id, body, srcs, land_shapes, scratch))


def _adam_vals(w, g, m, v):
    m = ADAM_B1 * m + (1.0 - ADAM_B1) * g
    v = ADAM_B2 * v + (1.0 - ADAM_B2) * (g * g)
    m_hat = m / (1.0 - ADAM_B1 ** ADAM_STEP)
    v_hat = v / (1.0 - ADAM_B2 ** ADAM_STEP)
    delta = -ADAM_LR * (m_hat / (jnp.sqrt(v_hat) + ADAM_EPS) + ADAM_WD * w)
    return delta, m, v


def _sum_parts(st_ref):
    g = st_ref[0].astype(F32)
    for k in range(1, st_ref.shape[0]):
        g = g + st_ref[k].astype(F32)
    return g


def _reduce_adam(name, st, w, m, v, deps=()):
    rows, cols = w.shape
    tr = _tile(rows, UPDATE_ROWS, 16)

    def body(st_ref, w_ref, m_ref, v_ref, *rest):
        g_out, d_out, m_out, v_out = rest[len(deps):]
        g = _sum_parts(st_ref)
        d, mm, vv = _adam_vals(w_ref[...], g, m_ref[...], v_ref[...])
        g_out[...] = g
        d_out[...] = d
        m_out[...] = mm
        v_out[...] = vv

    blk = pl.BlockSpec((tr, cols), lambda i: (i, 0))
    return pl.pallas_call(
        body,
        name=name,
        grid=(rows // tr,),
        in_specs=[pl.BlockSpec((st.shape[0], tr, cols), lambda i: (0, i, 0)), blk, blk, blk]
        + [pl.BlockSpec(memory_space=pl.ANY)] * len(deps),
        out_specs=[blk] * 4,
        out_shape=[jax.ShapeDtypeStruct(w.shape, F32)] * 4,
        compiler_params=_params(("arbitrary",)),
    )(st, w, m, v, *deps)


def _reduce_only(name, st, deps=()):
    _, rows, cols = st.shape
    tr = _tile(rows, UPDATE_ROWS, 16)

    def body(st_ref, *rest):
        rest[-1][...] = _sum_parts(st_ref)

    return pl.pallas_call(
        body,
        name=name,
        grid=(rows // tr,),
        in_specs=[pl.BlockSpec((st.shape[0], tr, cols), lambda i: (0, i, 0))]
        + [pl.BlockSpec(memory_space=pl.ANY)] * len(deps),
        out_specs=pl.BlockSpec((tr, cols), lambda i: (i, 0)),
        out_shape=jax.ShapeDtypeStruct((rows, cols), F32),
        compiler_params=_params(("arbitrary",)),
    )(st, *deps)


def _adam_only(name, w, g, m, v):
    rows, cols = w.shape
    tr = _tile(rows, UPDATE_ROWS, 16)

    def body(w_ref, g_ref, m_ref, v_ref, d_out, m_out, v_out):
        d, mm, vv = _adam_vals(w_ref[...], g_ref[...], m_ref[...], v_ref[...])
        d_out[...] = d
        m_out[...] = mm
        v_out[...] = vv

    blk = pl.BlockSpec((tr, cols), lambda i: (i, 0))
    return pl.pallas_call(
        body,
        name=name,
        grid=(rows // tr,),
        in_specs=[blk] * 4,
        out_specs=[blk] * 3,
        out_shape=[jax.ShapeDtypeStruct(w.shape, F32)] * 3,
        compiler_params=_params(("arbitrary",)),
    )(w, g, m, v)


def _prenorm(name, h, g, t, d, tm, deps=()):
    def fn(v):
        x, gg = v
        return [x * _rms_r(x) * gg]

    return _rowwise(name, [("row", h, d, 0), ("full", g)], [("row", (t, d), BF, d, 0)], fn, t=t, tm=tm, deps=deps)[0]


def _ffn_up(tag, xn, wgt, wut, t, d, f, tm):
    def up_epi(accs, ex):
        gg, uu = accs
        return [gg, uu, gg * _sig(gg) * uu]

    mats = [dict(a=xn, b=wgt, mode="nt", acc=0, tk=d), dict(a=xn, b=wut, mode="nt", acc=1, tk=d)]
    return _matmul(
        tag + "_up", mats, m=t, n=f, tm=tm, tn=_tile(f, 1408),
        outs=[("mn", (t, f), BF, 0)] * 3, epilogue=up_epi, n_acc=2, j_outer=True)


def _ffn_up_split(tag, xn, wgt, wut, t, d, f, tm):
    tf = _tile(f, 1408)
    gate = _matmul(
        tag + "_gate", [dict(a=xn, b=wgt, mode="nt", acc=0, tk=d)], m=t, n=f, tm=tm, tn=tf,
        outs=[("mn", (t, f), BF, 0)], epilogue=lambda accs, ex: accs, n_acc=1, j_outer=True)[0]

    def up_epi(accs, ex):
        gg = ex[0].astype(F32)
        return [accs[0], gg * _sig(gg) * accs[0]]

    up, hid = _matmul(
        tag + "_up", [dict(a=xn, b=wut, mode="nt", acc=0, tk=d)], m=t, n=f, tm=tm, tn=tf, extras=[("mn", gate, 0)],
        outs=[("mn", (t, f), BF, 0)] * 2, epilogue=up_epi, n_acc=1, j_outer=True)
    return gate, up, hid


def _ffn_down(tag, hid, wd, h, g_post, g_next, t, d, f, tm):
    def down_epi(accs, ex):
        ff = accs[0]
        hh, gg, gn = ex
        h_new = hh + 0.5 * ff * _rms_r(ff) * gg
        return [ff, h_new, h_new * _rms_r(h_new) * gn]

    return _matmul(
        tag + "_down", [dict(a=hid, b=wd, mode="nn", acc=0, tk=f)], m=t, n=d, tm=tm, tn=d,
        extras=[("mn", h, 0), ("n", g_post), ("n", g_next)],
        outs=[("mn", (t, d), F32, 0)] * 2 + [("mn", (t, d), BF, 0)], epilogue=down_epi, n_acc=1)


def _prenorm_bwd_epi(accs, ex):
    hh, dh, gg = ex
    dx, dg = _rms_bwd(hh, gg, accs[0])
    return [dh + dx, dg]


def _dx_extras(h_in, dh, g_pre, fo, g_post):
    return [("mn", h_in, 0), ("mn", dh, 0), ("n", g_pre), ("mn", fo, 0), ("n", g_post)]


def _dx_outs(t, d):
    return [("mn", (t, d), F32, 0), ("acc", (1, d), F32), ("mn", (t, d), BF, 0), ("acc", (1, d), F32)]


def _dx_epi(scale):
    def epi(accs, ex):
        hh, dh, gg, fo, gp = ex
        dx, dg = _rms_bwd(hh, gg, accs[0])
        dh_new = dh + dx
        dfo, dgp = _rms_bwd(fo, gp, dh_new * scale)
        return [dh_new, dg, dfo, dgp]

    return epi


def _ffn_dhid(tag, df, wd, gate, up, t, d, f, tm):
    def hid_epi(accs, ex):
        dhid = accs[0]
        gg, uu = ex[0].astype(F32), ex[1].astype(F32)
        s = _sig(gg)
        gs = gg * s
        return [dhid * uu * (s + gs * (1.0 - s)), dhid * gs]

    return _matmul(
        tag + "_dhid", [dict(a=df, b=wd, mode="nt", acc=0, tk=d)], m=t, n=f, tm=tm, tn=_tile(f, 1408),
        extras=[("mn", gate, 0), ("mn", up, 0)],
        outs=[("mn", (t, f), BF, 0)] * 2, epilogue=hid_epi, n_acc=1, j_outer=True)


def _ffn_dwd(tag, hid, df, t, d, f):
    tk_t = _tile(t, 2 * TOKEN_K_TILE)
    return _matmul(
        tag + "_dwd", [dict(a=hid, b=df, mode="tn", acc=0, tk=tk_t)], m=f, n=d, tm=_tile(f, 1408), tn=d,
        nk=t // tk_t, outs=[("mn", (f, d), BF, 0)], epilogue=lambda accs, ex: accs, n_acc=1)


def _ffn_dw_in(name, dact, xn, t, d, f, deps=()):
    tk_t = _tile(t, 2 * TOKEN_K_TILE)
    return _matmul(
        name, [dict(a=dact, b=xn, mode="tn", acc=0, tk=tk_t)], m=f, n=d, tm=_tile(f, 1408), tn=d, nk=t // tk_t,
        outs=[("mn", (f, d), BF, 0)], epilogue=lambda accs, ex: accs, n_acc=1, deps=deps)[0]


def _ffn_dx(tag, dh, h_in, dgate, dup, wgt, wut, g_pre, t, d, f, tm, deps=(), before=None):
    tf = _tile(f, 1408)
    mats = [dict(a=dgate, b=wgt, mode="nn", acc=0, tk=tf), dict(a=dup, b=wut, mode="nn", acc=0, tk=tf)]
    if before is None:
        return _matmul(
            tag + "_dx", mats, m=t, n=d, tm=tm, tn=d, nk=f // tf,
            extras=[("mn", h_in, 0), ("mn", dh, 0), ("n", g_pre)],
            outs=[("mn", (t, d), F32, 0), ("acc", (1, d), F32)], epilogue=_prenorm_bwd_epi, n_acc=1, deps=deps,
            epi_rows=EPI_ROWS)
    fo, g_post, scale = before
    return _matmul(
        tag + "_dx", mats, m=t, n=d, tm=tm, tn=d, nk=f // tf, extras=_dx_extras(h_in, dh, g_pre, fo, g_post),
        outs=_dx_outs(t, d), epilogue=_dx_epi(scale), n_acc=1, deps=deps, epi_rows=EPI_ROWS)


def _causal_mask():
    r = lax.broadcasted_iota(jnp.int32, (CHUNK, CHUNK), 0)
    c = lax.broadcasted_iota(jnp.int32, (CHUNK, CHUNK), 1)
    return r >= c


def _layernorm_parts(v):
    mu = jnp.mean(v, axis=-1, keepdims=True)
    vc = v - mu
    rstd = lax.rsqrt(jnp.mean(vc * vc, axis=-1, keepdims=True) + EPS)
    return vc * rstd, rstd


def _sgu_fwd(z, g_sgu, w_s, b_col, t, d, tm):
    dg = d // N_SGU_GROUPS
    n_chunk = tm // CHUNK

    def body(zu_ref, zv_ref, g_ref, w_ref, b_ref, a_ref):
        u = _gelu(zu_ref[...].astype(F32))
        vhat, _ = _layernorm_parts(_gelu(zv_ref[...].astype(F32)))
        vn = (vhat * g_ref[...]).astype(BF)
        mask = _causal_mask()
        for gi in range(N_SGU_GROUPS):
            ws = jnp.where(mask, w_ref[gi], 0.0).astype(BF)
            bias = b_ref[gi]
            for ci in range(n_chunk):
                rows, cols = slice(ci * CHUNK, (ci + 1) * CHUNK), slice(gi * dg, (gi + 1) * dg)
                sv = jnp.dot(ws, vn[rows, cols], preferred_element_type=F32) + bias
                a_ref[rows, cols] = (u[rows, cols] * sv).astype(BF)

    return pl.pallas_call(
        body,
        name="sgu_fwd",
        grid=(t // tm,),
        in_specs=[
            pl.BlockSpec((tm, d), lambda i: (i, 0)),
            pl.BlockSpec((tm, d), lambda i: (i, 1)),
            pl.BlockSpec((1, d), lambda i: (0, 0)),
            pl.BlockSpec(w_s.shape, lambda i: (0, 0, 0)),
            pl.BlockSpec(b_col.shape, lambda i: (0, 0, 0)),
        ],
        out_specs=pl.BlockSpec((tm, d), lambda i: (i, 0)),
        out_shape=jax.ShapeDtypeStruct((t, d), BF),
        compiler_params=_params(("arbitrary",)),
    )(z, z, g_sgu, w_s, b_col)


def _sgu_bwd(z, da, dz, g_sgu, w_s, b_col, t, d, tm):
    dg = d // N_SGU_GROUPS
    n_chunk = tm // CHUNK

    def body(zu_ref, zv_ref, da_ref, dz_in, g_ref, w_ref, b_ref, dz_ref, dw_ref, db_ref, dgn_ref, dvn_ref):
        del dz_in
        dzu_ref, dzv_ref = dz_ref.at[:, pl.ds(0, d)], dz_ref.at[:, pl.ds(d, d)]
        i = pl.program_id(0)
        zu, zv = zu_ref[...].astype(F32), zv_ref[...].astype(F32)
        u = _gelu(zu)
        vhat, rstd = _layernorm_parts(_gelu(zv))
        gn = g_ref[...]
        vn = (vhat * gn).astype(BF)
        da_v = da_ref[...].astype(F32)
        dsv_all = (da_v * u).astype(BF)
        mask = _causal_mask()
        for gi in range(N_SGU_GROUPS):
            ws = jnp.where(mask, w_ref[gi], 0.0).astype(BF)
            bias = b_ref[gi]
            dw = jnp.zeros((CHUNK, CHUNK), F32)
            dbias = jnp.zeros((CHUNK, 1), F32)
            for ci in range(n_chunk):
                rows, cols = slice(ci * CHUNK, (ci + 1) * CHUNK), slice(gi * dg, (gi + 1) * dg)
                vn_c, dsv = vn[rows, cols], dsv_all[rows, cols]
                sv = jnp.dot(ws, vn_c, preferred_element_type=F32) + bias
                dzu_ref[rows, cols] = (da_v[rows, cols] * sv * _gelu_grad(zu[rows, cols])).astype(BF)
                dw = dw + lax.dot_general(dsv, vn_c, _DN["nt"], preferred_element_type=F32)
                dbias = dbias + jnp.sum(dsv.astype(F32), axis=1, keepdims=True)
                dvn_ref[rows, cols] = lax.dot_general(ws, dsv, _DN["tn"], preferred_element_type=F32)
            dw = jnp.where(mask, dw, 0.0)

            @pl.when(i == 0)
            def _():
                dw_ref[gi] = dw
                db_ref[gi] = dbias

            @pl.when(i != 0)
            def _():
                dw_ref[gi] += dw
                db_ref[gi] += dbias

        dvn = dvn_ref[...]
        dgn = jnp.sum(dvn * vhat, axis=0, keepdims=True)

        @pl.when(i == 0)
        def _():
            dgn_ref[...] = dgn

        @pl.when(i != 0)
        def _():
            dgn_ref[...] += dgn

        dvh = dvn * gn
        dv = rstd * (dvh - jnp.mean(dvh, axis=-1, keepdims=True) - vhat * jnp.mean(dvh * vhat, axis=-1, keepdims=True))
        dzv_ref[...] = (dv * _gelu_grad(zv)).astype(BF)

    return pl.pallas_call(
        body,
        name="sgu_bwd",
        grid=(t // tm,),
        in_specs=[
            pl.BlockSpec((tm, d), lambda i: (i, 0)),
            pl.BlockSpec((tm, d), lambda i: (i, 1)),
            pl.BlockSpec((tm, d), lambda i: (i, 0)),
            pl.BlockSpec(memory_space=pl.ANY),
            pl.BlockSpec((1, d), lambda i: (0, 0)),
            pl.BlockSpec(w_s.shape, lambda i: (0, 0, 0)),
            pl.BlockSpec(b_col.shape, lambda i: (0, 0, 0)),
        ],
        out_specs=[
            pl.BlockSpec((tm, 2 * d), lambda i: (i, 0)),
            pl.BlockSpec(w_s.shape, lambda i: (0, 0, 0)),
            pl.BlockSpec(b_col.shape, lambda i: (0, 0, 0)),
            pl.BlockSpec((1, d), lambda i: (0, 0)),
        ],
        out_shape=[
            jax.ShapeDtypeStruct(dz.shape, BF),
            jax.ShapeDtypeStruct(w_s.shape, F32),
            jax.ShapeDtypeStruct(b_col.shape, F32),
            jax.ShapeDtypeStruct((1, d), F32),
        ],
        scratch_shapes=[pltpu.VMEM((tm, d), F32)],
        input_output_aliases={3: 0},
        compiler_params=_params(("arbitrary",)),
    )(z, z, da, dz, g_sgu, w_s, b_col)


def _shift_down(x, k, row):
    return jnp.where(row >= k, pltpu.roll(x, k, 0), 0.0)


def _shift_up(x, k, row, t):
    return jnp.where(row < t - k, pltpu.roll(x, t - k, 0), 0.0)


def _doublings(window):
    steps = int(math.log2(window))
    assert 2 ** steps == window
    return [2 ** s for s in range(steps)]


def _pool_diff(c, window, row):
    s = c
    for k in _doublings(window):
        s = s + _shift_down(s, k, row)
    count = jnp.minimum(row + 1, window).astype(F32)
    return s / count - c, count


def _pool_fwd(z, pool_w, pool_scale, t, d):
    dgp = d // len(POOL_WINDOWS)
    cblk = (2 * d) // dgp

    def body(zc_ref, w_ref, s_ref, b_ref):
        row = lax.broadcasted_iota(jnp.int32, (t, 1), 0)
        for gi, window in enumerate(POOL_WINDOWS):
            @pl.when(pl.program_id(0) == gi)
            def _(window=window):
                diff, _ = _pool_diff(zc_ref[...].astype(F32), window, row)
                out = jnp.dot(diff.astype(BF), w_ref[...], preferred_element_type=F32)
                b_ref[...] = (out * s_ref[...]).astype(BF)

    return pl.pallas_call(
        body,
        name="pool_fwd",
        grid=(len(POOL_WINDOWS),),
        in_specs=[
            pl.BlockSpec((t, dgp), lambda g: (0, cblk + g)),
            pl.BlockSpec((None, dgp, dgp), lambda g: (g, 0, 0)),
            pl.BlockSpec((1, dgp), lambda g: (0, g)),
        ],
        out_specs=pl.BlockSpec((t, dgp), lambda g: (0, g)),
        out_shape=jax.ShapeDtypeStruct((t, d), BF),
        compiler_params=_params(("arbitrary",)),
    )(z, pool_w, pool_scale)


def _pool_bwd(z, db, dz, pool_w, pool_scale, t, d):
    dgp = d // len(POOL_WINDOWS)
    cblk = (2 * d) // dgp

    def body(zc_ref, db_ref, dz_in, w_ref, s_ref, dzc_ref, dw_ref, ds_ref):
        del dz_in
        row = lax.broadcasted_iota(jnp.int32, (t, 1), 0)
        for gi, window in enumerate(POOL_WINDOWS):
            @pl.when(pl.program_id(0) == gi)
            def _(window=window):
                diff, count = _pool_diff(zc_ref[...].astype(F32), window, row)
                diff = diff.astype(BF)
                w = w_ref[...]
                dbv = db_ref[...].astype(F32)
                out = jnp.dot(diff, w, preferred_element_type=F32)
                ds_ref[...] = jnp.sum(dbv * out, axis=0, keepdims=True)
                dout = (dbv * s_ref[...]).astype(BF)
                dw_ref[...] = lax.dot_general(diff, dout, _DN["tn"], preferred_element_type=F32).astype(BF)
                ddiff = lax.dot_general(dout, w, _DN["nt"], preferred_element_type=F32)
                s = ddiff / count
                for k in _doublings(window):
                    s = s + _shift_up(s, k, row, t)
                dzc_ref[...] = (s - ddiff).astype(BF)

    return pl.pallas_call(
        body,
        name="pool_bwd",
        grid=(len(POOL_WINDOWS),),
        in_specs=[
            pl.BlockSpec((t, dgp), lambda g: (0, cblk + g)),
            pl.BlockSpec((t, dgp), lambda g: (0, g)),
            pl.BlockSpec(memory_space=pl.ANY),
            pl.BlockSpec((None, dgp, dgp), lambda g: (g, 0, 0)),
            pl.BlockSpec((1, dgp), lambda g: (0, g)),
        ],
        out_specs=[
            pl.BlockSpec((t, dgp), lambda g: (0, cblk + g)),
            pl.BlockSpec((None, dgp, dgp), lambda g: (g, 0, 0)),
            pl.BlockSpec((1, dgp), lambda g: (0, g)),
        ],
        out_shape=[
            jax.ShapeDtypeStruct(dz.shape, BF),
            jax.ShapeDtypeStruct(pool_w.shape, BF),
            jax.ShapeDtypeStruct((1, d), F32),
        ],
        input_output_aliases={2: 0},
        compiler_params=_params(("arbitrary",)),
    )(z, db, dz, pool_w, pool_scale)


def _mix_dy(dm, w_o, z, ya, yb, t, d, tm):
    def body(dm_ref, w_ref, gate_ref, ya_ref, yb_ref, dyab_ref, dz_ref, dy_ref):
        j = pl.program_id(1)

        @pl.when(j == 0)
        def _():
            dy_ref[...] = lax.dot_general(dm_ref[...], w_ref[...], _DN["nt"], preferred_element_type=F32)

        dy = dy_ref[...]
        s = _sig(gate_ref[...].astype(F32))
        yv = jnp.where(j == 0, ya_ref[...], yb_ref[...]).astype(F32)
        dyab_ref[...] = (dy * s).astype(BF)
        dz_ref[...] = (dy * yv * s * (1.0 - s)).astype(BF)

    row = pl.BlockSpec((tm, d), lambda i, j: (i, 0))
    return pl.pallas_call(
        body,
        name="mix_dy",
        grid=(t // tm, 2),
        in_specs=[row, pl.BlockSpec((d, d), lambda i, j: (0, 0)), pl.BlockSpec((tm, d), lambda i, j: (i, 3 + j)), row, row],
        out_specs=[pl.BlockSpec((tm, d), lambda i, j: (i, j)), pl.BlockSpec((tm, d), lambda i, j: (i, 3 + j))],
        out_shape=[jax.ShapeDtypeStruct((t, 2 * d), BF), jax.ShapeDtypeStruct((t, 5 * d), BF)],
        scratch_shapes=[pltpu.VMEM((tm, d), F32)],
        compiler_params=_params(("arbitrary", "arbitrary")),
    )(dm, w_o, z, ya, yb)


def _everyone():
    x, y, c = _place()
    return [(1 - x if dd & 4 else x, 1 - y if dd & 2 else y, 1 - c if dd & 1 else c) for dd in range(1, N_DEV)]


def _direct_plan(n):
    def plan(refs, send_sems, recv_sems):
        x, y, c = _place()
        me = 4 * x + 2 * y + c
        cps = []
        for i in range(n):
            for k, (px, py, pc) in enumerate(_everyone()):
                cps.append(_remote(refs[i].at[4 * px + 2 * py + pc], refs[n + i].at[me], send_sems, recv_sems,
                                   7 * i + k, (px, py, pc)))
        return cps

    def local(refs):
        x, y, c = _place()
        me = 4 * x + 2 * y + c
        return [(refs[i].at[me], refs[n + i].at[me]) for i in range(n)]

    return plan, local


def _sc_scatter_direct(name, collective_id, parts):
    n = len(parts)
    p8 = [a.reshape(N_DEV, a.shape[0] // N_DEV, a.shape[1]) for a in parts]
    plan, local = _direct_plan(n)
    return _sc_exchange(name, collective_id, p8, [jax.ShapeDtypeStruct(a.shape, a.dtype) for a in p8], plan,
                        7 * n, _everyone, local=local)


def _stack_rows(arrays):
    parts, starts, row = [], [], 0
    for a in arrays:
        pad = -a.shape[0] % 8
        starts.append(row)
        parts += [a, jnp.zeros((pad, a.shape[1]), a.dtype)] if pad else [a]
        row += a.shape[0] + pad
    return jnp.concatenate(parts, axis=0), starts


def _adam_replicated(name, stacks, layout, weights, with_loss, deps=()):
    n_st, n_par = len(stacks), len(weights)

    def body(*refs):
        st_refs = refs[:n_st]
        w_refs = refs[n_st:n_st + 3 * n_par]
        out_refs = refs[n_st + 3 * n_par + len(deps):]

        def summed(which, row, rows):
            g = st_refs[which][0, pl.ds(row, rows), :]
            for k in range(1, N_DEV):
                g = g + st_refs[which][k, pl.ds(row, rows), :]
            return g

        for i in range(n_par):
            w_ref, m_ref, v_ref = w_refs[3 * i:3 * i + 3]
            g = summed(layout[i][0], layout[i][1], w_ref.shape[0])
            dlt, mm, vv = _adam_vals(w_ref[...], g, m_ref[...], v_ref[...])
            for r, val in zip(out_refs[4 * i:4 * i + 4], (g, dlt, mm, vv)):
                r[...] = val
        if with_loss:
            out_refs[4 * n_par][...] = jnp.sum(summed(layout[n_par][0], layout[n_par][1], 1), axis=1, keepdims=True)

    flat_w = [a for wmv in weights for a in wmv]
    out_shape = [jax.ShapeDtypeStruct(wmv[0].shape, F32) for wmv in weights for _ in range(4)]
    if with_loss:
        out_shape.append(jax.ShapeDtypeStruct((1, 1), F32))
    vmem = pl.BlockSpec(memory_space=pltpu.VMEM)
    res = pl.pallas_call(
        body,
        name=name,
        in_specs=[vmem] * (n_st + len(flat_w)) + [pl.BlockSpec(memory_space=pl.ANY)] * len(deps),
        out_specs=[vmem] * len(out_shape),
        out_shape=out_shape,
        compiler_params=_params(),
    )(*stacks, *flat_w, *deps)
    return [res[4 * i:4 * i + 4] for i in range(n_par)], res[4 * n_par] if with_loss else None


def kernel(x, p, ffn1_pre_g, ffn1_w_gate, ffn1_w_up, ffn1_w_down, ffn1_post_g, mix_pre_g, w_in, sgu_norm_g, sgu_w, sgu_b, pool_w, pool_scale, w_out_a, w_out_b, w_o, mix_post_g, ffn2_pre_g, ffn2_w_gate, ffn2_w_up, ffn2_w_down, ffn2_post_g, ple_pre_g, ple_w_gate, ple_w_proj, ple_post_g, loss_target, m_ffn1_pre_g, m_ffn1_w_gate, m_ffn1_w_up, m_ffn1_w_down, m_ffn1_post_g, m_mix_pre_g, m_w_in, m_sgu_norm_g, m_sgu_w, m_sgu_b, m_pool_w, m_pool_scale, m_w_out_a, m_w_out_b, m_w_o, m_mix_post_g, m_ffn2_pre_g, m_ffn2_w_gate, m_ffn2_w_up, m_ffn2_w_down, m_ffn2_post_g, m_ple_pre_g, m_ple_w_gate, m_ple_w_proj, m_ple_post_g, v_ffn1_pre_g, v_ffn1_w_gate, v_ffn1_w_up, v_ffn1_w_down, v_ffn1_post_g, v_mix_pre_g, v_w_in, v_sgu_norm_g, v_sgu_w, v_sgu_b, v_pool_w, v_pool_scale, v_w_out_a, v_w_out_b, v_w_o, v_mix_post_g, v_ffn2_pre_g, v_ffn2_w_gate, v_ffn2_w_up, v_ffn2_w_down, v_ffn2_post_g, v_ple_pre_g, v_ple_w_gate, v_ple_w_proj, v_ple_post_g):
    args = dict(locals())
    names = ["ffn1_pre_g", "ffn1_w_gate", "ffn1_w_up", "ffn1_w_down", "ffn1_post_g", "mix_pre_g", "w_in",
             "sgu_norm_g", "sgu_w", "sgu_b", "pool_w", "pool_scale", "w_out_a", "w_out_b", "w_o", "mix_post_g",
             "ffn2_pre_g", "ffn2_w_gate", "ffn2_w_up", "ffn2_w_down", "ffn2_post_g", "ple_pre_g", "ple_w_gate",
             "ple_w_proj", "ple_post_g"]
    w = {k: args[k][0] for k in names}
    mom = {k: args["m_" + k][0] for k in names}
    var = {k: args["v_" + k][0] for k in names}

    assert x.shape[0] == 1 and p.shape[:2] == (1, 1)
    t, d = x.shape[1], x.shape[2]
    f = ffn1_w_gate.shape[2] * N_DEV
    d_in = w_in.shape[2] * N_DEV
    d_ple = p.shape[3]
    n_pool = len(POOL_WINDOWS)
    dgp = d // n_pool
    assert d_in == 5 * d and t % CHUNK == 0
    tm = _tile(t, 512, CHUNK)
    xs, ps, target = x[0], p[0, 0], loss_target[0]

    col_sharded = ("ffn1_w_gate", "ffn1_w_up", "ffn2_w_gate", "ffn2_w_up", "w_in", "ple_w_proj")

    def shard_of(k):
        if k in col_sharded:
            return w[k].T.astype(BF)
        if k == "pool_w":
            return w[k].reshape(-1, dgp).astype(BF)
        return w[k].astype(BF)

    rows_pw = dgp // N_DEV
    b_col = w["sgu_b"][:, :, None]
    gains = {k: w[k][None, :] for k in names if w[k].ndim == 1}
    full = {}

    groups = [["ffn1_w_gate"], ["ffn1_w_up"], ["ffn1_w_down"], ["w_in"], ["pool_w", "w_out_a", "w_out_b", "w_o"],
              ["ffn2_w_gate", "ffn2_w_up", "ffn2_w_down", "ple_w_gate", "ple_w_proj"]]
    def gather(gi, behind=None):
        shards = [shard_of(k) for k in groups[gi]]
        if behind is not None:
            shards = lax.optimization_barrier((shards, behind))[0]
        full.update(zip(groups[gi], _sc_gather("gather%d" % gi, gi, shards)))

    gather(0)
    xn1 = _prenorm("ffn1_prenorm", xs, gains["ffn1_pre_g"], t, d, tm)
    gather(1, xn1)
    gather(2, xn1)
    gather(3, xn1)
    g1, u1, hid1 = _ffn_up_split("ffn1", xn1, full["ffn1_w_gate"], full["ffn1_w_up"], t, d, f, tm)
    gather(4, hid1)
    f1, h1, xn2 = _ffn_down("ffn1", hid1, full["ffn1_w_down"], xs, gains["ffn1_post_g"], gains["mix_pre_g"], t, d, f, tm)
    gather(5, h1)

    pool_full = full["pool_w"].reshape(N_DEV, n_pool, rows_pw, dgp).transpose(1, 0, 2, 3).reshape(n_pool, dgp, dgp)
    z = _matmul("mix_in", [dict(a=xn2, b=full["w_in"], mode="nt", acc=0, tk=d)], m=t, n=d_in, tm=_tile(t, 1024, CHUNK), tn=d,
                outs=[("mn", (t, d_in), BF, 0)], epilogue=lambda accs, ex: accs, n_acc=1, j_outer=True)[0]
    a_br = _sgu_fwd(z, gains["sgu_norm_g"], w["sgu_w"], b_col, t, d, tm)
    b_br = _pool_fwd(z, pool_full, gains["pool_scale"], t, d)

    def merge_epi(accs, ex):
        ya, yb = accs
        ga, gb = ex[0].astype(F32), ex[1].astype(F32)
        return [ya, yb, _sig(ga) * ya + _sig(gb) * yb]

    ya, yb, y = _matmul(
        "mix_merge",
        [dict(a=a_br, b=full["w_out_a"], mode="nn", acc=0, tk=d), dict(a=b_br, b=full["w_out_b"], mode="nn", acc=1, tk=d)],
        m=t, n=d, tm=tm, tn=d, extras=[("mn", z, 3), ("mn", z, 4)],
        outs=[("mn", (t, d), BF, 0)] * 3, epilogue=merge_epi, n_acc=2)

    def proj_epi(accs, ex):
        mm = accs[0]
        hh, gg, gn = ex
        h_new = hh + mm * _rms_r(mm) * gg
        return [mm, h_new, h_new * _rms_r(h_new) * gn]

    m_out, h2, xn3 = _matmul(
        "mix_proj", [dict(a=y, b=full["w_o"], mode="nn", acc=0, tk=d)], m=t, n=d, tm=tm, tn=d,
        extras=[("mn", h1, 0), ("n", gains["mix_post_g"]), ("n", gains["ffn2_pre_g"])],
        outs=[("mn", (t, d), F32, 0)] * 2 + [("mn", (t, d), BF, 0)], epilogue=proj_epi, n_acc=1)

    g2, u2, hid2 = _ffn_up("ffn2", xn3, full["ffn2_w_gate"], full["ffn2_w_up"], t, d, f, tm)
    f2, h3, xn4 = _ffn_down("ffn2", hid2, full["ffn2_w_down"], h2, gains["ffn2_post_g"], gains["ple_pre_g"], t, d, f, tm)

    def ple_epi(accs, ex):
        gl, e = accs
        hh, tgt, gg = ex
        q = _sig(gl) * e
        err = hh + q * _rms_r(q) * gg - tgt
        return [gl, e, err * (1.0 / d), jnp.sum(err * err, axis=0, keepdims=True)]

    gl, e_ple, dh4, loss_vec = _matmul(
        "ple_fwd",
        [dict(a=xn4, b=full["ple_w_gate"], mode="nn", acc=0, tk=d), dict(a=ps, b=full["ple_w_proj"], mode="nt", acc=1, tk=d_ple)],
        m=t, n=d, tm=tm, tn=d, extras=[("mn", h3, 0), ("mn", target, 0), ("n", gains["ple_post_g"])],
        outs=[("mn", (t, d), F32, 0)] * 3 + [("acc", (1, d), F32)], epilogue=ple_epi, n_acc=2)

    tk_t = _tile(t, TOKEN_K_TILE)
    tk_1 = _tile(t, 2 * TOKEN_K_TILE)

    def ple_post_fn(v):
        dy, gl_v, e_v, gg = v
        s = _sig(gl_v)
        dq, dg = _rms_bwd(s * e_v, gg, dy)
        return [dq * e_v * s * (1.0 - s), dq * s, dg]

    dgl, de, dg_ple_post = _rowwise(
        "ple_post_bwd", [("row", dh4, d, 0), ("row", gl, d, 0), ("row", e_ple, d, 0), ("full", gains["ple_post_g"])],
        [("row", (t, d), BF, d, 0), ("row", (t, d), BF, d, 0), ("acc", (1, d), F32)], ple_post_fn, t=t, tm=tm)
    ident = lambda accs, ex: accs
    dw_ple_gate = _matmul("ple_dwg", [dict(a=xn4, b=dgl, mode="tn", acc=0, tk=tk_1)], m=d, n=d, tm=d, tn=d,
                          nk=t // tk_1, outs=[("mn", (d, d), BF, 0)], epilogue=ident, n_acc=1)[0]
    dw_ple_proj_t = _matmul("ple_dwp", [dict(a=de, b=ps, mode="tn", acc=0, tk=tk_1)], m=d, n=d_ple, tm=d, tn=d_ple,
                            nk=t // tk_1, outs=[("mn", (d, d_ple), BF, 0)], epilogue=ident, n_acc=1)[0]
    dh3, dg_ple_pre, df2, dg_f2_post = _matmul(
        "ple_dx", [dict(a=dgl, b=full["ple_w_gate"], mode="nt", acc=0, tk=d)], m=t, n=d, tm=tm, tn=d,
        extras=_dx_extras(h3, dh4, gains["ple_pre_g"], f2, gains["ffn2_post_g"]),
        outs=_dx_outs(t, d), epilogue=_dx_epi(0.5), n_acc=1, epi_rows=EPI_ROWS)

    staged = []
    n_gather = len(groups)

    def scatter(keys, parts):
        k = len(staged)
        staged.append((keys, _sc_scatter_direct("scatter%d" % k, n_gather + k, parts)))

    dgate2, dup2 = _ffn_dhid("ffn2", df2, full["ffn2_w_down"], g2, u2, t, d, f, tm)
    dwd2 = _ffn_dwd("ffn2", hid2, df2, t, d, f)[0]
    dwg2 = _ffn_dw_in("ffn2_dwg", dgate2, xn3, t, d, f)
    dwu2 = _ffn_dw_in("ffn2_dwu", dup2, xn3, t, d, f)
    scatter(["ple_w_gate", "ple_w_proj", "ffn2_w_down", "ffn2_w_gate", "ffn2_w_up"],
            [dw_ple_gate, dw_ple_proj_t, dwd2, dwg2, dwu2])
    dh2, dg_f2_pre, dm, dg_mix_post = _ffn_dx(
        "ffn2", dh3, h2, dgate2, dup2, full["ffn2_w_gate"], full["ffn2_w_up"], gains["ffn2_pre_g"], t, d, f, tm,
        deps=[dw_ple_gate, dw_ple_proj_t, dwd2, dwg2, dwu2], before=(m_out, gains["mix_post_g"], 1.0))

    dyab, dz = _mix_dy(dm, full["w_o"], z, ya, yb, t, d, tm)
    dw_o = _matmul("mix_dwo", [dict(a=y, b=dm, mode="tn", acc=0, tk=tk_1)], m=d, n=d, tm=d, tn=d, nk=t // tk_1,
                   outs=[("mn", (d, d), BF, 0)], epilogue=ident, n_acc=1)[0]
    dw_out_a, dw_out_b = _matmul(
        "mix_dwab",
        [dict(a=a_br, b=dyab, mode="tn", acc=0, tk=tk_t),
         dict(a=b_br, b=dyab, mode="tn", acc=1, tk=tk_t, b_off=(0, 1))],
        m=d, n=d, tm=d, tn=d, nk=t // tk_t, outs=[("mn", (d, d), BF, 0)] * 2, epilogue=ident, n_acc=2)
    scatter(["w_o", "w_out_a", "w_out_b"], [dw_o, dw_out_a, dw_out_b])
    da, db = _matmul(
        "mix_dab",
        [dict(a=dyab, b=full["w_out_a"], mode="nt", acc=0, tk=d),
         dict(a=dyab, b=full["w_out_b"], mode="nt", acc=1, tk=d, a_off=(0, 1))],
        m=t, n=d, tm=tm, tn=d, outs=[("mn", (t, d), BF, 0)] * 2, epilogue=ident, n_acc=2, deps=[dw_o, dw_out_a])
    dz, dsgu_w, dsgu_b, dg_sgu = _sgu_bwd(z, da, dz, gains["sgu_norm_g"], w["sgu_w"], b_col, t, d, _tile(t, 256, CHUNK))
    dz, dpool_w, dg_pool = _pool_bwd(z, db, dz, pool_full, gains["pool_scale"], t, d)
    dw_in_t = _matmul(
        "mix_dwin", [dict(a=dz, b=xn2, mode="tn", acc=0, tk=tk_1)], m=d_in, n=d, tm=d, tn=d,
        nk=t // tk_1, outs=[("mn", (d_in, d), BF, 0)], epilogue=ident, n_acc=1)[0]
    dpool_rows = dpool_w.reshape(n_pool, N_DEV, rows_pw, dgp).transpose(1, 0, 2, 3).reshape(N_DEV * n_pool * rows_pw, dgp)
    scatter(["pool_w", "w_in"], [dpool_rows, dw_in_t])
    dh1, dg_mix_pre, df1, dg_f1_post = _matmul(
        "mix_dx", [dict(a=dz, b=full["w_in"], mode="nn", acc=0, tk=d_in // 2)], m=t, n=d, tm=tm, tn=d, nk=2,
        extras=_dx_extras(h1, dh2, gains["mix_pre_g"], f1, gains["ffn1_post_g"]),
        outs=_dx_outs(t, d), epilogue=_dx_epi(0.5), n_acc=1, deps=[dpool_rows, dw_in_t], epi_rows=EPI_ROWS)

    dgate1, dup1 = _ffn_dhid("ffn1", df1, full["ffn1_w_down"], g1, u1, t, d, f, tm)
    dwd1 = _ffn_dwd("ffn1", hid1, df1, t, d, f)[0]
    scatter(["ffn1_w_down"], [dwd1])
    dwg1 = _ffn_dw_in("ffn1_dwg", dgate1, xn1, t, d, f, deps=[dwd1])
    scatter(["ffn1_w_gate"], [dwg1])
    dwu1 = _ffn_dw_in("ffn1_dwu", dup1, xn1, t, d, f, deps=[dwg1])
    scatter(["ffn1_w_up"], [dwu1])
    grad_x, dg_f1_pre = _ffn_dx("ffn1", dh1, xs, dgate1, dup1, full["ffn1_w_gate"], full["ffn1_w_up"],
                                gains["ffn1_pre_g"], t, d, f, tm, deps=[dwu1])

    grad, delta, new_m, new_v = {}, {}, {}, {}

    def update(group, partial_sums, behind):
        done = []
        for k, st in zip(group, partial_sums):
            if k in col_sharded and w[k].shape[1] % LANES:
                res = _reduce_adam("adam_" + k, st, w[k].T, mom[k].T, var[k].T, deps=behind)
                grad[k], delta[k], new_m[k], new_v[k] = [r.T for r in res]
                done.append(res[1])
                continue
            elif k in col_sharded:
                grad[k] = _reduce_only("sum_" + k, st, deps=behind).T
                delta[k], new_m[k], new_v[k] = _adam_only("adam_" + k, w[k], grad[k], mom[k], var[k])
            else:
                shape = w[k].shape
                flat = (-1, shape[-1])
                res = _reduce_adam("adam_" + k, st, w[k].reshape(flat), mom[k].reshape(flat), var[k].reshape(flat),
                                   deps=behind)
                grad[k], delta[k], new_m[k], new_v[k] = [r.reshape(shape) for r in res]
            done.append(delta[k])
        return done

    gain_grads = {
        "ffn1_post_g": dg_f1_post, "mix_pre_g": dg_mix_pre, "sgu_norm_g": dg_sgu, "pool_scale": dg_pool,
        "mix_post_g": dg_mix_post, "ffn2_pre_g": dg_f2_pre, "ffn2_post_g": dg_f2_post, "ple_pre_g": dg_ple_pre,
        "ple_post_g": dg_ple_post,
    }
    gain_names = list(gain_grads)
    n_mix = N_SGU_GROUPS * CHUNK
    rep_shape = {k: (1, d) for k in gain_names + ["ffn1_pre_g"]}
    rep_shape.update(sgu_b=(N_SGU_GROUPS, CHUNK), sgu_w=(n_mix, CHUNK))

    def replicated(tag, cid, arrays, rep_names, layout, with_loss, behind):
        plan_s, local_s = _broadcast_plan(len(arrays))
        stacks = _sc_exchange(
            tag, cid, arrays, [jax.ShapeDtypeStruct((N_DEV,) + a.shape, F32) for a in arrays],
            plan_s, len(arrays) * (N_DEV - 1), _everyone, local=local_s)
        res, loss_sum = _adam_replicated(
            "adam_" + tag, stacks, layout,
            [tuple(src[k].reshape(rep_shape[k]) for src in (w, mom, var)) for k in rep_names], with_loss, deps=behind)
        for k, quad in zip(rep_names, res):
            for dst, val in zip((grad, delta, new_m, new_v), quad):
                dst[k] = val.reshape(w[k].shape)
        return loss_sum, [res[-1][1]]

    wide, wide_rows = _stack_rows([gain_grads[k] for k in gain_names] + [loss_vec * (0.5 / d)])
    narrow, narrow_rows = _stack_rows([dsgu_b.reshape(N_SGU_GROUPS, CHUNK), dsgu_w.reshape(n_mix, CHUNK)])
    layout = [(0, r) for r in wide_rows[:-1]] + [(1, narrow_rows[0]), (1, narrow_rows[1]), (0, wide_rows[-1])]
    done = [grad_x]
    for keys, partial_sums in staged[:3]:
        done = update(keys, partial_sums, done)
    loss, done = replicated("small", n_gather + len(staged), [wide, narrow], gain_names + ["sgu_b", "sgu_w"], layout,
                            True, done)
    for keys, partial_sums in staged[3:]:
        done = update(keys, partial_sums, done)
    late, late_rows = _stack_rows([dg_f1_pre])
    replicated("late", n_gather + len(staged) + 1, [late], ["ffn1_pre_g"], [(0, late_rows[0])], False, done)

    out = [loss[0, 0], grad_x[None]]
    for group in (grad, delta, new_m, new_v):
        out += [group[k][None] for k in names]
    return tuple(out)
```
